```python
import jax
import jax.numpy as jnp
from jax import lax
import numpy as np

D_MODEL = 1024
BATCH = 8
SEQ = 4096
DEPTH = 2
DEC_BATCH = 32
DEC_SEQ = 4
PAST_LEN = 16384
PAGE_SIZE = 128

HEAD_DIM = 64
MIX_WIDTH = D_MODEL
HALF_MIX = MIX_WIDTH // 2
N_HEADS_A = HALF_MIX // HEAD_DIM
W_LRU = HALF_MIX
N_LRU_BLOCKS = 8
LRU_BLOCK = W_LRU // N_LRU_BLOCKS
CONV_WIDTH = 4
LRU_C = 8.0
N_HEADS_C = HALF_MIX // HEAD_DIM
N_KV_C = 2
HPG_C = N_HEADS_C // N_KV_C
CMP_LEN = 32
CMP_STRIDE = 16
CMP_HID = 64
SLC_BLOCK = 64
N_SLC = 16
WINDOW = 512
N_HEADS_D = HALF_MIX // HEAD_DIM
MOBA_BLOCK = 256
MOBA_TOPK = 3
D_FF = 256 * ((8 * D_MODEL // 3 + 255) // 256)
ROPE_THETA = 10000.0
RMS_EPS = 1e-6
Q_BLOCK = 128
N_EVEN = (DEPTH + 1) // 2
N_ODD = DEPTH // 2
NEG = -1e30
BIG = 1e30
EVEN_SPLIT = [HALF_MIX] * 3 + [W_LRU] * 2
ODD_SPLIT = [HALF_MIX] + [N_KV_C * HEAD_DIM] * 6 + [3 * N_HEADS_C] + [HALF_MIX] * 3

kernel_name = 'hybrid_sbattn_rglru_nsa_moba_step'


def rms_norm(x, g):
    xf = x.astype(jnp.float32)
    y = xf * lax.rsqrt(jnp.mean(xf * xf, axis=-1, keepdims=True) + RMS_EPS)
    return (y * g.astype(jnp.float32)).astype(x.dtype)


def half_ffn(h, g, w_gate, w_up, w_down):
    hn = rms_norm(h, g)
    return h + 0.5 * ((jax.nn.silu(hn @ w_gate) * (hn @ w_up)) @ w_down)


def split_cols(y, sizes):
    return jnp.split(y, np.cumsum(sizes)[:-1].tolist(), axis=-1)


def rotary(x, pos):
    half = HEAD_DIM // 2
    inv_freq = ROPE_THETA ** (-jnp.arange(half, dtype=jnp.float32) / half)
    ang = pos.astype(jnp.float32)[:, None] * inv_freq[None, :]
    cos = jnp.cos(ang)[:, None, :]
    sin = jnp.sin(ang)[:, None, :]
    xf = x.astype(jnp.float32)
    x1, x2 = xf[..., :half], xf[..., half:]
    return jnp.concatenate([x1 * cos - x2 * sin, x1 * sin + x2 * cos], axis=-1).astype(x.dtype)


def masked_softmax(s, mask):
    p = jax.nn.softmax(jnp.where(mask, s, NEG), axis=-1)
    return jnp.where(mask, p, 0.0)


def stick_breaking_attend(q, q_pos, k, v):
    L = k.shape[0]
    z = jnp.einsum('qhd,khd->hqk', q, k, preferred_element_type=jnp.float32) * (HEAD_DIM ** -0.5)
    mask = jnp.arange(L)[None, None, :] < q_pos[None, :, None]
    log_keep = jnp.where(mask, jax.nn.log_sigmoid(-z), 0.0)
    later = lax.cumsum(log_keep, axis=2, reverse=True) - log_keep
    w = jnp.where(mask, jnp.exp(jax.nn.log_sigmoid(z) + later), 0.0)
    return jnp.einsum('hqk,khd->qhd', w.astype(v.dtype), v)


def rg_lru(x, h0, buf0, conv_w, conv_b, w_r, b_r, w_i, b_i, lam):
    n, t, w = x.shape
    xp = jnp.concatenate([buf0.astype(x.dtype), x], axis=1)
    xc = conv_b + xp[:, 0:t] * conv_w[0]
    for j in range(1, CONV_WIDTH):
        xc = xc + xp[:, j:j + t] * conv_w[j]
    xblk = xc.reshape(n, t, N_LRU_BLOCKS, LRU_BLOCK)
    r = jax.nn.sigmoid(jnp.einsum('ntbi,bij->ntbj', xblk, w_r).reshape(n, t, w) + b_r)
    i = jax.nn.sigmoid(jnp.einsum('ntbi,bij->ntbj', xblk, w_i).reshape(n, t, w) + b_i)
    log_a = -LRU_C * r.astype(jnp.float32) * jax.nn.softplus(-lam.astype(jnp.float32))
    u = jnp.sqrt(-jnp.expm1(2.0 * log_a)) * (i * xc).astype(jnp.float32)

    def step(h, au):
        a_t, u_t = au
        h = a_t * h + u_t
        return h, h

    h_last, hs = lax.scan(step, h0.astype(jnp.float32),
                          (jnp.exp(log_a).transpose(1, 0, 2), u.transpose(1, 0, 2)))
    return hs.transpose(1, 0, 2).astype(x.dtype), h_last.astype(x.dtype), xp[:, -(CONV_WIDTH - 1):]


def compress(rows, pe, w1, w2):
    n_chunk = rows.shape[0] // CMP_STRIDE
    r = CMP_LEN // CMP_STRIDE
    ch = rows[:n_chunk * CMP_STRIDE].reshape(n_chunk, CMP_STRIDE, N_KV_C, HEAD_DIM)
    n_cmp = n_chunk - r + 1
    blocks = jnp.concatenate([ch[j:j + n_cmp] for j in range(r)], axis=1) + pe[None, :, None, :]
    hid = jax.nn.gelu(jnp.einsum('cpgd,pdh->cgh', blocks, w1))
    return jnp.einsum('cgh,hd->cgd', hid, w2)


def nsa_attend(q_rot, q_raw, q_pos, gates, kc, vc, ks, vs, kw, vw, w_pos):
    nq = q_pos.shape[0]
    scale = HEAD_DIM ** -0.5
    n_cmp = kc.shape[0]
    c_end = jnp.arange(n_cmp) * CMP_STRIDE + CMP_LEN - 1
    qg_raw = q_raw.reshape(nq, N_KV_C, HPG_C, HEAD_DIM)
    s_c = jnp.einsum('qgjd,cgd->qgjc', qg_raw, kc, preferred_element_type=jnp.float32) * scale
    p_c = masked_softmax(s_c, (c_end[None, :] <= q_pos[:, None])[:, None, None, :])
    o_c = jnp.einsum('qgjc,cgd->qgjd', p_c.astype(vc.dtype), vc)
    L = ks.shape[0]
    nsb = -(-L // SLC_BLOCK)
    ci = jnp.arange(n_cmp)[:, None]
    bj = jnp.arange(nsb)[None, :]
    overlap = ((ci * CMP_STRIDE < (bj + 1) * SLC_BLOCK) & (ci * CMP_STRIDE + CMP_LEN > bj * SLC_BLOCK)).astype(jnp.float32)
    imp = jnp.einsum('qgjc,cb->qgb', p_c, overlap)
    own = (q_pos // SLC_BLOCK)[:, None, None]
    blk = jnp.arange(nsb)[None, None, :]
    imp = jnp.where(blk == own, BIG, jnp.where(blk < own, imp, NEG))
    top_val, top_idx = lax.top_k(imp, min(N_SLC, nsb))
    pad = nsb * SLC_BLOCK - L
    ksb = jnp.pad(ks, ((0, pad), (0, 0), (0, 0))).reshape(nsb, SLC_BLOCK, N_KV_C, HEAD_DIM).transpose(2, 0, 1, 3)
    vsb = jnp.pad(vs, ((0, pad), (0, 0), (0, 0))).reshape(nsb, SLC_BLOCK, N_KV_C, HEAD_DIM).transpose(2, 0, 1, 3)
    gg = jnp.arange(N_KV_C)[None, :, None]
    k_sel = ksb[gg, top_idx]
    v_sel = vsb[gg, top_idx]
    key_pos = top_idx[..., None] * SLC_BLOCK + jnp.arange(SLC_BLOCK)
    m_s = (top_val > 0.5 * NEG)[..., None] & (key_pos <= q_pos[:, None, None, None])
    qg = q_rot.reshape(nq, N_KV_C, HPG_C, HEAD_DIM)
    s_s = jnp.einsum('qgjd,qgnkd->qgjnk', qg, k_sel, preferred_element_type=jnp.float32) * scale
    p_s = masked_softmax(s_s.reshape(nq, N_KV_C, HPG_C, -1), m_s.reshape(nq, N_KV_C, 1, -1)).reshape(s_s.shape)
    o_s = jnp.einsum('qgjnk,qgnkd->qgjd', p_s.astype(vs.dtype), v_sel)
    s_w = jnp.einsum('qgjd,wgd->qgjw', qg, kw, preferred_element_type=jnp.float32) * scale
    dist = q_pos[:, None] - w_pos[None, :]
    m_w = ((dist >= 0) & (dist <= WINDOW) & (w_pos[None, :] >= 0))[:, None, None, :]
    o_w = jnp.einsum('qgjw,wgd->qgjd', masked_softmax(s_w, m_w).astype(vw.dtype), vw)
    g = gates.reshape(nq, N_KV_C, HPG_C, 3)
    o = o_c * g[..., 0:1] + o_s * g[..., 1:2] + o_w * g[..., 2:3]
    return o.reshape(nq, N_HEADS_C, HEAD_DIM)


def moba_attend(q, q_pos, k, v):
    nq = q_pos.shape[0]
    L = k.shape[0]
    nb = -(-L // MOBA_BLOCK)
    pad = nb * MOBA_BLOCK - L
    kp = jnp.pad(k, ((0, pad), (0, 0), (0, 0))).reshape(nb, MOBA_BLOCK, N_HEADS_D, HEAD_DIM)
    vp = jnp.pad(v, ((0, pad), (0, 0), (0, 0))).reshape(nb, MOBA_BLOCK, N_HEADS_D, HEAD_DIM)
    k_mean = jnp.mean(kp.astype(jnp.float32), axis=1)
    own = q_pos // MOBA_BLOCK
    gate = jnp.einsum('qhd,bhd->qhb', q.astype(jnp.float32), k_mean)
    gate = jnp.where(jnp.arange(nb)[None, None, :] < own[:, None, None], gate, NEG)
    top_val, top_idx = lax.top_k(gate, min(MOBA_TOPK, nb))
    idx = jnp.concatenate([top_idx, jnp.broadcast_to(own[:, None, None], top_idx.shape[:2] + (1,))], axis=-1)
    ok = jnp.concatenate([top_val > 0.5 * NEG, jnp.ones(top_idx.shape[:2] + (1,), dtype=bool)], axis=-1)
    hh = jnp.arange(N_HEADS_D)[None, :, None]
    k_sel = kp.transpose(2, 0, 1, 3)[hh, idx]
    v_sel = vp.transpose(2, 0, 1, 3)[hh, idx]
    key_pos = idx[..., None] * MOBA_BLOCK + jnp.arange(MOBA_BLOCK)
    mask = ok[..., None] & (key_pos <= q_pos[:, None, None, None])
    s = jnp.einsum('qhd,qhnkd->qhnk', q, k_sel, preferred_element_type=jnp.float32) * (HEAD_DIM ** -0.5)
    p = masked_softmax(s.reshape(nq, N_HEADS_D, -1), mask.reshape(nq, N_HEADS_D, -1)).reshape(s.shape)
    return jnp.einsum('qhnk,qhnkd->qhd', p.astype(v.dtype), v_sel)


def sb_prompt(q, k, v):
    s_len = q.shape[1]
    nqb = s_len // Q_BLOCK
    qpos = jnp.arange(s_len, dtype=jnp.int32).reshape(nqb, Q_BLOCK)

    def per_seq(a):
        qs, ks, vs = a
        ob = lax.map(lambda b: stick_breaking_attend(b[0], b[1], ks, vs),
                     (qs.reshape(nqb, Q_BLOCK, N_HEADS_A, HEAD_DIM), qpos))
        return ob.reshape(s_len, HALF_MIX)

    return lax.map(per_seq, (q, k, v))


def sb_sample(q, k, v, pool, li, page_table):
    t = q.shape[1]
    n_past = page_table.shape[1] * PAGE_SIZE
    qpos = n_past + jnp.arange(t, dtype=jnp.int32)

    def per_seq(a):
        pt, qs, ks, vs = a
        past = pool[li, pt].reshape(n_past, 2, N_HEADS_A, HEAD_DIM)
        k_all = jnp.concatenate([past[:, 0], ks], axis=0)
        v_all = jnp.concatenate([past[:, 1], vs], axis=0)
        return stick_breaking_attend(qs, qpos, k_all, v_all).reshape(t, HALF_MIX)

    return lax.map(per_seq, (page_table, q, k, v))


def even_mixer(xn, w_in, w_out, lru, h0, buf0, attend):
    n, t, _ = xn.shape
    q, k, v, xb, gb = split_cols(xn @ w_in, EVEN_SPLIT)
    q, k, v = (a.reshape(n, t, N_HEADS_A, HEAD_DIM) for a in (q, k, v))
    o_a = attend(q, k, v)
    hs, h_new, buf_new = rg_lru(xb, h0, buf0, *lru)
    out = jnp.concatenate([o_a, jax.nn.gelu(gb) * hs], axis=-1) @ w_out
    return out, jnp.stack([k, v], axis=2), h_new, buf_new


def odd_project(xn, w_in, pos):
    n, t, _ = xn.shape
    qc, kc, vc, ks, vs, kw, vw, gc, qd, kd, vd = split_cols(xn @ w_in, ODD_SPLIT)
    grp = lambda a: a.reshape(n, t, N_KV_C, HEAD_DIM)
    qc = qc.reshape(n, t, N_HEADS_C, HEAD_DIM)
    qd = qd.reshape(n, t, N_HEADS_D, HEAD_DIM)
    kd = kd.reshape(n, t, N_HEADS_D, HEAD_DIM)
    gates = jax.nn.sigmoid(gc.reshape(n, t, N_HEADS_C, 3))
    return (rotary(qc, pos), qc, gates, grp(kc), grp(vc), rotary(grp(ks), pos), grp(vs),
            rotary(grp(kw), pos), grp(vw), rotary(qd, pos), rotary(kd, pos),
            vd.reshape(n, t, N_HEADS_D, HEAD_DIM))


def odd_mixer(xn, pos, w_in, w_out, attend):
    qcr, qc, g, kc, vc, ks, vs, kw, vw, qd, kd, vd = odd_project(xn, w_in, pos)
    o_c, o_d = attend(qcr, qc, g, kc, vc, ks, vs, kw, vw, qd, kd, vd)
    out = jnp.concatenate([o_c, o_d], axis=-1) @ w_out
    return (out, jnp.stack([kc, vc, ks, vs], axis=2), jnp.stack([kw, vw], axis=2),
            jnp.stack([kd, vd], axis=2))


def nsa_moba_prompt(cmp, qcr, qc, g, kc, vc, ks, vs, kw, vw, qd, kd, vd):
    pe_k, w1_k, w2_k, pe_v, w1_v, w2_v = cmp
    s_len = qc.shape[1]
    nqb = s_len // Q_BLOCK

    def per_seq(a):
        qcr_, qc_, g_, kc_, vc_, ks_, vs_, kw_, vw_, qd_, kd_, vd_ = a
        kcc = compress(kc_, pe_k, w1_k, w2_k)
        vcc = compress(vc_, pe_v, w1_v, w2_v)
        kw_p = jnp.pad(kw_, ((WINDOW, 0), (0, 0), (0, 0)))
        vw_p = jnp.pad(vw_, ((WINDOW, 0), (0, 0), (0, 0)))

        def per_block(i):
            start = i * Q_BLOCK
            qpos = start + jnp.arange(Q_BLOCK, dtype=jnp.int32)
            wpos = start - WINDOW + jnp.arange(WINDOW + Q_BLOCK, dtype=jnp.int32)
            rows = lambda z: lax.dynamic_slice_in_dim(z, start, Q_BLOCK, 0)
            band = lambda z: lax.dynamic_slice_in_dim(z, start, WINDOW + Q_BLOCK, 0)
            o_c = nsa_attend(rows(qcr_), rows(qc_), qpos, rows(g_), kcc, vcc, ks_, vs_, band(kw_p), band(vw_p), wpos)
            o_d = moba_attend(rows(qd_), qpos, kd_, vd_)
            return o_c.reshape(Q_BLOCK, HALF_MIX), o_d.reshape(Q_BLOCK, HALF_MIX)

        oc, od = lax.map(per_block, jnp.arange(nqb, dtype=jnp.int32))
        return oc.reshape(s_len, HALF_MIX), od.reshape(s_len, HALF_MIX)

    return lax.map(per_seq, (qcr, qc, g, kc, vc, ks, vs, kw, vw, qd, kd, vd))


def nsa_moba_sample(cmp, pool_c, pool_d, li, win_buf, page_table, qcr, qc, g, kc, vc, ks, vs, kw, vw, qd, kd, vd):
    pe_k, w1_k, w2_k, pe_v, w1_v, w2_v = cmp
    t = qc.shape[1]
    n_past = page_table.shape[1] * PAGE_SIZE
    wb = win_buf.shape[1]
    qpos = n_past + jnp.arange(t, dtype=jnp.int32)
    wpos = n_past - wb + jnp.arange(wb + t, dtype=jnp.int32)
    cat = lambda past, new: jnp.concatenate([past, new], axis=0)

    def per_seq(a):
        pt, qcr_, qc_, g_, kc_, vc_, ks_, vs_, kw_, vw_, buf_, qd_, kd_, vd_ = a
        past_c = pool_c[li, pt].reshape(n_past, 4, N_KV_C, HEAD_DIM)
        past_d = pool_d[li, pt].reshape(n_past, 2, N_HEADS_D, HEAD_DIM)
        kcc = compress(cat(past_c[:, 0], kc_), pe_k, w1_k, w2_k)
        vcc = compress(cat(past_c[:, 1], vc_), pe_v, w1_v, w2_v)
        o_c = nsa_attend(qcr_, qc_, qpos, g_, kcc, vcc, cat(past_c[:, 2], ks_), cat(past_c[:, 3], vs_),
                         cat(buf_[:, 0], kw_), cat(buf_[:, 1], vw_), wpos)
        o_d = moba_attend(qd_, qpos, cat(past_d[:, 0], kd_), cat(past_d[:, 1], vd_))
        return o_c.reshape(t, HALF_MIX), o_d.reshape(t, HALF_MIX)

    return lax.map(per_seq, (page_table, qcr, qc, g, kc, vc, ks, vs, kw, vw, win_buf, qd, kd, vd))


def setup_inputs(seed: int = 0) -> dict:
    key = jax.random.key(seed)
    keys = iter(jax.random.split(key, 64))
    nrm = lambda shape, scale: scale * jax.random.normal(next(keys), shape, jnp.float32)
    gain = lambda shape: 1.0 + nrm(shape, 0.05)
    n_pages = PAST_LEN // PAGE_SIZE
    n_used = DEC_BATCH * n_pages
    n_pool = n_used + max(1, n_used // 4)
    wb = min(WINDOW, PAST_LEN)
    even_in = sum(EVEN_SPLIT)
    odd_in = sum(ODD_SPLIT)
    inp = {}
    inp['x_prompt'] = nrm((BATCH, SEQ, D_MODEL), 1.0)
    inp['x_sample'] = nrm((DEC_BATCH, DEC_SEQ, D_MODEL), 1.0)
    inp['cache_a_kv'] = nrm((N_EVEN, n_pool, PAGE_SIZE, 2, N_HEADS_A, HEAD_DIM), 1.0)
    inp['state_b_h'] = nrm((N_EVEN, DEC_BATCH, W_LRU), 0.5)
    inp['state_b_conv'] = nrm((N_EVEN, DEC_BATCH, CONV_WIDTH - 1, W_LRU), 1.0)
    inp['cache_c_kv'] = nrm((N_ODD, n_pool, PAGE_SIZE, 4, N_KV_C, HEAD_DIM), 1.0)
    inp['state_c_win'] = nrm((N_ODD, DEC_BATCH, wb, 2, N_KV_C, HEAD_DIM), 1.0)
    inp['cache_d_kv'] = nrm((N_ODD, n_pool, PAGE_SIZE, 2, N_HEADS_D, HEAD_DIM), 1.0)
    inp['page_table'] = jax.random.permutation(next(keys), n_pool)[:n_used].reshape(DEC_BATCH, n_pages).astype(jnp.int32)
    inp['norm_ffn1'] = gain((DEPTH, D_MODEL))
    inp['w_ffn1_gate'] = nrm((DEPTH, D_MODEL, D_FF), D_MODEL ** -0.5)
    inp['w_ffn1_up'] = nrm((DEPTH, D_MODEL, D_FF), D_MODEL ** -0.5)
    inp['w_ffn1_down'] = nrm((DEPTH, D_FF, D_MODEL), D_FF ** -0.5)
    inp['norm_mix'] = gain((DEPTH, D_MODEL))
    inp['norm_ffn2'] = gain((DEPTH, D_MODEL))
    inp['w_ffn2_gate'] = nrm((DEPTH, D_MODEL, D_FF), D_MODEL ** -0.5)
    inp['w_ffn2_up'] = nrm((DEPTH, D_MODEL, D_FF), D_MODEL ** -0.5)
    inp['w_ffn2_down'] = nrm((DEPTH, D_FF, D_MODEL), D_FF ** -0.5)
    inp['w_in_even'] = nrm((N_EVEN, D_MODEL, even_in), D_MODEL ** -0.5)
    inp['w_out_even'] = nrm((N_EVEN, MIX_WIDTH, D_MODEL), MIX_WIDTH ** -0.5)
    inp['lru_conv_w'] = nrm((N_EVEN, CONV_WIDTH, W_LRU), CONV_WIDTH ** -0.5)
    inp['lru_conv_b'] = nrm((N_EVEN, W_LRU), 0.01)
    inp['lru_w_r'] = nrm((N_EVEN, N_LRU_BLOCKS, LRU_BLOCK, LRU_BLOCK), LRU_BLOCK ** -0.5)
    inp['lru_b_r'] = nrm((N_EVEN, W_LRU), 0.01)
    inp['lru_w_i'] = nrm((N_EVEN, N_LRU_BLOCKS, LRU_BLOCK, LRU_BLOCK), LRU_BLOCK ** -0.5)
    inp['lru_b_i'] = nrm((N_EVEN, W_LRU), 0.01)
    u = jax.random.uniform(next(keys), (N_EVEN, W_LRU), jnp.float32, 0.9, 0.999)
    a = u ** (1.0 / LRU_C)
    inp['lru_lambda'] = jnp.log(a) - jnp.log1p(-a)
    inp['w_in_odd'] = nrm((N_ODD, D_MODEL, odd_in), D_MODEL ** -0.5)
    inp['w_out_odd'] = nrm((N_ODD, MIX_WIDTH, D_MODEL), MIX_WIDTH ** -0.5)
    inp['cmp_pe_k'] = nrm((N_ODD, CMP_LEN, HEAD_DIM), 0.1)
    inp['cmp_w1_k'] = nrm((N_ODD, CMP_LEN, HEAD_DIM, CMP_HID), (CMP_LEN * HEAD_DIM) ** -0.5)
    inp['cmp_w2_k'] = nrm((N_ODD, CMP_HID, HEAD_DIM), CMP_HID ** -0.5)
    inp['cmp_pe_v'] = nrm((N_ODD, CMP_LEN, HEAD_DIM), 0.1)
    inp['cmp_w1_v'] = nrm((N_ODD, CMP_LEN, HEAD_DIM, CMP_HID), (CMP_LEN * HEAD_DIM) ** -0.5)
    inp['cmp_w2_v'] = nrm((N_ODD, CMP_HID, HEAD_DIM), CMP_HID ** -0.5)
    inp['norm_final'] = gain((D_MODEL,))
    return inp


def reference(x_prompt, x_sample, cache_a_kv, state_b_h, state_b_conv, cache_c_kv, state_c_win, cache_d_kv,
              page_table, norm_ffn1, w_ffn1_gate, w_ffn1_up, w_ffn1_down, norm_mix, norm_ffn2, w_ffn2_gate,
              w_ffn2_up, w_ffn2_down, w_in_even, w_out_even, lru_conv_w, lru_conv_b, lru_w_r, lru_b_r, lru_w_i,
              lru_b_i, lru_lambda, w_in_odd, w_out_odd, cmp_pe_k, cmp_w1_k, cmp_w2_k, cmp_pe_v, cmp_w1_v,
              cmp_w2_v, norm_final):
    n_past = page_table.shape[1] * PAGE_SIZE
    pos_p = jnp.arange(x_prompt.shape[1], dtype=jnp.int32)
    pos_s = n_past + jnp.arange(x_sample.shape[1], dtype=jnp.int32)
    yp, ys = x_prompt, x_sample
    a_kv_p, a_kv_s, bh_p, bh_s, bc_p, bc_s = [], [], [], [], [], []
    c_kv_p, c_kv_s, cw_p, cw_s, d_kv_p, d_kv_s = [], [], [], [], [], []
    for layer in range(DEPTH):
        li = layer // 2
        ffn1 = (norm_ffn1[layer], w_ffn1_gate[layer], w_ffn1_up[layer], w_ffn1_down[layer])
        ffn2 = (norm_ffn2[layer], w_ffn2_gate[layer], w_ffn2_up[layer], w_ffn2_down[layer])
        yp = half_ffn(yp, *ffn1)
        ys = half_ffn(ys, *ffn1)
        xpn = rms_norm(yp, norm_mix[layer])
        xsn = rms_norm(ys, norm_mix[layer])
        if layer % 2 == 0:
            lru = (lru_conv_w[li], lru_conv_b[li], lru_w_r[li], lru_b_r[li], lru_w_i[li], lru_b_i[li], lru_lambda[li])
            n_p = yp.shape[0]
            o, kv, h, buf = even_mixer(xpn, w_in_even[li], w_out_even[li], lru,
                                       jnp.zeros((n_p, W_LRU), yp.dtype),
                                       jnp.zeros((n_p, CONV_WIDTH - 1, W_LRU), yp.dtype), sb_prompt)
            yp = yp + o
            a_kv_p.append(kv)
            bh_p.append(h)
            bc_p.append(buf)
            o, kv, h, buf = even_mixer(xsn, w_in_even[li], w_out_even[li], lru, state_b_h[li], state_b_conv[li],
                                       lambda q, k, v: sb_sample(q, k, v, cache_a_kv, li, page_table))
            ys = ys + o
            a_kv_s.append(kv)
            bh_s.append(h)
            bc_s.append(buf)
        else:
            cmp = (cmp_pe_k[li], cmp_w1_k[li], cmp_w2_k[li], cmp_pe_v[li], cmp_w1_v[li], cmp_w2_v[li])
            o, c_rows, w_rows, d_rows = odd_mixer(xpn, pos_p, w_in_odd[li], w_out_odd[li],
                                                  lambda *a: nsa_moba_prompt(cmp, *a))
            yp = yp + o
            c_kv_p.append(c_rows)
            cw_p.append(w_rows[:, -min(WINDOW, yp.shape[1]):])
            d_kv_p.append(d_rows)
            win_l = state_c_win[li]
            o, c_rows, w_rows, d_rows = odd_mixer(
                xsn, pos_s, w_in_odd[li], w_out_odd[li],
                lambda *a: nsa_moba_sample(cmp, cache_c_kv, cache_d_kv, li, win_l, page_table, *a))
            ys = ys + o
            c_kv_s.append(c_rows)
            cw_s.append(jnp.concatenate([win_l, w_rows], axis=1)[:, -win_l.shape[1]:])
            d_kv_s.append(d_rows)
        yp = half_ffn(yp, *ffn2)
        ys = half_ffn(ys, *ffn2)
    y_prompt = rms_norm(yp, norm_final)
    y_sample = rms_norm(ys, norm_final)
    return (y_prompt, y_sample, jnp.stack(a_kv_p), jnp.stack(a_kv_s), jnp.stack(bh_p), jnp.stack(bh_s),
            jnp.stack(bc_p), jnp.stack(bc_s), jnp.stack(c_kv_p), jnp.stack(c_kv_s), jnp.stack(cw_p),
            jnp.stack(cw_s), jnp.stack(d_kv_p), jnp.stack(d_kv_s))
```

```python
import functools
import math

import jax
import jax.numpy as jnp
import numpy as np
from jax import lax
from jax.experimental import pallas as pl
from jax.experimental.pallas import tpu as pltpu

F32 = jnp.float32
BF16 = jnp.bfloat16

HEAD_DIM = 64
HALF_MIX = 512
N_HEADS = HALF_MIX // HEAD_DIM
N_KV_C = 2
HPG_C = N_HEADS // N_KV_C
PAGE_SIZE = 128
CONV_WIDTH = 4
LRU_C = 8.0
CMP_LEN = 32
CMP_STRIDE = 16
SLC_BLOCK = 64
N_SLC = 16
WINDOW = 512
MOBA_BLOCK = 256
MOBA_TOPK = 3
ROPE_THETA = 10000.0
RMS_EPS = 1e-6
NEG = -1e30
BIG = 1e30
SCALE = HEAD_DIM ** -0.5

LANES = 128
VMEM_LIMIT_BYTES = 56 * 1024 * 1024
SB_CUTOFF = 120.0
HIGHEST = lax.Precision.HIGHEST


def _cparams(*sem):
    return pltpu.CompilerParams(dimension_semantics=sem, vmem_limit_bytes=VMEM_LIMIT_BYTES)


def _tile(n, prefs):
    for t in prefs:
        if n % t == 0:
            return t
    return n


def _dot(a, b, precision=None):
    return jnp.dot(a, b, preferred_element_type=F32, precision=precision)


def _dot_nt(a, b, precision=None):
    return lax.dot_general(a, b, (((1,), (1,)), ((), ())), preferred_element_type=F32, precision=precision)


def _rms(x, g):
    ms = jnp.mean(x * x, axis=-1, keepdims=True)
    return x * lax.rsqrt(ms + RMS_EPS) * g


def _softplus(z):
    return jnp.maximum(z, 0.0) + jnp.log1p(jnp.exp(-jnp.abs(z)))


def _one_minus_exp(x):
    poly = x
    for k in range(7, 1, -1):
        poly = x * (1.0 + poly * (1.0 / k))
    return jnp.where(x > -0.125, -poly, 1.0 - jnp.exp(x))


def _split_dot(x, t):
    hi = x.astype(BF16)
    lo = (x - hi.astype(F32)).astype(BF16)
    return _dot(hi, t) + _dot(lo, t)


def _masked_softmax(s, mask):
    sm = jnp.where(mask, s, NEG)
    m = jnp.max(sm, axis=-1, keepdims=True)
    e = jnp.where(mask, jnp.exp(sm - m), 0.0)
    d = jnp.sum(e, axis=-1, keepdims=True)
    return e / jnp.maximum(d, 1e-30)


def _online_step(s, mask, v, m, l, acc):
    sm = jnp.where(mask, s, NEG)
    m_new = jnp.maximum(m, jnp.max(sm, axis=-1, keepdims=True))
    alpha = jnp.exp(m - m_new)
    p = jnp.where(mask, jnp.exp(sm - m_new), 0.0)
    l = alpha * l + jnp.sum(p, axis=-1, keepdims=True)
    rows = acc.shape[0]
    acc = alpha.reshape(rows, 1) * acc + _dot(p.reshape(rows, p.shape[-1]).astype(BF16), v)
    return m_new, l, acc


def _top_select(val, idx, n):
    sel = jnp.zeros(val.shape, F32)
    big_i = jnp.int32(2 ** 30)
    for _ in range(n):
        m = jnp.max(val, axis=-1, keepdims=True)
        first = jnp.min(jnp.where(val == m, idx, big_i), axis=-1, keepdims=True)
        hit = idx == first
        sel = jnp.where(hit & (m > 0.5 * NEG), 1.0, sel)
        val = jnp.where(hit, -jnp.inf, val)
    return sel


def _ffn_kernel(*refs, n_f, post_norm):
    if post_norm:
        x_ref, g_ref, wg_ref, wu_ref, wd_ref, gf_ref, o_ref, xn_ref, acc_ref = refs
    else:
        x_ref, g_ref, wg_ref, wu_ref, wd_ref, o_ref, xn_ref, acc_ref = refs
    j = pl.program_id(1)

    @pl.when(j == 0)
    def _():
        xn_ref[...] = _rms(x_ref[...], g_ref[...]).astype(BF16)
        acc_ref[...] = jnp.zeros_like(acc_ref)

    xn = xn_ref[...]
    hg = _dot(xn, wg_ref[...])
    hu = _dot(xn, wu_ref[...])
    a = (hg * jax.nn.sigmoid(hg) * hu).astype(BF16)
    acc_ref[...] += _dot(a, wd_ref[...])

    @pl.when(j == n_f - 1)
    def _():
        y = x_ref[...] + 0.5 * acc_ref[...]
        if post_norm:
            y = _rms(y, gf_ref[...])
        o_ref[...] = y


def _half_ffn(x, g, wg, wu, wd, g_final=None):
    m, d = x.shape
    f = wg.shape[1]
    tm = _tile(m, (1024, 512, 256, 128))
    tf = _tile(f, (256, 128))
    n_f = f // tf
    post = g_final is not None
    in_specs = [pl.BlockSpec((tm, d), lambda i, j: (i, 0)),
                pl.BlockSpec((1, d), lambda i, j: (0, 0)),
                pl.BlockSpec((d, tf), lambda i, j: (0, j)),
                pl.BlockSpec((d, tf), lambda i, j: (0, j)),
                pl.BlockSpec((tf, d), lambda i, j: (j, 0))]
    args = [x, g.reshape(1, d), wg, wu, wd]
    if post:
        in_specs.append(pl.BlockSpec((1, d), lambda i, j: (0, 0)))
        args.append(g_final.reshape(1, d))
    return pl.pallas_call(
        functools.partial(_ffn_kernel, n_f=n_f, post_norm=post),
        grid=(m // tm, n_f),
        in_specs=in_specs,
        out_specs=pl.BlockSpec((tm, d), lambda i, j: (i, 0)),
        out_shape=jax.ShapeDtypeStruct((m, d), F32),
        scratch_shapes=[pltpu.VMEM((tm, d), BF16), pltpu.VMEM((tm, d), F32)],
        compiler_params=_cparams("parallel", "arbitrary"),
        name="half_ffn",
    )(*args)


def _rope(seg, cos_t, sin_t):
    w = seg.shape[1]
    reps = w // LANES
    c = jnp.concatenate([cos_t] * reps, axis=1) if reps > 1 else cos_t
    s = jnp.concatenate([sin_t] * reps, axis=1) if reps > 1 else sin_t
    lane = lax.broadcasted_iota(jnp.int32, seg.shape, 1)
    first = (lane % HEAD_DIM) < (HEAD_DIM // 2)
    rot = jnp.where(first, pltpu.roll(seg, w - HEAD_DIM // 2, 1), pltpu.roll(seg, HEAD_DIM // 2, 1))
    return seg * c + rot * s


def _store_heads(ref, seg, scale=None):
    for h in range(seg.shape[1] // HEAD_DIM):
        piece = seg[:, h * HEAD_DIM:(h + 1) * HEAD_DIM]
        if scale is not None:
            piece = piece * scale
        ref[h] = piece.astype(ref.dtype)


def _proj_even_kernel(x_ref, g_ref, w_ref, kv_ref, xb_ref, gb_ref, *outs, heads):
    hm = HALF_MIX
    xn = _rms(x_ref[...], g_ref[...]).astype(BF16)
    y = _dot(xn, w_ref[...])
    kv_ref[...] = y[:, hm:3 * hm]
    xb_ref[...] = y[:, 3 * hm:4 * hm]
    gb_ref[...] = y[:, 4 * hm:5 * hm]
    if heads:
        qh_ref, kh_ref, vh_ref = outs
        _store_heads(qh_ref, y[:, 0:hm], SCALE)
        _store_heads(kh_ref, y[:, hm:2 * hm])
        _store_heads(vh_ref, y[:, 2 * hm:3 * hm])
    else:
        (q_ref,) = outs
        q_ref[...] = y[:, 0:hm]


def _proj_even(x, g, w, heads):
    m, d = x.shape
    n = w.shape[1]
    hm = HALF_MIX
    tm = _tile(m, (512, 256, 128))
    row = lambda i: (i, 0)
    out_shape = [jax.ShapeDtypeStruct((m, 2 * hm), F32), jax.ShapeDtypeStruct((m, hm), F32),
                 jax.ShapeDtypeStruct((m, hm), F32)]
    out_specs = [pl.BlockSpec((tm, 2 * hm), row), pl.BlockSpec((tm, hm), row), pl.BlockSpec((tm, hm), row)]
    if heads:
        for _ in range(3):
            out_shape.append(jax.ShapeDtypeStruct((N_HEADS, m, HEAD_DIM), BF16))
            out_specs.append(pl.BlockSpec((N_HEADS, tm, HEAD_DIM), lambda i: (0, i, 0)))
    else:
        out_shape.append(jax.ShapeDtypeStruct((m, hm), F32))
        out_specs.append(pl.BlockSpec((tm, hm), row))
    return pl.pallas_call(
        functools.partial(_proj_even_kernel, heads=heads),
        grid=(m // tm,),
        in_specs=[pl.BlockSpec((tm, d), row), pl.BlockSpec((1, d), lambda i: (0, 0)),
                  pl.BlockSpec((d, n), lambda i: (0, 0))],
        out_specs=out_specs, out_shape=out_shape,
        compiler_params=_cparams("parallel"),
        name="proj_even",
    )(x, g.reshape(1, d), w)


_O_QC, _O_KC, _O_VC, _O_KS, _O_VS, _O_KW, _O_VW, _O_QD, _O_KD, _O_VD, _O_GT, _O_END = (
    0, 512, 640, 768, 896, 1024, 1152, 1280, 1792, 2304, 2816, 3072)


def _proj_odd_kernel(x_ref, g_ref, w_ref, cos_ref, sin_ref, c_ref, w_out_ref, d_ref, gt_ref, *outs, heads):
    xn = _rms(x_ref[...], g_ref[...]).astype(BF16)
    y = _dot(xn, w_ref[...])
    cos_t = cos_ref[...]
    sin_t = sin_ref[...]
    qc = y[:, _O_QC:_O_KC]
    qc_rot = _rope(qc, cos_t, sin_t)
    ks_rot = _rope(y[:, _O_KS:_O_VS], cos_t, sin_t)
    kw_rot = _rope(y[:, _O_KW:_O_VW], cos_t, sin_t)
    qd_rot = _rope(y[:, _O_QD:_O_KD], cos_t, sin_t)
    kd_rot = _rope(y[:, _O_KD:_O_VD], cos_t, sin_t)
    vs = y[:, _O_VS:_O_KW]
    vw = y[:, _O_VW:_O_QD]
    vd = y[:, _O_VD:_O_GT]
    c_ref[:, 0:256] = y[:, _O_KC:_O_KS]
    c_ref[:, 256:384] = ks_rot
    c_ref[:, 384:512] = vs
    w_out_ref[:, 0:128] = kw_rot
    w_out_ref[:, 128:256] = vw
    d_ref[:, 0:512] = kd_rot
    d_ref[:, 512:1024] = vd
    gt_ref[...] = jax.nn.sigmoid(y[:, _O_GT:_O_END])
    if heads:
        qraw_h, qrot_h, ks_h, vs_h, kw_h, vw_h, qd_h, kd_h, vd_h = outs
        _store_heads(qraw_h, qc, SCALE)
        _store_heads(qrot_h, qc_rot, SCALE)
        _store_heads(ks_h, ks_rot)
        _store_heads(vs_h, vs)
        _store_heads(kw_h, kw_rot)
        _store_heads(vw_h, vw)
        _store_heads(qd_h, qd_rot, SCALE)
        _store_heads(kd_h, kd_rot)
        _store_heads(vd_h, vd)
    else:
        qraw_ref, qrot_ref, qd_ref = outs
        qraw_ref[...] = qc
        qrot_ref[...] = qc_rot
        qd_ref[...] = qd_rot


def _proj_odd(x, g, w, cos_t, sin_t, heads):
    m, d = x.shape
    n = w.shape[1]
    tm = _tile(m, (512, 256, 128))
    period = cos_t.shape[0] // tm
    row = lambda i: (i, 0)
    tab = lambda i: (i % period, 0)
    out_shape = [jax.ShapeDtypeStruct((m, 512), F32), jax.ShapeDtypeStruct((m, 256), F32),
                 jax.ShapeDtypeStruct((m, 1024), F32), jax.ShapeDtypeStruct((m, 256), F32)]
    out_specs = [pl.BlockSpec((tm, 512), row), pl.BlockSpec((tm, 256), row), pl.BlockSpec((tm, 1024), row),
                 pl.BlockSpec((tm, 256), row)]
    if heads:
        for nh in (N_HEADS, N_HEADS, N_KV_C, N_KV_C, N_KV_C, N_KV_C, N_HEADS, N_HEADS, N_HEADS):
            out_shape.append(jax.ShapeDtypeStruct((nh, m, HEAD_DIM), BF16))
            out_specs.append(pl.BlockSpec((nh, tm, HEAD_DIM), lambda i: (0, i, 0)))
    else:
        for _ in range(3):
            out_shape.append(jax.ShapeDtypeStruct((m, HALF_MIX), F32))
            out_specs.append(pl.BlockSpec((tm, HALF_MIX), row))
    return pl.pallas_call(
        functools.partial(_proj_odd_kernel, heads=heads),
        grid=(m // tm,),
        in_specs=[pl.BlockSpec((tm, d), row), pl.BlockSpec((1, d), lambda i: (0, 0)),
                  pl.BlockSpec((d, n), lambda i: (0, 0)),
                  pl.BlockSpec((tm, LANES), tab), pl.BlockSpec((tm, LANES), tab)],
        out_specs=out_specs, out_shape=out_shape,
        compiler_params=_cparams("parallel"),
        name="proj_odd",
    )(x, g.reshape(1, d), w, cos_t, sin_t)


def _outproj_kernel(x_ref, a_ref, b_ref, w_ref, o_ref, *, heads):
    hm = HALF_MIX
    if heads:
        a = jnp.concatenate([a_ref[h] for h in range(N_HEADS)], axis=1)
        b = b_ref[...] if len(b_ref.shape) == 2 else jnp.concatenate([b_ref[h] for h in range(N_HEADS)], axis=1)
    else:
        a = a_ref[...]
        b = b_ref[...]
    o_ref[...] = x_ref[...] + _dot(a, w_ref[0:hm, :]) + _dot(b, w_ref[hm:2 * hm, :])


def _outproj(x, a, b, w):
    m, d = x.shape
    tm = _tile(m, (512, 256, 128))
    row = lambda i: (i, 0)

    def spec(t):
        if t.ndim == 2:
            return pl.BlockSpec((tm, t.shape[1]), row)
        return pl.BlockSpec((t.shape[0], tm, t.shape[2]), lambda i: (0, i, 0))

    return pl.pallas_call(
        functools.partial(_outproj_kernel, heads=a.ndim == 3),
        grid=(m // tm,),
        in_specs=[pl.BlockSpec((tm, d), row), spec(a), spec(b), pl.BlockSpec(w.shape, lambda i: (0, 0))],
        out_specs=pl.BlockSpec((tm, d), row),
        out_shape=jax.ShapeDtypeStruct((m, d), F32),
        compiler_params=_cparams("parallel"),
        name="outproj",
    )(x, a, b, w)


def _sb_tile(q, k, v, t_incl, carry, acc, mask):
    z = _dot_nt(q, k)
    sp = _softplus(z)
    lk = -sp
    if mask is not None:
        lk = jnp.where(mask, lk, 0.0)
    incl = _split_dot(lk, t_incl)
    w = jnp.exp(z - sp + (incl - lk + carry))
    if mask is not None:
        w = jnp.where(mask, w, 0.0)
    acc = acc + _dot(w.astype(BF16), v)
    return carry + incl[:, 0:1], acc


def _tri_incl(n):
    r = lax.broadcasted_iota(jnp.int32, (n, n), 0)
    c = lax.broadcasted_iota(jnp.int32, (n, n), 1)
    return (r >= c).astype(BF16)


def _sb_prompt_kernel(q_ref, k_ref, v_ref, o_ref, *, tq):
    i = pl.program_id(2)
    q = q_ref[0]
    t_incl = _tri_incl(tq)
    r = lax.broadcasted_iota(jnp.int32, (tq, tq), 0)
    c = lax.broadcasted_iota(jnp.int32, (tq, tq), 1)

    def kv(kt):
        start = pl.multiple_of(kt * tq, tq)
        return k_ref[0, pl.ds(start, tq), :], v_ref[0, pl.ds(start, tq), :]

    k0, v0 = kv(i)
    carry, acc = _sb_tile(q, k0, v0, t_incl, jnp.zeros((tq, 1), F32), jnp.zeros((tq, HEAD_DIM), F32), c < r)

    def cond(s):
        kt, carry, _ = s
        return jnp.logical_and(kt >= 0, jnp.max(carry) > -SB_CUTOFF)

    def body(s):
        kt, carry, acc = s
        kk, vv = kv(kt)
        carry, acc = _sb_tile(q, kk, vv, t_incl, carry, acc, None)
        return kt - 1, carry, acc

    _, _, acc = lax.while_loop(cond, body, (i - 1, carry, acc))
    o_ref[0] = acc.astype(o_ref.dtype)


def _sb_prompt(qh, kh, vh, n_seq, s_len):
    tq = _tile(s_len, (256, 128))
    nq = s_len // tq
    return pl.pallas_call(
        functools.partial(_sb_prompt_kernel, tq=tq),
        grid=(n_seq, N_HEADS, nq),
        in_specs=[pl.BlockSpec((1, tq, HEAD_DIM), lambda b, h, i: (h, b * nq + i, 0)),
                  pl.BlockSpec((1, s_len, HEAD_DIM), lambda b, h, i: (h, b, 0)),
                  pl.BlockSpec((1, s_len, HEAD_DIM), lambda b, h, i: (h, b, 0))],
        out_specs=pl.BlockSpec((1, tq, HEAD_DIM), lambda b, h, i: (h, b * nq + i, 0)),
        out_shape=jax.ShapeDtypeStruct(qh.shape, BF16),
        compiler_params=_cparams("parallel", "parallel", "arbitrary"),
        name="sb_prompt",
    )(qh, kh, vh)


def _lru_kernel(xb_ref, gb_ref, h0_ref, buf0_ref, cw_ref, cb_ref, wr_ref, br_ref, wi_ref, bi_ref, lam_ref,
                g_ref, hl_ref, bn_ref, xpad, a_s, u_s, hs_s, h_s, *, tc, n_t):
    j = pl.program_id(1)
    keep = CONV_WIDTH - 1

    @pl.when(j == 0)
    def _():
        xpad[8 - keep:8, :] = buf0_ref[0]
        h_s[...] = h0_ref[0]

    @pl.when(j > 0)
    def _():
        xpad[8 - keep:8, :] = xpad[8 + tc - keep:8 + tc, :]

    x = xb_ref[0]
    xpad[8:8 + tc, :] = x
    xc = cb_ref[...] + x * cw_ref[keep:keep + 1, :]
    for d in range(1, CONV_WIDTH):
        xc = xc + xpad[8 - d:8 - d + tc, :] * cw_ref[keep - d:keep - d + 1, :]
    xcb = xc.astype(BF16)
    r = jax.nn.sigmoid(_dot(xcb, wr_ref[...]) + br_ref[...])
    gi = jax.nn.sigmoid(_dot(xcb, wi_ref[...]) + bi_ref[...])
    log_a = -LRU_C * r * _softplus(-lam_ref[...])
    a_s[...] = jnp.exp(log_a)
    u_s[...] = jnp.sqrt(_one_minus_exp(2.0 * log_a)) * (gi * xc)

    def step(t, h):
        h = a_s[pl.ds(t, 1), :] * h + u_s[pl.ds(t, 1), :]
        hs_s[pl.ds(t, 1), :] = h
        return h

    h = lax.fori_loop(0, tc, step, h_s[...], unroll=min(tc, 8))
    h_s[...] = h
    g_ref[0] = (jax.nn.gelu(gb_ref[0]) * hs_s[...]).astype(g_ref.dtype)

    @pl.when(j == n_t - 1)
    def _():
        hl_ref[0] = h
        bn_ref[0] = xpad[8 + tc - keep:8 + tc, :]


def _rg_lru(xb, gb, h0, buf0, conv_w, conv_b, wr_bd, b_r, wi_bd, b_i, lam):
    n, t, w = xb.shape
    keep = CONV_WIDTH - 1
    assert t >= keep
    tc = _tile(t, (512, 256, 128))
    n_t = t // tc
    seq = lambda b, j: (b, j, 0)
    one = lambda b, j: (b, 0, 0)
    const = lambda b, j: (0, 0)
    vec = pl.BlockSpec((1, w), const)
    g, hl, bn = pl.pallas_call(
        functools.partial(_lru_kernel, tc=tc, n_t=n_t),
        grid=(n, n_t),
        in_specs=[pl.BlockSpec((1, tc, w), seq), pl.BlockSpec((1, tc, w), seq),
                  pl.BlockSpec((1, 1, w), one), pl.BlockSpec((1, keep, w), one),
                  pl.BlockSpec((CONV_WIDTH, w), const), vec,
                  pl.BlockSpec((w, w), const), vec, pl.BlockSpec((w, w), const), vec, vec],
        out_specs=[pl.BlockSpec((1, tc, w), seq), pl.BlockSpec((1, 1, w), one), pl.BlockSpec((1, keep, w), one)],
        out_shape=[jax.ShapeDtypeStruct((n, t, w), BF16), jax.ShapeDtypeStruct((n, 1, w), F32),
                   jax.ShapeDtypeStruct((n, keep, w), F32)],
        scratch_shapes=[pltpu.VMEM((tc + 8, w), F32), pltpu.VMEM((tc, w), F32), pltpu.VMEM((tc, w), F32),
                        pltpu.VMEM((tc, w), F32), pltpu.VMEM((1, w), F32)],
        compiler_params=_cparams("parallel", "arbitrary"),
        name="rg_lru",
    )(xb, gb, h0.reshape(n, 1, w), buf0, conv_w, conv_b.reshape(1, w), wr_bd, b_r.reshape(1, w), wi_bd,
      b_i.reshape(1, w), lam.reshape(1, w))
    return g, hl.reshape(n, w), bn


def _cmp_ab_kernel(*refs, n_in, rows, n_prefetch):
    refs = refs[n_prefetch:]
    k_refs = refs[:n_in]
    v_refs = refs[n_in:2 * n_in]
    pe_ak, pe_bk, pe_av, pe_bv, w_ak, w_bk, w_av, w_bv, o_ref = refs[2 * n_in:]
    nch = rows // CMP_STRIDE

    def chunks(in_refs):
        per = [jnp.concatenate([r[pl.ds(p, nch, stride=CMP_STRIDE), :] for p in range(CMP_STRIDE)], axis=1)
               for r in in_refs]
        return per[0] if n_in == 1 else jnp.concatenate(per, axis=0)

    xk = chunks(k_refs)
    xv = chunks(v_refs)
    o_ref[0, :, 0:128] = _dot((xk + pe_ak[...]).astype(BF16), w_ak[...])
    o_ref[0, :, 128:256] = _dot((xk + pe_bk[...]).astype(BF16), w_bk[...])
    o_ref[0, :, 256:384] = _dot((xv + pe_av[...]).astype(BF16), w_av[...])
    o_ref[0, :, 384:512] = _dot((xv + pe_bv[...]).astype(BF16), w_bv[...])


def _cmp_weights(pe, w1):
    hid = w1.shape[-1]
    eye = jnp.eye(N_KV_C, dtype=F32)
    out = []
    for half in range(CMP_LEN // CMP_STRIDE):
        sl = slice(half * CMP_STRIDE, (half + 1) * CMP_STRIDE)
        pe_row = jnp.broadcast_to(pe[sl][:, None, :], (CMP_STRIDE, N_KV_C, HEAD_DIM)).reshape(1, -1)
        wbd = jnp.einsum('pdh,gk->pgdkh', w1[sl], eye).reshape(CMP_STRIDE * N_KV_C * HEAD_DIM, N_KV_C * hid)
        out.append((pe_row, wbd.astype(BF16)))
    return out


def _cmp_ab(in_arrays, k_specs, v_specs, grid, out_rows, n_seq, n_chunks, rows, cmp_k, cmp_v, num_prefetch,
            prefetch):
    in_specs = list(k_specs) + list(v_specs)
    (pe_ak, w_ak), (pe_bk, w_bk) = cmp_k
    (pe_av, w_av), (pe_bv, w_bv) = cmp_v
    nd = len(grid)
    const = lambda *a: (0, 0)
    pes = [pe_ak, pe_bk, pe_av, pe_bv]
    ws = [w_ak, w_bk, w_av, w_bv]
    specs = list(in_specs) + [pl.BlockSpec(p.shape, const) for p in pes] + [pl.BlockSpec(w.shape, const) for w in ws]
    gs = pltpu.PrefetchScalarGridSpec(
        num_scalar_prefetch=num_prefetch, grid=grid, in_specs=specs,
        out_specs=pl.BlockSpec((1, out_rows, 512), lambda b, j, *a: (b, j, 0)))
    return pl.pallas_call(
        functools.partial(_cmp_ab_kernel, n_in=len(in_arrays), rows=rows, n_prefetch=num_prefetch),
        grid_spec=gs,
        out_shape=jax.ShapeDtypeStruct((n_seq, n_chunks, 512), F32),
        compiler_params=_cparams(*(["parallel"] + ["arbitrary"] * (nd - 1))),
        name="cmp_ab",
    )(*prefetch, *in_arrays, *in_arrays, *pes, *ws)


def _cmp_finish_kernel(ab_ref, w2k_ref, w2v_ref, kf_ref, vf_ref, kg_ref, vg_ref, *, nc):
    n = nc - 1
    hk = jax.nn.gelu(ab_ref[0, 0:n, 0:128] + ab_ref[0, 1:nc, 128:256]).astype(BF16)
    hv = jax.nn.gelu(ab_ref[0, 0:n, 256:384] + ab_ref[0, 1:nc, 384:512]).astype(BF16)
    kc = _dot(hk, w2k_ref[...]).astype(BF16)
    vc = _dot(hv, w2v_ref[...]).astype(BF16)
    zero = jnp.zeros((1, LANES), BF16)
    for ref_f, ref_g, val in ((kf_ref, kg_ref, kc), (vf_ref, vg_ref, vc)):
        full = jnp.concatenate([val, zero], axis=0)
        ref_f[0] = full
        for g in range(N_KV_C):
            ref_g[0, g] = full[:, g * HEAD_DIM:(g + 1) * HEAD_DIM]


def _cmp_finish(ab, w2k_bd, w2v_bd):
    n, nc, _ = ab.shape
    flat = jax.ShapeDtypeStruct((n, nc, LANES), BF16)
    grp = jax.ShapeDtypeStruct((n, N_KV_C, nc, HEAD_DIM), BF16)
    fspec = pl.BlockSpec((1, nc, LANES), lambda b: (b, 0, 0))
    gspec = pl.BlockSpec((1, N_KV_C, nc, HEAD_DIM), lambda b: (b, 0, 0, 0))
    return pl.pallas_call(
        functools.partial(_cmp_finish_kernel, nc=nc),
        grid=(n,),
        in_specs=[pl.BlockSpec((1, nc, 512), lambda b: (b, 0, 0)),
                  pl.BlockSpec((LANES, LANES), lambda b: (0, 0)), pl.BlockSpec((LANES, LANES), lambda b: (0, 0))],
        out_specs=[fspec, fspec, gspec, gspec], out_shape=[flat, flat, grp, grp],
        compiler_params=_cparams("parallel"),
        name="cmp_finish",
    )(ab, w2k_bd, w2v_bd)


def _overlap_t(nb, nc):
    b = lax.broadcasted_iota(jnp.int32, (nb, nc), 0)
    c = lax.broadcasted_iota(jnp.int32, (nb, nc), 1)
    return ((c * CMP_STRIDE < (b + 1) * SLC_BLOCK) & (c * CMP_STRIDE + CMP_LEN > b * SLC_BLOCK)).astype(F32)


def _nsa_prompt_kernel(qraw_ref, qrot_ref, gt_ref, kcc_ref, vcc_ref, ks_ref, vs_ref, kw_ref, vw_ref, o_ref,
                       *, tq, nc, nsb):
    i = pl.program_id(2)
    q0 = i * tq
    rows = HPG_C * tq
    nbp = LANES
    qpos3 = q0 + lax.broadcasted_iota(jnp.int32, (1, tq, 1), 1)

    qraw = qraw_ref[...].reshape(rows, HEAD_DIM)
    sc = _dot_nt(qraw, kcc_ref[0, 0]).reshape(HPG_C, tq, nc)
    cidx = lax.broadcasted_iota(jnp.int32, (1, 1, nc), 2)
    mask_c = (cidx * CMP_STRIDE + CMP_LEN - 1 <= qpos3) & (cidx < nc - 1)
    pc = _masked_softmax(sc, mask_c)
    o_c = _dot(pc.reshape(rows, nc).astype(BF16), vcc_ref[0, 0])

    psum = jnp.sum(pc, axis=0)
    imp_t = _dot_nt(_overlap_t(nsb, nc), psum, precision=HIGHEST)
    blk = lax.broadcasted_iota(jnp.int32, (nsb, tq), 0)
    own = (q0 + lax.broadcasted_iota(jnp.int32, (nsb, tq), 1)) // SLC_BLOCK
    imp_t = jnp.where(blk == own, BIG, jnp.where(blk < own, imp_t, NEG))
    cnt = jnp.zeros((nsb, tq), jnp.int32)
    for b2 in range(nsb):
        rowv = imp_t[b2:b2 + 1, :]
        ahead = (rowv > imp_t) | ((rowv == imp_t) & (b2 < blk))
        cnt = cnt + ahead.astype(jnp.int32)
    sel_t = ((cnt < N_SLC) & (blk <= own)).astype(F32)
    if nsb < nbp:
        sel_t = jnp.concatenate([sel_t, jnp.zeros((nbp - nsb, tq), F32)], axis=0)
    sel = sel_t.T.astype(BF16)

    qrot = qrot_ref[...].reshape(rows, HEAD_DIM)
    m0 = jnp.full((HPG_C, tq, 1), NEG, F32)
    l0 = jnp.zeros((HPG_C, tq, 1), F32)
    a0 = jnp.zeros((rows, HEAD_DIM), F32)
    eb = lax.broadcasted_iota(jnp.int32, (nbp, tq), 0)
    ej = lax.broadcasted_iota(jnp.int32, (nbp, tq), 1)
    kj3 = lax.broadcasted_iota(jnp.int32, (1, 1, tq), 2)

    def slc_step(kt, st):
        start = pl.multiple_of(kt * tq, tq)
        expand = ((start + ej) // SLC_BLOCK == eb).astype(BF16)
        picked = _dot(sel, expand).reshape(1, tq, tq) > 0.5
        mask = picked & (start + kj3 <= qpos3)
        s = _dot_nt(qrot, ks_ref[0, pl.ds(start, tq), :]).reshape(HPG_C, tq, tq)
        return _online_step(s, mask, vs_ref[0, pl.ds(start, tq), :], *st)

    _, l_s, a_s = lax.fori_loop(0, i + 1, slc_step, (m0, l0, a0))
    o_s = a_s / l_s.reshape(rows, 1)

    def win_step(kt, st):
        start = pl.multiple_of(kt * tq, tq)
        dist = qpos3 - (start + kj3)
        mask = (dist >= 0) & (dist <= WINDOW)
        s = _dot_nt(qrot, kw_ref[0, pl.ds(start, tq), :]).reshape(HPG_C, tq, tq)
        return _online_step(s, mask, vw_ref[0, pl.ds(start, tq), :], *st)

    _, l_w, a_w = lax.fori_loop(jnp.maximum(i - WINDOW // tq, 0), i + 1, win_step, (m0, l0, a0))
    o_w = a_w / l_w.reshape(rows, 1)

    gt = gt_ref[...]
    for j in range(HPG_C):
        sl = slice(j * tq, (j + 1) * tq)
        g_c = gt[:, j:j + 1]
        g_s = gt[:, HPG_C + j:HPG_C + j + 1]
        g_w = gt[:, 2 * HPG_C + j:2 * HPG_C + j + 1]
        o_ref[j] = (o_c[sl] * g_c + o_s[sl] * g_s + o_w[sl] * g_w).astype(o_ref.dtype)


def _nsa_prompt(qraw_h, qrot_h, gates, kcc_g, vcc_g, ks_h, vs_h, kw_h, vw_h, n_seq, s_len):
    tq = 128
    nq = s_len // tq
    nc = kcc_g.shape[2]
    nsb = -(-s_len // SLC_BLOCK)
    assert s_len % tq == 0 and nsb <= LANES and WINDOW % tq == 0
    qspec = pl.BlockSpec((HPG_C, tq, HEAD_DIM), lambda b, g, i: (g, b * nq + i, 0))
    cspec = pl.BlockSpec((1, 1, nc, HEAD_DIM), lambda b, g, i: (b, g, 0, 0))
    kspec = pl.BlockSpec((1, s_len, HEAD_DIM), lambda b, g, i: (g, b, 0))
    return pl.pallas_call(
        functools.partial(_nsa_prompt_kernel, tq=tq, nc=nc, nsb=nsb),
        grid=(n_seq, N_KV_C, nq),
        in_specs=[qspec, qspec, pl.BlockSpec((tq, LANES), lambda b, g, i: (b * nq + i, g)),
                  cspec, cspec, kspec, kspec, kspec, kspec],
        out_specs=qspec,
        out_shape=jax.ShapeDtypeStruct(qraw_h.shape, BF16),
        compiler_params=_cparams("parallel", "parallel", "arbitrary"),
        name="nsa_prompt",
    )(qraw_h, qrot_h, gates, kcc_g, vcc_g, ks_h, vs_h, kw_h, vw_h)


def _block_mean_kernel(x_ref, o_ref, *, nb):
    x = x_ref[...]
    m = jnp.mean(x.reshape(nb, MOBA_BLOCK, x.shape[1]), axis=1)
    for h in range(N_HEADS):
        o_ref[h] = m[:, h * HEAD_DIM:(h + 1) * HEAD_DIM]


def _block_mean(d_rows):
    m = d_rows.shape[0]
    nb_total = m // MOBA_BLOCK
    nb = _tile(nb_total, (8,))
    return pl.pallas_call(
        functools.partial(_block_mean_kernel, nb=nb),
        grid=(nb_total // nb,),
        in_specs=[pl.BlockSpec((nb * MOBA_BLOCK, HALF_MIX), lambda i: (i, 0))],
        out_specs=pl.BlockSpec((N_HEADS, nb, HEAD_DIM), lambda i: (0, i, 0)),
        out_shape=jax.ShapeDtypeStruct((N_HEADS, nb_total, HEAD_DIM), F32),
        compiler_params=_cparams("parallel"),
        name="moba_block_mean",
    )(d_rows)


def _moba_prompt_kernel(q_ref, km_ref, k_ref, v_ref, o_ref, *, tq, nb):
    i = pl.program_id(2)
    q = q_ref[0]
    nbp = LANES
    gate = _dot_nt(q.astype(F32) * (1.0 / SCALE), km_ref[0], precision=HIGHEST)
    if nb < nbp:
        gate = jnp.concatenate([gate, jnp.full((tq, nbp - nb), NEG, F32)], axis=1)
    blk = lax.broadcasted_iota(jnp.int32, (tq, nbp), 1)
    gate = jnp.where(blk < i, gate, NEG)
    sel = _top_select(gate, blk, MOBA_TOPK).astype(BF16)
    eb = lax.broadcasted_iota(jnp.int32, (nbp, tq), 0)
    r = lax.broadcasted_iota(jnp.int32, (tq, tq), 0)
    c = lax.broadcasted_iota(jnp.int32, (tq, tq), 1)

    def kv(kt):
        start = pl.multiple_of(kt * tq, tq)
        return k_ref[0, pl.ds(start, tq), :], v_ref[0, pl.ds(start, tq), :]

    k0, v0 = kv(i)
    st = _online_step(_dot_nt(q, k0), c <= r, v0, jnp.full((tq, 1), NEG, F32), jnp.zeros((tq, 1), F32),
                      jnp.zeros((tq, HEAD_DIM), F32))

    def step(kt, st):
        kk, vv = kv(kt)
        mask = _dot(sel, (eb == kt).astype(BF16)) > 0.5
        return _online_step(_dot_nt(q, kk), mask, vv, *st)

    _, l, acc = lax.fori_loop(0, i, step, st)
    o_ref[0] = (acc / l).astype(o_ref.dtype)


def _moba_prompt(qd_h, kmean_h, kd_h, vd_h, n_seq, s_len):
    tq = MOBA_BLOCK
    assert s_len % tq == 0
    nq = s_len // tq
    assert nq <= LANES
    kspec = pl.BlockSpec((1, s_len, HEAD_DIM), lambda b, h, i: (h, b, 0))
    qspec = pl.BlockSpec((1, tq, HEAD_DIM), lambda b, h, i: (h, b * nq + i, 0))
    return pl.pallas_call(
        functools.partial(_moba_prompt_kernel, tq=tq, nb=nq),
        grid=(n_seq, N_HEADS, nq),
        in_specs=[qspec, pl.BlockSpec((1, nq, HEAD_DIM), lambda b, h, i: (h, b, 0)), kspec, kspec],
        out_specs=qspec,
        out_shape=jax.ShapeDtypeStruct(qd_h.shape, BF16),
        compiler_params=_cparams("parallel", "parallel", "arbitrary"),
        name="moba_prompt",
    )(qd_h, kmean_h, kd_h, vd_h)


def _page_specs(n_slots, n_pages, width, col_block, base, reverse):
    n_steps = n_pages // n_slots

    def make(s):
        if reverse:
            return lambda b, j, pt: (base + pt[b, (n_steps - 1 - j) * n_slots + (n_slots - 1 - s)], 0, col_block)
        return lambda b, j, pt: (base + pt[b, j * n_slots + s], 0, col_block)

    return [pl.BlockSpec((None, PAGE_SIZE, width), make(s)) for s in range(n_slots)]


def _head_diag(rows, width, per):
    r = lax.broadcasted_iota(jnp.int32, (rows, width), 0)
    c = lax.broadcasted_iota(jnp.int32, (rows, width), 1)
    return ((r % N_HEADS) * HEAD_DIM // per) == (c // HEAD_DIM * HEAD_DIM // per)


def _fold_heads(acc, t):
    kept = jnp.where(_head_diag(acc.shape[0], acc.shape[1], HEAD_DIM), acc, 0.0)
    return jnp.sum(kept.reshape(t, N_HEADS, acc.shape[1]), axis=1)


def _sb_sample_kernel(pt_ref, q_ref, kn_ref, vn_ref, *refs, n_slots, n_steps, t, pad):
    page_refs = refs[:n_slots]
    o_ref, carry_s, acc_s = refs[n_slots:]
    j = pl.program_id(1)
    rows = t * N_HEADS
    q = q_ref[0]

    @pl.when(j == 0)
    def _():
        r = lax.broadcasted_iota(jnp.int32, (rows, pad), 0) // N_HEADS
        c = lax.broadcasted_iota(jnp.int32, (rows, pad), 1)
        carry, acc = _sb_tile(q, kn_ref[0], vn_ref[0], _tri_incl(pad), jnp.zeros((rows, 1), F32),
                              jnp.zeros((rows, HALF_MIX), F32), (c < r) & (c < t))
        carry_s[...] = carry
        acc_s[...] = acc

    t_incl = _tri_incl(PAGE_SIZE)
    for s in range(n_slots):
        @pl.when(jnp.max(carry_s[...]) > -SB_CUTOFF)
        def _():
            page = page_refs[s][...]
            carry, acc = _sb_tile(q, page[:, 0:HALF_MIX].astype(BF16), page[:, HALF_MIX:].astype(BF16), t_incl,
                                  carry_s[...], acc_s[...], None)
            carry_s[...] = carry
            acc_s[...] = acc

    @pl.when(j == n_steps - 1)
    def _():
        o_ref[0] = _fold_heads(acc_s[...], t)


def _sb_sample(qbd, k_new, v_new, pool, base, page_table):
    n, rows, _ = qbd.shape
    t = rows // N_HEADS
    pad = k_new.shape[1]
    n_pages = page_table.shape[1]
    n_slots = _tile(n_pages, (8, 4, 2))
    n_steps = n_pages // n_slots
    seq = lambda b, j, pt: (b, 0, 0)
    gs = pltpu.PrefetchScalarGridSpec(
        num_scalar_prefetch=1, grid=(n, n_steps),
        in_specs=[pl.BlockSpec((1, rows, HALF_MIX), seq), pl.BlockSpec((1, pad, HALF_MIX), seq),
                  pl.BlockSpec((1, pad, HALF_MIX), seq)]
        + _page_specs(n_slots, n_pages, 2 * HALF_MIX, 0, base, True),
        out_specs=pl.BlockSpec((1, t, HALF_MIX), seq),
        scratch_shapes=[pltpu.VMEM((rows, 1), F32), pltpu.VMEM((rows, HALF_MIX), F32)])
    return pl.pallas_call(
        functools.partial(_sb_sample_kernel, n_slots=n_slots, n_steps=n_steps, t=t, pad=pad),
        grid_spec=gs,
        out_shape=jax.ShapeDtypeStruct((n, t, HALF_MIX), F32),
        compiler_params=_cparams("parallel", "arbitrary"),
        name="sb_sample",
    )(page_table, qbd, k_new, v_new, *([pool] * n_slots))


def _nsa_sample_head_kernel(qraw_ref, qrot_ref, kcc_ref, vcc_ref, win_ref, wn_ref, oc_ref, ow_ref, sel_ref,
                            *, t, nc, n_past, nbp, pad):
    rows = t * N_HEADS
    wb = win_ref.shape[1]
    tok = lax.broadcasted_iota(jnp.int32, (rows, 1), 0) // N_HEADS
    qpos = n_past + tok

    def fold(x):
        g0 = (lax.broadcasted_iota(jnp.int32, (rows, HEAD_DIM), 0) % N_HEADS) < HPG_C
        return jnp.where(g0, x[:, 0:HEAD_DIM], x[:, HEAD_DIM:2 * HEAD_DIM])

    sc = _dot_nt(qraw_ref[0], kcc_ref[0])
    cidx = lax.broadcasted_iota(jnp.int32, (1, nc), 1)
    pc = _masked_softmax(sc, (cidx * CMP_STRIDE + CMP_LEN - 1 <= qpos) & (cidx < nc - 1))
    oc_ref[0] = fold(_dot(pc.astype(BF16), vcc_ref[0]))

    ng = t * N_KV_C
    gr = lax.broadcasted_iota(jnp.int32, (ng, rows), 0)
    rr = lax.broadcasted_iota(jnp.int32, (ng, rows), 1)
    same = (gr // N_KV_C == rr // N_HEADS) & (gr % N_KV_C == (rr % N_HEADS) // HPG_C)
    psum = _dot(same.astype(F32), pc, precision=HIGHEST)
    imp = _dot_nt(psum, _overlap_t(nbp, nc), precision=HIGHEST)
    blk = lax.broadcasted_iota(jnp.int32, (ng, nbp), 1)
    own = (n_past + lax.broadcasted_iota(jnp.int32, (ng, nbp), 0) // N_KV_C) // SLC_BLOCK
    imp = jnp.where(blk == own, BIG, jnp.where(blk < own, imp, NEG))
    sel_ref[0] = _top_select(imp, blk, N_SLC)

    qrot = qrot_ref[0]
    win = win_ref[0]
    wn = wn_ref[0]
    s1 = _dot_nt(qrot, win[:, 0:LANES].astype(BF16))
    wpos1 = n_past - wb + lax.broadcasted_iota(jnp.int32, (1, wb), 1)
    d1 = qpos - wpos1
    mk1 = (d1 >= 0) & (d1 <= WINDOW) & (wpos1 >= 0)
    s2 = _dot_nt(qrot, wn[:, 0:LANES])
    j2 = lax.broadcasted_iota(jnp.int32, (1, pad), 1)
    mk2 = (j2 <= tok) & (j2 < t)
    m = jnp.maximum(jnp.max(jnp.where(mk1, s1, NEG), axis=-1, keepdims=True),
                    jnp.max(jnp.where(mk2, s2, NEG), axis=-1, keepdims=True))
    p1 = jnp.where(mk1, jnp.exp(jnp.where(mk1, s1, NEG) - m), 0.0)
    p2 = jnp.where(mk2, jnp.exp(jnp.where(mk2, s2, NEG) - m), 0.0)
    den = jnp.sum(p1, axis=-1, keepdims=True) + jnp.sum(p2, axis=-1, keepdims=True)
    ow = _dot(p1.astype(BF16), win[:, LANES:2 * LANES].astype(BF16)) + _dot(p2.astype(BF16), wn[:, LANES:2 * LANES])
    ow_ref[0] = fold(ow / den)


def _nsa_sample_head(qraw_bd, qrot_bd, kcc_f, vcc_f, win, w_new, n_past, nbp):
    n, rows, _ = qraw_bd.shape
    t = rows // N_HEADS
    nc = kcc_f.shape[1]
    wb = win.shape[1]
    pad = w_new.shape[1]
    seq = lambda b: (b, 0, 0)
    return pl.pallas_call(
        functools.partial(_nsa_sample_head_kernel, t=t, nc=nc, n_past=n_past, nbp=nbp, pad=pad),
        grid=(n,),
        in_specs=[pl.BlockSpec((1, rows, LANES), seq), pl.BlockSpec((1, rows, LANES), seq),
                  pl.BlockSpec((1, nc, LANES), seq), pl.BlockSpec((1, nc, LANES), seq),
                  pl.BlockSpec((1, wb, 2 * LANES), seq), pl.BlockSpec((1, pad, 2 * LANES), seq)],
        out_specs=[pl.BlockSpec((1, rows, HEAD_DIM), seq), pl.BlockSpec((1, rows, HEAD_DIM), seq),
                   pl.BlockSpec((1, t * N_KV_C, nbp), seq)],
        out_shape=[jax.ShapeDtypeStruct((n, rows, HEAD_DIM), F32), jax.ShapeDtypeStruct((n, rows, HEAD_DIM), F32),
                   jax.ShapeDtypeStruct((n, t * N_KV_C, nbp), F32)],
        compiler_params=_cparams("parallel"),
        name="nsa_sample_head",
    )(qraw_bd, qrot_bd, kcc_f, vcc_f, win, w_new)


def _nsa_sample_slc_kernel(pt_ref, q_ref, sel_ref, sn_ref, oc_ref, ow_ref, gt_ref, *refs,
                           n_slots, n_steps, t, nbp, pad):
    page_refs = refs[:n_slots]
    o_ref, m_s, l_s, acc_s = refs[n_slots:]
    j = pl.program_id(1)
    rows = t * N_HEADS
    q = q_ref[0]
    tok = lax.broadcasted_iota(jnp.int32, (rows, 1), 0) // N_HEADS

    @pl.when(j == 0)
    def _():
        sn = sn_ref[0]
        j2 = lax.broadcasted_iota(jnp.int32, (1, pad), 1)
        m, l, acc = _online_step(_dot_nt(q, sn[:, 0:LANES]), (j2 <= tok) & (j2 < t), sn[:, LANES:2 * LANES],
                                 jnp.full((rows, 1), NEG, F32), jnp.zeros((rows, 1), F32),
                                 jnp.zeros((rows, LANES), F32))
        m_s[...] = m
        l_s[...] = l
        acc_s[...] = acc

    sel = sel_ref[0]
    eb = lax.broadcasted_iota(jnp.int32, (nbp, PAGE_SIZE), 0)
    ej = lax.broadcasted_iota(jnp.int32, (nbp, PAGE_SIZE), 1)
    for s in range(n_slots):
        first_blk = (j * n_slots + s) * (PAGE_SIZE // SLC_BLOCK)
        mask = _dot(sel, (first_blk + ej // SLC_BLOCK == eb).astype(BF16)) > 0.5
        page = page_refs[s][...]
        m, l, acc = _online_step(_dot_nt(q, page[:, 0:LANES].astype(BF16)), mask,
                                 page[:, LANES:2 * LANES].astype(BF16), m_s[...], l_s[...], acc_s[...])
        m_s[...] = m
        l_s[...] = l
        acc_s[...] = acc

    @pl.when(j == n_steps - 1)
    def _():
        o = acc_s[...] / l_s[...]
        g0 = (lax.broadcasted_iota(jnp.int32, (rows, HEAD_DIM), 0) % N_HEADS) < HPG_C
        o_s = jnp.where(g0, o[:, 0:HEAD_DIM], o[:, HEAD_DIM:2 * HEAD_DIM])
        gt = gt_ref[0]
        o_ref[0] = oc_ref[0] * gt[:, 0:1] + o_s * gt[:, 1:2] + ow_ref[0] * gt[:, 2:3]


def _nsa_sample_slc(qrot_bd, sel_rows, s_new, o_c, o_w, gates, pool, base, page_table):
    n, rows, _ = qrot_bd.shape
    t = rows // N_HEADS
    nbp = sel_rows.shape[2]
    pad = s_new.shape[1]
    n_pages = page_table.shape[1]
    n_slots = _tile(n_pages, (16, 8, 4, 2))
    n_steps = n_pages // n_slots
    seq = lambda b, j, pt: (b, 0, 0)
    gs = pltpu.PrefetchScalarGridSpec(
        num_scalar_prefetch=1, grid=(n, n_steps),
        in_specs=[pl.BlockSpec((1, rows, LANES), seq), pl.BlockSpec((1, rows, nbp), seq),
                  pl.BlockSpec((1, pad, 2 * LANES), seq), pl.BlockSpec((1, rows, HEAD_DIM), seq),
                  pl.BlockSpec((1, rows, HEAD_DIM), seq), pl.BlockSpec((1, rows, 3), seq)]
        + _page_specs(n_slots, n_pages, 2 * LANES, 1, base, False),
        out_specs=pl.BlockSpec((1, rows, HEAD_DIM), seq),
        scratch_shapes=[pltpu.VMEM((rows, 1), F32), pltpu.VMEM((rows, 1), F32), pltpu.VMEM((rows, LANES), F32)])
    return pl.pallas_call(
        functools.partial(_nsa_sample_slc_kernel, n_slots=n_slots, n_steps=n_steps, t=t, nbp=nbp, pad=pad),
        grid_spec=gs,
        out_shape=jax.ShapeDtypeStruct((n, rows, HEAD_DIM), F32),
        compiler_params=_cparams("parallel", "arbitrary"),
        name="nsa_sample_slc",
    )(page_table, qrot_bd, sel_rows, s_new, o_c, o_w, gates, *([pool] * n_slots))


def _page_sum_kernel(pt_ref, *refs, n_slots):
    o_ref = refs[n_slots]
    ppb = MOBA_BLOCK // PAGE_SIZE
    for blk in range(n_slots // ppb):
        tot = jnp.sum(refs[blk * ppb][...], axis=0, keepdims=True)
        for p in range(1, ppb):
            tot = tot + jnp.sum(refs[blk * ppb + p][...], axis=0, keepdims=True)
        o_ref[0, blk:blk + 1, :] = tot


def _page_sums(pool, base, page_table):
    n, n_pages = page_table.shape
    ppb = MOBA_BLOCK // PAGE_SIZE
    n_slots = _tile(n_pages, (8 * ppb,))
    assert n_slots % ppb == 0
    gs = pltpu.PrefetchScalarGridSpec(
        num_scalar_prefetch=1, grid=(n, n_pages // n_slots),
        in_specs=_page_specs(n_slots, n_pages, HALF_MIX, 0, base, False),
        out_specs=pl.BlockSpec((1, n_slots // ppb, HALF_MIX), lambda b, j, pt: (b, j, 0)))
    return pl.pallas_call(
        functools.partial(_page_sum_kernel, n_slots=n_slots),
        grid_spec=gs,
        out_shape=jax.ShapeDtypeStruct((n, n_pages // ppb, HALF_MIX), F32),
        compiler_params=_cparams("parallel", "arbitrary"),
        name="moba_page_sums",
    )(page_table, *([pool] * n_slots))


def _moba_sample_kernel(pt_ref, q_ref, qf_ref, ks_ref, kn_ref, vn_ref, *refs, n_slots, n_steps, t, nbp, pad,
                        n_past):
    page_refs = refs[:n_slots]
    o_ref, sel_s, m_s, l_s, acc_s = refs[n_slots:]
    j = pl.program_id(1)
    rows = t * N_HEADS
    ppb = MOBA_BLOCK // PAGE_SIZE
    nb_full = ks_ref.shape[1]
    q = q_ref[0]
    tok = lax.broadcasted_iota(jnp.int32, (rows, 1), 0) // N_HEADS

    @pl.when(j == 0)
    def _():
        kmean = ks_ref[0] * (1.0 / MOBA_BLOCK)
        gate = _dot_nt(qf_ref[0], kmean, precision=HIGHEST)
        if nb_full < nbp:
            gate = jnp.concatenate([gate, jnp.full((rows, nbp - nb_full), NEG, F32)], axis=1)
        blk = lax.broadcasted_iota(jnp.int32, (rows, nbp), 1)
        own = (n_past + tok) // MOBA_BLOCK
        gate = jnp.where((blk < own) & (blk < nb_full), gate, NEG)
        sel_s[...] = _top_select(gate, blk, MOBA_TOPK).astype(BF16)
        j2 = lax.broadcasted_iota(jnp.int32, (1, pad), 1)
        m, l, acc = _online_step(_dot_nt(q, kn_ref[0]), (j2 <= tok) & (j2 < t), vn_ref[0],
                                 jnp.full((rows, 1), NEG, F32), jnp.zeros((rows, 1), F32),
                                 jnp.zeros((rows, HALF_MIX), F32))
        m_s[...] = m
        l_s[...] = l
        acc_s[...] = acc

    sel = sel_s[...]
    eb = lax.broadcasted_iota(jnp.int32, (nbp, PAGE_SIZE), 0)
    for s in range(n_slots):
        blk_of_page = (j * n_slots + s) // ppb
        mask = _dot(sel, (eb == blk_of_page).astype(BF16)) > 0.5
        page = page_refs[s][...]
        m, l, acc = _online_step(_dot_nt(q, page[:, 0:HALF_MIX].astype(BF16)), mask,
                                 page[:, HALF_MIX:].astype(BF16), m_s[...], l_s[...], acc_s[...])
        m_s[...] = m
        l_s[...] = l
        acc_s[...] = acc

    @pl.when(j == n_steps - 1)
    def _():
        o_ref[0] = _fold_heads(acc_s[...] / l_s[...], t)


def _moba_sample(qbd, qbd_f32, ksums, k_new, v_new, pool, base, page_table, n_past):
    n, rows, _ = qbd.shape
    t = rows // N_HEADS
    pad = k_new.shape[1]
    n_pages = page_table.shape[1]
    assert n_past % MOBA_BLOCK == 0 and t <= MOBA_BLOCK
    nbp = LANES * (-(-(n_pages * PAGE_SIZE // MOBA_BLOCK) // LANES))
    n_slots = _tile(n_pages, (8, 4, 2))
    n_steps = n_pages // n_slots
    seq = lambda b, j, pt: (b, 0, 0)
    gs = pltpu.PrefetchScalarGridSpec(
        num_scalar_prefetch=1, grid=(n, n_steps),
        in_specs=[pl.BlockSpec((1, rows, HALF_MIX), seq), pl.BlockSpec((1, rows, HALF_MIX), seq),
                  pl.BlockSpec((1, ksums.shape[1], HALF_MIX), seq),
                  pl.BlockSpec((1, pad, HALF_MIX), seq), pl.BlockSpec((1, pad, HALF_MIX), seq)]
        + _page_specs(n_slots, n_pages, 2 * HALF_MIX, 0, base, False),
        out_specs=pl.BlockSpec((1, t, HALF_MIX), seq),
        scratch_shapes=[pltpu.VMEM((rows, nbp), BF16), pltpu.VMEM((rows, 1), F32), pltpu.VMEM((rows, 1), F32),
                        pltpu.VMEM((rows, HALF_MIX), F32)])
    return pl.pallas_call(
        functools.partial(_moba_sample_kernel, n_slots=n_slots, n_steps=n_steps, t=t, nbp=nbp, pad=pad,
                          n_past=n_past),
        grid_spec=gs,
        out_shape=jax.ShapeDtypeStruct((n, t, HALF_MIX), F32),
        compiler_params=_cparams("parallel", "arbitrary"),
        name="moba_sample",
    )(page_table, qbd, qbd_f32, ksums, k_new, v_new, *([pool] * n_slots))


def _block_diag(blocks):
    nb, bi, bj = blocks.shape
    eye = jnp.eye(nb, dtype=blocks.dtype)
    return jnp.einsum('bij,bc->bicj', blocks, eye).reshape(nb * bi, nb * bj)


def _rope_tables(pos):
    half = HEAD_DIM // 2
    inv_freq = ROPE_THETA ** (-jnp.arange(half, dtype=F32) / half)
    ang = pos.astype(F32)[:, None] * inv_freq[None, :]
    cos = jnp.cos(ang)
    sin = jnp.sin(ang)
    return jnp.concatenate([cos, cos, cos, cos], axis=1), jnp.concatenate([-sin, sin, -sin, sin], axis=1)


def _queries_bd(q, per, scale, dtype):
    n, t, _ = q.shape
    groups = N_HEADS // per
    qh = q.reshape(n, t, N_HEADS, 1, HEAD_DIM) * scale
    pick = (jnp.arange(N_HEADS)[:, None] // per == jnp.arange(groups)[None, :]).astype(q.dtype)
    return (qh * pick[None, None, :, :, None]).reshape(n, t * N_HEADS, groups * HEAD_DIM).astype(dtype)


def _pad_rows(x, pad):
    n, t, w = x.shape
    return jnp.concatenate([x, jnp.zeros((n, pad - t, w), x.dtype)], axis=1)


def _odd_weight(w_in):
    d = w_in.shape[0]
    gc0 = _O_VW + 128
    main = jnp.concatenate([w_in[:, :gc0], w_in[:, gc0 + 3 * N_HEADS:]], axis=1)
    gc = w_in[:, gc0:gc0 + 3 * N_HEADS].reshape(d, N_KV_C, HPG_C, 3).transpose(0, 1, 3, 2).reshape(d, N_KV_C, 3 * HPG_C)
    gc = jnp.concatenate([gc, jnp.zeros((d, N_KV_C, LANES - 3 * HPG_C), w_in.dtype)], axis=2).reshape(d, N_KV_C * LANES)
    return jnp.concatenate([main, gc], axis=1).astype(BF16)


def kernel(x_prompt, x_sample, cache_a_kv, state_b_h, state_b_conv, cache_c_kv, state_c_win, cache_d_kv, page_table, norm_ffn1, w_ffn1_gate, w_ffn1_up, w_ffn1_down, norm_mix, norm_ffn2, w_ffn2_gate, w_ffn2_up, w_ffn2_down, w_in_even, w_out_even, lru_conv_w, lru_conv_b, lru_w_r, lru_b_r, lru_w_i, lru_b_i, lru_lambda, w_in_odd, w_out_odd, cmp_pe_k, cmp_w1_k, cmp_w2_k, cmp_pe_v, cmp_w1_v, cmp_w2_v, norm_final):
    nb, s_len, d = x_prompt.shape
    ns, t = x_sample.shape[:2]
    depth = norm_mix.shape[0]
    n_pages = page_table.shape[1]
    n_past = n_pages * PAGE_SIZE
    n_pool = cache_a_kv.shape[1]
    hm = HALF_MIX
    pad = 16
    assert t <= pad and n_past % SLC_BLOCK == 0

    yp = x_prompt.reshape(nb * s_len, d)
    ys = x_sample.reshape(ns * t, d)
    cos_p, sin_p = _rope_tables(jnp.arange(s_len, dtype=jnp.int32))
    cos_s, sin_s = _rope_tables(jnp.tile(n_past + jnp.arange(t, dtype=jnp.int32), ns))
    pool_a = cache_a_kv.reshape(-1, PAGE_SIZE, 2 * hm)
    pool_c = cache_c_kv.reshape(-1, PAGE_SIZE, 4 * N_KV_C * HEAD_DIM)
    pool_d = cache_d_kv.reshape(-1, PAGE_SIZE, 2 * hm)

    outs = {k: [] for k in ("a_p", "a_s", "bh_p", "bh_s", "bc_p", "bc_s", "c_p", "c_s", "cw_p", "cw_s", "d_p", "d_s")}
    for layer in range(depth):
        li = layer // 2
        last = layer == depth - 1
        f1 = (norm_ffn1[layer], w_ffn1_gate[layer].astype(BF16), w_ffn1_up[layer].astype(BF16),
              w_ffn1_down[layer].astype(BF16))
        f2 = (norm_ffn2[layer], w_ffn2_gate[layer].astype(BF16), w_ffn2_up[layer].astype(BF16),
              w_ffn2_down[layer].astype(BF16))
        yp = _half_ffn(yp, *f1)
        ys = _half_ffn(ys, *f1)
        if layer % 2 == 0:
            w_in = w_in_even[li].astype(BF16)
            w_out = w_out_even[li].astype(BF16)
            lru = (lru_conv_w[li], lru_conv_b[li], _block_diag(lru_w_r[li]).astype(BF16), lru_b_r[li],
                   _block_diag(lru_w_i[li]).astype(BF16), lru_b_i[li], lru_lambda[li])
            kv, xb, gb, qh, kh, vh = _proj_even(yp, norm_mix[layer], w_in, True)
            o_a = _sb_prompt(qh, kh, vh, nb, s_len)
            g, h_new, buf_new = _rg_lru(xb.reshape(nb, s_len, hm), gb.reshape(nb, s_len, hm),
                                        jnp.zeros((nb, hm), F32), jnp.zeros((nb, CONV_WIDTH - 1, hm), F32), *lru)
            yp = _outproj(yp, o_a, g.reshape(nb * s_len, hm), w_out)
            outs["a_p"].append(kv.reshape(nb, s_len, 2, N_HEADS, HEAD_DIM))
            outs["bh_p"].append(h_new)
            outs["bc_p"].append(buf_new)
            kv, xb, gb, q = _proj_even(ys, norm_mix[layer], w_in, False)
            kv3 = kv.reshape(ns, t, 2 * hm)
            o_a = _sb_sample(_queries_bd(q.reshape(ns, t, hm), 1, SCALE, BF16),
                             _pad_rows(kv3[:, :, :hm], pad).astype(BF16), _pad_rows(kv3[:, :, hm:], pad).astype(BF16),
                             pool_a, li * n_pool, page_table)
            g, h_new, buf_new = _rg_lru(xb.reshape(ns, t, hm), gb.reshape(ns, t, hm), state_b_h[li],
                                        state_b_conv[li], *lru)
            ys = _outproj(ys, o_a.reshape(ns * t, hm).astype(BF16), g.reshape(ns * t, hm), w_out)
            outs["a_s"].append(kv.reshape(ns, t, 2, N_HEADS, HEAD_DIM))
            outs["bh_s"].append(h_new)
            outs["bc_s"].append(buf_new)
        else:
            w_in = _odd_weight(w_in_odd[li])
            w_out = w_out_odd[li].astype(BF16)
            cmp_k = _cmp_weights(cmp_pe_k[li], cmp_w1_k[li])
            cmp_v = _cmp_weights(cmp_pe_v[li], cmp_w1_v[li])
            w2k = _block_diag(jnp.stack([cmp_w2_k[li]] * N_KV_C)).astype(BF16)
            w2v = _block_diag(jnp.stack([cmp_w2_v[li]] * N_KV_C)).astype(BF16)
            (c_rows, w_rows, d_rows, gates, qraw_h, qrot_h, ks_h, vs_h, kw_h, vw_h, qd_h, kd_h, vd_h) = _proj_odd(
                yp, norm_mix[layer], w_in, cos_p, sin_p, True)
            n_chunk = s_len // CMP_STRIDE
            rows_c = _tile(s_len, (2048, 1024, 512, 256, 128))
            ab = _cmp_ab([c_rows], [pl.BlockSpec((rows_c, LANES), lambda b, j: (b * (s_len // rows_c) + j, 0))],
                         [pl.BlockSpec((rows_c, LANES), lambda b, j: (b * (s_len // rows_c) + j, 1))],
                         (nb, s_len // rows_c), rows_c // CMP_STRIDE, nb, n_chunk, rows_c, cmp_k, cmp_v, 0, ())
            _, _, kcc_g, vcc_g = _cmp_finish(ab, w2k, w2v)
            o_c = _nsa_prompt(qraw_h, qrot_h, gates, kcc_g, vcc_g, ks_h, vs_h, kw_h, vw_h, nb, s_len)
            o_d = _moba_prompt(qd_h, _block_mean(d_rows), kd_h, vd_h, nb, s_len)
            yp = _outproj(yp, o_c, o_d, w_out)
            wlen = min(WINDOW, s_len)
            outs["c_p"].append(c_rows.reshape(nb, s_len, 4, N_KV_C, HEAD_DIM))
            outs["cw_p"].append(w_rows.reshape(nb, s_len, 2, N_KV_C, HEAD_DIM)[:, s_len - wlen:])
            outs["d_p"].append(d_rows.reshape(nb, s_len, 2, N_HEADS, HEAD_DIM))
            c_rows, w_rows, d_rows, gates, qraw, qrot, qd = _proj_odd(ys, norm_mix[layer], w_in, cos_s, sin_s, False)
            slots_c = _tile(n_pages, (16, 8))
            ab = _cmp_ab([pool_c] * slots_c, _page_specs(slots_c, n_pages, LANES, 0, li * n_pool, False),
                         _page_specs(slots_c, n_pages, LANES, 1, li * n_pool, False),
                         (ns, n_pages // slots_c), slots_c * PAGE_SIZE // CMP_STRIDE, ns, n_past // CMP_STRIDE,
                         PAGE_SIZE, cmp_k, cmp_v, 1, (page_table,))
            kcc_f, vcc_f, _, _ = _cmp_finish(ab, w2k, w2v)
            nsb = -(-(n_past + t) // SLC_BLOCK)
            nbp = LANES * (-(-nsb // LANES))
            win_l = state_c_win[li]
            wb = win_l.shape[1]
            qraw_bd = _queries_bd(qraw.reshape(ns, t, hm), HPG_C, SCALE, BF16)
            qrot_bd = _queries_bd(qrot.reshape(ns, t, hm), HPG_C, SCALE, BF16)
            c3 = c_rows.reshape(ns, t, 4 * LANES)
            w3 = w_rows.reshape(ns, t, 2 * LANES)
            d3 = d_rows.reshape(ns, t, 2 * hm)
            o_cmp, o_win, sel = _nsa_sample_head(qraw_bd, qrot_bd, kcc_f, vcc_f, win_l.reshape(ns, wb, 2 * LANES),
                                                 _pad_rows(w3, pad).astype(BF16), n_past, nbp)
            sel_rows = jnp.repeat(sel.reshape(ns, t, N_KV_C, nbp), HPG_C, axis=2).reshape(ns, t * N_HEADS, nbp)
            gt = gates.reshape(ns, t, N_KV_C, LANES)[..., :3 * HPG_C].reshape(ns, t, N_KV_C, 3, HPG_C)
            gt = gt.transpose(0, 1, 2, 4, 3).reshape(ns, t * N_HEADS, 3)
            o_c = _nsa_sample_slc(qrot_bd, sel_rows.astype(BF16), _pad_rows(c3[:, :, 2 * LANES:], pad).astype(BF16),
                                  o_cmp, o_win, gt, pool_c, li * n_pool, page_table)
            ksums = _page_sums(pool_d, li * n_pool, page_table)
            qd3 = qd.reshape(ns, t, hm)
            o_d = _moba_sample(_queries_bd(qd3, 1, SCALE, BF16), _queries_bd(qd3, 1, 1.0, F32), ksums,
                               _pad_rows(d3[:, :, :hm], pad).astype(BF16), _pad_rows(d3[:, :, hm:], pad).astype(BF16),
                               pool_d, li * n_pool, page_table, n_past)
            ys = _outproj(ys, o_c.reshape(ns * t, hm).astype(BF16), o_d.reshape(ns * t, hm).astype(BF16), w_out)
            w5 = w_rows.reshape(ns, t, 2, N_KV_C, HEAD_DIM)
            outs["c_s"].append(c_rows.reshape(ns, t, 4, N_KV_C, HEAD_DIM))
            outs["cw_s"].append(jnp.concatenate([win_l, w5], axis=1)[:, -wb:])
            outs["d_s"].append(d_rows.reshape(ns, t, 2, N_HEADS, HEAD_DIM))
        gf = norm_final if last else None
        yp = _half_ffn(yp, *f2, g_final=gf)
        ys = _half_ffn(ys, *f2, g_final=gf)
    st = lambda k: jnp.stack(outs[k])
    return (yp.reshape(nb, s_len, d), ys.reshape(ns, t, d), st("a_p"), st("a_s"), st("bh_p"), st("bh_s"),
            st("bc_p"), st("bc_s"), st("c_p"), st("c_s"), st("cw_p"), st("cw_s"), st("d_p"), st("d_s"))
```

```python
import functools
import math

import jax
import jax.numpy as jnp
import numpy as np
from jax import lax
from jax.experimental import pallas as pl
from jax.experimental.pallas import tpu as pltpu

F32 = jnp.float32
BF16 = jnp.bfloat16

HEAD_DIM = 64
HALF_MIX = 512
N_HEADS = HALF_MIX // HEAD_DIM
N_KV_C = 2
HPG_C = N_HEADS // N_KV_C
PAGE_SIZE = 128
CONV_WIDTH = 4
LRU_C = 8.0
CMP_LEN = 32
CMP_STRIDE = 16
SLC_BLOCK = 64
N_SLC = 16
WINDOW = 512
MOBA_BLOCK = 256
MOBA_TOPK = 3
ROPE_THETA = 10000.0
RMS_EPS = 1e-6
NEG = -1e30
BIG = 1e30
SCALE = HEAD_DIM ** -0.5

LANES = 128
VMEM_LIMIT_BYTES = 56 * 1024 * 1024
SB_CUTOFF = 120.0
HIGHEST = lax.Precision.HIGHEST


def _cparams(*sem):
    return pltpu.CompilerParams(dimension_semantics=sem, vmem_limit_bytes=VMEM_LIMIT_BYTES)


def _tile(n, prefs):
    for t in prefs:
        if n % t == 0:
            return t
    return n


def _dot(a, b, precision=None):
    return jnp.dot(a, b, preferred_element_type=F32, precision=precision)


def _dot_nt(a, b, precision=None):
    return lax.dot_general(a, b, (((1,), (1,)), ((), ())), preferred_element_type=F32, precision=precision)


def _rms(x, g):
    ms = jnp.mean(x * x, axis=-1, keepdims=True)
    return x * lax.rsqrt(ms + RMS_EPS) * g


def _softplus(z):
    return jnp.maximum(z, 0.0) + jnp.log(1.0 + jnp.exp(-jnp.abs(z)))


def _one_minus_exp(x):
    poly = x
    for k in range(7, 1, -1):
        poly = x * (1.0 + poly * (1.0 / k))
    return jnp.where(x > -0.125, -poly, 1.0 - jnp.exp(x))


def _split_dot(x, t):
    hi = x.astype(BF16)
    lo = (x - hi.astype(F32)).astype(BF16)
    return _dot(hi, t) + _dot(lo, t)


def _masked_softmax(s, mask):
    sm = jnp.where(mask, s, NEG)
    m = jnp.max(sm, axis=-1, keepdims=True)
    e = jnp.where(mask, jnp.exp(sm - m), 0.0)
    d = jnp.sum(e, axis=-1, keepdims=True)
    return e / jnp.maximum(d, 1e-30)


def _online_step(s, mask, v, m, l, acc):
    sm = jnp.where(mask, s, NEG)
    m_new = jnp.maximum(m, jnp.max(sm, axis=-1, keepdims=True))
    alpha = jnp.exp(m - m_new)
    p = jnp.where(mask, jnp.exp(sm - m_new), 0.0)
    l = alpha * l + jnp.sum(p, axis=-1, keepdims=True)
    rows = acc.shape[0]
    acc = alpha.reshape(rows, 1) * acc + _dot(p.reshape(rows, p.shape[-1]).astype(BF16), v)
    return m_new, l, acc


def _flash_multi(qs, ks, v1s, biases, m_refs, acc_refs):
    n = len(qs)
    ss = [_dot_nt(qs[i], ks[i]) for i in range(n)]
    ss = [s if b is None else s + b for s, b in zip(ss, biases)]
    m_old = [ref[...] for ref in m_refs]
    m_new = [jnp.maximum(m_old[i], jnp.max(ss[i], axis=-1, keepdims=True)) for i in range(n)]
    ps = [jnp.exp(ss[i] - m_new[i]).astype(BF16) for i in range(n)]
    pvs = [_dot(ps[i], v1s[i]) for i in range(n)]
    for i in range(n):
        acc_refs[i][...] = jnp.exp(m_old[i] - m_new[i]) * acc_refs[i][...] + pvs[i]
        m_refs[i][...] = m_new[i]


def _joint_update(m_s, l_s, acc_s, scores, vals):
    m_old = m_s[...]
    m_new = m_old
    for sc in scores:
        m_new = jnp.maximum(m_new, jnp.max(sc, axis=-1, keepdims=True))
    alpha = jnp.exp(m_old - m_new)
    l = alpha * l_s[...]
    acc = alpha * acc_s[...]
    for sc, v in zip(scores, vals):
        p = jnp.exp(sc - m_new)
        l = l + jnp.sum(p, axis=-1, keepdims=True)
        acc = acc + _dot(p.astype(BF16), v)
    m_s[...] = m_new
    l_s[...] = l
    acc_s[...] = acc


def _flash_result(acc):
    return acc[:, 0:HEAD_DIM] / acc[:, HEAD_DIM:HEAD_DIM + 1]


def _top_select(val, idx, n):
    sel = jnp.zeros(val.shape, F32)
    big_i = jnp.int32(2 ** 30)
    for _ in range(n):
        m = jnp.max(val, axis=-1, keepdims=True)
        first = jnp.min(jnp.where(val == m, idx, big_i), axis=-1, keepdims=True)
        hit = idx == first
        sel = jnp.where(hit & (m > 0.5 * NEG), 1.0, sel)
        val = jnp.where(hit, -jnp.inf, val)
    return sel


def _ffn_kernel(*refs, n_f, post_norm):
    if post_norm:
        x_ref, g_ref, wg_ref, wu_ref, wd_ref, gf_ref, o_ref, xn_ref, acc_ref = refs
    else:
        x_ref, g_ref, wg_ref, wu_ref, wd_ref, o_ref, xn_ref, acc_ref = refs
    j = pl.program_id(1)

    @pl.when(j == 0)
    def _():
        xn_ref[...] = _rms(x_ref[...], g_ref[...]).astype(BF16)
        acc_ref[...] = jnp.zeros_like(acc_ref)

    xn = xn_ref[...]
    hg = _dot(xn, wg_ref[...])
    hu = _dot(xn, wu_ref[...])
    a = (hg * jax.nn.sigmoid(hg) * hu).astype(BF16)
    acc_ref[...] += _dot(a, wd_ref[...])

    @pl.when(j == n_f - 1)
    def _():
        y = x_ref[...] + 0.5 * acc_ref[...]
        if post_norm:
            y = _rms(y, gf_ref[...])
        o_ref[...] = y


def _half_ffn(x, g, wg, wu, wd, g_final=None):
    m, d = x.shape
    f = wg.shape[1]
    tm = _tile(m, (1024, 512, 256, 128))
    tf = _tile(f, (256, 128))
    n_f = f // tf
    post = g_final is not None
    in_specs = [pl.BlockSpec((tm, d), lambda i, j: (i, 0)),
                pl.BlockSpec((1, d), lambda i, j: (0, 0)),
                pl.BlockSpec((d, tf), lambda i, j: (0, j)),
                pl.BlockSpec((d, tf), lambda i, j: (0, j)),
                pl.BlockSpec((tf, d), lambda i, j: (j, 0))]
    args = [x, g.reshape(1, d), wg, wu, wd]
    if post:
        in_specs.append(pl.BlockSpec((1, d), lambda i, j: (0, 0)))
        args.append(g_final.reshape(1, d))
    return pl.pallas_call(
        functools.partial(_ffn_kernel, n_f=n_f, post_norm=post),
        grid=(m // tm, n_f),
        in_specs=in_specs,
        out_specs=pl.BlockSpec((tm, d), lambda i, j: (i, 0)),
        out_shape=jax.ShapeDtypeStruct((m, d), F32),
        scratch_shapes=[pltpu.VMEM((tm, d), BF16), pltpu.VMEM((tm, d), F32)],
        compiler_params=_cparams("parallel", "arbitrary"),
        name="half_ffn",
    )(*args)


def _rope(seg, cos_t, sin_t):
    w = seg.shape[1]
    reps = w // LANES
    c = jnp.concatenate([cos_t] * reps, axis=1) if reps > 1 else cos_t
    s = jnp.concatenate([sin_t] * reps, axis=1) if reps > 1 else sin_t
    lane = lax.broadcasted_iota(jnp.int32, seg.shape, 1)
    first = (lane % HEAD_DIM) < (HEAD_DIM // 2)
    rot = jnp.where(first, pltpu.roll(seg, w - HEAD_DIM // 2, 1), pltpu.roll(seg, HEAD_DIM // 2, 1))
    return seg * c + rot * s


def _store_heads(ref, seg, scale=None, ones=False):
    if ones:
        lane = lax.broadcasted_iota(jnp.int32, (seg.shape[0], HEAD_DIM), 1)
        tail = (lane == 0).astype(seg.dtype)
    for h in range(seg.shape[1] // HEAD_DIM):
        piece = seg[:, h * HEAD_DIM:(h + 1) * HEAD_DIM]
        if scale is not None:
            piece = piece * scale
        if ones:
            piece = jnp.concatenate([piece, tail], axis=1)
        ref[h] = piece.astype(ref.dtype)


def _proj_even_kernel(x_ref, g_ref, w_ref, kv_ref, xb_ref, gb_ref, *outs, heads):
    hm = HALF_MIX
    xn = _rms(x_ref[...], g_ref[...]).astype(BF16)
    y = _dot(xn, w_ref[...])
    kv_ref[...] = y[:, hm:3 * hm]
    xb_ref[...] = y[:, 3 * hm:4 * hm]
    gb_ref[...] = y[:, 4 * hm:5 * hm]
    if heads:
        qh_ref, kh_ref, vh_ref = outs
        _store_heads(qh_ref, y[:, 0:hm], SCALE)
        _store_heads(kh_ref, y[:, hm:2 * hm])
        _store_heads(vh_ref, y[:, 2 * hm:3 * hm])
    else:
        (q_ref,) = outs
        q_ref[...] = y[:, 0:hm]


def _proj_even(x, g, w, heads):
    m, d = x.shape
    n = w.shape[1]
    hm = HALF_MIX
    tm = _tile(m, (512, 256, 128))
    row = lambda i: (i, 0)
    out_shape = [jax.ShapeDtypeStruct((m, 2 * hm), F32), jax.ShapeDtypeStruct((m, hm), F32),
                 jax.ShapeDtypeStruct((m, hm), F32)]
    out_specs = [pl.BlockSpec((tm, 2 * hm), row), pl.BlockSpec((tm, hm), row), pl.BlockSpec((tm, hm), row)]
    if heads:
        for _ in range(3):
            out_shape.append(jax.ShapeDtypeStruct((N_HEADS, m, HEAD_DIM), BF16))
            out_specs.append(pl.BlockSpec((N_HEADS, tm, HEAD_DIM), lambda i: (0, i, 0)))
    else:
        out_shape.append(jax.ShapeDtypeStruct((m, hm), F32))
        out_specs.append(pl.BlockSpec((tm, hm), row))
    return pl.pallas_call(
        functools.partial(_proj_even_kernel, heads=heads),
        grid=(m // tm,),
        in_specs=[pl.BlockSpec((tm, d), row), pl.BlockSpec((1, d), lambda i: (0, 0)),
                  pl.BlockSpec((d, n), lambda i: (0, 0))],
        out_specs=out_specs, out_shape=out_shape,
        compiler_params=_cparams("parallel"),
        name="proj_even",
    )(x, g.reshape(1, d), w)


_O_QC, _O_KC, _O_VC, _O_KS, _O_VS, _O_KW, _O_VW, _O_QD, _O_KD, _O_VD, _O_GT, _O_END = (
    0, 512, 640, 768, 896, 1024, 1152, 1280, 1792, 2304, 2816, 3072)


def _proj_odd_kernel(x_ref, g_ref, w_ref, cos_ref, sin_ref, c_ref, w_out_ref, d_ref, gt_ref, *outs, heads):
    xn = _rms(x_ref[...], g_ref[...]).astype(BF16)
    y = _dot(xn, w_ref[...])
    cos_t = cos_ref[...]
    sin_t = sin_ref[...]
    qc = y[:, _O_QC:_O_KC]
    qc_rot = _rope(qc, cos_t, sin_t)
    ks_rot = _rope(y[:, _O_KS:_O_VS], cos_t, sin_t)
    kw_rot = _rope(y[:, _O_KW:_O_VW], cos_t, sin_t)
    qd_rot = _rope(y[:, _O_QD:_O_KD], cos_t, sin_t)
    kd_rot = _rope(y[:, _O_KD:_O_VD], cos_t, sin_t)
    vs = y[:, _O_VS:_O_KW]
    vw = y[:, _O_VW:_O_QD]
    vd = y[:, _O_VD:_O_GT]
    c_ref[:, 0:256] = y[:, _O_KC:_O_KS]
    c_ref[:, 256:384] = ks_rot
    c_ref[:, 384:512] = vs
    w_out_ref[:, 0:128] = kw_rot
    w_out_ref[:, 128:256] = vw
    d_ref[:, 0:512] = kd_rot
    d_ref[:, 512:1024] = vd
    gt_ref[...] = jax.nn.sigmoid(y[:, _O_GT:_O_END])
    if heads:
        qraw_h, qrot_h, ks_h, vs_h, kw_h, vw_h, qd_h, kd_h, vd_h = outs
        _store_heads(qraw_h, qc, SCALE)
        _store_heads(qrot_h, qc_rot, SCALE)
        _store_heads(ks_h, ks_rot)
        _store_heads(vs_h, vs, ones=True)
        _store_heads(kw_h, kw_rot)
        _store_heads(vw_h, vw, ones=True)
        _store_heads(qd_h, qd_rot, SCALE)
        _store_heads(kd_h, kd_rot)
        _store_heads(vd_h, vd, ones=True)
    else:
        qraw_ref, qrot_ref, qd_ref = outs
        qraw_ref[...] = qc
        qrot_ref[...] = qc_rot
        qd_ref[...] = qd_rot


def _proj_odd(x, g, w, cos_t, sin_t, heads):
    m, d = x.shape
    n = w.shape[1]
    tm = _tile(m, (512, 256, 128))
    period = cos_t.shape[0] // tm
    row = lambda i: (i, 0)
    tab = lambda i: (i % period, 0)
    out_shape = [jax.ShapeDtypeStruct((m, 512), F32), jax.ShapeDtypeStruct((m, 256), F32),
                 jax.ShapeDtypeStruct((m, 1024), F32), jax.ShapeDtypeStruct((m, 256), F32)]
    out_specs = [pl.BlockSpec((tm, 512), row), pl.BlockSpec((tm, 256), row), pl.BlockSpec((tm, 1024), row),
                 pl.BlockSpec((tm, 256), row)]
    if heads:
        wide = (False, False, False, True, False, True, False, False, True)
        for nh, wd in zip((N_HEADS, N_HEADS, N_KV_C, N_KV_C, N_KV_C, N_KV_C, N_HEADS, N_HEADS, N_HEADS), wide):
            width = LANES if wd else HEAD_DIM
            out_shape.append(jax.ShapeDtypeStruct((nh, m, width), BF16))
            out_specs.append(pl.BlockSpec((nh, tm, width), lambda i: (0, i, 0)))
    else:
        for _ in range(3):
            out_shape.append(jax.ShapeDtypeStruct((m, HALF_MIX), F32))
            out_specs.append(pl.BlockSpec((tm, HALF_MIX), row))
    return pl.pallas_call(
        functools.partial(_proj_odd_kernel, heads=heads),
        grid=(m // tm,),
        in_specs=[pl.BlockSpec((tm, d), row), pl.BlockSpec((1, d), lambda i: (0, 0)),
                  pl.BlockSpec((d, n), lambda i: (0, 0)),
                  pl.BlockSpec((tm, LANES), tab), pl.BlockSpec((tm, LANES), tab)],
        out_specs=out_specs, out_shape=out_shape,
        compiler_params=_cparams("parallel"),
        name="proj_odd",
    )(x, g.reshape(1, d), w, cos_t, sin_t)


def _outproj_kernel(x_ref, a_ref, b_ref, w_ref, o_ref, *, heads):
    hm = HALF_MIX
    if heads:
        a = jnp.concatenate([a_ref[h] for h in range(N_HEADS)], axis=1)
        b = b_ref[...] if len(b_ref.shape) == 2 else jnp.concatenate([b_ref[h] for h in range(N_HEADS)], axis=1)
    else:
        a = a_ref[...]
        b = b_ref[...]
    o_ref[...] = x_ref[...] + _dot(a, w_ref[0:hm, :]) + _dot(b, w_ref[hm:2 * hm, :])


def _outproj(x, a, b, w):
    m, d = x.shape
    tm = _tile(m, (512, 256, 128))
    row = lambda i: (i, 0)

    def spec(t):
        if t.ndim == 2:
            return pl.BlockSpec((tm, t.shape[1]), row)
        return pl.BlockSpec((t.shape[0], tm, t.shape[2]), lambda i: (0, i, 0))

    return pl.pallas_call(
        functools.partial(_outproj_kernel, heads=a.ndim == 3),
        grid=(m // tm,),
        in_specs=[pl.BlockSpec((tm, d), row), spec(a), spec(b), pl.BlockSpec(w.shape, lambda i: (0, 0))],
        out_specs=pl.BlockSpec((tm, d), row),
        out_shape=jax.ShapeDtypeStruct((m, d), F32),
        compiler_params=_cparams("parallel"),
        name="outproj",
    )(x, a, b, w)


def _sb_tile(q, k, v, t_incl, carry, acc, mask):
    z = _dot_nt(q, k)
    lk = -_softplus(z)
    if mask is not None:
        lk = jnp.where(mask, lk, 0.0)
    incl = _split_dot(lk, t_incl)
    w = jnp.exp(z + incl + carry)
    if mask is not None:
        w = jnp.where(mask, w, 0.0)
    acc = acc + _dot(w.astype(BF16), v)
    return carry + incl[:, 0:1], acc


def _tri_incl(n):
    r = lax.broadcasted_iota(jnp.int32, (n, n), 0)
    c = lax.broadcasted_iota(jnp.int32, (n, n), 1)
    return (r >= c).astype(BF16)


def _sb_prompt_kernel(q_ref, k_ref, v_ref, o_ref, *, tq):
    i = pl.program_id(2)
    q = q_ref[0]
    t_incl = _tri_incl(tq)
    r = lax.broadcasted_iota(jnp.int32, (tq, tq), 0)
    c = lax.broadcasted_iota(jnp.int32, (tq, tq), 1)

    def kv(kt):
        start = pl.multiple_of(kt * tq, tq)
        return k_ref[0, pl.ds(start, tq), :], v_ref[0, pl.ds(start, tq), :]

    k0, v0 = kv(i)
    carry, acc = _sb_tile(q, k0, v0, t_incl, jnp.zeros((tq, 1), F32), jnp.zeros((tq, HEAD_DIM), F32), c < r)

    def cond(s):
        kt, carry, _ = s
        return jnp.logical_and(kt >= 0, jnp.max(carry) > -SB_CUTOFF)

    def body(s):
        kt, carry, acc = s
        kk, vv = kv(kt)
        carry, acc = _sb_tile(q, kk, vv, t_incl, carry, acc, None)
        return kt - 1, carry, acc

    _, _, acc = lax.while_loop(cond, body, (i - 1, carry, acc))
    o_ref[0] = acc.astype(o_ref.dtype)


def _sb_prompt(qh, kh, vh, n_seq, s_len):
    tq = _tile(s_len, (256, 128))
    nq = s_len // tq
    return pl.pallas_call(
        functools.partial(_sb_prompt_kernel, tq=tq),
        grid=(n_seq, N_HEADS, nq),
        in_specs=[pl.BlockSpec((1, tq, HEAD_DIM), lambda b, h, i: (h, b * nq + i, 0)),
                  pl.BlockSpec((1, s_len, HEAD_DIM), lambda b, h, i: (h, b, 0)),
                  pl.BlockSpec((1, s_len, HEAD_DIM), lambda b, h, i: (h, b, 0))],
        out_specs=pl.BlockSpec((1, tq, HEAD_DIM), lambda b, h, i: (h, b * nq + i, 0)),
        out_shape=jax.ShapeDtypeStruct(qh.shape, BF16),
        compiler_params=_cparams("parallel", "parallel", "arbitrary"),
        name="sb_prompt",
    )(qh, kh, vh)


def _lru_kernel(xb_ref, gb_ref, h0_ref, buf0_ref, cw_ref, cb_ref, wr_ref, br_ref, wi_ref, bi_ref, lam_ref,
                g_ref, hl_ref, bn_ref, xpad, a_s, u_s, hs_s, h_s, *, tc, n_t):
    j = pl.program_id(1)
    keep = CONV_WIDTH - 1

    @pl.when(j == 0)
    def _():
        xpad[8 - keep:8, :] = buf0_ref[0]
        h_s[...] = h0_ref[0]

    @pl.when(j > 0)
    def _():
        xpad[8 - keep:8, :] = xpad[8 + tc - keep:8 + tc, :]

    x = xb_ref[0]
    xpad[8:8 + tc, :] = x
    xc = cb_ref[...] + x * cw_ref[keep:keep + 1, :]
    for d in range(1, CONV_WIDTH):
        xc = xc + xpad[8 - d:8 - d + tc, :] * cw_ref[keep - d:keep - d + 1, :]
    xcb = xc.astype(BF16)
    r = jax.nn.sigmoid(_dot(xcb, wr_ref[...]) + br_ref[...])
    gi = jax.nn.sigmoid(_dot(xcb, wi_ref[...]) + bi_ref[...])
    log_a = -LRU_C * r * _softplus(-lam_ref[...])
    a_s[...] = jnp.exp(log_a)
    u_s[...] = jnp.sqrt(_one_minus_exp(2.0 * log_a)) * (gi * xc)

    def step(t, h):
        h = a_s[pl.ds(t, 1), :] * h + u_s[pl.ds(t, 1), :]
        hs_s[pl.ds(t, 1), :] = h
        return h

    h = lax.fori_loop(0, tc, step, h_s[...], unroll=min(tc, 8))
    h_s[...] = h
    g_ref[0] = (jax.nn.gelu(gb_ref[0]) * hs_s[...]).astype(g_ref.dtype)

    @pl.when(j == n_t - 1)
    def _():
        hl_ref[0] = h
        bn_ref[0] = xpad[8 + tc - keep:8 + tc, :]


def _rg_lru(xb, gb, h0, buf0, conv_w, conv_b, wr_bd, b_r, wi_bd, b_i, lam):
    n, t, w = xb.shape
    keep = CONV_WIDTH - 1
    assert t >= keep
    tc = _tile(t, (512, 256, 128))
    n_t = t // tc
    seq = lambda b, j: (b, j, 0)
    one = lambda b, j: (b, 0, 0)
    const = lambda b, j: (0, 0)
    vec = pl.BlockSpec((1, w), const)
    g, hl, bn = pl.pallas_call(
        functools.partial(_lru_kernel, tc=tc, n_t=n_t),
        grid=(n, n_t),
        in_specs=[pl.BlockSpec((1, tc, w), seq), pl.BlockSpec((1, tc, w), seq),
                  pl.BlockSpec((1, 1, w), one), pl.BlockSpec((1, keep, w), one),
                  pl.BlockSpec((CONV_WIDTH, w), const), vec,
                  pl.BlockSpec((w, w), const), vec, pl.BlockSpec((w, w), const), vec, vec],
        out_specs=[pl.BlockSpec((1, tc, w), seq), pl.BlockSpec((1, 1, w), one), pl.BlockSpec((1, keep, w), one)],
        out_shape=[jax.ShapeDtypeStruct((n, t, w), BF16), jax.ShapeDtypeStruct((n, 1, w), F32),
                   jax.ShapeDtypeStruct((n, keep, w), F32)],
        scratch_shapes=[pltpu.VMEM((tc + 8, w), F32), pltpu.VMEM((tc, w), F32), pltpu.VMEM((tc, w), F32),
                        pltpu.VMEM((tc, w), F32), pltpu.VMEM((1, w), F32)],
        compiler_params=_cparams("parallel", "arbitrary"),
        name="rg_lru",
    )(xb, gb, h0.reshape(n, 1, w), buf0, conv_w, conv_b.reshape(1, w), wr_bd, b_r.reshape(1, w), wi_bd,
      b_i.reshape(1, w), lam.reshape(1, w))
    return g, hl.reshape(n, w), bn


def _cmp_ab_kernel(*refs, n_in, rows, n_prefetch):
    refs = refs[n_prefetch:]
    k_refs = refs[:n_in]
    v_refs = refs[n_in:2 * n_in]
    pe_ak, pe_bk, pe_av, pe_bv, w_ak, w_bk, w_av, w_bv, o_ref = refs[2 * n_in:]
    nch = rows // CMP_STRIDE

    def chunks(in_refs):
        per = [jnp.concatenate([r[pl.ds(p, nch, stride=CMP_STRIDE), :] for p in range(CMP_STRIDE)], axis=1)
               for r in in_refs]
        return per[0] if n_in == 1 else jnp.concatenate(per, axis=0)

    xk = chunks(k_refs)
    xv = chunks(v_refs)
    o_ref[0, :, 0:128] = _dot((xk + pe_ak[...]).astype(BF16), w_ak[...])
    o_ref[0, :, 128:256] = _dot((xk + pe_bk[...]).astype(BF16), w_bk[...])
    o_ref[0, :, 256:384] = _dot((xv + pe_av[...]).astype(BF16), w_av[...])
    o_ref[0, :, 384:512] = _dot((xv + pe_bv[...]).astype(BF16), w_bv[...])


def _cmp_weights(pe, w1):
    hid = w1.shape[-1]
    eye = jnp.eye(N_KV_C, dtype=F32)
    out = []
    for half in range(CMP_LEN // CMP_STRIDE):
        sl = slice(half * CMP_STRIDE, (half + 1) * CMP_STRIDE)
        pe_row = jnp.broadcast_to(pe[sl][:, None, :], (CMP_STRIDE, N_KV_C, HEAD_DIM)).reshape(1, -1)
        wbd = jnp.einsum('pdh,gk->pgdkh', w1[sl], eye).reshape(CMP_STRIDE * N_KV_C * HEAD_DIM, N_KV_C * hid)
        out.append((pe_row, wbd.astype(BF16)))
    return out


def _cmp_ab(in_arrays, k_specs, v_specs, grid, out_rows, n_seq, n_chunks, rows, cmp_k, cmp_v, num_prefetch,
            prefetch):
    in_specs = list(k_specs) + list(v_specs)
    (pe_ak, w_ak), (pe_bk, w_bk) = cmp_k
    (pe_av, w_av), (pe_bv, w_bv) = cmp_v
    nd = len(grid)
    const = lambda *a: (0, 0)
    pes = [pe_ak, pe_bk, pe_av, pe_bv]
    ws = [w_ak, w_bk, w_av, w_bv]
    specs = list(in_specs) + [pl.BlockSpec(p.shape, const) for p in pes] + [pl.BlockSpec(w.shape, const) for w in ws]
    gs = pltpu.PrefetchScalarGridSpec(
        num_scalar_prefetch=num_prefetch, grid=grid, in_specs=specs,
        out_specs=pl.BlockSpec((1, out_rows, 512), lambda b, j, *a: (b, j, 0)))
    return pl.pallas_call(
        functools.partial(_cmp_ab_kernel, n_in=len(in_arrays), rows=rows, n_prefetch=num_prefetch),
        grid_spec=gs,
        out_shape=jax.ShapeDtypeStruct((n_seq, n_chunks, 512), F32),
        compiler_params=_cparams(*(["parallel"] + ["arbitrary"] * (nd - 1))),
        name="cmp_ab",
    )(*prefetch, *in_arrays, *in_arrays, *pes, *ws)


def _cmp_finish_kernel(ab_ref, w2k_ref, w2v_ref, kf_ref, vf_ref, kg_ref, vg_ref, *, nc):
    n = nc - 1
    hk = jax.nn.gelu(ab_ref[0, 0:n, 0:128] + ab_ref[0, 1:nc, 128:256]).astype(BF16)
    hv = jax.nn.gelu(ab_ref[0, 0:n, 256:384] + ab_ref[0, 1:nc, 384:512]).astype(BF16)
    kc = _dot(hk, w2k_ref[...]).astype(BF16)
    vc = _dot(hv, w2v_ref[...]).astype(BF16)
    zero = jnp.zeros((1, LANES), BF16)
    for ref_f, ref_g, val in ((kf_ref, kg_ref, kc), (vf_ref, vg_ref, vc)):
        full = jnp.concatenate([val, zero], axis=0)
        ref_f[0] = full
        for g in range(N_KV_C):
            ref_g[0, g] = full[:, g * HEAD_DIM:(g + 1) * HEAD_DIM]


def _cmp_finish(ab, w2k_bd, w2v_bd):
    n, nc, _ = ab.shape
    flat = jax.ShapeDtypeStruct((n, nc, LANES), BF16)
    grp = jax.ShapeDtypeStruct((n, N_KV_C, nc, HEAD_DIM), BF16)
    fspec = pl.BlockSpec((1, nc, LANES), lambda b: (b, 0, 0))
    gspec = pl.BlockSpec((1, N_KV_C, nc, HEAD_DIM), lambda b: (b, 0, 0, 0))
    return pl.pallas_call(
        functools.partial(_cmp_finish_kernel, nc=nc),
        grid=(n,),
        in_specs=[pl.BlockSpec((1, nc, 512), lambda b: (b, 0, 0)),
                  pl.BlockSpec((LANES, LANES), lambda b: (0, 0)), pl.BlockSpec((LANES, LANES), lambda b: (0, 0))],
        out_specs=[fspec, fspec, gspec, gspec], out_shape=[flat, flat, grp, grp],
        compiler_params=_cparams("parallel"),
        name="cmp_finish",
    )(ab, w2k_bd, w2v_bd)


def _overlap_t(nb, nc):
    b = lax.broadcasted_iota(jnp.int32, (nb, nc), 0)
    c = lax.broadcasted_iota(jnp.int32, (nb, nc), 1)
    return ((c * CMP_STRIDE < (b + 1) * SLC_BLOCK) & (c * CMP_STRIDE + CMP_LEN > b * SLC_BLOCK)).astype(F32)


def _nsa_prompt_kernel(qraw_ref, qrot_ref, gt_ref, kcc_ref, vcc_ref, ks_ref, vs_ref, kw_ref, vw_ref, o_ref,
                       *state, tq, nc, nsb, ck):
    i = pl.program_id(2)
    q0 = i * tq
    rows = HPG_C * tq
    nbp = LANES
    qpos3 = q0 + lax.broadcasted_iota(jnp.int32, (1, tq, 1), 1)

    qraw = qraw_ref[...].reshape(rows, HEAD_DIM)
    sc = _dot_nt(qraw, kcc_ref[0, 0]).reshape(HPG_C, tq, nc)
    cidx = lax.broadcasted_iota(jnp.int32, (1, 1, nc), 2)
    mask_c = (cidx * CMP_STRIDE + CMP_LEN - 1 <= qpos3) & (cidx < nc - 1)
    pc = _masked_softmax(sc, mask_c)
    o_c = _dot(pc.reshape(rows, nc).astype(BF16), vcc_ref[0, 0])

    psum = jnp.sum(pc, axis=0)
    imp_t = _dot_nt(_overlap_t(nsb, nc), psum, precision=HIGHEST)
    blk = lax.broadcasted_iota(jnp.int32, (nsb, tq), 0)
    own = (q0 + lax.broadcasted_iota(jnp.int32, (nsb, tq), 1)) // SLC_BLOCK
    imp_t = jnp.where(blk == own, BIG, jnp.where(blk < own, imp_t, NEG))
    cnt = jnp.zeros((nsb, tq), jnp.int32)
    for b2 in range(nsb):
        rowv = imp_t[b2:b2 + 1, :]
        ahead = (rowv > imp_t) | ((rowv == imp_t) & (b2 < blk))
        cnt = cnt + ahead.astype(jnp.int32)
    sel_t = ((cnt < N_SLC) & (blk <= own)).astype(F32)
    if nsb < nbp:
        sel_t = jnp.concatenate([sel_t, jnp.zeros((nbp - nsb, tq), F32)], axis=0)
    sel_bias = jnp.where(sel_t.T > 0.5, 0.0, NEG).astype(BF16)
    m_refs = state[:HPG_C]
    acc_refs = state[HPG_C:]
    qs = [qrot_ref[j] for j in range(HPG_C)]

    def reset():
        for ref in m_refs:
            ref[...] = jnp.full(ref.shape, NEG, F32)
        for ref in acc_refs:
            ref[...] = jnp.zeros_like(ref)

    def update(k, v1, bias):
        _flash_multi(qs, [k] * HPG_C, [v1] * HPG_C, [bias] * HPG_C, m_refs, acc_refs)

    eb = lax.broadcasted_iota(jnp.int32, (nbp, ck), 0)
    ej = lax.broadcasted_iota(jnp.int32, (nbp, ck), 1)

    def slc_bias(start):
        expand = ((start + ej) // SLC_BLOCK == eb).astype(BF16)
        return _dot(sel_bias, expand)

    reset()
    c_own = q0 // ck
    start = pl.multiple_of(c_own * ck, ck)
    row = lax.broadcasted_iota(jnp.int32, (tq, ck), 0)
    lane = lax.broadcasted_iota(jnp.int32, (tq, ck), 1)
    update(ks_ref[0, pl.ds(start, ck), :], vs_ref[0, pl.ds(start, ck), :],
           jnp.where(start + lane <= q0 + row, slc_bias(start), NEG))

    def slc_step(cc, carry):
        st = pl.multiple_of(cc * ck, ck)
        update(ks_ref[0, pl.ds(st, ck), :], vs_ref[0, pl.ds(st, ck), :], slc_bias(st))
        return carry

    lax.fori_loop(0, c_own, slc_step, 0)
    o_s = [_flash_result(acc_refs[j][...]) for j in range(HPG_C)]

    reset()
    wk = WINDOW + tq
    start_w = pl.multiple_of(jnp.maximum(q0 - WINDOW, 0), tq)
    dist = (q0 + lax.broadcasted_iota(jnp.int32, (tq, wk), 0)) - (start_w + lax.broadcasted_iota(jnp.int32, (tq, wk), 1))
    update(kw_ref[0, pl.ds(start_w, wk), :], vw_ref[0, pl.ds(start_w, wk), :],
           jnp.where((dist >= 0) & (dist <= WINDOW), 0.0, NEG))

    gt = gt_ref[...]
    for j in range(HPG_C):
        sl = slice(j * tq, (j + 1) * tq)
        g_c = gt[:, j:j + 1]
        g_s = gt[:, HPG_C + j:HPG_C + j + 1]
        g_w = gt[:, 2 * HPG_C + j:2 * HPG_C + j + 1]
        o_ref[j] = (o_c[sl] * g_c + o_s[j] * g_s + _flash_result(acc_refs[j][...]) * g_w).astype(o_ref.dtype)


def _nsa_prompt(qraw_h, qrot_h, gates, kcc_g, vcc_g, ks_h, vs_h, kw_h, vw_h, n_seq, s_len):
    tq = 128
    nq = s_len // tq
    nc = kcc_g.shape[2]
    nsb = -(-s_len // SLC_BLOCK)
    ck = _tile(s_len, (512, 256, 128))
    assert s_len % tq == 0 and nsb <= LANES and WINDOW % tq == 0 and s_len >= WINDOW + tq and ck % tq == 0
    qspec = pl.BlockSpec((HPG_C, tq, HEAD_DIM), lambda b, g, i: (g, b * nq + i, 0))
    cspec = pl.BlockSpec((1, 1, nc, HEAD_DIM), lambda b, g, i: (b, g, 0, 0))
    kspec = pl.BlockSpec((1, s_len, HEAD_DIM), lambda b, g, i: (g, b, 0))
    vspec = pl.BlockSpec((1, s_len, LANES), lambda b, g, i: (g, b, 0))
    return pl.pallas_call(
        functools.partial(_nsa_prompt_kernel, tq=tq, nc=nc, nsb=nsb, ck=ck),
        grid=(n_seq, N_KV_C, nq),
        in_specs=[qspec, qspec, pl.BlockSpec((tq, LANES), lambda b, g, i: (b * nq + i, g)),
                  cspec, cspec, kspec, vspec, kspec, vspec],
        out_specs=qspec,
        out_shape=jax.ShapeDtypeStruct(qraw_h.shape, BF16),
        scratch_shapes=[pltpu.VMEM((tq, 1), F32)] * HPG_C + [pltpu.VMEM((tq, LANES), F32)] * HPG_C,
        compiler_params=_cparams("parallel", "parallel", "arbitrary"),
        name="nsa_prompt",
    )(qraw_h, qrot_h, gates, kcc_g, vcc_g, ks_h, vs_h, kw_h, vw_h)


def _block_mean_kernel(x_ref, o_ref, *, nb):
    x = x_ref[...]
    m = jnp.mean(x.reshape(nb, MOBA_BLOCK, x.shape[1]), axis=1)
    for h in range(N_HEADS):
        o_ref[h] = m[:, h * HEAD_DIM:(h + 1) * HEAD_DIM]


def _block_mean(d_rows):
    m = d_rows.shape[0]
    nb_total = m // MOBA_BLOCK
    nb = _tile(nb_total, (8,))
    return pl.pallas_call(
        functools.partial(_block_mean_kernel, nb=nb),
        grid=(nb_total // nb,),
        in_specs=[pl.BlockSpec((nb * MOBA_BLOCK, HALF_MIX), lambda i: (i, 0))],
        out_specs=pl.BlockSpec((N_HEADS, nb, HEAD_DIM), lambda i: (0, i, 0)),
        out_shape=jax.ShapeDtypeStruct((N_HEADS, nb_total, HEAD_DIM), F32),
        compiler_params=_cparams("parallel"),
        name="moba_block_mean",
    )(d_rows)


def _moba_prompt_kernel(q_ref, km_ref, k_ref, v_ref, o_ref, *state, tq, nb, sub, hb):
    i = pl.program_id(2)
    nh = tq // sub
    nchain = hb * nh
    m_refs = state[:nchain]
    acc_refs = state[nchain:]
    nbp = LANES

    blk_t = lax.broadcasted_iota(jnp.int32, (nb, tq), 0)
    sel_bias = []
    for hh in range(hb):
        gate_t = _dot_nt(km_ref[hh], q_ref[hh].astype(F32) * (1.0 / SCALE), precision=HIGHEST)
        gate_t = jnp.where(blk_t < i, gate_t, NEG)
        cnt = jnp.zeros((nb, tq), jnp.int32)
        for b2 in range(nb):
            rowv = gate_t[b2:b2 + 1, :]
            cnt = cnt + ((rowv > gate_t) | ((rowv == gate_t) & (b2 < blk_t))).astype(jnp.int32)
        sel_t = ((cnt < MOBA_TOPK) & (blk_t < i)).astype(F32)
        sel_t = jnp.concatenate([sel_t, jnp.zeros((nbp - nb, tq), F32)], axis=0)
        sel_bias.append(jnp.where(sel_t.T > 0.5, 0.0, NEG).astype(BF16))

    qs = [q_ref[ci // nh, (ci % nh) * sub:(ci % nh + 1) * sub, :] for ci in range(nchain)]

    def update(start, width, biases):
        ks = [k_ref[hh, pl.ds(start, width), :] for hh in range(hb)]
        v1s = [v_ref[hh, pl.ds(start, width), :] for hh in range(hb)]
        _flash_multi(qs, [ks[ci // nh] for ci in range(nchain)], [v1s[ci // nh] for ci in range(nchain)], biases,
                     m_refs, acc_refs)

    def block_bias(expand):
        return [_dot(sel_bias[ci // nh][(ci % nh) * sub:(ci % nh + 1) * sub, :], expand) for ci in range(nchain)]

    for ref in m_refs:
        ref[...] = jnp.full(ref.shape, NEG, F32)
    for ref in acc_refs:
        ref[...] = jnp.zeros_like(ref)

    r = lax.broadcasted_iota(jnp.int32, (sub, tq), 0)
    c = lax.broadcasted_iota(jnp.int32, (sub, tq), 1)
    update(pl.multiple_of(i * tq, tq), tq, [jnp.where(c <= r + (ci % nh) * sub, 0.0, NEG) for ci in range(nchain)])

    pk = 2 * tq
    eb2 = lax.broadcasted_iota(jnp.int32, (nbp, pk), 0)
    ej2 = lax.broadcasted_iota(jnp.int32, (nbp, pk), 1)

    def pair_step(cc, carry):
        update(pl.multiple_of(cc * pk, pk), pk, block_bias((eb2 == 2 * cc + ej2 // tq).astype(BF16)))
        return carry

    lax.fori_loop(0, i // 2, pair_step, 0)

    @pl.when(i % 2 == 1)
    def _():
        eb1 = lax.broadcasted_iota(jnp.int32, (nbp, tq), 0)
        update(pl.multiple_of((i - 1) * tq, tq), tq, block_bias((eb1 == i - 1).astype(BF16)))

    for ci in range(nchain):
        o_ref[ci // nh, (ci % nh) * sub:(ci % nh + 1) * sub, :] = _flash_result(acc_refs[ci][...]).astype(o_ref.dtype)


def _moba_prompt(qd_h, kmean_h, kd_h, vd_h, n_seq, s_len):
    tq = MOBA_BLOCK
    assert s_len % tq == 0
    nq = s_len // tq
    assert nq <= LANES
    sub = 128
    hb = 4
    nchain = hb * (tq // sub)
    kspec = pl.BlockSpec((hb, s_len, HEAD_DIM), lambda b, h, i: (h, b, 0))
    vspec = pl.BlockSpec((hb, s_len, LANES), lambda b, h, i: (h, b, 0))
    qspec = pl.BlockSpec((hb, tq, HEAD_DIM), lambda b, h, i: (h, b * nq + i, 0))
    return pl.pallas_call(
        functools.partial(_moba_prompt_kernel, tq=tq, nb=nq, sub=sub, hb=hb),
        grid=(n_seq, N_HEADS // hb, nq),
        in_specs=[qspec, pl.BlockSpec((hb, nq, HEAD_DIM), lambda b, h, i: (h, b, 0)), kspec, vspec],
        out_specs=qspec,
        out_shape=jax.ShapeDtypeStruct(qd_h.shape, BF16),
        scratch_shapes=[pltpu.VMEM((sub, 1), F32)] * nchain + [pltpu.VMEM((sub, LANES), F32)] * nchain,
        compiler_params=_cparams("parallel", "parallel", "arbitrary"),
        name="moba_prompt",
    )(qd_h, kmean_h, kd_h, vd_h)


def _page_specs(n_slots, n_pages, width, col_block, base, reverse):
    n_steps = n_pages // n_slots

    def make(s):
        if reverse:
            return lambda b, j, pt: (base + pt[b, (n_steps - 1 - j) * n_slots + (n_slots - 1 - s)], 0, col_block)
        return lambda b, j, pt: (base + pt[b, j * n_slots + s], 0, col_block)

    return [pl.BlockSpec((None, PAGE_SIZE, width), make(s)) for s in range(n_slots)]


def _head_diag(rows, width, per):
    r = lax.broadcasted_iota(jnp.int32, (rows, width), 0)
    c = lax.broadcasted_iota(jnp.int32, (rows, width), 1)
    return ((r % N_HEADS) * HEAD_DIM // per) == (c // HEAD_DIM * HEAD_DIM // per)


def _fold_heads(acc, t):
    kept = jnp.where(_head_diag(acc.shape[0], acc.shape[1], HEAD_DIM), acc, 0.0)
    return jnp.sum(kept.reshape(t, N_HEADS, acc.shape[1]), axis=1)


def _sb_sample_kernel(pt_ref, q_ref, kn_ref, vn_ref, *refs, n_slots, n_steps, t, pad):
    page_refs = refs[:n_slots]
    o_ref, carry_s, acc_s = refs[n_slots:]
    j = pl.program_id(1)
    rows = t * N_HEADS
    q = q_ref[0]

    @pl.when(j == 0)
    def _():
        r = lax.broadcasted_iota(jnp.int32, (rows, pad), 0) // N_HEADS
        c = lax.broadcasted_iota(jnp.int32, (rows, pad), 1)
        carry, acc = _sb_tile(q, kn_ref[0], vn_ref[0], _tri_incl(pad), jnp.zeros((rows, 1), F32),
                              jnp.zeros((rows, HALF_MIX), F32), (c < r) & (c < t))
        carry_s[...] = carry
        acc_s[...] = acc

    t_incl = _tri_incl(PAGE_SIZE)
    for s in range(n_slots):
        @pl.when(jnp.max(carry_s[...]) > -SB_CUTOFF)
        def _():
            page = page_refs[s][...]
            carry, acc = _sb_tile(q, page[:, 0:HALF_MIX].astype(BF16), page[:, HALF_MIX:].astype(BF16), t_incl,
                                  carry_s[...], acc_s[...], None)
            carry_s[...] = carry
            acc_s[...] = acc

    @pl.when(j == n_steps - 1)
    def _():
        o_ref[0] = _fold_heads(acc_s[...], t)


def _sb_sample(qbd, k_new, v_new, pool, base, page_table):
    n, rows, _ = qbd.shape
    t = rows // N_HEADS
    pad = k_new.shape[1]
    n_pages = page_table.shape[1]
    n_slots = _tile(n_pages, (8, 4, 2))
    n_steps = n_pages // n_slots
    seq = lambda b, j, pt: (b, 0, 0)
    gs = pltpu.PrefetchScalarGridSpec(
        num_scalar_prefetch=1, grid=(n, n_steps),
        in_specs=[pl.BlockSpec((1, rows, HALF_MIX), seq), pl.BlockSpec((1, pad, HALF_MIX), seq),
                  pl.BlockSpec((1, pad, HALF_MIX), seq)]
        + _page_specs(n_slots, n_pages, 2 * HALF_MIX, 0, base, True),
        out_specs=pl.BlockSpec((1, t, HALF_MIX), seq),
        scratch_shapes=[pltpu.VMEM((rows, 1), F32), pltpu.VMEM((rows, HALF_MIX), F32)])
    return pl.pallas_call(
        functools.partial(_sb_sample_kernel, n_slots=n_slots, n_steps=n_steps, t=t, pad=pad),
        grid_spec=gs,
        out_shape=jax.ShapeDtypeStruct((n, t, HALF_MIX), F32),
        compiler_params=_cparams("parallel", "arbitrary"),
        name="sb_sample",
    )(page_table, qbd, k_new, v_new, *([pool] * n_slots))


def _nsa_sample_head_kernel(qraw_ref, qrot_ref, kcc_ref, vcc_ref, win_ref, wn_ref, oc_ref, ow_ref, sel_ref,
                            *, t, nc, n_past, nbp, pad):
    rows = t * N_HEADS
    wb = win_ref.shape[1]
    tok = lax.broadcasted_iota(jnp.int32, (rows, 1), 0) // N_HEADS
    qpos = n_past + tok

    def fold(x):
        g0 = (lax.broadcasted_iota(jnp.int32, (rows, HEAD_DIM), 0) % N_HEADS) < HPG_C
        return jnp.where(g0, x[:, 0:HEAD_DIM], x[:, HEAD_DIM:2 * HEAD_DIM])

    sc = _dot_nt(qraw_ref[0], kcc_ref[0])
    cidx = lax.broadcasted_iota(jnp.int32, (1, nc), 1)
    pc = _masked_softmax(sc, (cidx * CMP_STRIDE + CMP_LEN - 1 <= qpos) & (cidx < nc - 1))
    oc_ref[0] = fold(_dot(pc.astype(BF16), vcc_ref[0]))

    ng = t * N_KV_C
    gr = lax.broadcasted_iota(jnp.int32, (ng, rows), 0)
    rr = lax.broadcasted_iota(jnp.int32, (ng, rows), 1)
    same = (gr // N_KV_C == rr // N_HEADS) & (gr % N_KV_C == (rr % N_HEADS) // HPG_C)
    psum = _dot(same.astype(F32), pc, precision=HIGHEST)
    imp = _dot_nt(psum, _overlap_t(nbp, nc), precision=HIGHEST)
    blk = lax.broadcasted_iota(jnp.int32, (ng, nbp), 1)
    own = (n_past + lax.broadcasted_iota(jnp.int32, (ng, nbp), 0) // N_KV_C) // SLC_BLOCK
    imp = jnp.where(blk == own, BIG, jnp.where(blk < own, imp, NEG))
    sel_ref[0] = _top_select(imp, blk, N_SLC)

    qrot = qrot_ref[0]
    win = win_ref[0]
    wn = wn_ref[0]
    s1 = _dot_nt(qrot, win[:, 0:LANES].astype(BF16))
    wpos1 = n_past - wb + lax.broadcasted_iota(jnp.int32, (1, wb), 1)
    d1 = qpos - wpos1
    mk1 = (d1 >= 0) & (d1 <= WINDOW) & (wpos1 >= 0)
    s2 = _dot_nt(qrot, wn[:, 0:LANES])
    j2 = lax.broadcasted_iota(jnp.int32, (1, pad), 1)
    mk2 = (j2 <= tok) & (j2 < t)
    m = jnp.maximum(jnp.max(jnp.where(mk1, s1, NEG), axis=-1, keepdims=True),
                    jnp.max(jnp.where(mk2, s2, NEG), axis=-1, keepdims=True))
    p1 = jnp.where(mk1, jnp.exp(jnp.where(mk1, s1, NEG) - m), 0.0)
    p2 = jnp.where(mk2, jnp.exp(jnp.where(mk2, s2, NEG) - m), 0.0)
    den = jnp.sum(p1, axis=-1, keepdims=True) + jnp.sum(p2, axis=-1, keepdims=True)
    ow = _dot(p1.astype(BF16), win[:, LANES:2 * LANES].astype(BF16)) + _dot(p2.astype(BF16), wn[:, LANES:2 * LANES])
    ow_ref[0] = fold(ow / den)


def _nsa_sample_head(qraw_bd, qrot_bd, kcc_f, vcc_f, win, w_new, n_past, nbp):
    n, rows, _ = qraw_bd.shape
    t = rows // N_HEADS
    nc = kcc_f.shape[1]
    wb = win.shape[1]
    pad = w_new.shape[1]
    seq = lambda b: (b, 0, 0)
    return pl.pallas_call(
        functools.partial(_nsa_sample_head_kernel, t=t, nc=nc, n_past=n_past, nbp=nbp, pad=pad),
        grid=(n,),
        in_specs=[pl.BlockSpec((1, rows, LANES), seq), pl.BlockSpec((1, rows, LANES), seq),
                  pl.BlockSpec((1, nc, LANES), seq), pl.BlockSpec((1, nc, LANES), seq),
                  pl.BlockSpec((1, wb, 2 * LANES), seq), pl.BlockSpec((1, pad, 2 * LANES), seq)],
        out_specs=[pl.BlockSpec((1, rows, HEAD_DIM), seq), pl.BlockSpec((1, rows, HEAD_DIM), seq),
                   pl.BlockSpec((1, t * N_KV_C, nbp), seq)],
        out_shape=[jax.ShapeDtypeStruct((n, rows, HEAD_DIM), F32), jax.ShapeDtypeStruct((n, rows, HEAD_DIM), F32),
                   jax.ShapeDtypeStruct((n, t * N_KV_C, nbp), F32)],
        compiler_params=_cparams("parallel"),
        name="nsa_sample_head",
    )(qraw_bd, qrot_bd, kcc_f, vcc_f, win, w_new)


def _nsa_sample_slc_kernel(pt_ref, q_ref, sel_ref, sn_ref, oc_ref, ow_ref, gt_ref, *refs,
                           n_slots, n_steps, t, nbp, pad):
    page_refs = refs[:n_slots]
    o_ref, m_s, l_s, acc_s = refs[n_slots:]
    j = pl.program_id(1)
    rows = t * N_HEADS
    q = q_ref[0]
    tok = lax.broadcasted_iota(jnp.int32, (rows, 1), 0) // N_HEADS

    @pl.when(j == 0)
    def _():
        sn = sn_ref[0]
        j2 = lax.broadcasted_iota(jnp.int32, (1, pad), 1)
        m, l, acc = _online_step(_dot_nt(q, sn[:, 0:LANES]), (j2 <= tok) & (j2 < t), sn[:, LANES:2 * LANES],
                                 jnp.full((rows, 1), NEG, F32), jnp.zeros((rows, 1), F32),
                                 jnp.zeros((rows, LANES), F32))
        m_s[...] = m
        l_s[...] = l
        acc_s[...] = acc

    sel_bias = sel_ref[0]
    eb = lax.broadcasted_iota(jnp.int32, (nbp, PAGE_SIZE), 0)
    ej = lax.broadcasted_iota(jnp.int32, (nbp, PAGE_SIZE), 1)
    group = 8
    for s0 in range(0, n_slots, group):
        scores, vals = [], []
        for s in range(s0, min(s0 + group, n_slots)):
            first_blk = (j * n_slots + s) * (PAGE_SIZE // SLC_BLOCK)
            bias = _dot(sel_bias, (first_blk + ej // SLC_BLOCK == eb).astype(BF16))
            page = page_refs[s][...]
            scores.append(_dot_nt(q, page[:, 0:LANES].astype(BF16)) + bias)
            vals.append(page[:, LANES:2 * LANES].astype(BF16))
        _joint_update(m_s, l_s, acc_s, scores, vals)

    @pl.when(j == n_steps - 1)
    def _():
        o = acc_s[...] / l_s[...]
        g0 = (lax.broadcasted_iota(jnp.int32, (rows, HEAD_DIM), 0) % N_HEADS) < HPG_C
        o_s = jnp.where(g0, o[:, 0:HEAD_DIM], o[:, HEAD_DIM:2 * HEAD_DIM])
        gt = gt_ref[0]
        o_ref[0] = oc_ref[0] * gt[:, 0:1] + o_s * gt[:, 1:2] + ow_ref[0] * gt[:, 2:3]


def _nsa_sample_slc(qrot_bd, sel_rows, s_new, o_c, o_w, gates, pool, base, page_table):
    n, rows, _ = qrot_bd.shape
    t = rows // N_HEADS
    nbp = sel_rows.shape[2]
    pad = s_new.shape[1]
    n_pages = page_table.shape[1]
    n_slots = _tile(n_pages, (16, 8, 4, 2))
    n_steps = n_pages // n_slots
    seq = lambda b, j, pt: (b, 0, 0)
    gs = pltpu.PrefetchScalarGridSpec(
        num_scalar_prefetch=1, grid=(n, n_steps),
        in_specs=[pl.BlockSpec((1, rows, LANES), seq), pl.BlockSpec((1, rows, nbp), seq),
                  pl.BlockSpec((1, pad, 2 * LANES), seq), pl.BlockSpec((1, rows, HEAD_DIM), seq),
                  pl.BlockSpec((1, rows, HEAD_DIM), seq), pl.BlockSpec((1, rows, 3), seq)]
        + _page_specs(n_slots, n_pages, 2 * LANES, 0, base, False),
        out_specs=pl.BlockSpec((1, rows, HEAD_DIM), seq),
        scratch_shapes=[pltpu.VMEM((rows, 1), F32), pltpu.VMEM((rows, 1), F32), pltpu.VMEM((rows, LANES), F32)])
    return pl.pallas_call(
        functools.partial(_nsa_sample_slc_kernel, n_slots=n_slots, n_steps=n_steps, t=t, nbp=nbp, pad=pad),
        grid_spec=gs,
        out_shape=jax.ShapeDtypeStruct((n, rows, HEAD_DIM), F32),
        compiler_params=_cparams("parallel", "arbitrary"),
        name="nsa_sample_slc",
    )(page_table, qrot_bd, sel_rows, s_new, o_c, o_w, gates, *([pool] * n_slots))


def _page_sum_kernel(pt_ref, *refs, n_slots):
    o_ref = refs[n_slots]
    ppb = MOBA_BLOCK // PAGE_SIZE
    for blk in range(n_slots // ppb):
        tot = jnp.sum(refs[blk * ppb][...].astype(F32), axis=0, keepdims=True)
        for p in range(1, ppb):
            tot = tot + jnp.sum(refs[blk * ppb + p][...].astype(F32), axis=0, keepdims=True)
        o_ref[0, blk:blk + 1, :] = tot


def _page_sums(pool, base, page_table):
    n, n_pages = page_table.shape
    ppb = MOBA_BLOCK // PAGE_SIZE
    n_slots = _tile(n_pages, (8 * ppb,))
    assert n_slots % ppb == 0
    gs = pltpu.PrefetchScalarGridSpec(
        num_scalar_prefetch=1, grid=(n, n_pages // n_slots),
        in_specs=_page_specs(n_slots, n_pages, HALF_MIX, 0, base, False),
        out_specs=pl.BlockSpec((1, n_slots // ppb, HALF_MIX), lambda b, j, pt: (b, j, 0)))
    return pl.pallas_call(
        functools.partial(_page_sum_kernel, n_slots=n_slots),
        grid_spec=gs,
        out_shape=jax.ShapeDtypeStruct((n, n_pages // ppb, HALF_MIX), F32),
        compiler_params=_cparams("parallel", "arbitrary"),
        name="moba_page_sums",
    )(page_table, *([pool] * n_slots))


def _moba_sample_kernel(pt_ref, q_ref, qf_ref, ks_ref, kn_ref, vn_ref, *refs, n_slots, n_steps, t, nbp, pad,
                        n_past):
    page_refs = refs[:n_slots]
    o_ref, sel_s, m_s, l_s, acc_s = refs[n_slots:]
    j = pl.program_id(1)
    rows = t * N_HEADS
    ppb = MOBA_BLOCK // PAGE_SIZE
    nb_full = ks_ref.shape[1]
    q = q_ref[0]
    tok = lax.broadcasted_iota(jnp.int32, (rows, 1), 0) // N_HEADS

    @pl.when(j == 0)
    def _():
        kmean = ks_ref[0] * (1.0 / MOBA_BLOCK)
        gate = _dot_nt(qf_ref[0], kmean, precision=HIGHEST)
        if nb_full < nbp:
            gate = jnp.concatenate([gate, jnp.full((rows, nbp - nb_full), NEG, F32)], axis=1)
        blk = lax.broadcasted_iota(jnp.int32, (rows, nbp), 1)
        own = (n_past + tok) // MOBA_BLOCK
        gate = jnp.where((blk < own) & (blk < nb_full), gate, NEG)
        sel_s[...] = jnp.where(_top_select(gate, blk, MOBA_TOPK) > 0.5, 0.0, NEG).astype(BF16)
        j2 = lax.broadcasted_iota(jnp.int32, (1, pad), 1)
        m, l, acc = _online_step(_dot_nt(q, kn_ref[0]), (j2 <= tok) & (j2 < t), vn_ref[0],
                                 jnp.full((rows, 1), NEG, F32), jnp.zeros((rows, 1), F32),
                                 jnp.zeros((rows, HALF_MIX), F32))
        m_s[...] = m
        l_s[...] = l
        acc_s[...] = acc

    sel_bias = sel_s[...]
    eb = lax.broadcasted_iota(jnp.int32, (nbp, PAGE_SIZE), 0)
    scores, vals = [], []
    for s in range(n_slots):
        blk_of_page = (j * n_slots + s) // ppb
        bias = _dot(sel_bias, (eb == blk_of_page).astype(BF16))
        page = page_refs[s][...]
        scores.append(_dot_nt(q, page[:, 0:HALF_MIX].astype(BF16)) + bias)
        vals.append(page[:, HALF_MIX:].astype(BF16))
    _joint_update(m_s, l_s, acc_s, scores, vals)

    @pl.when(j == n_steps - 1)
    def _():
        o_ref[0] = _fold_heads(acc_s[...] / l_s[...], t)


def _moba_sample(qbd, qbd_f32, ksums, k_new, v_new, pool, base, page_table, n_past):
    n, rows, _ = qbd.shape
    t = rows // N_HEADS
    pad = k_new.shape[1]
    n_pages = page_table.shape[1]
    assert n_past % MOBA_BLOCK == 0 and t <= MOBA_BLOCK
    nbp = LANES * (-(-(n_pages * PAGE_SIZE // MOBA_BLOCK) // LANES))
    n_slots = _tile(n_pages, (8, 4, 2))
    n_steps = n_pages // n_slots
    seq = lambda b, j, pt: (b, 0, 0)
    gs = pltpu.PrefetchScalarGridSpec(
        num_scalar_prefetch=1, grid=(n, n_steps),
        in_specs=[pl.BlockSpec((1, rows, HALF_MIX), seq), pl.BlockSpec((1, rows, HALF_MIX), seq),
                  pl.BlockSpec((1, ksums.shape[1], HALF_MIX), seq),
                  pl.BlockSpec((1, pad, HALF_MIX), seq), pl.BlockSpec((1, pad, HALF_MIX), seq)]
        + _page_specs(n_slots, n_pages, 2 * HALF_MIX, 0, base, False),
        out_specs=pl.BlockSpec((1, t, HALF_MIX), seq),
        scratch_shapes=[pltpu.VMEM((rows, nbp), BF16), pltpu.VMEM((rows, 1), F32), pltpu.VMEM((rows, 1), F32),
                        pltpu.VMEM((rows, HALF_MIX), F32)])
    return pl.pallas_call(
        functools.partial(_moba_sample_kernel, n_slots=n_slots, n_steps=n_steps, t=t, nbp=nbp, pad=pad,
                          n_past=n_past),
        grid_spec=gs,
        out_shape=jax.ShapeDtypeStruct((n, t, HALF_MIX), F32),
        compiler_params=_cparams("parallel", "arbitrary"),
        name="moba_sample",
    )(page_table, qbd, qbd_f32, ksums, k_new, v_new, *([pool] * n_slots))


def _block_diag(blocks):
    nb, bi, bj = blocks.shape
    eye = jnp.eye(nb, dtype=blocks.dtype)
    return jnp.einsum('bij,bc->bicj', blocks, eye).reshape(nb * bi, nb * bj)


def _rope_tables(pos):
    half = HEAD_DIM // 2
    inv_freq = ROPE_THETA ** (-jnp.arange(half, dtype=F32) / half)
    ang = pos.astype(F32)[:, None] * inv_freq[None, :]
    cos = jnp.cos(ang)
    sin = jnp.sin(ang)
    return jnp.concatenate([cos, cos, cos, cos], axis=1), jnp.concatenate([-sin, sin, -sin, sin], axis=1)


def _queries_bd(q, per, scale, dtype):
    n, t, _ = q.shape
    groups = N_HEADS // per
    qh = q.reshape(n, t, N_HEADS, 1, HEAD_DIM) * scale
    pick = (jnp.arange(N_HEADS)[:, None] // per == jnp.arange(groups)[None, :]).astype(q.dtype)
    return (qh * pick[None, None, :, :, None]).reshape(n, t * N_HEADS, groups * HEAD_DIM).astype(dtype)


def _pad_rows(x, pad):
    n, t, w = x.shape
    return jnp.concatenate([x, jnp.zeros((n, pad - t, w), x.dtype)], axis=1)


def _odd_weight(w_in):
    d = w_in.shape[0]
    gc0 = _O_VW + 128
    main = jnp.concatenate([w_in[:, :gc0], w_in[:, gc0 + 3 * N_HEADS:]], axis=1)
    gc = w_in[:, gc0:gc0 + 3 * N_HEADS].reshape(d, N_KV_C, HPG_C, 3).transpose(0, 1, 3, 2).reshape(d, N_KV_C, 3 * HPG_C)
    gc = jnp.concatenate([gc, jnp.zeros((d, N_KV_C, LANES - 3 * HPG_C), w_in.dtype)], axis=2).reshape(d, N_KV_C * LANES)
    return jnp.concatenate([main, gc], axis=1).astype(BF16)


def kernel(x_prompt, x_sample, cache_a_kv, state_b_h, state_b_conv, cache_c_kv, state_c_win, cache_d_kv, page_table, norm_ffn1, w_ffn1_gate, w_ffn1_up, w_ffn1_down, norm_mix, norm_ffn2, w_ffn2_gate, w_ffn2_up, w_ffn2_down, w_in_even, w_out_even, lru_conv_w, lru_conv_b, lru_w_r, lru_b_r, lru_w_i, lru_b_i, lru_lambda, w_in_odd, w_out_odd, cmp_pe_k, cmp_w1_k, cmp_w2_k, cmp_pe_v, cmp_w1_v, cmp_w2_v, norm_final):
    nb, s_len, d = x_prompt.shape
    ns, t = x_sample.shape[:2]
    depth = norm_mix.shape[0]
    n_pages = page_table.shape[1]
    n_past = n_pages * PAGE_SIZE
    n_pool = cache_a_kv.shape[1]
    hm = HALF_MIX
    pad = 16
    assert t <= pad and n_past % SLC_BLOCK == 0

    yp = x_prompt.reshape(nb * s_len, d)
    ys = x_sample.reshape(ns * t, d)
    cos_p, sin_p = _rope_tables(jnp.arange(s_len, dtype=jnp.int32))
    cos_s, sin_s = _rope_tables(jnp.tile(n_past + jnp.arange(t, dtype=jnp.int32), ns))
    pool_a = cache_a_kv.reshape(-1, PAGE_SIZE, 2 * hm).astype(BF16)
    pool_c = cache_c_kv.reshape(-1, PAGE_SIZE, 4 * N_KV_C * HEAD_DIM)
    pool_c_cmp = pool_c[:, :, :2 * LANES]
    pool_c_slc = pool_c[:, :, 2 * LANES:].astype(BF16)
    pool_d = cache_d_kv.reshape(-1, PAGE_SIZE, 2 * hm).astype(BF16)

    outs = {k: [] for k in ("a_p", "a_s", "bh_p", "bh_s", "bc_p", "bc_s", "c_p", "c_s", "cw_p", "cw_s", "d_p", "d_s")}
    for layer in range(depth):
        li = layer // 2
        last = layer == depth - 1
        f1 = (norm_ffn1[layer], w_ffn1_gate[layer].astype(BF16), w_ffn1_up[layer].astype(BF16),
              w_ffn1_down[layer].astype(BF16))
        f2 = (norm_ffn2[layer], w_ffn2_gate[layer].astype(BF16), w_ffn2_up[layer].astype(BF16),
              w_ffn2_down[layer].astype(BF16))
        yp = _half_ffn(yp, *f1)
        ys = _half_ffn(ys, *f1)
        if layer % 2 == 0:
            w_in = w_in_even[li].astype(BF16)
            w_out = w_out_even[li].astype(BF16)
            lru = (lru_conv_w[li], lru_conv_b[li], _block_diag(lru_w_r[li]).astype(BF16), lru_b_r[li],
                   _block_diag(lru_w_i[li]).astype(BF16), lru_b_i[li], lru_lambda[li])
            kv, xb, gb, qh, kh, vh = _proj_even(yp, norm_mix[layer], w_in, True)
            o_a = _sb_prompt(qh, kh, vh, nb, s_len)
            g, h_new, buf_new = _rg_lru(xb.reshape(nb, s_len, hm), gb.reshape(nb, s_len, hm),
                                        jnp.zeros((nb, hm), F32), jnp.zeros((nb, CONV_WIDTH - 1, hm), F32), *lru)
            yp = _outproj(yp, o_a, g.reshape(nb * s_len, hm), w_out)
            outs["a_p"].append(kv.reshape(nb, s_len, 2, N_HEADS, HEAD_DIM))
            outs["bh_p"].append(h_new)
            outs["bc_p"].append(buf_new)
            kv, xb, gb, q = _proj_even(ys, norm_mix[layer], w_in, False)
            kv3 = kv.reshape(ns, t, 2 * hm)
            o_a = _sb_sample(_queries_bd(q.reshape(ns, t, hm), 1, SCALE, BF16),
                             _pad_rows(kv3[:, :, :hm], pad).astype(BF16), _pad_rows(kv3[:, :, hm:], pad).astype(BF16),
                             pool_a, li * n_pool, page_table)
            g, h_new, buf_new = _rg_lru(xb.reshape(ns, t, hm), gb.reshape(ns, t, hm), state_b_h[li],
                                        state_b_conv[li], *lru)
            ys = _outproj(ys, o_a.reshape(ns * t, hm).astype(BF16), g.reshape(ns * t, hm), w_out)
            outs["a_s"].append(kv.reshape(ns, t, 2, N_HEADS, HEAD_DIM))
            outs["bh_s"].append(h_new)
            outs["bc_s"].append(buf_new)
        else:
            w_in = _odd_weight(w_in_odd[li])
            w_out = w_out_odd[li].astype(BF16)
            cmp_k = _cmp_weights(cmp_pe_k[li], cmp_w1_k[li])
            cmp_v = _cmp_weights(cmp_pe_v[li], cmp_w1_v[li])
            w2k = _block_diag(jnp.stack([cmp_w2_k[li]] * N_KV_C)).astype(BF16)
            w2v = _block_diag(jnp.stack([cmp_w2_v[li]] * N_KV_C)).astype(BF16)
            (c_rows, w_rows, d_rows, gates, qraw_h, qrot_h, ks_h, vs_h, kw_h, vw_h, qd_h, kd_h, vd_h) = _proj_odd(
                yp, norm_mix[layer], w_in, cos_p, sin_p, True)
            n_chunk = s_len // CMP_STRIDE
            rows_c = _tile(s_len, (2048, 1024, 512, 256, 128))
            ab = _cmp_ab([c_rows], [pl.BlockSpec((rows_c, LANES), lambda b, j: (b * (s_len // rows_c) + j, 0))],
                         [pl.BlockSpec((rows_c, LANES), lambda b, j: (b * (s_len // rows_c) + j, 1))],
                         (nb, s_len // rows_c), rows_c // CMP_STRIDE, nb, n_chunk, rows_c, cmp_k, cmp_v, 0, ())
            _, _, kcc_g, vcc_g = _cmp_finish(ab, w2k, w2v)
            o_c = _nsa_prompt(qraw_h, qrot_h, gates, kcc_g, vcc_g, ks_h, vs_h, kw_h, vw_h, nb, s_len)
            o_d = _moba_prompt(qd_h, _block_mean(d_rows), kd_h, vd_h, nb, s_len)
            yp = _outproj(yp, o_c, o_d, w_out)
            wlen = min(WINDOW, s_len)
            outs["c_p"].append(c_rows.reshape(nb, s_len, 4, N_KV_C, HEAD_DIM))
            outs["cw_p"].append(w_rows.reshape(nb, s_len, 2, N_KV_C, HEAD_DIM)[:, s_len - wlen:])
            outs["d_p"].append(d_rows.reshape(nb, s_len, 2, N_HEADS, HEAD_DIM))
            c_rows, w_rows, d_rows, gates, qraw, qrot, qd = _proj_odd(ys, norm_mix[layer], w_in, cos_s, sin_s, False)
            slots_c = _tile(n_pages, (16, 8))
            ab = _cmp_ab([pool_c_cmp] * slots_c,_page_specs(slots_c, n_pages, LANES, 0, li * n_pool, False),
                         _page_specs(slots_c, n_pages, LANES, 1, li * n_pool, False),
                         (ns, n_pages // slots_c), slots_c * PAGE_SIZE // CMP_STRIDE, ns, n_past // CMP_STRIDE,
                         PAGE_SIZE, cmp_k, cmp_v, 1, (page_table,))
            kcc_f, vcc_f, _, _ = _cmp_finish(ab, w2k, w2v)
            nsb = -(-(n_past + t) // SLC_BLOCK)
            nbp = LANES * (-(-nsb // LANES))
            win_l = state_c_win[li]
            wb = win_l.shape[1]
            qraw_bd = _queries_bd(qraw.reshape(ns, t, hm), HPG_C, SCALE, BF16)
            qrot_bd = _queries_bd(qrot.reshape(ns, t, hm), HPG_C, SCALE, BF16)
            c3 = c_rows.reshape(ns, t, 4 * LANES)
            w3 = w_rows.reshape(ns, t, 2 * LANES)
            d3 = d_rows.reshape(ns, t, 2 * hm)
            o_cmp, o_win, sel = _nsa_sample_head(qraw_bd, qrot_bd, kcc_f, vcc_f, win_l.reshape(ns, wb, 2 * LANES),
                                                 _pad_rows(w3, pad).astype(BF16), n_past, nbp)
            sel_rows = jnp.repeat(sel.reshape(ns, t, N_KV_C, nbp), HPG_C, axis=2).reshape(ns, t * N_HEADS, nbp)
            gt = gates.reshape(ns, t, N_KV_C, LANES)[..., :3 * HPG_C].reshape(ns, t, N_KV_C, 3, HPG_C)
            gt = gt.transpose(0, 1, 2, 4, 3).reshape(ns, t * N_HEADS, 3)
            sel_bias = jnp.where(sel_rows > 0.5, 0.0, NEG).astype(BF16)
            o_c = _nsa_sample_slc(qrot_bd, sel_bias, _pad_rows(c3[:, :, 2 * LANES:], pad).astype(BF16),
                                  o_cmp, o_win, gt, pool_c_slc, li * n_pool, page_table)
            ksums = _page_sums(pool_d, li * n_pool, page_table)
            qd3 = qd.reshape(ns, t, hm)
            o_d = _moba_sample(_queries_bd(qd3, 1, SCALE, BF16), _queries_bd(qd3, 1, 1.0, F32), ksums,
                               _pad_rows(d3[:, :, :hm], pad).astype(BF16), _pad_rows(d3[:, :, hm:], pad).astype(BF16),
                               pool_d, li * n_pool, page_table, n_past)
            ys = _outproj(ys, o_c.reshape(ns * t, hm).astype(BF16), o_d.reshape(ns * t, hm).astype(BF16), w_out)
            w5 = w_rows.reshape(ns, t, 2, N_KV_C, HEAD_DIM)
            outs["c_s"].append(c_rows.reshape(ns, t, 4, N_KV_C, HEAD_DIM))
            outs["cw_s"].append(jnp.concatenate([win_l, w5], axis=1)[:, -wb:])
            outs["d_s"].append(d_rows.reshape(ns, t, 2, N_HEADS, HEAD_DIM))
        gf = norm_final if last else None
        yp = _half_ffn(yp, *f2, g_final=gf)
        ys = _half_ffn(ys, *f2, g_final=gf)
    st = lambda k: jnp.stack(outs[k])
    return (yp.reshape(nb, s_len, d), ys.reshape(ns, t, d), st("a_p"), st("a_s"), st("bh_p"), st("bh_s"),
            st("bc_p"), st("bc_s"), st("c_p"), st("c_s"), st("cw_p"), st("cw_s"), st("d_p"), st("d_s"))
```

```python
import functools
import math

import jax
import jax.numpy as jnp
import numpy as np
from jax import lax
from jax.experimental import pallas as pl
from jax.experimental.pallas import tpu as pltpu

F32 = jnp.float32
BF16 = jnp.bfloat16

HEAD_DIM = 64
HALF_MIX = 512
N_HEADS = HALF_MIX // HEAD_DIM
N_KV_C = 2
HPG_C = N_HEADS // N_KV_C
PAGE_SIZE = 128
CONV_WIDTH = 4
LRU_C = 8.0
CMP_LEN = 32
CMP_STRIDE = 16
SLC_BLOCK = 64
N_SLC = 16
WINDOW = 512
MOBA_BLOCK = 256
MOBA_TOPK = 3
ROPE_THETA = 10000.0
RMS_EPS = 1e-6
NEG = -1e30
BIG = 1e30
SCALE = HEAD_DIM ** -0.5

LANES = 128
VMEM_LIMIT_BYTES = 56 * 1024 * 1024
SB_CUTOFF = 120.0
HIGHEST = lax.Precision.HIGHEST


def _cparams(*sem):
    return pltpu.CompilerParams(dimension_semantics=sem, vmem_limit_bytes=VMEM_LIMIT_BYTES)


def _tile(n, prefs):
    for t in prefs:
        if n % t == 0:
            return t
    return n


def _dot(a, b, precision=None):
    return jnp.dot(a, b, preferred_element_type=F32, precision=precision)


def _dot_nt(a, b, precision=None):
    return lax.dot_general(a, b, (((1,), (1,)), ((), ())), preferred_element_type=F32, precision=precision)


def _rms(x, g):
    ms = jnp.mean(x * x, axis=-1, keepdims=True)
    return x * lax.rsqrt(ms + RMS_EPS) * g


def _softplus(z):
    return jnp.maximum(z, 0.0) + jnp.log(1.0 + jnp.exp(-jnp.abs(z)))


def _one_minus_exp(x):
    poly = x
    for k in range(7, 1, -1):
        poly = x * (1.0 + poly * (1.0 / k))
    return jnp.where(x > -0.125, -poly, 1.0 - jnp.exp(x))


def _split_dot(x, t):
    hi = x.astype(BF16)
    lo = (x - hi.astype(F32)).astype(BF16)
    return _dot(hi, t) + _dot(lo, t)


def _masked_softmax(s, mask):
    sm = jnp.where(mask, s, NEG)
    m = jnp.max(sm, axis=-1, keepdims=True)
    e = jnp.where(mask, jnp.exp(sm - m), 0.0)
    d = jnp.sum(e, axis=-1, keepdims=True)
    return e / jnp.maximum(d, 1e-30)


def _online_step(s, mask, v, m, l, acc):
    sm = jnp.where(mask, s, NEG)
    m_new = jnp.maximum(m, jnp.max(sm, axis=-1, keepdims=True))
    alpha = jnp.exp(m - m_new)
    p = jnp.where(mask, jnp.exp(sm - m_new), 0.0)
    l = alpha * l + jnp.sum(p, axis=-1, keepdims=True)
    rows = acc.shape[0]
    acc = alpha.reshape(rows, 1) * acc + _dot(p.reshape(rows, p.shape[-1]).astype(BF16), v)
    return m_new, l, acc


def _flash_multi(qs, ks, v1s, biases, m_refs, acc_refs):
    n = len(qs)
    ss = [_dot_nt(qs[i], ks[i]) for i in range(n)]
    ss = [s if b is None else s + b for s, b in zip(ss, biases)]
    m_old = [ref[...] for ref in m_refs]
    m_new = [jnp.maximum(m_old[i], jnp.max(ss[i], axis=-1, keepdims=True)) for i in range(n)]
    ps = [jnp.exp(ss[i] - m_new[i]).astype(BF16) for i in range(n)]
    pvs = [_dot(ps[i], v1s[i]) for i in range(n)]
    for i in range(n):
        acc_refs[i][...] = jnp.exp(m_old[i] - m_new[i]) * acc_refs[i][...] + pvs[i]
        m_refs[i][...] = m_new[i]


def _joint_update(m_s, l_s, acc_s, scores, vals):
    m_old = m_s[...]
    m_new = m_old
    for sc in scores:
        m_new = jnp.maximum(m_new, jnp.max(sc, axis=-1, keepdims=True))
    alpha = jnp.exp(m_old - m_new)
    l = alpha * l_s[...]
    acc = alpha * acc_s[...]
    for sc, v in zip(scores, vals):
        p = jnp.exp(sc - m_new)
        l = l + jnp.sum(p, axis=-1, keepdims=True)
        acc = acc + _dot(p.astype(BF16), v)
    m_s[...] = m_new
    l_s[...] = l
    acc_s[...] = acc


def _flash_result(acc):
    return acc[:, 0:HEAD_DIM] / acc[:, HEAD_DIM:HEAD_DIM + 1]


def _top_select(val, idx, n):
    sel = jnp.zeros(val.shape, F32)
    big_i = jnp.int32(2 ** 30)
    for _ in range(n):
        m = jnp.max(val, axis=-1, keepdims=True)
        first = jnp.min(jnp.where(val == m, idx, big_i), axis=-1, keepdims=True)
        hit = idx == first
        sel = jnp.where(hit & (m > 0.5 * NEG), 1.0, sel)
        val = jnp.where(hit, -jnp.inf, val)
    return sel


def _ffn_kernel(*refs, n_f, post_norm):
    if post_norm:
        x_ref, g_ref, wg_ref, wu_ref, wd_ref, gf_ref, o_ref, xn_ref, acc_ref = refs
    else:
        x_ref, g_ref, wg_ref, wu_ref, wd_ref, o_ref, xn_ref, acc_ref = refs
    j = pl.program_id(1)

    @pl.when(j == 0)
    def _():
        xn_ref[...] = _rms(x_ref[...], g_ref[...]).astype(BF16)
        acc_ref[...] = jnp.zeros_like(acc_ref)

    xn = xn_ref[...]
    hg = _dot(xn, wg_ref[...])
    hu = _dot(xn, wu_ref[...])
    a = (hg * jax.nn.sigmoid(hg) * hu).astype(BF16)
    acc_ref[...] += _dot(a, wd_ref[...])

    @pl.when(j == n_f - 1)
    def _():
        y = x_ref[...] + 0.5 * acc_ref[...]
        if post_norm:
            y = _rms(y, gf_ref[...])
        o_ref[...] = y


def _half_ffn(x, g, wg, wu, wd, g_final=None):
    m, d = x.shape
    f = wg.shape[1]
    tm = _tile(m, (1024, 512, 256, 128))
    tf = _tile(f, (256, 128))
    n_f = f // tf
    post = g_final is not None
    in_specs = [pl.BlockSpec((tm, d), lambda i, j: (i, 0)),
                pl.BlockSpec((1, d), lambda i, j: (0, 0)),
                pl.BlockSpec((d, tf), lambda i, j: (0, j)),
                pl.BlockSpec((d, tf), lambda i, j: (0, j)),
                pl.BlockSpec((tf, d), lambda i, j: (j, 0))]
    args = [x, g.reshape(1, d), wg, wu, wd]
    if post:
        in_specs.append(pl.BlockSpec((1, d), lambda i, j: (0, 0)))
        args.append(g_final.reshape(1, d))
    return pl.pallas_call(
        functools.partial(_ffn_kernel, n_f=n_f, post_norm=post),
        grid=(m // tm, n_f),
        in_specs=in_specs,
        out_specs=pl.BlockSpec((tm, d), lambda i, j: (i, 0)),
        out_shape=jax.ShapeDtypeStruct((m, d), F32),
        scratch_shapes=[pltpu.VMEM((tm, d), BF16), pltpu.VMEM((tm, d), F32)],
        compiler_params=_cparams("parallel", "arbitrary"),
        name="half_ffn",
    )(*args)


def _rope(seg, cos_t, sin_t):
    w = seg.shape[1]
    reps = w // LANES
    c = jnp.concatenate([cos_t] * reps, axis=1) if reps > 1 else cos_t
    s = jnp.concatenate([sin_t] * reps, axis=1) if reps > 1 else sin_t
    lane = lax.broadcasted_iota(jnp.int32, seg.shape, 1)
    first = (lane % HEAD_DIM) < (HEAD_DIM // 2)
    rot = jnp.where(first, pltpu.roll(seg, w - HEAD_DIM // 2, 1), pltpu.roll(seg, HEAD_DIM // 2, 1))
    return seg * c + rot * s


def _store_heads(ref, seg, scale=None, ones=False):
    if ones:
        lane = lax.broadcasted_iota(jnp.int32, (seg.shape[0], HEAD_DIM), 1)
        tail = (lane == 0).astype(seg.dtype)
    for h in range(seg.shape[1] // HEAD_DIM):
        piece = seg[:, h * HEAD_DIM:(h + 1) * HEAD_DIM]
        if scale is not None:
            piece = piece * scale
        if ones:
            piece = jnp.concatenate([piece, tail], axis=1)
        ref[h] = piece.astype(ref.dtype)


def _proj_even_kernel(x_ref, g_ref, w_ref, kv_ref, xb_ref, gb_ref, *outs, heads):
    hm = HALF_MIX
    xn = _rms(x_ref[...], g_ref[...]).astype(BF16)
    y = _dot(xn, w_ref[...])
    kv_ref[...] = y[:, hm:3 * hm]
    xb_ref[...] = y[:, 3 * hm:4 * hm]
    gb_ref[...] = y[:, 4 * hm:5 * hm]
    if heads:
        qh_ref, kh_ref, vh_ref = outs
        _store_heads(qh_ref, y[:, 0:hm], SCALE)
        _store_heads(kh_ref, y[:, hm:2 * hm])
        _store_heads(vh_ref, y[:, 2 * hm:3 * hm])
    else:
        (q_ref,) = outs
        q_ref[...] = y[:, 0:hm]


def _proj_even(x, g, w, heads):
    m, d = x.shape
    n = w.shape[1]
    hm = HALF_MIX
    tm = _tile(m, (512, 256, 128))
    row = lambda i: (i, 0)
    out_shape = [jax.ShapeDtypeStruct((m, 2 * hm), F32), jax.ShapeDtypeStruct((m, hm), F32),
                 jax.ShapeDtypeStruct((m, hm), F32)]
    out_specs = [pl.BlockSpec((tm, 2 * hm), row), pl.BlockSpec((tm, hm), row), pl.BlockSpec((tm, hm), row)]
    if heads:
        for _ in range(3):
            out_shape.append(jax.ShapeDtypeStruct((N_HEADS, m, HEAD_DIM), BF16))
            out_specs.append(pl.BlockSpec((N_HEADS, tm, HEAD_DIM), lambda i: (0, i, 0)))
    else:
        out_shape.append(jax.ShapeDtypeStruct((m, hm), F32))
        out_specs.append(pl.BlockSpec((tm, hm), row))
    return pl.pallas_call(
        functools.partial(_proj_even_kernel, heads=heads),
        grid=(m // tm,),
        in_specs=[pl.BlockSpec((tm, d), row), pl.BlockSpec((1, d), lambda i: (0, 0)),
                  pl.BlockSpec((d, n), lambda i: (0, 0))],
        out_specs=out_specs, out_shape=out_shape,
        compiler_params=_cparams("parallel"),
        name="proj_even",
    )(x, g.reshape(1, d), w)


_O_QC, _O_KC, _O_VC, _O_KS, _O_VS, _O_KW, _O_VW, _O_QD, _O_KD, _O_VD, _O_GT, _O_END = (
    0, 512, 640, 768, 896, 1024, 1152, 1280, 1792, 2304, 2816, 3072)


def _proj_odd_kernel(x_ref, g_ref, w_ref, cos_ref, sin_ref, c_ref, w_out_ref, d_ref, gt_ref, *outs, heads):
    xn = _rms(x_ref[...], g_ref[...]).astype(BF16)
    y = _dot(xn, w_ref[...])
    cos_t = cos_ref[...]
    sin_t = sin_ref[...]
    qc = y[:, _O_QC:_O_KC]
    qc_rot = _rope(qc, cos_t, sin_t)
    ks_rot = _rope(y[:, _O_KS:_O_VS], cos_t, sin_t)
    kw_rot = _rope(y[:, _O_KW:_O_VW], cos_t, sin_t)
    qd_rot = _rope(y[:, _O_QD:_O_KD], cos_t, sin_t)
    kd_rot = _rope(y[:, _O_KD:_O_VD], cos_t, sin_t)
    vs = y[:, _O_VS:_O_KW]
    vw = y[:, _O_VW:_O_QD]
    vd = y[:, _O_VD:_O_GT]
    c_ref[:, 0:256] = y[:, _O_KC:_O_KS]
    c_ref[:, 256:384] = ks_rot
    c_ref[:, 384:512] = vs
    w_out_ref[:, 0:128] = kw_rot
    w_out_ref[:, 128:256] = vw
    d_ref[:, 0:512] = kd_rot
    d_ref[:, 512:1024] = vd
    gt_ref[...] = jax.nn.sigmoid(y[:, _O_GT:_O_END])
    if heads:
        qraw_h, qrot_h, ks_h, vs_h, kw_h, vw_h, qd_h, kd_h, vd_h = outs
        _store_heads(qraw_h, qc, SCALE)
        _store_heads(qrot_h, qc_rot, SCALE)
        _store_heads(ks_h, ks_rot)
        _store_heads(vs_h, vs, ones=True)
        _store_heads(kw_h, kw_rot)
        _store_heads(vw_h, vw, ones=True)
        _store_heads(qd_h, qd_rot, SCALE)
        _store_heads(kd_h, kd_rot)
        _store_heads(vd_h, vd, ones=True)
    else:
        qraw_ref, qrot_ref, qd_ref = outs
        qraw_ref[...] = qc
        qrot_ref[...] = qc_rot
        qd_ref[...] = qd_rot


def _proj_odd(x, g, w, cos_t, sin_t, heads):
    m, d = x.shape
    n = w.shape[1]
    tm = _tile(m, (512, 256, 128))
    period = cos_t.shape[0] // tm
    row = lambda i: (i, 0)
    tab = lambda i: (i % period, 0)
    out_shape = [jax.ShapeDtypeStruct((m, 512), F32), jax.ShapeDtypeStruct((m, 256), F32),
                 jax.ShapeDtypeStruct((m, 1024), F32), jax.ShapeDtypeStruct((m, 256), F32)]
    out_specs = [pl.BlockSpec((tm, 512), row), pl.BlockSpec((tm, 256), row), pl.BlockSpec((tm, 1024), row),
                 pl.BlockSpec((tm, 256), row)]
    if heads:
        wide = (False, False, False, True, False, True, False, False, True)
        for nh, wd in zip((N_HEADS, N_HEADS, N_KV_C, N_KV_C, N_KV_C, N_KV_C, N_HEADS, N_HEADS, N_HEADS), wide):
            width = LANES if wd else HEAD_DIM
            out_shape.append(jax.ShapeDtypeStruct((nh, m, width), BF16))
            out_specs.append(pl.BlockSpec((nh, tm, width), lambda i: (0, i, 0)))
    else:
        for _ in range(3):
            out_shape.append(jax.ShapeDtypeStruct((m, HALF_MIX), F32))
            out_specs.append(pl.BlockSpec((tm, HALF_MIX), row))
    return pl.pallas_call(
        functools.partial(_proj_odd_kernel, heads=heads),
        grid=(m // tm,),
        in_specs=[pl.BlockSpec((tm, d), row), pl.BlockSpec((1, d), lambda i: (0, 0)),
                  pl.BlockSpec((d, n), lambda i: (0, 0)),
                  pl.BlockSpec((tm, LANES), tab), pl.BlockSpec((tm, LANES), tab)],
        out_specs=out_specs, out_shape=out_shape,
        compiler_params=_cparams("parallel"),
        name="proj_odd",
    )(x, g.reshape(1, d), w, cos_t, sin_t)


def _outproj_kernel(x_ref, a_ref, b_ref, w_ref, o_ref, *, heads):
    hm = HALF_MIX
    if heads:
        a = jnp.concatenate([a_ref[h] for h in range(N_HEADS)], axis=1)
        b = b_ref[...] if len(b_ref.shape) == 2 else jnp.concatenate([b_ref[h] for h in range(N_HEADS)], axis=1)
    else:
        a = a_ref[...]
        b = b_ref[...]
    o_ref[...] = x_ref[...] + _dot(a, w_ref[0:hm, :]) + _dot(b, w_ref[hm:2 * hm, :])


def _outproj(x, a, b, w):
    m, d = x.shape
    tm = _tile(m, (512, 256, 128))
    row = lambda i: (i, 0)

    def spec(t):
        if t.ndim == 2:
            return pl.BlockSpec((tm, t.shape[1]), row)
        return pl.BlockSpec((t.shape[0], tm, t.shape[2]), lambda i: (0, i, 0))

    return pl.pallas_call(
        functools.partial(_outproj_kernel, heads=a.ndim == 3),
        grid=(m // tm,),
        in_specs=[pl.BlockSpec((tm, d), row), spec(a), spec(b), pl.BlockSpec(w.shape, lambda i: (0, 0))],
        out_specs=pl.BlockSpec((tm, d), row),
        out_shape=jax.ShapeDtypeStruct((m, d), F32),
        compiler_params=_cparams("parallel"),
        name="outproj",
    )(x, a, b, w)


def _sb_tile(q, k, v, t_incl, carry, acc, mask):
    z = _dot_nt(q, k)
    lk = -_softplus(z)
    if mask is not None:
        lk = jnp.where(mask, lk, 0.0)
    incl = _split_dot(lk, t_incl)
    w = jnp.exp(z + incl + carry)
    if mask is not None:
        w = jnp.where(mask, w, 0.0)
    acc = acc + _dot(w.astype(BF16), v)
    return carry + incl[:, 0:1], acc


def _tri_incl(n):
    r = lax.broadcasted_iota(jnp.int32, (n, n), 0)
    c = lax.broadcasted_iota(jnp.int32, (n, n), 1)
    return (r >= c).astype(BF16)


def _sb_prompt_kernel(q_ref, k_ref, v_ref, o_ref, *, tq):
    i = pl.program_id(2)
    q = q_ref[0]
    t_incl = _tri_incl(tq)
    r = lax.broadcasted_iota(jnp.int32, (tq, tq), 0)
    c = lax.broadcasted_iota(jnp.int32, (tq, tq), 1)

    def kv(kt):
        start = pl.multiple_of(kt * tq, tq)
        return k_ref[0, pl.ds(start, tq), :], v_ref[0, pl.ds(start, tq), :]

    k0, v0 = kv(i)
    carry, acc = _sb_tile(q, k0, v0, t_incl, jnp.zeros((tq, 1), F32), jnp.zeros((tq, HEAD_DIM), F32), c < r)

    def cond(s):
        kt, carry, _ = s
        return jnp.logical_and(kt >= 0, jnp.max(carry) > -SB_CUTOFF)

    def body(s):
        kt, carry, acc = s
        kk, vv = kv(kt)
        carry, acc = _sb_tile(q, kk, vv, t_incl, carry, acc, None)
        return kt - 1, carry, acc

    _, _, acc = lax.while_loop(cond, body, (i - 1, carry, acc))
    o_ref[0] = acc.astype(o_ref.dtype)


def _sb_prompt(qh, kh, vh, n_seq, s_len):
    tq = _tile(s_len, (256, 128))
    nq = s_len // tq
    return pl.pallas_call(
        functools.partial(_sb_prompt_kernel, tq=tq),
        grid=(n_seq, N_HEADS, nq),
        in_specs=[pl.BlockSpec((1, tq, HEAD_DIM), lambda b, h, i: (h, b * nq + i, 0)),
                  pl.BlockSpec((1, s_len, HEAD_DIM), lambda b, h, i: (h, b, 0)),
                  pl.BlockSpec((1, s_len, HEAD_DIM), lambda b, h, i: (h, b, 0))],
        out_specs=pl.BlockSpec((1, tq, HEAD_DIM), lambda b, h, i: (h, b * nq + i, 0)),
        out_shape=jax.ShapeDtypeStruct(qh.shape, BF16),
        compiler_params=_cparams("parallel", "parallel", "arbitrary"),
        name="sb_prompt",
    )(qh, kh, vh)


def _lru_kernel(xb_ref, gb_ref, h0_ref, buf0_ref, cw_ref, cb_ref, wr_ref, br_ref, wi_ref, bi_ref, lam_ref,
                g_ref, hl_ref, bn_ref, xpad, a_s, u_s, hs_s, h_s, *, tc, n_t):
    j = pl.program_id(1)
    keep = CONV_WIDTH - 1

    @pl.when(j == 0)
    def _():
        xpad[8 - keep:8, :] = buf0_ref[0]
        h_s[...] = h0_ref[0]

    @pl.when(j > 0)
    def _():
        xpad[8 - keep:8, :] = xpad[8 + tc - keep:8 + tc, :]

    x = xb_ref[0]
    xpad[8:8 + tc, :] = x
    xc = cb_ref[...] + x * cw_ref[keep:keep + 1, :]
    for d in range(1, CONV_WIDTH):
        xc = xc + xpad[8 - d:8 - d + tc, :] * cw_ref[keep - d:keep - d + 1, :]
    xcb = xc.astype(BF16)
    r = jax.nn.sigmoid(_dot(xcb, wr_ref[...]) + br_ref[...])
    gi = jax.nn.sigmoid(_dot(xcb, wi_ref[...]) + bi_ref[...])
    log_a = -LRU_C * r * _softplus(-lam_ref[...])
    a_s[...] = jnp.exp(log_a)
    u_s[...] = jnp.sqrt(_one_minus_exp(2.0 * log_a)) * (gi * xc)

    def step(t, h):
        h = a_s[pl.ds(t, 1), :] * h + u_s[pl.ds(t, 1), :]
        hs_s[pl.ds(t, 1), :] = h
        return h

    h = lax.fori_loop(0, tc, step, h_s[...], unroll=min(tc, 8))
    h_s[...] = h
    g_ref[0] = (jax.nn.gelu(gb_ref[0]) * hs_s[...]).astype(g_ref.dtype)

    @pl.when(j == n_t - 1)
    def _():
        hl_ref[0] = h
        bn_ref[0] = xpad[8 + tc - keep:8 + tc, :]


def _rg_lru(xb, gb, h0, buf0, conv_w, conv_b, wr_bd, b_r, wi_bd, b_i, lam):
    n, t, w = xb.shape
    keep = CONV_WIDTH - 1
    assert t >= keep
    tc = _tile(t, (512, 256, 128))
    n_t = t // tc
    seq = lambda b, j: (b, j, 0)
    one = lambda b, j: (b, 0, 0)
    const = lambda b, j: (0, 0)
    vec = pl.BlockSpec((1, w), const)
    g, hl, bn = pl.pallas_call(
        functools.partial(_lru_kernel, tc=tc, n_t=n_t),
        grid=(n, n_t),
        in_specs=[pl.BlockSpec((1, tc, w), seq), pl.BlockSpec((1, tc, w), seq),
                  pl.BlockSpec((1, 1, w), one), pl.BlockSpec((1, keep, w), one),
                  pl.BlockSpec((CONV_WIDTH, w), const), vec,
                  pl.BlockSpec((w, w), const), vec, pl.BlockSpec((w, w), const), vec, vec],
        out_specs=[pl.BlockSpec((1, tc, w), seq), pl.BlockSpec((1, 1, w), one), pl.BlockSpec((1, keep, w), one)],
        out_shape=[jax.ShapeDtypeStruct((n, t, w), BF16), jax.ShapeDtypeStruct((n, 1, w), F32),
                   jax.ShapeDtypeStruct((n, keep, w), F32)],
        scratch_shapes=[pltpu.VMEM((tc + 8, w), F32), pltpu.VMEM((tc, w), F32), pltpu.VMEM((tc, w), F32),
                        pltpu.VMEM((tc, w), F32), pltpu.VMEM((1, w), F32)],
        compiler_params=_cparams("parallel", "arbitrary"),
        name="rg_lru",
    )(xb, gb, h0.reshape(n, 1, w), buf0, conv_w, conv_b.reshape(1, w), wr_bd, b_r.reshape(1, w), wi_bd,
      b_i.reshape(1, w), lam.reshape(1, w))
    return g, hl.reshape(n, w), bn


def _cmp_ab_kernel(*refs, n_in, rows, n_prefetch):
    refs = refs[n_prefetch:]
    k_refs = refs[:n_in]
    v_refs = refs[n_in:2 * n_in]
    pe_ak, pe_bk, pe_av, pe_bv, w_ak, w_bk, w_av, w_bv, o_ref = refs[2 * n_in:]
    nch = rows // CMP_STRIDE

    def chunks(in_refs):
        per = [jnp.concatenate([r[pl.ds(p, nch, stride=CMP_STRIDE), :] for p in range(CMP_STRIDE)], axis=1)
               for r in in_refs]
        return per[0] if n_in == 1 else jnp.concatenate(per, axis=0)

    xk = chunks(k_refs)
    xv = chunks(v_refs)
    o_ref[0, :, 0:128] = _dot((xk + pe_ak[...]).astype(BF16), w_ak[...])
    o_ref[0, :, 128:256] = _dot((xk + pe_bk[...]).astype(BF16), w_bk[...])
    o_ref[0, :, 256:384] = _dot((xv + pe_av[...]).astype(BF16), w_av[...])
    o_ref[0, :, 384:512] = _dot((xv + pe_bv[...]).astype(BF16), w_bv[...])


def _cmp_weights(pe, w1):
    hid = w1.shape[-1]
    eye = jnp.eye(N_KV_C, dtype=F32)
    out = []
    for half in range(CMP_LEN // CMP_STRIDE):
        sl = slice(half * CMP_STRIDE, (half + 1) * CMP_STRIDE)
        pe_row = jnp.broadcast_to(pe[sl][:, None, :], (CMP_STRIDE, N_KV_C, HEAD_DIM)).reshape(1, -1)
        wbd = jnp.einsum('pdh,gk->pgdkh', w1[sl], eye).reshape(CMP_STRIDE * N_KV_C * HEAD_DIM, N_KV_C * hid)
        out.append((pe_row, wbd.astype(BF16)))
    return out


def _cmp_ab(in_arrays, k_specs, v_specs, grid, out_rows, n_seq, n_chunks, rows, cmp_k, cmp_v, num_prefetch,
            prefetch):
    in_specs = list(k_specs) + list(v_specs)
    (pe_ak, w_ak), (pe_bk, w_bk) = cmp_k
    (pe_av, w_av), (pe_bv, w_bv) = cmp_v
    nd = len(grid)
    const = lambda *a: (0, 0)
    pes = [pe_ak, pe_bk, pe_av, pe_bv]
    ws = [w_ak, w_bk, w_av, w_bv]
    specs = list(in_specs) + [pl.BlockSpec(p.shape, const) for p in pes] + [pl.BlockSpec(w.shape, const) for w in ws]
    gs = pltpu.PrefetchScalarGridSpec(
        num_scalar_prefetch=num_prefetch, grid=grid, in_specs=specs,
        out_specs=pl.BlockSpec((1, out_rows, 512), lambda b, j, *a: (b, j, 0)))
    return pl.pallas_call(
        functools.partial(_cmp_ab_kernel, n_in=len(in_arrays), rows=rows, n_prefetch=num_prefetch),
        grid_spec=gs,
        out_shape=jax.ShapeDtypeStruct((n_seq, n_chunks, 512), F32),
        compiler_params=_cparams(*(["parallel"] + ["arbitrary"] * (nd - 1))),
        name="cmp_ab",
    )(*prefetch, *in_arrays, *in_arrays, *pes, *ws)


def _cmp_finish_kernel(ab_ref, w2k_ref, w2v_ref, kf_ref, vf_ref, kg_ref, vg_ref, *, nc):
    n = nc - 1
    hk = jax.nn.gelu(ab_ref[0, 0:n, 0:128] + ab_ref[0, 1:nc, 128:256]).astype(BF16)
    hv = jax.nn.gelu(ab_ref[0, 0:n, 256:384] + ab_ref[0, 1:nc, 384:512]).astype(BF16)
    kc = _dot(hk, w2k_ref[...]).astype(BF16)
    vc = _dot(hv, w2v_ref[...]).astype(BF16)
    zero = jnp.zeros((1, LANES), BF16)
    for ref_f, ref_g, val in ((kf_ref, kg_ref, kc), (vf_ref, vg_ref, vc)):
        full = jnp.concatenate([val, zero], axis=0)
        ref_f[0] = full
        for g in range(N_KV_C):
            ref_g[0, g] = full[:, g * HEAD_DIM:(g + 1) * HEAD_DIM]


def _cmp_finish(ab, w2k_bd, w2v_bd):
    n, nc, _ = ab.shape
    flat = jax.ShapeDtypeStruct((n, nc, LANES), BF16)
    grp = jax.ShapeDtypeStruct((n, N_KV_C, nc, HEAD_DIM), BF16)
    fspec = pl.BlockSpec((1, nc, LANES), lambda b: (b, 0, 0))
    gspec = pl.BlockSpec((1, N_KV_C, nc, HEAD_DIM), lambda b: (b, 0, 0, 0))
    return pl.pallas_call(
        functools.partial(_cmp_finish_kernel, nc=nc),
        grid=(n,),
        in_specs=[pl.BlockSpec((1, nc, 512), lambda b: (b, 0, 0)),
                  pl.BlockSpec((LANES, LANES), lambda b: (0, 0)), pl.BlockSpec((LANES, LANES), lambda b: (0, 0))],
        out_specs=[fspec, fspec, gspec, gspec], out_shape=[flat, flat, grp, grp],
        compiler_params=_cparams("parallel"),
        name="cmp_finish",
    )(ab, w2k_bd, w2v_bd)


def _overlap_t(nb, nc):
    b = lax.broadcasted_iota(jnp.int32, (nb, nc), 0)
    c = lax.broadcasted_iota(jnp.int32, (nb, nc), 1)
    return ((c * CMP_STRIDE < (b + 1) * SLC_BLOCK) & (c * CMP_STRIDE + CMP_LEN > b * SLC_BLOCK)).astype(F32)


def _nsa_prompt_kernel(qraw_ref, qrot_ref, gt_ref, kcc_ref, vcc_ref, ks_ref, vs_ref, kw_ref, vw_ref, o_ref,
                       *state, tq, nc, nsb, ck):
    i = pl.program_id(2)
    q0 = i * tq
    rows = HPG_C * tq
    nbp = LANES
    qpos3 = q0 + lax.broadcasted_iota(jnp.int32, (1, tq, 1), 1)

    qraw = qraw_ref[...].reshape(rows, HEAD_DIM)
    sc = _dot_nt(qraw, kcc_ref[0, 0]).reshape(HPG_C, tq, nc)
    cidx = lax.broadcasted_iota(jnp.int32, (1, 1, nc), 2)
    mask_c = (cidx * CMP_STRIDE + CMP_LEN - 1 <= qpos3) & (cidx < nc - 1)
    pc = _masked_softmax(sc, mask_c)
    o_c = _dot(pc.reshape(rows, nc).astype(BF16), vcc_ref[0, 0])

    psum = jnp.sum(pc, axis=0)
    imp_t = _dot_nt(_overlap_t(nsb, nc), psum, precision=HIGHEST)
    blk = lax.broadcasted_iota(jnp.int32, (nsb, tq), 0)
    own = (q0 + lax.broadcasted_iota(jnp.int32, (nsb, tq), 1)) // SLC_BLOCK
    imp_t = jnp.where(blk == own, BIG, jnp.where(blk < own, imp_t, NEG))
    cnt = jnp.zeros((nsb, tq), jnp.int32)
    for b2 in range(nsb):
        rowv = imp_t[b2:b2 + 1, :]
        ahead = (rowv > imp_t) | ((rowv == imp_t) & (b2 < blk))
        cnt = cnt + ahead.astype(jnp.int32)
    sel_t = ((cnt < N_SLC) & (blk <= own)).astype(F32)
    if nsb < nbp:
        sel_t = jnp.concatenate([sel_t, jnp.zeros((nbp - nsb, tq), F32)], axis=0)
    sel_bias = jnp.where(sel_t.T > 0.5, 0.0, NEG).astype(BF16)
    m_refs = state[:HPG_C]
    acc_refs = state[HPG_C:]
    qs = [qrot_ref[j] for j in range(HPG_C)]

    def reset():
        for ref in m_refs:
            ref[...] = jnp.full(ref.shape, NEG, F32)
        for ref in acc_refs:
            ref[...] = jnp.zeros_like(ref)

    def update(k, v1, bias):
        _flash_multi(qs, [k] * HPG_C, [v1] * HPG_C, [bias] * HPG_C, m_refs, acc_refs)

    eb = lax.broadcasted_iota(jnp.int32, (nbp, ck), 0)
    ej = lax.broadcasted_iota(jnp.int32, (nbp, ck), 1)

    def slc_bias(start):
        expand = ((start + ej) // SLC_BLOCK == eb).astype(BF16)
        return _dot(sel_bias, expand)

    reset()
    c_own = q0 // ck
    start = pl.multiple_of(c_own * ck, ck)
    row = lax.broadcasted_iota(jnp.int32, (tq, ck), 0)
    lane = lax.broadcasted_iota(jnp.int32, (tq, ck), 1)
    update(ks_ref[0, pl.ds(start, ck), :], vs_ref[0, pl.ds(start, ck), :],
           jnp.where(start + lane <= q0 + row, slc_bias(start), NEG))

    def slc_step(cc, carry):
        st = pl.multiple_of(cc * ck, ck)
        update(ks_ref[0, pl.ds(st, ck), :], vs_ref[0, pl.ds(st, ck), :], slc_bias(st))
        return carry

    lax.fori_loop(0, c_own, slc_step, 0)
    o_s = [_flash_result(acc_refs[j][...]) for j in range(HPG_C)]

    reset()
    wk = WINDOW + tq
    start_w = pl.multiple_of(jnp.maximum(q0 - WINDOW, 0), tq)
    dist = (q0 + lax.broadcasted_iota(jnp.int32, (tq, wk), 0)) - (start_w + lax.broadcasted_iota(jnp.int32, (tq, wk), 1))
    update(kw_ref[0, pl.ds(start_w, wk), :], vw_ref[0, pl.ds(start_w, wk), :],
           jnp.where((dist >= 0) & (dist <= WINDOW), 0.0, NEG))

    gt = gt_ref[...]
    for j in range(HPG_C):
        sl = slice(j * tq, (j + 1) * tq)
        g_c = gt[:, j:j + 1]
        g_s = gt[:, HPG_C + j:HPG_C + j + 1]
        g_w = gt[:, 2 * HPG_C + j:2 * HPG_C + j + 1]
        o_ref[j] = (o_c[sl] * g_c + o_s[j] * g_s + _flash_result(acc_refs[j][...]) * g_w).astype(o_ref.dtype)


def _nsa_prompt(qraw_h, qrot_h, gates, kcc_g, vcc_g, ks_h, vs_h, kw_h, vw_h, n_seq, s_len):
    tq = 128
    nq = s_len // tq
    nc = kcc_g.shape[2]
    nsb = -(-s_len // SLC_BLOCK)
    ck = _tile(s_len, (512, 256, 128))
    assert s_len % tq == 0 and nsb <= LANES and WINDOW % tq == 0 and s_len >= WINDOW + tq and ck % tq == 0
    qspec = pl.BlockSpec((HPG_C, tq, HEAD_DIM), lambda b, g, i: (g, b * nq + i, 0))
    cspec = pl.BlockSpec((1, 1, nc, HEAD_DIM), lambda b, g, i: (b, g, 0, 0))
    kspec = pl.BlockSpec((1, s_len, HEAD_DIM), lambda b, g, i: (g, b, 0))
    vspec = pl.BlockSpec((1, s_len, LANES), lambda b, g, i: (g, b, 0))
    return pl.pallas_call(
        functools.partial(_nsa_prompt_kernel, tq=tq, nc=nc, nsb=nsb, ck=ck),
        grid=(n_seq, N_KV_C, nq),
        in_specs=[qspec, qspec, pl.BlockSpec((tq, LANES), lambda b, g, i: (b * nq + i, g)),
                  cspec, cspec, kspec, vspec, kspec, vspec],
        out_specs=qspec,
        out_shape=jax.ShapeDtypeStruct(qraw_h.shape, BF16),
        scratch_shapes=[pltpu.VMEM((tq, 1), F32)] * HPG_C + [pltpu.VMEM((tq, LANES), F32)] * HPG_C,
        compiler_params=_cparams("parallel", "parallel", "arbitrary"),
        name="nsa_prompt",
    )(qraw_h, qrot_h, gates, kcc_g, vcc_g, ks_h, vs_h, kw_h, vw_h)


def _block_mean_kernel(x_ref, o_ref, *, nb):
    x = x_ref[...]
    m = jnp.mean(x.reshape(nb, MOBA_BLOCK, x.shape[1]), axis=1)
    for h in range(N_HEADS):
        o_ref[h] = m[:, h * HEAD_DIM:(h + 1) * HEAD_DIM]


def _block_mean(d_rows):
    m = d_rows.shape[0]
    nb_total = m // MOBA_BLOCK
    nb = _tile(nb_total, (8,))
    return pl.pallas_call(
        functools.partial(_block_mean_kernel, nb=nb),
        grid=(nb_total // nb,),
        in_specs=[pl.BlockSpec((nb * MOBA_BLOCK, HALF_MIX), lambda i: (i, 0))],
        out_specs=pl.BlockSpec((N_HEADS, nb, HEAD_DIM), lambda i: (0, i, 0)),
        out_shape=jax.ShapeDtypeStruct((N_HEADS, nb_total, HEAD_DIM), F32),
        compiler_params=_cparams("parallel"),
        name="moba_block_mean",
    )(d_rows)


def _moba_prompt_kernel(q_ref, km_ref, k_ref, v_ref, o_ref, *state, tq, nb, sub, hb):
    i = pl.program_id(2)
    nh = tq // sub
    nchain = hb * nh
    m_refs = state[:nchain]
    acc_refs = state[nchain:]
    nbp = LANES

    blk_t = lax.broadcasted_iota(jnp.int32, (nb, tq), 0)
    sel_bias = []
    for hh in range(hb):
        gate_t = _dot_nt(km_ref[hh], q_ref[hh].astype(F32) * (1.0 / SCALE), precision=HIGHEST)
        gate_t = jnp.where(blk_t < i, gate_t, NEG)
        cnt = jnp.zeros((nb, tq), jnp.int32)
        for b2 in range(nb):
            rowv = gate_t[b2:b2 + 1, :]
            cnt = cnt + ((rowv > gate_t) | ((rowv == gate_t) & (b2 < blk_t))).astype(jnp.int32)
        sel_t = ((cnt < MOBA_TOPK) & (blk_t < i)).astype(F32)
        sel_t = jnp.concatenate([sel_t, jnp.zeros((nbp - nb, tq), F32)], axis=0)
        sel_bias.append(jnp.where(sel_t.T > 0.5, 0.0, NEG).astype(BF16))

    qs = [q_ref[ci // nh, (ci % nh) * sub:(ci % nh + 1) * sub, :] for ci in range(nchain)]

    def update(start, width, biases):
        ks = [k_ref[hh, pl.ds(start, width), :] for hh in range(hb)]
        v1s = [v_ref[hh, pl.ds(start, width), :] for hh in range(hb)]
        _flash_multi(qs, [ks[ci // nh] for ci in range(nchain)], [v1s[ci // nh] for ci in range(nchain)], biases,
                     m_refs, acc_refs)

    def block_bias(expand):
        return [_dot(sel_bias[ci // nh][(ci % nh) * sub:(ci % nh + 1) * sub, :], expand) for ci in range(nchain)]

    for ref in m_refs:
        ref[...] = jnp.full(ref.shape, NEG, F32)
    for ref in acc_refs:
        ref[...] = jnp.zeros_like(ref)

    r = lax.broadcasted_iota(jnp.int32, (sub, tq), 0)
    c = lax.broadcasted_iota(jnp.int32, (sub, tq), 1)
    update(pl.multiple_of(i * tq, tq), tq, [jnp.where(c <= r + (ci % nh) * sub, 0.0, NEG) for ci in range(nchain)])

    pk = 2 * tq
    eb2 = lax.broadcasted_iota(jnp.int32, (nbp, pk), 0)
    ej2 = lax.broadcasted_iota(jnp.int32, (nbp, pk), 1)

    def pair_step(cc, carry):
        update(pl.multiple_of(cc * pk, pk), pk, block_bias((eb2 == 2 * cc + ej2 // tq).astype(BF16)))
        return carry

    lax.fori_loop(0, i // 2, pair_step, 0)

    @pl.when(i % 2 == 1)
    def _():
        eb1 = lax.broadcasted_iota(jnp.int32, (nbp, tq), 0)
        update(pl.multiple_of((i - 1) * tq, tq), tq, block_bias((eb1 == i - 1).astype(BF16)))

    for ci in range(nchain):
        o_ref[ci // nh, (ci % nh) * sub:(ci % nh + 1) * sub, :] = _flash_result(acc_refs[ci][...]).astype(o_ref.dtype)


def _moba_prompt(qd_h, kmean_h, kd_h, vd_h, n_seq, s_len):
    tq = MOBA_BLOCK
    assert s_len % tq == 0
    nq = s_len // tq
    assert nq <= LANES
    sub = 128
    hb = 4
    nchain = hb * (tq // sub)
    kspec = pl.BlockSpec((hb, s_len, HEAD_DIM), lambda b, h, i: (h, b, 0))
    vspec = pl.BlockSpec((hb, s_len, LANES), lambda b, h, i: (h, b, 0))
    qspec = pl.BlockSpec((hb, tq, HEAD_DIM), lambda b, h, i: (h, b * nq + i, 0))
    return pl.pallas_call(
        functools.partial(_moba_prompt_kernel, tq=tq, nb=nq, sub=sub, hb=hb),
        grid=(n_seq, N_HEADS // hb, nq),
        in_specs=[qspec, pl.BlockSpec((hb, nq, HEAD_DIM), lambda b, h, i: (h, b, 0)), kspec, vspec],
        out_specs=qspec,
        out_shape=jax.ShapeDtypeStruct(qd_h.shape, BF16),
        scratch_shapes=[pltpu.VMEM((sub, 1), F32)] * nchain + [pltpu.VMEM((sub, LANES), F32)] * nchain,
        compiler_params=_cparams("parallel", "parallel", "arbitrary"),
        name="moba_prompt",
    )(qd_h, kmean_h, kd_h, vd_h)


def _page_specs(n_slots, n_pages, width, col_block, base, reverse):
    n_steps = n_pages // n_slots

    def make(s):
        if reverse:
            return lambda b, j, pt: (base + pt[b, (n_steps - 1 - j) * n_slots + (n_slots - 1 - s)], 0, col_block)
        return lambda b, j, pt: (base + pt[b, j * n_slots + s], 0, col_block)

    return [pl.BlockSpec((None, PAGE_SIZE, width), make(s)) for s in range(n_slots)]


def _head_diag(rows, width, per):
    r = lax.broadcasted_iota(jnp.int32, (rows, width), 0)
    c = lax.broadcasted_iota(jnp.int32, (rows, width), 1)
    return ((r % N_HEADS) * HEAD_DIM // per) == (c // HEAD_DIM * HEAD_DIM // per)


def _fold_heads(acc, t):
    kept = jnp.where(_head_diag(acc.shape[0], acc.shape[1], HEAD_DIM), acc, 0.0)
    return jnp.sum(kept.reshape(t, N_HEADS, acc.shape[1]), axis=1)


def _sb_sample_kernel(pt_ref, q_ref, kn_ref, vn_ref, pool_ref, o_ref, buf, sem, tri_s, *, n_pages, t, li):
    b = pl.program_id(0)
    rows = t * N_HEADS
    flat = PAGE_SIZE * N_HEADS
    new_flat = kn_ref.shape[1]
    q = q_ref[0]
    tri_s[...] = _tri_incl(flat)

    def page_copy(p, slot):
        return pltpu.make_async_copy(pool_ref.at[li, pt_ref[b, p]], buf.at[slot], sem.at[slot])

    page_copy(n_pages - 1, 0).start()

    def head_match(width):
        r = lax.broadcasted_iota(jnp.int32, (rows, width), 0)
        c = lax.broadcasted_iota(jnp.int32, (rows, width), 1)
        return r, c, (r % N_HEADS) == (c % N_HEADS)

    r, c, same = head_match(new_flat)
    carry, acc = _sb_tile(q, kn_ref[0], vn_ref[0], tri_s[0:new_flat, 0:new_flat], jnp.zeros((rows, 1), F32),
                          jnp.zeros((rows, HEAD_DIM), F32), same & (c // N_HEADS < r // N_HEADS))
    _, _, page_mask = head_match(flat)

    def cond(s):
        p, carry, _ = s
        return jnp.logical_and(p >= 0, jnp.max(carry) > -SB_CUTOFF)

    def body(s):
        p, carry, acc = s
        slot = (n_pages - 1 - p) % 2
        page_copy(p, slot).wait()

        @pl.when(p > 0)
        def _():
            page_copy(p - 1, 1 - slot).start()

        k = buf[slot, :, 0].reshape(flat, HEAD_DIM).astype(BF16)
        v = buf[slot, :, 1].reshape(flat, HEAD_DIM).astype(BF16)
        carry, acc = _sb_tile(q, k, v, tri_s[...], carry, acc, page_mask)
        return p - 1, carry, acc

    p_end, _, acc = lax.while_loop(cond, body, (jnp.int32(n_pages - 1), carry, acc))

    @pl.when(p_end >= 0)
    def _():
        page_copy(p_end, (n_pages - 1 - p_end) % 2).wait()

    o_ref[0] = acc


def _sb_sample(q, k_new, v_new, cache, li, page_table):
    n, rows, _ = q.shape
    t = rows // N_HEADS
    n_pages = page_table.shape[1]
    flat = PAGE_SIZE * N_HEADS
    seq = lambda b, pt: (b, 0, 0)
    gs = pltpu.PrefetchScalarGridSpec(
        num_scalar_prefetch=1, grid=(n,),
        in_specs=[pl.BlockSpec((1, rows, HEAD_DIM), seq), pl.BlockSpec((1, k_new.shape[1], HEAD_DIM), seq),
                  pl.BlockSpec((1, k_new.shape[1], HEAD_DIM), seq), pl.BlockSpec(memory_space=pl.ANY)],
        out_specs=pl.BlockSpec((1, rows, HEAD_DIM), seq),
        scratch_shapes=[pltpu.VMEM((2,) + cache.shape[2:], cache.dtype), pltpu.SemaphoreType.DMA((2,)),
                        pltpu.VMEM((flat, flat), BF16)])
    return pl.pallas_call(
        functools.partial(_sb_sample_kernel, n_pages=n_pages, t=t, li=li),
        grid_spec=gs,
        out_shape=jax.ShapeDtypeStruct((n, rows, HEAD_DIM), F32),
        compiler_params=_cparams("arbitrary"),
        name="sb_sample",
    )(page_table, q, k_new, v_new, cache)


def _nsa_sample_head_kernel(qraw_ref, qrot_ref, kcc_ref, vcc_ref, win_ref, wn_ref, oc_ref, ow_ref, sel_ref,
                            *, t, nc, n_past, nbp, pad):
    rows = t * N_HEADS
    wb = win_ref.shape[1]
    tok = lax.broadcasted_iota(jnp.int32, (rows, 1), 0) // N_HEADS
    qpos = n_past + tok

    def fold(x):
        g0 = (lax.broadcasted_iota(jnp.int32, (rows, HEAD_DIM), 0) % N_HEADS) < HPG_C
        return jnp.where(g0, x[:, 0:HEAD_DIM], x[:, HEAD_DIM:2 * HEAD_DIM])

    sc = _dot_nt(qraw_ref[0], kcc_ref[0])
    cidx = lax.broadcasted_iota(jnp.int32, (1, nc), 1)
    pc = _masked_softmax(sc, (cidx * CMP_STRIDE + CMP_LEN - 1 <= qpos) & (cidx < nc - 1))
    oc_ref[0] = fold(_dot(pc.astype(BF16), vcc_ref[0]))

    ng = t * N_KV_C
    gr = lax.broadcasted_iota(jnp.int32, (ng, rows), 0)
    rr = lax.broadcasted_iota(jnp.int32, (ng, rows), 1)
    same = (gr // N_KV_C == rr // N_HEADS) & (gr % N_KV_C == (rr % N_HEADS) // HPG_C)
    psum = _dot(same.astype(F32), pc, precision=HIGHEST)
    imp = _dot_nt(psum, _overlap_t(nbp, nc), precision=HIGHEST)
    blk = lax.broadcasted_iota(jnp.int32, (ng, nbp), 1)
    own = (n_past + lax.broadcasted_iota(jnp.int32, (ng, nbp), 0) // N_KV_C) // SLC_BLOCK
    imp = jnp.where(blk == own, BIG, jnp.where(blk < own, imp, NEG))
    sel_ref[0] = _top_select(imp, blk, N_SLC)

    qrot = qrot_ref[0]
    win = win_ref[0]
    wn = wn_ref[0]
    s1 = _dot_nt(qrot, win[:, 0:LANES].astype(BF16))
    wpos1 = n_past - wb + lax.broadcasted_iota(jnp.int32, (1, wb), 1)
    d1 = qpos - wpos1
    mk1 = (d1 >= 0) & (d1 <= WINDOW) & (wpos1 >= 0)
    s2 = _dot_nt(qrot, wn[:, 0:LANES])
    j2 = lax.broadcasted_iota(jnp.int32, (1, pad), 1)
    mk2 = (j2 <= tok) & (j2 < t)
    m = jnp.maximum(jnp.max(jnp.where(mk1, s1, NEG), axis=-1, keepdims=True),
                    jnp.max(jnp.where(mk2, s2, NEG), axis=-1, keepdims=True))
    p1 = jnp.where(mk1, jnp.exp(jnp.where(mk1, s1, NEG) - m), 0.0)
    p2 = jnp.where(mk2, jnp.exp(jnp.where(mk2, s2, NEG) - m), 0.0)
    den = jnp.sum(p1, axis=-1, keepdims=True) + jnp.sum(p2, axis=-1, keepdims=True)
    ow = _dot(p1.astype(BF16), win[:, LANES:2 * LANES].astype(BF16)) + _dot(p2.astype(BF16), wn[:, LANES:2 * LANES])
    ow_ref[0] = fold(ow / den)


def _nsa_sample_head(qraw_bd, qrot_bd, kcc_f, vcc_f, win, w_new, n_past, nbp):
    n, rows, _ = qraw_bd.shape
    t = rows // N_HEADS
    nc = kcc_f.shape[1]
    wb = win.shape[1]
    pad = w_new.shape[1]
    seq = lambda b: (b, 0, 0)
    return pl.pallas_call(
        functools.partial(_nsa_sample_head_kernel, t=t, nc=nc, n_past=n_past, nbp=nbp, pad=pad),
        grid=(n,),
        in_specs=[pl.BlockSpec((1, rows, LANES), seq), pl.BlockSpec((1, rows, LANES), seq),
                  pl.BlockSpec((1, nc, LANES), seq), pl.BlockSpec((1, nc, LANES), seq),
                  pl.BlockSpec((1, wb, 2 * LANES), seq), pl.BlockSpec((1, pad, 2 * LANES), seq)],
        out_specs=[pl.BlockSpec((1, rows, HEAD_DIM), seq), pl.BlockSpec((1, rows, HEAD_DIM), seq),
                   pl.BlockSpec((1, t * N_KV_C, nbp), seq)],
        out_shape=[jax.ShapeDtypeStruct((n, rows, HEAD_DIM), F32), jax.ShapeDtypeStruct((n, rows, HEAD_DIM), F32),
                   jax.ShapeDtypeStruct((n, t * N_KV_C, nbp), F32)],
        compiler_params=_cparams("parallel"),
        name="nsa_sample_head",
    )(qraw_bd, qrot_bd, kcc_f, vcc_f, win, w_new)


def _nsa_sample_slc_kernel(pt_ref, q_ref, sel_ref, sn_ref, oc_ref, ow_ref, gt_ref, *refs,
                           n_slots, n_steps, t, nbp, pad):
    page_refs = refs[:n_slots]
    o_ref, m_s, l_s, acc_s = refs[n_slots:]
    j = pl.program_id(1)
    rows = t * N_HEADS
    q = q_ref[0]
    tok = lax.broadcasted_iota(jnp.int32, (rows, 1), 0) // N_HEADS

    @pl.when(j == 0)
    def _():
        sn = sn_ref[0]
        j2 = lax.broadcasted_iota(jnp.int32, (1, pad), 1)
        m, l, acc = _online_step(_dot_nt(q, sn[:, 0:LANES]), (j2 <= tok) & (j2 < t), sn[:, LANES:2 * LANES],
                                 jnp.full((rows, 1), NEG, F32), jnp.zeros((rows, 1), F32),
                                 jnp.zeros((rows, LANES), F32))
        m_s[...] = m
        l_s[...] = l
        acc_s[...] = acc

    sel_bias = sel_ref[0]
    eb = lax.broadcasted_iota(jnp.int32, (nbp, PAGE_SIZE), 0)
    ej = lax.broadcasted_iota(jnp.int32, (nbp, PAGE_SIZE), 1)
    group = 8
    for s0 in range(0, n_slots, group):
        scores, vals = [], []
        for s in range(s0, min(s0 + group, n_slots)):
            first_blk = (j * n_slots + s) * (PAGE_SIZE // SLC_BLOCK)
            bias = _dot(sel_bias, (first_blk + ej // SLC_BLOCK == eb).astype(BF16))
            page = page_refs[s][...]
            scores.append(_dot_nt(q, page[:, 0:LANES].astype(BF16)) + bias)
            vals.append(page[:, LANES:2 * LANES].astype(BF16))
        _joint_update(m_s, l_s, acc_s, scores, vals)

    @pl.when(j == n_steps - 1)
    def _():
        o = acc_s[...] / l_s[...]
        g0 = (lax.broadcasted_iota(jnp.int32, (rows, HEAD_DIM), 0) % N_HEADS) < HPG_C
        o_s = jnp.where(g0, o[:, 0:HEAD_DIM], o[:, HEAD_DIM:2 * HEAD_DIM])
        gt = gt_ref[0]
        o_ref[0] = oc_ref[0] * gt[:, 0:1] + o_s * gt[:, 1:2] + ow_ref[0] * gt[:, 2:3]


def _nsa_sample_slc(qrot_bd, sel_rows, s_new, o_c, o_w, gates, pool, base, page_table):
    n, rows, _ = qrot_bd.shape
    t = rows // N_HEADS
    nbp = sel_rows.shape[2]
    pad = s_new.shape[1]
    n_pages = page_table.shape[1]
    n_slots = _tile(n_pages, (16, 8, 4, 2))
    n_steps = n_pages // n_slots
    seq = lambda b, j, pt: (b, 0, 0)
    gs = pltpu.PrefetchScalarGridSpec(
        num_scalar_prefetch=1, grid=(n, n_steps),
        in_specs=[pl.BlockSpec((1, rows, LANES), seq), pl.BlockSpec((1, rows, nbp), seq),
                  pl.BlockSpec((1, pad, 2 * LANES), seq), pl.BlockSpec((1, rows, HEAD_DIM), seq),
                  pl.BlockSpec((1, rows, HEAD_DIM), seq), pl.BlockSpec((1, rows, 3), seq)]
        + _page_specs(n_slots, n_pages, 2 * LANES, 1, base, False),
        out_specs=pl.BlockSpec((1, rows, HEAD_DIM), seq),
        scratch_shapes=[pltpu.VMEM((rows, 1), F32), pltpu.VMEM((rows, 1), F32), pltpu.VMEM((rows, LANES), F32)])
    return pl.pallas_call(
        functools.partial(_nsa_sample_slc_kernel, n_slots=n_slots, n_steps=n_steps, t=t, nbp=nbp, pad=pad),
        grid_spec=gs,
        out_shape=jax.ShapeDtypeStruct((n, rows, HEAD_DIM), F32),
        compiler_params=_cparams("parallel", "arbitrary"),
        name="nsa_sample_slc",
    )(page_table, qrot_bd, sel_rows, s_new, o_c, o_w, gates, *([pool] * n_slots))


def _page_sum_kernel(pt_ref, *refs, n_slots):
    o_ref = refs[n_slots]
    ppb = MOBA_BLOCK // PAGE_SIZE
    for blk in range(n_slots // ppb):
        tot = jnp.sum(refs[blk * ppb][...].astype(F32), axis=0, keepdims=True)
        for p in range(1, ppb):
            tot = tot + jnp.sum(refs[blk * ppb + p][...].astype(F32), axis=0, keepdims=True)
        o_ref[0, blk:blk + 1, :] = tot


def _page_sums(pool, base, page_table):
    n, n_pages = page_table.shape
    ppb = MOBA_BLOCK // PAGE_SIZE
    n_slots = _tile(n_pages, (8 * ppb,))
    assert n_slots % ppb == 0
    gs = pltpu.PrefetchScalarGridSpec(
        num_scalar_prefetch=1, grid=(n, n_pages // n_slots),
        in_specs=_page_specs(n_slots, n_pages, HALF_MIX, 0, base, False),
        out_specs=pl.BlockSpec((1, n_slots // ppb, HALF_MIX), lambda b, j, pt: (b, j, 0)))
    return pl.pallas_call(
        functools.partial(_page_sum_kernel, n_slots=n_slots),
        grid_spec=gs,
        out_shape=jax.ShapeDtypeStruct((n, n_pages // ppb, HALF_MIX), F32),
        compiler_params=_cparams("parallel", "arbitrary"),
        name="moba_page_sums",
    )(page_table, *([pool] * n_slots))


def _moba_sample_kernel(pt_ref, q_ref, qf_ref, ks_ref, kn_ref, vn_ref, *refs, n_slots, n_steps, t, nbp, pad,
                        n_past):
    page_refs = refs[:n_slots]
    o_ref, sel_s, m_s, l_s, acc_s = refs[n_slots:]
    j = pl.program_id(1)
    rows = t * N_HEADS
    ppb = MOBA_BLOCK // PAGE_SIZE
    nb_full = ks_ref.shape[1]
    q = q_ref[0]
    tok = lax.broadcasted_iota(jnp.int32, (rows, 1), 0) // N_HEADS

    @pl.when(j == 0)
    def _():
        kmean = ks_ref[0] * (1.0 / MOBA_BLOCK)
        gate = _dot_nt(qf_ref[0], kmean, precision=HIGHEST)
        if nb_full < nbp:
            gate = jnp.concatenate([gate, jnp.full((rows, nbp - nb_full), NEG, F32)], axis=1)
        blk = lax.broadcasted_iota(jnp.int32, (rows, nbp), 1)
        own = (n_past + tok) // MOBA_BLOCK
        gate = jnp.where((blk < own) & (blk < nb_full), gate, NEG)
        sel_s[...] = jnp.where(_top_select(gate, blk, MOBA_TOPK) > 0.5, 0.0, NEG).astype(BF16)
        j2 = lax.broadcasted_iota(jnp.int32, (1, pad), 1)
        m, l, acc = _online_step(_dot_nt(q, kn_ref[0]), (j2 <= tok) & (j2 < t), vn_ref[0],
                                 jnp.full((rows, 1), NEG, F32), jnp.zeros((rows, 1), F32),
                                 jnp.zeros((rows, HALF_MIX), F32))
        m_s[...] = m
        l_s[...] = l
        acc_s[...] = acc

    sel_bias = sel_s[...]
    eb = lax.broadcasted_iota(jnp.int32, (nbp, PAGE_SIZE), 0)
    scores, vals = [], []
    for s in range(n_slots):
        blk_of_page = (j * n_slots + s) // ppb
        bias = _dot(sel_bias, (eb == blk_of_page).astype(BF16))
        page = page_refs[s][...]
        scores.append(_dot_nt(q, page[:, 0:HALF_MIX].astype(BF16)) + bias)
        vals.append(page[:, HALF_MIX:].astype(BF16))
    _joint_update(m_s, l_s, acc_s, scores, vals)

    @pl.when(j == n_steps - 1)
    def _():
        o_ref[0] = _fold_heads(acc_s[...] / l_s[...], t)


def _moba_sample(qbd, qbd_f32, ksums, k_new, v_new, pool, base, page_table, n_past):
    n, rows, _ = qbd.shape
    t = rows // N_HEADS
    pad = k_new.shape[1]
    n_pages = page_table.shape[1]
    assert n_past % MOBA_BLOCK == 0 and t <= MOBA_BLOCK
    nbp = LANES * (-(-(n_pages * PAGE_SIZE // MOBA_BLOCK) // LANES))
    n_slots = _tile(n_pages, (8, 4, 2))
    n_steps = n_pages // n_slots
    seq = lambda b, j, pt: (b, 0, 0)
    gs = pltpu.PrefetchScalarGridSpec(
        num_scalar_prefetch=1, grid=(n, n_steps),
        in_specs=[pl.BlockSpec((1, rows, HALF_MIX), seq), pl.BlockSpec((1, rows, HALF_MIX), seq),
                  pl.BlockSpec((1, ksums.shape[1], HALF_MIX), seq),
                  pl.BlockSpec((1, pad, HALF_MIX), seq), pl.BlockSpec((1, pad, HALF_MIX), seq)]
        + _page_specs(n_slots, n_pages, 2 * HALF_MIX, 0, base, False),
        out_specs=pl.BlockSpec((1, t, HALF_MIX), seq),
        scratch_shapes=[pltpu.VMEM((rows, nbp), BF16), pltpu.VMEM((rows, 1), F32), pltpu.VMEM((rows, 1), F32),
                        pltpu.VMEM((rows, HALF_MIX), F32)])
    return pl.pallas_call(
        functools.partial(_moba_sample_kernel, n_slots=n_slots, n_steps=n_steps, t=t, nbp=nbp, pad=pad,
                          n_past=n_past),
        grid_spec=gs,
        out_shape=jax.ShapeDtypeStruct((n, t, HALF_MIX), F32),
        compiler_params=_cparams("parallel", "arbitrary"),
        name="moba_sample",
    )(page_table, qbd, qbd_f32, ksums, k_new, v_new, *([pool] * n_slots))


def _block_diag(blocks):
    nb, bi, bj = blocks.shape
    eye = jnp.eye(nb, dtype=blocks.dtype)
    return jnp.einsum('bij,bc->bicj', blocks, eye).reshape(nb * bi, nb * bj)


def _rope_tables(pos):
    half = HEAD_DIM // 2
    inv_freq = ROPE_THETA ** (-jnp.arange(half, dtype=F32) / half)
    ang = pos.astype(F32)[:, None] * inv_freq[None, :]
    cos = jnp.cos(ang)
    sin = jnp.sin(ang)
    return jnp.concatenate([cos, cos, cos, cos], axis=1), jnp.concatenate([-sin, sin, -sin, sin], axis=1)


def _queries_bd(q, per, scale, dtype):
    n, t, _ = q.shape
    groups = N_HEADS // per
    qh = q.reshape(n, t, N_HEADS, 1, HEAD_DIM) * scale
    pick = (jnp.arange(N_HEADS)[:, None] // per == jnp.arange(groups)[None, :]).astype(q.dtype)
    return (qh * pick[None, None, :, :, None]).reshape(n, t * N_HEADS, groups * HEAD_DIM).astype(dtype)


def _pad_rows(x, pad):
    n, t, w = x.shape
    return jnp.concatenate([x, jnp.zeros((n, pad - t, w), x.dtype)], axis=1)


def _odd_weight(w_in):
    d = w_in.shape[0]
    gc0 = _O_VW + 128
    main = jnp.concatenate([w_in[:, :gc0], w_in[:, gc0 + 3 * N_HEADS:]], axis=1)
    gc = w_in[:, gc0:gc0 + 3 * N_HEADS].reshape(d, N_KV_C, HPG_C, 3).transpose(0, 1, 3, 2).reshape(d, N_KV_C, 3 * HPG_C)
    gc = jnp.concatenate([gc, jnp.zeros((d, N_KV_C, LANES - 3 * HPG_C), w_in.dtype)], axis=2).reshape(d, N_KV_C * LANES)
    return jnp.concatenate([main, gc], axis=1).astype(BF16)


def kernel(x_prompt, x_sample, cache_a_kv, state_b_h, state_b_conv, cache_c_kv, state_c_win, cache_d_kv, page_table, norm_ffn1, w_ffn1_gate, w_ffn1_up, w_ffn1_down, norm_mix, norm_ffn2, w_ffn2_gate, w_ffn2_up, w_ffn2_down, w_in_even, w_out_even, lru_conv_w, lru_conv_b, lru_w_r, lru_b_r, lru_w_i, lru_b_i, lru_lambda, w_in_odd, w_out_odd, cmp_pe_k, cmp_w1_k, cmp_w2_k, cmp_pe_v, cmp_w1_v, cmp_w2_v, norm_final):
    nb, s_len, d = x_prompt.shape
    ns, t = x_sample.shape[:2]
    depth = norm_mix.shape[0]
    n_pages = page_table.shape[1]
    n_past = n_pages * PAGE_SIZE
    n_pool = cache_a_kv.shape[1]
    hm = HALF_MIX
    pad = 16
    assert t <= pad and n_past % SLC_BLOCK == 0

    yp = x_prompt.reshape(nb * s_len, d)
    ys = x_sample.reshape(ns * t, d)
    cos_p, sin_p = _rope_tables(jnp.arange(s_len, dtype=jnp.int32))
    cos_s, sin_s = _rope_tables(jnp.tile(n_past + jnp.arange(t, dtype=jnp.int32), ns))
    pool_c = cache_c_kv.reshape(-1, PAGE_SIZE, 4 * N_KV_C * HEAD_DIM)
    pool_d = cache_d_kv.reshape(-1, PAGE_SIZE, 2 * hm)

    outs = {k: [] for k in ("a_p", "a_s", "bh_p", "bh_s", "bc_p", "bc_s", "c_p", "c_s", "cw_p", "cw_s", "d_p", "d_s")}
    for layer in range(depth):
        li = layer // 2
        last = layer == depth - 1
        f1 = (norm_ffn1[layer], w_ffn1_gate[layer].astype(BF16), w_ffn1_up[layer].astype(BF16),
              w_ffn1_down[layer].astype(BF16))
        f2 = (norm_ffn2[layer], w_ffn2_gate[layer].astype(BF16), w_ffn2_up[layer].astype(BF16),
              w_ffn2_down[layer].astype(BF16))
        yp = _half_ffn(yp, *f1)
        ys = _half_ffn(ys, *f1)
        if layer % 2 == 0:
            w_in = w_in_even[li].astype(BF16)
            w_out = w_out_even[li].astype(BF16)
            lru = (lru_conv_w[li], lru_conv_b[li], _block_diag(lru_w_r[li]).astype(BF16), lru_b_r[li],
                   _block_diag(lru_w_i[li]).astype(BF16), lru_b_i[li], lru_lambda[li])
            kv, xb, gb, qh, kh, vh = _proj_even(yp, norm_mix[layer], w_in, True)
            o_a = _sb_prompt(qh, kh, vh, nb, s_len)
            g, h_new, buf_new = _rg_lru(xb.reshape(nb, s_len, hm), gb.reshape(nb, s_len, hm),
                                        jnp.zeros((nb, hm), F32), jnp.zeros((nb, CONV_WIDTH - 1, hm), F32), *lru)
            yp = _outproj(yp, o_a, g.reshape(nb * s_len, hm), w_out)
            outs["a_p"].append(kv.reshape(nb, s_len, 2, N_HEADS, HEAD_DIM))
            outs["bh_p"].append(h_new)
            outs["bc_p"].append(buf_new)
            kv, xb, gb, q = _proj_even(ys, norm_mix[layer], w_in, False)
            kv3 = kv.reshape(ns, t, 2 * hm)
            per_head = lambda a: _pad_rows(a, pad).astype(BF16).reshape(ns, pad * N_HEADS, HEAD_DIM)
            o_a = _sb_sample((q * SCALE).astype(BF16).reshape(ns, t * N_HEADS, HEAD_DIM), per_head(kv3[:, :, :hm]),
                             per_head(kv3[:, :, hm:]), cache_a_kv, li, page_table)
            g, h_new, buf_new = _rg_lru(xb.reshape(ns, t, hm), gb.reshape(ns, t, hm), state_b_h[li],
                                        state_b_conv[li], *lru)
            ys = _outproj(ys, o_a.reshape(ns * t, hm).astype(BF16), g.reshape(ns * t, hm), w_out)
            outs["a_s"].append(kv.reshape(ns, t, 2, N_HEADS, HEAD_DIM))
            outs["bh_s"].append(h_new)
            outs["bc_s"].append(buf_new)
        else:
            w_in = _odd_weight(w_in_odd[li])
            w_out = w_out_odd[li].astype(BF16)
            cmp_k = _cmp_weights(cmp_pe_k[li], cmp_w1_k[li])
            cmp_v = _cmp_weights(cmp_pe_v[li], cmp_w1_v[li])
            w2k = _block_diag(jnp.stack([cmp_w2_k[li]] * N_KV_C)).astype(BF16)
            w2v = _block_diag(jnp.stack([cmp_w2_v[li]] * N_KV_C)).astype(BF16)
            (c_rows, w_rows, d_rows, gates, qraw_h, qrot_h, ks_h, vs_h, kw_h, vw_h, qd_h, kd_h, vd_h) = _proj_odd(
                yp, norm_mix[layer], w_in, cos_p, sin_p, True)
            n_chunk = s_len // CMP_STRIDE
            rows_c = _tile(s_len, (2048, 1024, 512, 256, 128))
            ab = _cmp_ab([c_rows], [pl.BlockSpec((rows_c, LANES), lambda b, j: (b * (s_len // rows_c) + j, 0))],
                         [pl.BlockSpec((rows_c, LANES), lambda b, j: (b * (s_len // rows_c) + j, 1))],
                         (nb, s_len // rows_c), rows_c // CMP_STRIDE, nb, n_chunk, rows_c, cmp_k, cmp_v, 0, ())
            _, _, kcc_g, vcc_g = _cmp_finish(ab, w2k, w2v)
            o_c = _nsa_prompt(qraw_h, qrot_h, gates, kcc_g, vcc_g, ks_h, vs_h, kw_h, vw_h, nb, s_len)
            o_d = _moba_prompt(qd_h, _block_mean(d_rows), kd_h, vd_h, nb, s_len)
            yp = _outproj(yp, o_c, o_d, w_out)
            wlen = min(WINDOW, s_len)
            outs["c_p"].append(c_rows.reshape(nb, s_len, 4, N_KV_C, HEAD_DIM))
            outs["cw_p"].append(w_rows.reshape(nb, s_len, 2, N_KV_C, HEAD_DIM)[:, s_len - wlen:])
            outs["d_p"].append(d_rows.reshape(nb, s_len, 2, N_HEADS, HEAD_DIM))
            c_rows, w_rows, d_rows, gates, qraw, qrot, qd = _proj_odd(ys, norm_mix[layer], w_in, cos_s, sin_s, False)
            slots_c = _tile(n_pages, (16, 8))
            ab = _cmp_ab([pool_c] * slots_c, _page_specs(slots_c, n_pages, LANES, 0, li * n_pool, False),
                         _page_specs(slots_c, n_pages, LANES, 1, li * n_pool, False),
                         (ns, n_pages // slots_c), slots_c * PAGE_SIZE // CMP_STRIDE, ns, n_past // CMP_STRIDE,
                         PAGE_SIZE, cmp_k, cmp_v, 1, (page_table,))
            kcc_f, vcc_f, _, _ = _cmp_finish(ab, w2k, w2v)
            nsb = -(-(n_past + t) // SLC_BLOCK)
            nbp = LANES * (-(-nsb // LANES))
            win_l = state_c_win[li]
            wb = win_l.shape[1]
            qraw_bd = _queries_bd(qraw.reshape(ns, t, hm), HPG_C, SCALE, BF16)
            qrot_bd = _queries_bd(qrot.reshape(ns, t, hm), HPG_C, SCALE, BF16)
            c3 = c_rows.reshape(ns, t, 4 * LANES)
            w3 = w_rows.reshape(ns, t, 2 * LANES)
            d3 = d_rows.reshape(ns, t, 2 * hm)
            o_cmp, o_win, sel = _nsa_sample_head(qraw_bd, qrot_bd, kcc_f, vcc_f, win_l.reshape(ns, wb, 2 * LANES),
                                                 _pad_rows(w3, pad).astype(BF16), n_past, nbp)
            sel_rows = jnp.repeat(sel.reshape(ns, t, N_KV_C, nbp), HPG_C, axis=2).reshape(ns, t * N_HEADS, nbp)
            gt = gates.reshape(ns, t, N_KV_C, LANES)[..., :3 * HPG_C].reshape(ns, t, N_KV_C, 3, HPG_C)
            gt = gt.transpose(0, 1, 2, 4, 3).reshape(ns, t * N_HEADS, 3)
            sel_bias = jnp.where(sel_rows > 0.5, 0.0, NEG).astype(BF16)
            o_c = _nsa_sample_slc(qrot_bd, sel_bias, _pad_rows(c3[:, :, 2 * LANES:], pad).astype(BF16),
                                  o_cmp, o_win, gt, pool_c, li * n_pool, page_table)
            ksums = _page_sums(pool_d, li * n_pool, page_table)
            qd3 = qd.reshape(ns, t, hm)
            o_d = _moba_sample(_queries_bd(qd3, 1, SCALE, BF16), _queries_bd(qd3, 1, 1.0, F32), ksums,
                               _pad_rows(d3[:, :, :hm], pad).astype(BF16), _pad_rows(d3[:, :, hm:], pad).astype(BF16),
                               pool_d, li * n_pool, page_table, n_past)
            ys = _outproj(ys, o_c.reshape(ns * t, hm).astype(BF16), o_d.reshape(ns * t, hm).astype(BF16), w_out)
            w5 = w_rows.reshape(ns, t, 2, N_KV_C, HEAD_DIM)
            outs["c_s"].append(c_rows.reshape(ns, t, 4, N_KV_C, HEAD_DIM))
            outs["cw_s"].append(jnp.concatenate([win_l, w5], axis=1)[:, -wb:])
            outs["d_s"].append(d_rows.reshape(ns, t, 2, N_HEADS, HEAD_DIM))
        gf = norm_final if last else None
        yp = _half_ffn(yp, *f2, g_final=gf)
        ys = _half_ffn(ys, *f2, g_final=gf)
    st = lambda k: jnp.stack(outs[k])
    return (yp.reshape(nb, s_len, d), ys.reshape(ns, t, d), st("a_p"), st("a_s"), st("bh_p"), st("bh_s"),
            st("bc_p"), st("bc_s"), st("c_p"), st("c_s"), st("cw_p"), st("cw_s"), st("d_p"), st("d_s"))
```

```python
import functools
import math

import jax
import jax.numpy as jnp
import numpy as np
from jax import lax
from jax.experimental import pallas as pl
from jax.experimental.pallas import tpu as pltpu

F32 = jnp.float32
BF16 = jnp.bfloat16

HEAD_DIM = 64
HALF_MIX = 512
N_HEADS = HALF_MIX // HEAD_DIM
N_KV_C = 2
HPG_C = N_HEADS // N_KV_C
PAGE_SIZE = 128
CONV_WIDTH = 4
LRU_C = 8.0
CMP_LEN = 32
CMP_STRIDE = 16
SLC_BLOCK = 64
N_SLC = 16
WINDOW = 512
MOBA_BLOCK = 256
MOBA_TOPK = 3
ROPE_THETA = 10000.0
RMS_EPS = 1e-6
NEG = -1e30
BIG = 1e30
SCALE = HEAD_DIM ** -0.5

LANES = 128
VMEM_LIMIT_BYTES = 56 * 1024 * 1024
SB_CUTOFF = 120.0
HIGHEST = lax.Precision.HIGHEST


def _cparams(*sem):
    return pltpu.CompilerParams(dimension_semantics=sem, vmem_limit_bytes=VMEM_LIMIT_BYTES)


def _tile(n, prefs):
    for t in prefs:
        if n % t == 0:
            return t
    return n


def _dot(a, b, precision=None):
    return jnp.dot(a, b, preferred_element_type=F32, precision=precision)


def _dot_nt(a, b, precision=None):
    return lax.dot_general(a, b, (((1,), (1,)), ((), ())), preferred_element_type=F32, precision=precision)


def _rms(x, g):
    ms = jnp.mean(x * x, axis=-1, keepdims=True)
    return x * lax.rsqrt(ms + RMS_EPS) * g


def _softplus(z):
    return jnp.maximum(z, 0.0) + jnp.log(1.0 + jnp.exp(-jnp.abs(z)))


def _one_minus_exp(x):
    poly = x
    for k in range(7, 1, -1):
        poly = x * (1.0 + poly * (1.0 / k))
    return jnp.where(x > -0.125, -poly, 1.0 - jnp.exp(x))


def _split_dot(x, t):
    hi = x.astype(BF16)
    lo = (x - hi.astype(F32)).astype(BF16)
    return _dot(hi, t) + _dot(lo, t)


def _masked_softmax(s, mask):
    sm = jnp.where(mask, s, NEG)
    m = jnp.max(sm, axis=-1, keepdims=True)
    e = jnp.where(mask, jnp.exp(sm - m), 0.0)
    d = jnp.sum(e, axis=-1, keepdims=True)
    return e / jnp.maximum(d, 1e-30)


def _online_step(s, mask, v, m, l, acc):
    sm = jnp.where(mask, s, NEG)
    m_new = jnp.maximum(m, jnp.max(sm, axis=-1, keepdims=True))
    alpha = jnp.exp(m - m_new)
    p = jnp.where(mask, jnp.exp(sm - m_new), 0.0)
    l = alpha * l + jnp.sum(p, axis=-1, keepdims=True)
    rows = acc.shape[0]
    acc = alpha.reshape(rows, 1) * acc + _dot(p.reshape(rows, p.shape[-1]).astype(BF16), v)
    return m_new, l, acc


def _flash_multi(qs, ks, v1s, biases, m_refs, acc_refs):
    n = len(qs)
    ss = [_dot_nt(qs[i], ks[i]) for i in range(n)]
    ss = [s if b is None else s + b for s, b in zip(ss, biases)]
    m_old = [ref[...] for ref in m_refs]
    m_new = [jnp.maximum(m_old[i], jnp.max(ss[i], axis=-1, keepdims=True)) for i in range(n)]
    ps = [jnp.exp(ss[i] - m_new[i]).astype(BF16) for i in range(n)]
    pvs = [_dot(ps[i], v1s[i]) for i in range(n)]
    for i in range(n):
        acc_refs[i][...] = jnp.exp(m_old[i] - m_new[i]) * acc_refs[i][...] + pvs[i]
        m_refs[i][...] = m_new[i]


def _joint_update(m_s, l_s, acc_s, scores, vals, keys_on_lanes=False):
    m_old = m_s[...]
    m_new = m_old
    for sc in scores:
        m_new = jnp.maximum(m_new, jnp.max(sc, axis=-1, keepdims=True))
    alpha = jnp.exp(m_old - m_new)
    l = alpha * l_s[...]
    acc = alpha * acc_s[...]
    for sc, v in zip(scores, vals):
        p = jnp.exp(sc - m_new)
        l = l + jnp.sum(p, axis=-1, keepdims=True)
        acc = acc + (_dot_nt(p.astype(BF16), v) if keys_on_lanes else _dot(p.astype(BF16), v))
    m_s[...] = m_new
    l_s[...] = l
    acc_s[...] = acc


def _flash_result(acc):
    return acc[:, 0:HEAD_DIM] / acc[:, HEAD_DIM:HEAD_DIM + 1]


def _top_select(val, idx, n):
    sel = jnp.zeros(val.shape, F32)
    big_i = jnp.int32(2 ** 30)
    for _ in range(n):
        m = jnp.max(val, axis=-1, keepdims=True)
        first = jnp.min(jnp.where(val == m, idx, big_i), axis=-1, keepdims=True)
        hit = idx == first
        sel = jnp.where(hit & (m > 0.5 * NEG), 1.0, sel)
        val = jnp.where(hit, -jnp.inf, val)
    return sel


def _ffn_kernel(*refs, n_f, post_norm):
    if post_norm:
        x_ref, g_ref, wg_ref, wu_ref, wd_ref, gf_ref, o_ref, xn_ref, acc_ref = refs
    else:
        x_ref, g_ref, wg_ref, wu_ref, wd_ref, o_ref, xn_ref, acc_ref = refs
    j = pl.program_id(1)

    @pl.when(j == 0)
    def _():
        xn_ref[...] = _rms(x_ref[...], g_ref[...]).astype(BF16)
        acc_ref[...] = jnp.zeros_like(acc_ref)

    xn = xn_ref[...]
    hg = _dot(xn, wg_ref[...])
    hu = _dot(xn, wu_ref[...])
    a = (hg * jax.nn.sigmoid(hg) * hu).astype(BF16)
    acc_ref[...] += _dot(a, wd_ref[...])

    @pl.when(j == n_f - 1)
    def _():
        y = x_ref[...] + 0.5 * acc_ref[...]
        if post_norm:
            y = _rms(y, gf_ref[...])
        o_ref[...] = y


def _half_ffn(x, g, wg, wu, wd, g_final=None):
    m, d = x.shape
    f = wg.shape[1]
    tm = _tile(m, (1024, 512, 256, 128))
    tf = _tile(f, (256, 128))
    n_f = f // tf
    post = g_final is not None
    in_specs = [pl.BlockSpec((tm, d), lambda i, j: (i, 0)),
                pl.BlockSpec((1, d), lambda i, j: (0, 0)),
                pl.BlockSpec((d, tf), lambda i, j: (0, j)),
                pl.BlockSpec((d, tf), lambda i, j: (0, j)),
                pl.BlockSpec((tf, d), lambda i, j: (j, 0))]
    args = [x, g.reshape(1, d), wg, wu, wd]
    if post:
        in_specs.append(pl.BlockSpec((1, d), lambda i, j: (0, 0)))
        args.append(g_final.reshape(1, d))
    return pl.pallas_call(
        functools.partial(_ffn_kernel, n_f=n_f, post_norm=post),
        grid=(m // tm, n_f),
        in_specs=in_specs,
        out_specs=pl.BlockSpec((tm, d), lambda i, j: (i, 0)),
        out_shape=jax.ShapeDtypeStruct((m, d), F32),
        scratch_shapes=[pltpu.VMEM((tm, d), BF16), pltpu.VMEM((tm, d), F32)],
        compiler_params=_cparams("parallel", "arbitrary"),
        name="half_ffn",
    )(*args)


def _rope(seg, cos_t, sin_t):
    w = seg.shape[1]
    reps = w // LANES
    c = jnp.concatenate([cos_t] * reps, axis=1) if reps > 1 else cos_t
    s = jnp.concatenate([sin_t] * reps, axis=1) if reps > 1 else sin_t
    lane = lax.broadcasted_iota(jnp.int32, seg.shape, 1)
    first = (lane % HEAD_DIM) < (HEAD_DIM // 2)
    rot = jnp.where(first, pltpu.roll(seg, w - HEAD_DIM // 2, 1), pltpu.roll(seg, HEAD_DIM // 2, 1))
    return seg * c + rot * s


def _store_heads(ref, seg, scale=None, ones=False):
    if ones:
        lane = lax.broadcasted_iota(jnp.int32, (seg.shape[0], HEAD_DIM), 1)
        tail = (lane == 0).astype(seg.dtype)
    for h in range(seg.shape[1] // HEAD_DIM):
        piece = seg[:, h * HEAD_DIM:(h + 1) * HEAD_DIM]
        if scale is not None:
            piece = piece * scale
        if ones:
            piece = jnp.concatenate([piece, tail], axis=1)
        ref[h] = piece.astype(ref.dtype)


def _proj_even_kernel(x_ref, g_ref, w_ref, kv_ref, xb_ref, gb_ref, *outs, heads):
    hm = HALF_MIX
    xn = _rms(x_ref[...], g_ref[...]).astype(BF16)
    y = _dot(xn, w_ref[...])
    kv_ref[...] = y[:, hm:3 * hm]
    xb_ref[...] = y[:, 3 * hm:4 * hm]
    gb_ref[...] = y[:, 4 * hm:5 * hm]
    if heads:
        qh_ref, kh_ref, vh_ref = outs
        _store_heads(qh_ref, y[:, 0:hm], SCALE)
        _store_heads(kh_ref, y[:, hm:2 * hm])
        _store_heads(vh_ref, y[:, 2 * hm:3 * hm])
    else:
        (q_ref,) = outs
        q_ref[...] = y[:, 0:hm]


def _proj_even(x, g, w, heads):
    m, d = x.shape
    n = w.shape[1]
    hm = HALF_MIX
    tm = _tile(m, (512, 256, 128))
    row = lambda i: (i, 0)
    out_shape = [jax.ShapeDtypeStruct((m, 2 * hm), F32), jax.ShapeDtypeStruct((m, hm), F32),
                 jax.ShapeDtypeStruct((m, hm), F32)]
    out_specs = [pl.BlockSpec((tm, 2 * hm), row), pl.BlockSpec((tm, hm), row), pl.BlockSpec((tm, hm), row)]
    if heads:
        for _ in range(3):
            out_shape.append(jax.ShapeDtypeStruct((N_HEADS, m, HEAD_DIM), BF16))
            out_specs.append(pl.BlockSpec((N_HEADS, tm, HEAD_DIM), lambda i: (0, i, 0)))
    else:
        out_shape.append(jax.ShapeDtypeStruct((m, hm), F32))
        out_specs.append(pl.BlockSpec((tm, hm), row))
    return pl.pallas_call(
        functools.partial(_proj_even_kernel, heads=heads),
        grid=(m // tm,),
        in_specs=[pl.BlockSpec((tm, d), row), pl.BlockSpec((1, d), lambda i: (0, 0)),
                  pl.BlockSpec((d, n), lambda i: (0, 0))],
        out_specs=out_specs, out_shape=out_shape,
        compiler_params=_cparams("parallel"),
        name="proj_even",
    )(x, g.reshape(1, d), w)


_O_QC, _O_KC, _O_VC, _O_KS, _O_VS, _O_KW, _O_VW, _O_QD, _O_KD, _O_VD, _O_GT, _O_END = (
    0, 512, 640, 768, 896, 1024, 1152, 1280, 1792, 2304, 2816, 3072)


def _proj_odd_kernel(x_ref, g_ref, w_ref, cos_ref, sin_ref, c_ref, w_out_ref, d_ref, gt_ref, *outs, heads):
    xn = _rms(x_ref[...], g_ref[...]).astype(BF16)
    y = _dot(xn, w_ref[...])
    cos_t = cos_ref[...]
    sin_t = sin_ref[...]
    qc = y[:, _O_QC:_O_KC]
    qc_rot = _rope(qc, cos_t, sin_t)
    ks_rot = _rope(y[:, _O_KS:_O_VS], cos_t, sin_t)
    kw_rot = _rope(y[:, _O_KW:_O_VW], cos_t, sin_t)
    qd_rot = _rope(y[:, _O_QD:_O_KD], cos_t, sin_t)
    kd_rot = _rope(y[:, _O_KD:_O_VD], cos_t, sin_t)
    vs = y[:, _O_VS:_O_KW]
    vw = y[:, _O_VW:_O_QD]
    vd = y[:, _O_VD:_O_GT]
    c_ref[:, 0:256] = y[:, _O_KC:_O_KS]
    c_ref[:, 256:384] = ks_rot
    c_ref[:, 384:512] = vs
    w_out_ref[:, 0:128] = kw_rot
    w_out_ref[:, 128:256] = vw
    d_ref[:, 0:512] = kd_rot
    d_ref[:, 512:1024] = vd
    gt_ref[...] = jax.nn.sigmoid(y[:, _O_GT:_O_END])
    if heads:
        qraw_h, qrot_h, ks_h, vs_h, kw_h, vw_h, qd_h, kd_h, vd_h = outs
        _store_heads(qraw_h, qc, SCALE)
        _store_heads(qrot_h, qc_rot, SCALE)
        _store_heads(ks_h, ks_rot)
        _store_heads(vs_h, vs, ones=True)
        _store_heads(kw_h, kw_rot)
        _store_heads(vw_h, vw, ones=True)
        _store_heads(qd_h, qd_rot, SCALE)
        _store_heads(kd_h, kd_rot)
        _store_heads(vd_h, vd, ones=True)
    else:
        qraw_ref, qrot_ref, qd_ref = outs
        qraw_ref[...] = qc
        qrot_ref[...] = qc_rot
        qd_ref[...] = qd_rot


def _proj_odd(x, g, w, cos_t, sin_t, heads):
    m, d = x.shape
    n = w.shape[1]
    tm = _tile(m, (512, 256, 128))
    period = cos_t.shape[0] // tm
    row = lambda i: (i, 0)
    tab = lambda i: (i % period, 0)
    out_shape = [jax.ShapeDtypeStruct((m, 512), F32), jax.ShapeDtypeStruct((m, 256), F32),
                 jax.ShapeDtypeStruct((m, 1024), F32), jax.ShapeDtypeStruct((m, 256), F32)]
    out_specs = [pl.BlockSpec((tm, 512), row), pl.BlockSpec((tm, 256), row), pl.BlockSpec((tm, 1024), row),
                 pl.BlockSpec((tm, 256), row)]
    if heads:
        wide = (False, False, False, True, False, True, False, False, True)
        for nh, wd in zip((N_HEADS, N_HEADS, N_KV_C, N_KV_C, N_KV_C, N_KV_C, N_HEADS, N_HEADS, N_HEADS), wide):
            width = LANES if wd else HEAD_DIM
            out_shape.append(jax.ShapeDtypeStruct((nh, m, width), BF16))
            out_specs.append(pl.BlockSpec((nh, tm, width), lambda i: (0, i, 0)))
    else:
        for _ in range(3):
            out_shape.append(jax.ShapeDtypeStruct((m, HALF_MIX), F32))
            out_specs.append(pl.BlockSpec((tm, HALF_MIX), row))
    return pl.pallas_call(
        functools.partial(_proj_odd_kernel, heads=heads),
        grid=(m // tm,),
        in_specs=[pl.BlockSpec((tm, d), row), pl.BlockSpec((1, d), lambda i: (0, 0)),
                  pl.BlockSpec((d, n), lambda i: (0, 0)),
                  pl.BlockSpec((tm, LANES), tab), pl.BlockSpec((tm, LANES), tab)],
        out_specs=out_specs, out_shape=out_shape,
        compiler_params=_cparams("parallel"),
        name="proj_odd",
    )(x, g.reshape(1, d), w, cos_t, sin_t)


def _outproj_kernel(x_ref, a_ref, b_ref, w_ref, o_ref, *, heads):
    hm = HALF_MIX
    if heads:
        a = jnp.concatenate([a_ref[h] for h in range(N_HEADS)], axis=1)
        b = b_ref[...] if len(b_ref.shape) == 2 else jnp.concatenate([b_ref[h] for h in range(N_HEADS)], axis=1)
    else:
        a = a_ref[...]
        b = b_ref[...]
    o_ref[...] = x_ref[...] + _dot(a, w_ref[0:hm, :]) + _dot(b, w_ref[hm:2 * hm, :])


def _outproj(x, a, b, w):
    m, d = x.shape
    tm = _tile(m, (512, 256, 128))
    row = lambda i: (i, 0)

    def spec(t):
        if t.ndim == 2:
            return pl.BlockSpec((tm, t.shape[1]), row)
        return pl.BlockSpec((t.shape[0], tm, t.shape[2]), lambda i: (0, i, 0))

    return pl.pallas_call(
        functools.partial(_outproj_kernel, heads=a.ndim == 3),
        grid=(m // tm,),
        in_specs=[pl.BlockSpec((tm, d), row), spec(a), spec(b), pl.BlockSpec(w.shape, lambda i: (0, 0))],
        out_specs=pl.BlockSpec((tm, d), row),
        out_shape=jax.ShapeDtypeStruct((m, d), F32),
        compiler_params=_cparams("parallel"),
        name="outproj",
    )(x, a, b, w)


def _sb_tile(q, k, v, t_incl, carry, acc, mask, keys_on_lanes=False):
    z = _dot(q, k) if keys_on_lanes else _dot_nt(q, k)
    lk = -_softplus(z)
    if mask is not None:
        lk = jnp.where(mask, lk, 0.0)
    incl = _split_dot(lk, t_incl)
    w = jnp.exp(z + incl + carry)
    if mask is not None:
        w = jnp.where(mask, w, 0.0)
    wb = w.astype(BF16)
    acc = acc + (_dot_nt(wb, v) if keys_on_lanes else _dot(wb, v))
    return carry + incl[:, 0:1], acc


def _tri_incl(n):
    r = lax.broadcasted_iota(jnp.int32, (n, n), 0)
    c = lax.broadcasted_iota(jnp.int32, (n, n), 1)
    return (r >= c).astype(BF16)


def _sb_prompt_kernel(q_ref, k_ref, v_ref, o_ref, *, tq):
    i = pl.program_id(2)
    q = q_ref[0]
    t_incl = _tri_incl(tq)
    r = lax.broadcasted_iota(jnp.int32, (tq, tq), 0)
    c = lax.broadcasted_iota(jnp.int32, (tq, tq), 1)

    def kv(kt):
        start = pl.multiple_of(kt * tq, tq)
        return k_ref[0, pl.ds(start, tq), :], v_ref[0, pl.ds(start, tq), :]

    k0, v0 = kv(i)
    carry, acc = _sb_tile(q, k0, v0, t_incl, jnp.zeros((tq, 1), F32), jnp.zeros((tq, HEAD_DIM), F32), c < r)

    def cond(s):
        kt, carry, _ = s
        return jnp.logical_and(kt >= 0, jnp.max(carry) > -SB_CUTOFF)

    def body(s):
        kt, carry, acc = s
        kk, vv = kv(kt)
        carry, acc = _sb_tile(q, kk, vv, t_incl, carry, acc, None)
        return kt - 1, carry, acc

    _, _, acc = lax.while_loop(cond, body, (i - 1, carry, acc))
    o_ref[0] = acc.astype(o_ref.dtype)


def _sb_prompt(qh, kh, vh, n_seq, s_len):
    tq = _tile(s_len, (256, 128))
    nq = s_len // tq
    return pl.pallas_call(
        functools.partial(_sb_prompt_kernel, tq=tq),
        grid=(n_seq, N_HEADS, nq),
        in_specs=[pl.BlockSpec((1, tq, HEAD_DIM), lambda b, h, i: (h, b * nq + i, 0)),
                  pl.BlockSpec((1, s_len, HEAD_DIM), lambda b, h, i: (h, b, 0)),
                  pl.BlockSpec((1, s_len, HEAD_DIM), lambda b, h, i: (h, b, 0))],
        out_specs=pl.BlockSpec((1, tq, HEAD_DIM), lambda b, h, i: (h, b * nq + i, 0)),
        out_shape=jax.ShapeDtypeStruct(qh.shape, BF16),
        compiler_params=_cparams("parallel", "parallel", "arbitrary"),
        name="sb_prompt",
    )(qh, kh, vh)


def _lru_kernel(xb_ref, gb_ref, h0_ref, buf0_ref, cw_ref, cb_ref, wr_ref, br_ref, wi_ref, bi_ref, lam_ref,
                g_ref, hl_ref, bn_ref, xpad, a_s, u_s, hs_s, h_s, *, tc, n_t):
    j = pl.program_id(1)
    keep = CONV_WIDTH - 1

    @pl.when(j == 0)
    def _():
        xpad[8 - keep:8, :] = buf0_ref[0]
        h_s[...] = h0_ref[0]

    @pl.when(j > 0)
    def _():
        xpad[8 - keep:8, :] = xpad[8 + tc - keep:8 + tc, :]

    x = xb_ref[0]
    xpad[8:8 + tc, :] = x
    xc = cb_ref[...] + x * cw_ref[keep:keep + 1, :]
    for d in range(1, CONV_WIDTH):
        xc = xc + xpad[8 - d:8 - d + tc, :] * cw_ref[keep - d:keep - d + 1, :]
    xcb = xc.astype(BF16)
    r = jax.nn.sigmoid(_dot(xcb, wr_ref[...]) + br_ref[...])
    gi = jax.nn.sigmoid(_dot(xcb, wi_ref[...]) + bi_ref[...])
    log_a = -LRU_C * r * _softplus(-lam_ref[...])
    a_s[...] = jnp.exp(log_a)
    u_s[...] = jnp.sqrt(_one_minus_exp(2.0 * log_a)) * (gi * xc)

    def step(t, h):
        h = a_s[pl.ds(t, 1), :] * h + u_s[pl.ds(t, 1), :]
        hs_s[pl.ds(t, 1), :] = h
        return h

    h = lax.fori_loop(0, tc, step, h_s[...], unroll=min(tc, 8))
    h_s[...] = h
    g_ref[0] = (jax.nn.gelu(gb_ref[0]) * hs_s[...]).astype(g_ref.dtype)

    @pl.when(j == n_t - 1)
    def _():
        hl_ref[0] = h
        bn_ref[0] = xpad[8 + tc - keep:8 + tc, :]


def _rg_lru(xb, gb, h0, buf0, conv_w, conv_b, wr_bd, b_r, wi_bd, b_i, lam):
    n, t, w = xb.shape
    keep = CONV_WIDTH - 1
    assert t >= keep
    tc = _tile(t, (512, 256, 128))
    n_t = t // tc
    seq = lambda b, j: (b, j, 0)
    one = lambda b, j: (b, 0, 0)
    const = lambda b, j: (0, 0)
    vec = pl.BlockSpec((1, w), const)
    g, hl, bn = pl.pallas_call(
        functools.partial(_lru_kernel, tc=tc, n_t=n_t),
        grid=(n, n_t),
        in_specs=[pl.BlockSpec((1, tc, w), seq), pl.BlockSpec((1, tc, w), seq),
                  pl.BlockSpec((1, 1, w), one), pl.BlockSpec((1, keep, w), one),
                  pl.BlockSpec((CONV_WIDTH, w), const), vec,
                  pl.BlockSpec((w, w), const), vec, pl.BlockSpec((w, w), const), vec, vec],
        out_specs=[pl.BlockSpec((1, tc, w), seq), pl.BlockSpec((1, 1, w), one), pl.BlockSpec((1, keep, w), one)],
        out_shape=[jax.ShapeDtypeStruct((n, t, w), BF16), jax.ShapeDtypeStruct((n, 1, w), F32),
                   jax.ShapeDtypeStruct((n, keep, w), F32)],
        scratch_shapes=[pltpu.VMEM((tc + 8, w), F32), pltpu.VMEM((tc, w), F32), pltpu.VMEM((tc, w), F32),
                        pltpu.VMEM((tc, w), F32), pltpu.VMEM((1, w), F32)],
        compiler_params=_cparams("parallel", "arbitrary"),
        name="rg_lru",
    )(xb, gb, h0.reshape(n, 1, w), buf0, conv_w, conv_b.reshape(1, w), wr_bd, b_r.reshape(1, w), wi_bd,
      b_i.reshape(1, w), lam.reshape(1, w))
    return g, hl.reshape(n, w), bn


def _cmp_ab_kernel(*refs, n_in, rows, n_prefetch):
    refs = refs[n_prefetch:]
    k_refs = refs[:n_in]
    v_refs = refs[n_in:2 * n_in]
    pe_ak, pe_bk, pe_av, pe_bv, w_ak, w_bk, w_av, w_bv, o_ref = refs[2 * n_in:]
    nch = rows // CMP_STRIDE

    def chunks(in_refs):
        per = [jnp.concatenate([r[pl.ds(p, nch, stride=CMP_STRIDE), :] for p in range(CMP_STRIDE)], axis=1)
               for r in in_refs]
        return per[0] if n_in == 1 else jnp.concatenate(per, axis=0)

    xk = chunks(k_refs)
    xv = chunks(v_refs)
    o_ref[0, :, 0:128] = _dot((xk + pe_ak[...]).astype(BF16), w_ak[...])
    o_ref[0, :, 128:256] = _dot((xk + pe_bk[...]).astype(BF16), w_bk[...])
    o_ref[0, :, 256:384] = _dot((xv + pe_av[...]).astype(BF16), w_av[...])
    o_ref[0, :, 384:512] = _dot((xv + pe_bv[...]).astype(BF16), w_bv[...])


def _cmp_weights(pe, w1):
    hid = w1.shape[-1]
    eye = jnp.eye(N_KV_C, dtype=F32)
    out = []
    for half in range(CMP_LEN // CMP_STRIDE):
        sl = slice(half * CMP_STRIDE, (half + 1) * CMP_STRIDE)
        pe_row = jnp.broadcast_to(pe[sl][:, None, :], (CMP_STRIDE, N_KV_C, HEAD_DIM)).reshape(1, -1)
        wbd = jnp.einsum('pdh,gk->pgdkh', w1[sl], eye).reshape(CMP_STRIDE * N_KV_C * HEAD_DIM, N_KV_C * hid)
        out.append((pe_row, wbd.astype(BF16)))
    return out


def _cmp_ab(in_arrays, k_specs, v_specs, grid, out_rows, n_seq, n_chunks, rows, cmp_k, cmp_v, num_prefetch,
            prefetch):
    in_specs = list(k_specs) + list(v_specs)
    (pe_ak, w_ak), (pe_bk, w_bk) = cmp_k
    (pe_av, w_av), (pe_bv, w_bv) = cmp_v
    nd = len(grid)
    const = lambda *a: (0, 0)
    pes = [pe_ak, pe_bk, pe_av, pe_bv]
    ws = [w_ak, w_bk, w_av, w_bv]
    specs = list(in_specs) + [pl.BlockSpec(p.shape, const) for p in pes] + [pl.BlockSpec(w.shape, const) for w in ws]
    gs = pltpu.PrefetchScalarGridSpec(
        num_scalar_prefetch=num_prefetch, grid=grid, in_specs=specs,
        out_specs=pl.BlockSpec((1, out_rows, 512), lambda b, j, *a: (b, j, 0)))
    return pl.pallas_call(
        functools.partial(_cmp_ab_kernel, n_in=len(in_arrays), rows=rows, n_prefetch=num_prefetch),
        grid_spec=gs,
        out_shape=jax.ShapeDtypeStruct((n_seq, n_chunks, 512), F32),
        compiler_params=_cparams(*(["parallel"] + ["arbitrary"] * (nd - 1))),
        name="cmp_ab",
    )(*prefetch, *in_arrays, *in_arrays, *pes, *ws)


def _cmp_finish_kernel(ab_ref, w2k_ref, w2v_ref, kf_ref, vf_ref, kg_ref, vg_ref, *, nc):
    n = nc - 1
    hk = jax.nn.gelu(ab_ref[0, 0:n, 0:128] + ab_ref[0, 1:nc, 128:256]).astype(BF16)
    hv = jax.nn.gelu(ab_ref[0, 0:n, 256:384] + ab_ref[0, 1:nc, 384:512]).astype(BF16)
    kc = _dot(hk, w2k_ref[...]).astype(BF16)
    vc = _dot(hv, w2v_ref[...]).astype(BF16)
    zero = jnp.zeros((1, LANES), BF16)
    for ref_f, ref_g, val in ((kf_ref, kg_ref, kc), (vf_ref, vg_ref, vc)):
        full = jnp.concatenate([val, zero], axis=0)
        ref_f[0] = full
        for g in range(N_KV_C):
            ref_g[0, g] = full[:, g * HEAD_DIM:(g + 1) * HEAD_DIM]


def _cmp_finish(ab, w2k_bd, w2v_bd):
    n, nc, _ = ab.shape
    flat = jax.ShapeDtypeStruct((n, nc, LANES), BF16)
    grp = jax.ShapeDtypeStruct((n, N_KV_C, nc, HEAD_DIM), BF16)
    fspec = pl.BlockSpec((1, nc, LANES), lambda b: (b, 0, 0))
    gspec = pl.BlockSpec((1, N_KV_C, nc, HEAD_DIM), lambda b: (b, 0, 0, 0))
    return pl.pallas_call(
        functools.partial(_cmp_finish_kernel, nc=nc),
        grid=(n,),
        in_specs=[pl.BlockSpec((1, nc, 512), lambda b: (b, 0, 0)),
                  pl.BlockSpec((LANES, LANES), lambda b: (0, 0)), pl.BlockSpec((LANES, LANES), lambda b: (0, 0))],
        out_specs=[fspec, fspec, gspec, gspec], out_shape=[flat, flat, grp, grp],
        compiler_params=_cparams("parallel"),
        name="cmp_finish",
    )(ab, w2k_bd, w2v_bd)


def _overlap_t(nb, nc):
    b = lax.broadcasted_iota(jnp.int32, (nb, nc), 0)
    c = lax.broadcasted_iota(jnp.int32, (nb, nc), 1)
    return ((c * CMP_STRIDE < (b + 1) * SLC_BLOCK) & (c * CMP_STRIDE + CMP_LEN > b * SLC_BLOCK)).astype(F32)


def _nsa_prompt_kernel(qraw_ref, qrot_ref, gt_ref, kcc_ref, vcc_ref, ks_ref, vs_ref, kw_ref, vw_ref, o_ref,
                       *state, tq, nc, nsb, ck):
    i = pl.program_id(2)
    q0 = i * tq
    rows = HPG_C * tq
    nbp = LANES
    qpos3 = q0 + lax.broadcasted_iota(jnp.int32, (1, tq, 1), 1)

    qraw = qraw_ref[...].reshape(rows, HEAD_DIM)
    sc = _dot_nt(qraw, kcc_ref[0, 0]).reshape(HPG_C, tq, nc)
    cidx = lax.broadcasted_iota(jnp.int32, (1, 1, nc), 2)
    mask_c = (cidx * CMP_STRIDE + CMP_LEN - 1 <= qpos3) & (cidx < nc - 1)
    pc = _masked_softmax(sc, mask_c)
    o_c = _dot(pc.reshape(rows, nc).astype(BF16), vcc_ref[0, 0])

    psum = jnp.sum(pc, axis=0)
    imp_t = _dot_nt(_overlap_t(nsb, nc), psum, precision=HIGHEST)
    blk = lax.broadcasted_iota(jnp.int32, (nsb, tq), 0)
    own = (q0 + lax.broadcasted_iota(jnp.int32, (nsb, tq), 1)) // SLC_BLOCK
    imp_t = jnp.where(blk == own, BIG, jnp.where(blk < own, imp_t, NEG))
    cnt = jnp.zeros((nsb, tq), jnp.int32)
    for b2 in range(nsb):
        rowv = imp_t[b2:b2 + 1, :]
        ahead = (rowv > imp_t) | ((rowv == imp_t) & (b2 < blk))
        cnt = cnt + ahead.astype(jnp.int32)
    sel_t = ((cnt < N_SLC) & (blk <= own)).astype(F32)
    if nsb < nbp:
        sel_t = jnp.concatenate([sel_t, jnp.zeros((nbp - nsb, tq), F32)], axis=0)
    sel_bias = jnp.where(sel_t.T > 0.5, 0.0, NEG).astype(BF16)
    m_refs = state[:HPG_C]
    acc_refs = state[HPG_C:]
    qs = [qrot_ref[j] for j in range(HPG_C)]

    def reset():
        for ref in m_refs:
            ref[...] = jnp.full(ref.shape, NEG, F32)
        for ref in acc_refs:
            ref[...] = jnp.zeros_like(ref)

    def update(k, v1, bias):
        _flash_multi(qs, [k] * HPG_C, [v1] * HPG_C, [bias] * HPG_C, m_refs, acc_refs)

    eb = lax.broadcasted_iota(jnp.int32, (nbp, ck), 0)
    ej = lax.broadcasted_iota(jnp.int32, (nbp, ck), 1)

    def slc_bias(start):
        expand = ((start + ej) // SLC_BLOCK == eb).astype(BF16)
        return _dot(sel_bias, expand)

    reset()
    c_own = q0 // ck
    start = pl.multiple_of(c_own * ck, ck)
    row = lax.broadcasted_iota(jnp.int32, (tq, ck), 0)
    lane = lax.broadcasted_iota(jnp.int32, (tq, ck), 1)
    update(ks_ref[0, pl.ds(start, ck), :], vs_ref[0, pl.ds(start, ck), :],
           jnp.where(start + lane <= q0 + row, slc_bias(start), NEG))

    def slc_step(cc, carry):
        st = pl.multiple_of(cc * ck, ck)
        update(ks_ref[0, pl.ds(st, ck), :], vs_ref[0, pl.ds(st, ck), :], slc_bias(st))
        return carry

    lax.fori_loop(0, c_own, slc_step, 0)
    o_s = [_flash_result(acc_refs[j][...]) for j in range(HPG_C)]

    reset()
    wk = WINDOW + tq
    start_w = pl.multiple_of(jnp.maximum(q0 - WINDOW, 0), tq)
    dist = (q0 + lax.broadcasted_iota(jnp.int32, (tq, wk), 0)) - (start_w + lax.broadcasted_iota(jnp.int32, (tq, wk), 1))
    update(kw_ref[0, pl.ds(start_w, wk), :], vw_ref[0, pl.ds(start_w, wk), :],
           jnp.where((dist >= 0) & (dist <= WINDOW), 0.0, NEG))

    gt = gt_ref[...]
    for j in range(HPG_C):
        sl = slice(j * tq, (j + 1) * tq)
        g_c = gt[:, j:j + 1]
        g_s = gt[:, HPG_C + j:HPG_C + j + 1]
        g_w = gt[:, 2 * HPG_C + j:2 * HPG_C + j + 1]
        o_ref[j] = (o_c[sl] * g_c + o_s[j] * g_s + _flash_result(acc_refs[j][...]) * g_w).astype(o_ref.dtype)


def _nsa_prompt(qraw_h, qrot_h, gates, kcc_g, vcc_g, ks_h, vs_h, kw_h, vw_h, n_seq, s_len):
    tq = 128
    nq = s_len // tq
    nc = kcc_g.shape[2]
    nsb = -(-s_len // SLC_BLOCK)
    ck = _tile(s_len, (512, 256, 128))
    assert s_len % tq == 0 and nsb <= LANES and WINDOW % tq == 0 and s_len >= WINDOW + tq and ck % tq == 0
    qspec = pl.BlockSpec((HPG_C, tq, HEAD_DIM), lambda b, g, i: (g, b * nq + i, 0))
    cspec = pl.BlockSpec((1, 1, nc, HEAD_DIM), lambda b, g, i: (b, g, 0, 0))
    kspec = pl.BlockSpec((1, s_len, HEAD_DIM), lambda b, g, i: (g, b, 0))
    vspec = pl.BlockSpec((1, s_len, LANES), lambda b, g, i: (g, b, 0))
    return pl.pallas_call(
        functools.partial(_nsa_prompt_kernel, tq=tq, nc=nc, nsb=nsb, ck=ck),
        grid=(n_seq, N_KV_C, nq),
        in_specs=[qspec, qspec, pl.BlockSpec((tq, LANES), lambda b, g, i: (b * nq + i, g)),
                  cspec, cspec, kspec, vspec, kspec, vspec],
        out_specs=qspec,
        out_shape=jax.ShapeDtypeStruct(qraw_h.shape, BF16),
        scratch_shapes=[pltpu.VMEM((tq, 1), F32)] * HPG_C + [pltpu.VMEM((tq, LANES), F32)] * HPG_C,
        compiler_params=_cparams("parallel", "parallel", "arbitrary"),
        name="nsa_prompt",
    )(qraw_h, qrot_h, gates, kcc_g, vcc_g, ks_h, vs_h, kw_h, vw_h)


def _block_mean_kernel(x_ref, o_ref, *, nb):
    x = x_ref[...]
    m = jnp.mean(x.reshape(nb, MOBA_BLOCK, x.shape[1]), axis=1)
    for h in range(N_HEADS):
        o_ref[h] = m[:, h * HEAD_DIM:(h + 1) * HEAD_DIM]


def _block_mean(d_rows):
    m = d_rows.shape[0]
    nb_total = m // MOBA_BLOCK
    nb = _tile(nb_total, (8,))
    return pl.pallas_call(
        functools.partial(_block_mean_kernel, nb=nb),
        grid=(nb_total // nb,),
        in_specs=[pl.BlockSpec((nb * MOBA_BLOCK, HALF_MIX), lambda i: (i, 0))],
        out_specs=pl.BlockSpec((N_HEADS, nb, HEAD_DIM), lambda i: (0, i, 0)),
        out_shape=jax.ShapeDtypeStruct((N_HEADS, nb_total, HEAD_DIM), F32),
        compiler_params=_cparams("parallel"),
        name="moba_block_mean",
    )(d_rows)


def _moba_prompt_kernel(q_ref, km_ref, k_ref, v_ref, o_ref, *state, tq, nb, sub, hb):
    i = pl.program_id(2)
    nh = tq // sub
    nchain = hb * nh
    m_refs = state[:nchain]
    acc_refs = state[nchain:]
    nbp = LANES

    blk_t = lax.broadcasted_iota(jnp.int32, (nb, tq), 0)
    sel_bias = []
    for hh in range(hb):
        gate_t = _dot_nt(km_ref[hh], q_ref[hh].astype(F32) * (1.0 / SCALE), precision=HIGHEST)
        gate_t = jnp.where(blk_t < i, gate_t, NEG)
        cnt = jnp.zeros((nb, tq), jnp.int32)
        for b2 in range(nb):
            rowv = gate_t[b2:b2 + 1, :]
            cnt = cnt + ((rowv > gate_t) | ((rowv == gate_t) & (b2 < blk_t))).astype(jnp.int32)
        sel_t = ((cnt < MOBA_TOPK) & (blk_t < i)).astype(F32)
        sel_t = jnp.concatenate([sel_t, jnp.zeros((nbp - nb, tq), F32)], axis=0)
        sel_bias.append(jnp.where(sel_t.T > 0.5, 0.0, NEG).astype(BF16))

    qs = [q_ref[ci // nh, (ci % nh) * sub:(ci % nh + 1) * sub, :] for ci in range(nchain)]

    def update(start, width, biases):
        ks = [k_ref[hh, pl.ds(start, width), :] for hh in range(hb)]
        v1s = [v_ref[hh, pl.ds(start, width), :] for hh in range(hb)]
        _flash_multi(qs, [ks[ci // nh] for ci in range(nchain)], [v1s[ci // nh] for ci in range(nchain)], biases,
                     m_refs, acc_refs)

    def block_bias(expand):
        return [_dot(sel_bias[ci // nh][(ci % nh) * sub:(ci % nh + 1) * sub, :], expand) for ci in range(nchain)]

    for ref in m_refs:
        ref[...] = jnp.full(ref.shape, NEG, F32)
    for ref in acc_refs:
        ref[...] = jnp.zeros_like(ref)

    r = lax.broadcasted_iota(jnp.int32, (sub, tq), 0)
    c = lax.broadcasted_iota(jnp.int32, (sub, tq), 1)
    update(pl.multiple_of(i * tq, tq), tq, [jnp.where(c <= r + (ci % nh) * sub, 0.0, NEG) for ci in range(nchain)])

    pk = 2 * tq
    eb2 = lax.broadcasted_iota(jnp.int32, (nbp, pk), 0)
    ej2 = lax.broadcasted_iota(jnp.int32, (nbp, pk), 1)

    def pair_step(cc, carry):
        update(pl.multiple_of(cc * pk, pk), pk, block_bias((eb2 == 2 * cc + ej2 // tq).astype(BF16)))
        return carry

    lax.fori_loop(0, i // 2, pair_step, 0)

    @pl.when(i % 2 == 1)
    def _():
        eb1 = lax.broadcasted_iota(jnp.int32, (nbp, tq), 0)
        update(pl.multiple_of((i - 1) * tq, tq), tq, block_bias((eb1 == i - 1).astype(BF16)))

    for ci in range(nchain):
        o_ref[ci // nh, (ci % nh) * sub:(ci % nh + 1) * sub, :] = _flash_result(acc_refs[ci][...]).astype(o_ref.dtype)


def _moba_prompt(qd_h, kmean_h, kd_h, vd_h, n_seq, s_len):
    tq = MOBA_BLOCK
    assert s_len % tq == 0
    nq = s_len // tq
    assert nq <= LANES
    sub = 128
    hb = 4
    nchain = hb * (tq // sub)
    kspec = pl.BlockSpec((hb, s_len, HEAD_DIM), lambda b, h, i: (h, b, 0))
    vspec = pl.BlockSpec((hb, s_len, LANES), lambda b, h, i: (h, b, 0))
    qspec = pl.BlockSpec((hb, tq, HEAD_DIM), lambda b, h, i: (h, b * nq + i, 0))
    return pl.pallas_call(
        functools.partial(_moba_prompt_kernel, tq=tq, nb=nq, sub=sub, hb=hb),
        grid=(n_seq, N_HEADS // hb, nq),
        in_specs=[qspec, pl.BlockSpec((hb, nq, HEAD_DIM), lambda b, h, i: (h, b, 0)), kspec, vspec],
        out_specs=qspec,
        out_shape=jax.ShapeDtypeStruct(qd_h.shape, BF16),
        scratch_shapes=[pltpu.VMEM((sub, 1), F32)] * nchain + [pltpu.VMEM((sub, LANES), F32)] * nchain,
        compiler_params=_cparams("parallel", "parallel", "arbitrary"),
        name="moba_prompt",
    )(qd_h, kmean_h, kd_h, vd_h)


def _page_specs(n_slots, block, tail, base):
    def make(s):
        return lambda b, j, pt: (base + pt[b, j * n_slots + s],) + tuple(tail)

    return [pl.BlockSpec((None,) + tuple(block), make(s)) for s in range(n_slots)]


def _pages_t(cache):
    l, p, r, a, b, d = cache.shape
    return jnp.transpose(cache, (0, 1, 3, 4, 5, 2)).reshape(l * p, a, b * d, r)


def _head_diag(rows, width, per):
    r = lax.broadcasted_iota(jnp.int32, (rows, width), 0)
    c = lax.broadcasted_iota(jnp.int32, (rows, width), 1)
    return ((r % N_HEADS) * HEAD_DIM // per) == (c // HEAD_DIM * HEAD_DIM // per)


def _fold_heads(acc, t):
    kept = jnp.where(_head_diag(acc.shape[0], acc.shape[1], HEAD_DIM), acc, 0.0)
    return jnp.sum(kept.reshape(t, N_HEADS, acc.shape[1]), axis=1)


def _sb_sample_kernel(pt_ref, q_ref, kn_ref, vn_ref, pool_ref, o_ref, buf, sem, *, n_pages, t, pad, base):
    b = pl.program_id(0)
    rows = t * N_HEADS
    q = q_ref[0]

    def page_copy(p, slot):
        return pltpu.make_async_copy(pool_ref.at[base + pt_ref[b, p]], buf.at[slot], sem.at[slot])

    page_copy(n_pages - 1, 0).start()

    r = lax.broadcasted_iota(jnp.int32, (rows, pad), 0) // N_HEADS
    c = lax.broadcasted_iota(jnp.int32, (rows, pad), 1)
    carry, acc = _sb_tile(q, kn_ref[0], vn_ref[0], _tri_incl(pad), jnp.zeros((rows, 1), F32),
                          jnp.zeros((rows, HALF_MIX), F32), (c < r) & (c < t))
    t_incl = _tri_incl(PAGE_SIZE)

    def cond(s):
        p, carry, _ = s
        return jnp.logical_and(p >= 0, jnp.max(carry) > -SB_CUTOFF)

    def body(s):
        p, carry, acc = s
        slot = (n_pages - 1 - p) % 2
        page_copy(p, slot).wait()

        @pl.when(p > 0)
        def _():
            page_copy(p - 1, 1 - slot).start()

        carry, acc = _sb_tile(q, buf[slot, 0].astype(BF16), buf[slot, 1].astype(BF16), t_incl, carry, acc, None,
                              keys_on_lanes=True)
        return p - 1, carry, acc

    p_end, _, acc = lax.while_loop(cond, body, (jnp.int32(n_pages - 1), carry, acc))

    @pl.when(p_end >= 0)
    def _():
        page_copy(p_end, (n_pages - 1 - p_end) % 2).wait()

    o_ref[0] = _fold_heads(acc, t)


def _sb_sample(qbd, k_new, v_new, pool_t, base, page_table):
    n, rows, _ = qbd.shape
    t = rows // N_HEADS
    pad = k_new.shape[1]
    n_pages = page_table.shape[1]
    seq = lambda b, pt: (b, 0, 0)
    gs = pltpu.PrefetchScalarGridSpec(
        num_scalar_prefetch=1, grid=(n,),
        in_specs=[pl.BlockSpec((1, rows, HALF_MIX), seq), pl.BlockSpec((1, pad, HALF_MIX), seq),
                  pl.BlockSpec((1, pad, HALF_MIX), seq), pl.BlockSpec(memory_space=pl.ANY)],
        out_specs=pl.BlockSpec((1, t, HALF_MIX), seq),
        scratch_shapes=[pltpu.VMEM((2,) + pool_t.shape[1:], pool_t.dtype), pltpu.SemaphoreType.DMA((2,))])
    return pl.pallas_call(
        functools.partial(_sb_sample_kernel, n_pages=n_pages, t=t, pad=pad, base=base),
        grid_spec=gs,
        out_shape=jax.ShapeDtypeStruct((n, t, HALF_MIX), F32),
        compiler_params=_cparams("arbitrary"),
        name="sb_sample",
    )(page_table, qbd, k_new, v_new, pool_t)


def _nsa_sample_head_kernel(qraw_ref, qrot_ref, kcc_ref, vcc_ref, win_ref, wn_ref, oc_ref, ow_ref, sel_ref,
                            *, t, nc, n_past, nbp, pad):
    rows = t * N_HEADS
    wb = win_ref.shape[1]
    tok = lax.broadcasted_iota(jnp.int32, (rows, 1), 0) // N_HEADS
    qpos = n_past + tok

    def fold(x):
        g0 = (lax.broadcasted_iota(jnp.int32, (rows, HEAD_DIM), 0) % N_HEADS) < HPG_C
        return jnp.where(g0, x[:, 0:HEAD_DIM], x[:, HEAD_DIM:2 * HEAD_DIM])

    sc = _dot_nt(qraw_ref[0], kcc_ref[0])
    cidx = lax.broadcasted_iota(jnp.int32, (1, nc), 1)
    pc = _masked_softmax(sc, (cidx * CMP_STRIDE + CMP_LEN - 1 <= qpos) & (cidx < nc - 1))
    oc_ref[0] = fold(_dot(pc.astype(BF16), vcc_ref[0]))

    ng = t * N_KV_C
    gr = lax.broadcasted_iota(jnp.int32, (ng, rows), 0)
    rr = lax.broadcasted_iota(jnp.int32, (ng, rows), 1)
    same = (gr // N_KV_C == rr // N_HEADS) & (gr % N_KV_C == (rr % N_HEADS) // HPG_C)
    psum = _dot(same.astype(F32), pc, precision=HIGHEST)
    imp = _dot_nt(psum, _overlap_t(nbp, nc), precision=HIGHEST)
    blk = lax.broadcasted_iota(jnp.int32, (ng, nbp), 1)
    own = (n_past + lax.broadcasted_iota(jnp.int32, (ng, nbp), 0) // N_KV_C) // SLC_BLOCK
    imp = jnp.where(blk == own, BIG, jnp.where(blk < own, imp, NEG))
    sel_ref[0] = _top_select(imp, blk, N_SLC)

    qrot = qrot_ref[0]
    win = win_ref[0]
    wn = wn_ref[0]
    s1 = _dot_nt(qrot, win[:, 0:LANES].astype(BF16))
    wpos1 = n_past - wb + lax.broadcasted_iota(jnp.int32, (1, wb), 1)
    d1 = qpos - wpos1
    mk1 = (d1 >= 0) & (d1 <= WINDOW) & (wpos1 >= 0)
    s2 = _dot_nt(qrot, wn[:, 0:LANES])
    j2 = lax.broadcasted_iota(jnp.int32, (1, pad), 1)
    mk2 = (j2 <= tok) & (j2 < t)
    m = jnp.maximum(jnp.max(jnp.where(mk1, s1, NEG), axis=-1, keepdims=True),
                    jnp.max(jnp.where(mk2, s2, NEG), axis=-1, keepdims=True))
    p1 = jnp.where(mk1, jnp.exp(jnp.where(mk1, s1, NEG) - m), 0.0)
    p2 = jnp.where(mk2, jnp.exp(jnp.where(mk2, s2, NEG) - m), 0.0)
    den = jnp.sum(p1, axis=-1, keepdims=True) + jnp.sum(p2, axis=-1, keepdims=True)
    ow = _dot(p1.astype(BF16), win[:, LANES:2 * LANES].astype(BF16)) + _dot(p2.astype(BF16), wn[:, LANES:2 * LANES])
    ow_ref[0] = fold(ow / den)


def _nsa_sample_head(qraw_bd, qrot_bd, kcc_f, vcc_f, win, w_new, n_past, nbp):
    n, rows, _ = qraw_bd.shape
    t = rows // N_HEADS
    nc = kcc_f.shape[1]
    wb = win.shape[1]
    pad = w_new.shape[1]
    seq = lambda b: (b, 0, 0)
    return pl.pallas_call(
        functools.partial(_nsa_sample_head_kernel, t=t, nc=nc, n_past=n_past, nbp=nbp, pad=pad),
        grid=(n,),
        in_specs=[pl.BlockSpec((1, rows, LANES), seq), pl.BlockSpec((1, rows, LANES), seq),
                  pl.BlockSpec((1, nc, LANES), seq), pl.BlockSpec((1, nc, LANES), seq),
                  pl.BlockSpec((1, wb, 2 * LANES), seq), pl.BlockSpec((1, pad, 2 * LANES), seq)],
        out_specs=[pl.BlockSpec((1, rows, HEAD_DIM), seq), pl.BlockSpec((1, rows, HEAD_DIM), seq),
                   pl.BlockSpec((1, t * N_KV_C, nbp), seq)],
        out_shape=[jax.ShapeDtypeStruct((n, rows, HEAD_DIM), F32), jax.ShapeDtypeStruct((n, rows, HEAD_DIM), F32),
                   jax.ShapeDtypeStruct((n, t * N_KV_C, nbp), F32)],
        compiler_params=_cparams("parallel"),
        name="nsa_sample_head",
    )(qraw_bd, qrot_bd, kcc_f, vcc_f, win, w_new)


def _nsa_sample_slc_kernel(pt_ref, q_ref, sel_ref, sn_ref, oc_ref, ow_ref, gt_ref, *refs,
                           n_slots, n_steps, t, nbp, pad):
    page_refs = refs[:n_slots]
    o_ref, m_s, l_s, acc_s = refs[n_slots:]
    j = pl.program_id(1)
    rows = t * N_HEADS
    q = q_ref[0]
    tok = lax.broadcasted_iota(jnp.int32, (rows, 1), 0) // N_HEADS

    @pl.when(j == 0)
    def _():
        sn = sn_ref[0]
        j2 = lax.broadcasted_iota(jnp.int32, (1, pad), 1)
        m, l, acc = _online_step(_dot_nt(q, sn[:, 0:LANES]), (j2 <= tok) & (j2 < t), sn[:, LANES:2 * LANES],
                                 jnp.full((rows, 1), NEG, F32), jnp.zeros((rows, 1), F32),
                                 jnp.zeros((rows, LANES), F32))
        m_s[...] = m
        l_s[...] = l
        acc_s[...] = acc

    sel_bias = sel_ref[0]
    eb = lax.broadcasted_iota(jnp.int32, (nbp, PAGE_SIZE), 0)
    ej = lax.broadcasted_iota(jnp.int32, (nbp, PAGE_SIZE), 1)
    group = 8
    for s0 in range(0, n_slots, group):
        scores, vals = [], []
        for s in range(s0, min(s0 + group, n_slots)):
            first_blk = (j * n_slots + s) * (PAGE_SIZE // SLC_BLOCK)
            bias = _dot(sel_bias, (first_blk + ej // SLC_BLOCK == eb).astype(BF16))
            scores.append(_dot(q, page_refs[s][0].astype(BF16)) + bias)
            vals.append(page_refs[s][1].astype(BF16))
        _joint_update(m_s, l_s, acc_s, scores, vals, keys_on_lanes=True)

    @pl.when(j == n_steps - 1)
    def _():
        o = acc_s[...] / l_s[...]
        g0 = (lax.broadcasted_iota(jnp.int32, (rows, HEAD_DIM), 0) % N_HEADS) < HPG_C
        o_s = jnp.where(g0, o[:, 0:HEAD_DIM], o[:, HEAD_DIM:2 * HEAD_DIM])
        gt = gt_ref[0]
        o_ref[0] = oc_ref[0] * gt[:, 0:1] + o_s * gt[:, 1:2] + ow_ref[0] * gt[:, 2:3]


def _nsa_sample_slc(qrot_bd, sel_rows, s_new, o_c, o_w, gates, pool, base, page_table):
    n, rows, _ = qrot_bd.shape
    t = rows // N_HEADS
    nbp = sel_rows.shape[2]
    pad = s_new.shape[1]
    n_pages = page_table.shape[1]
    n_slots = _tile(n_pages, (16, 8, 4, 2))
    n_steps = n_pages // n_slots
    seq = lambda b, j, pt: (b, 0, 0)
    gs = pltpu.PrefetchScalarGridSpec(
        num_scalar_prefetch=1, grid=(n, n_steps),
        in_specs=[pl.BlockSpec((1, rows, LANES), seq), pl.BlockSpec((1, rows, nbp), seq),
                  pl.BlockSpec((1, pad, 2 * LANES), seq), pl.BlockSpec((1, rows, HEAD_DIM), seq),
                  pl.BlockSpec((1, rows, HEAD_DIM), seq), pl.BlockSpec((1, rows, 3), seq)]
        + _page_specs(n_slots, (2, LANES, PAGE_SIZE), (1, 0, 0), base),
        out_specs=pl.BlockSpec((1, rows, HEAD_DIM), seq),
        scratch_shapes=[pltpu.VMEM((rows, 1), F32), pltpu.VMEM((rows, 1), F32), pltpu.VMEM((rows, LANES), F32)])
    return pl.pallas_call(
        functools.partial(_nsa_sample_slc_kernel, n_slots=n_slots, n_steps=n_steps, t=t, nbp=nbp, pad=pad),
        grid_spec=gs,
        out_shape=jax.ShapeDtypeStruct((n, rows, HEAD_DIM), F32),
        compiler_params=_cparams("parallel", "arbitrary"),
        name="nsa_sample_slc",
    )(page_table, qrot_bd, sel_rows, s_new, o_c, o_w, gates, *([pool] * n_slots))


def _page_sum_kernel(pt_ref, *refs, n_slots):
    o_ref = refs[n_slots]
    j = pl.program_id(1)
    ppb = MOBA_BLOCK // PAGE_SIZE

    @pl.when(j == 0)
    def _():
        o_ref[...] = jnp.zeros_like(o_ref)

    lane = lax.broadcasted_iota(jnp.int32, o_ref.shape[1:], 1)
    acc = o_ref[0]
    for s in range(n_slots):
        tot = jnp.sum(refs[s][...], axis=1, keepdims=True)
        acc = acc + jnp.where(lane == (j * n_slots + s) // ppb, tot, 0.0)
    o_ref[0] = acc


def _page_sums(pool_t, base, page_table, nbp):
    n, n_pages = page_table.shape
    n_slots = _tile(n_pages, (16, 8, 4, 2))
    gs = pltpu.PrefetchScalarGridSpec(
        num_scalar_prefetch=1, grid=(n, n_pages // n_slots),
        in_specs=_page_specs(n_slots, (None, HALF_MIX, PAGE_SIZE), (0, 0, 0), base),
        out_specs=pl.BlockSpec((1, HALF_MIX, nbp), lambda b, j, pt: (b, 0, 0)))
    return pl.pallas_call(
        functools.partial(_page_sum_kernel, n_slots=n_slots),
        grid_spec=gs,
        out_shape=jax.ShapeDtypeStruct((n, HALF_MIX, nbp), F32),
        compiler_params=_cparams("parallel", "arbitrary"),
        name="moba_page_sums",
    )(page_table, *([pool_t] * n_slots))


def _moba_sample_kernel(pt_ref, q_ref, qf_ref, ks_ref, kn_ref, vn_ref, *refs, n_slots, n_steps, t, nbp, pad,
                        n_past):
    page_refs = refs[:n_slots]
    o_ref, sel_s, m_s, l_s, acc_s = refs[n_slots:]
    j = pl.program_id(1)
    rows = t * N_HEADS
    ppb = MOBA_BLOCK // PAGE_SIZE
    nb_full = n_past // MOBA_BLOCK
    q = q_ref[0]
    tok = lax.broadcasted_iota(jnp.int32, (rows, 1), 0) // N_HEADS

    @pl.when(j == 0)
    def _():
        gate = _dot(qf_ref[0], ks_ref[0] * (1.0 / MOBA_BLOCK), precision=HIGHEST)
        blk = lax.broadcasted_iota(jnp.int32, (rows, nbp), 1)
        own = (n_past + tok) // MOBA_BLOCK
        gate = jnp.where((blk < own) & (blk < nb_full), gate, NEG)
        sel_s[...] = jnp.where(_top_select(gate, blk, MOBA_TOPK) > 0.5, 0.0, NEG).astype(BF16)
        j2 = lax.broadcasted_iota(jnp.int32, (1, pad), 1)
        m, l, acc = _online_step(_dot_nt(q, kn_ref[0]), (j2 <= tok) & (j2 < t), vn_ref[0],
                                 jnp.full((rows, 1), NEG, F32), jnp.zeros((rows, 1), F32),
                                 jnp.zeros((rows, HALF_MIX), F32))
        m_s[...] = m
        l_s[...] = l
        acc_s[...] = acc

    sel_bias = sel_s[...]
    eb = lax.broadcasted_iota(jnp.int32, (nbp, PAGE_SIZE), 0)
    scores, vals = [], []
    for s in range(n_slots):
        blk_of_page = (j * n_slots + s) // ppb
        bias = _dot(sel_bias, (eb == blk_of_page).astype(BF16))
        scores.append(_dot(q, page_refs[s][0].astype(BF16)) + bias)
        vals.append(page_refs[s][1].astype(BF16))
    _joint_update(m_s, l_s, acc_s, scores, vals, keys_on_lanes=True)

    @pl.when(j == n_steps - 1)
    def _():
        o_ref[0] = _fold_heads(acc_s[...] / l_s[...], t)


def _moba_sample(qbd, qbd_f32, ksums, k_new, v_new, pool, base, page_table, n_past):
    n, rows, _ = qbd.shape
    t = rows // N_HEADS
    pad = k_new.shape[1]
    n_pages = page_table.shape[1]
    assert n_past % MOBA_BLOCK == 0 and t <= MOBA_BLOCK
    nbp = ksums.shape[2]
    n_slots = _tile(n_pages, (8, 4, 2))
    n_steps = n_pages // n_slots
    seq = lambda b, j, pt: (b, 0, 0)
    gs = pltpu.PrefetchScalarGridSpec(
        num_scalar_prefetch=1, grid=(n, n_steps),
        in_specs=[pl.BlockSpec((1, rows, HALF_MIX), seq), pl.BlockSpec((1, rows, HALF_MIX), seq),
                  pl.BlockSpec((1, HALF_MIX, nbp), seq),
                  pl.BlockSpec((1, pad, HALF_MIX), seq), pl.BlockSpec((1, pad, HALF_MIX), seq)]
        + _page_specs(n_slots, (2, HALF_MIX, PAGE_SIZE), (0, 0, 0), base),
        out_specs=pl.BlockSpec((1, t, HALF_MIX), seq),
        scratch_shapes=[pltpu.VMEM((rows, nbp), BF16), pltpu.VMEM((rows, 1), F32), pltpu.VMEM((rows, 1), F32),
                        pltpu.VMEM((rows, HALF_MIX), F32)])
    return pl.pallas_call(
        functools.partial(_moba_sample_kernel, n_slots=n_slots, n_steps=n_steps, t=t, nbp=nbp, pad=pad,
                          n_past=n_past),
        grid_spec=gs,
        out_shape=jax.ShapeDtypeStruct((n, t, HALF_MIX), F32),
        compiler_params=_cparams("parallel", "arbitrary"),
        name="moba_sample",
    )(page_table, qbd, qbd_f32, ksums, k_new, v_new, *([pool] * n_slots))


def _block_diag(blocks):
    nb, bi, bj = blocks.shape
    eye = jnp.eye(nb, dtype=blocks.dtype)
    return jnp.einsum('bij,bc->bicj', blocks, eye).reshape(nb * bi, nb * bj)


def _rope_tables(pos):
    half = HEAD_DIM // 2
    inv_freq = ROPE_THETA ** (-jnp.arange(half, dtype=F32) / half)
    ang = pos.astype(F32)[:, None] * inv_freq[None, :]
    cos = jnp.cos(ang)
    sin = jnp.sin(ang)
    return jnp.concatenate([cos, cos, cos, cos], axis=1), jnp.concatenate([-sin, sin, -sin, sin], axis=1)


def _queries_bd(q, per, scale, dtype):
    n, t, _ = q.shape
    groups = N_HEADS // per
    qh = q.reshape(n, t, N_HEADS, 1, HEAD_DIM) * scale
    pick = (jnp.arange(N_HEADS)[:, None] // per == jnp.arange(groups)[None, :]).astype(q.dtype)
    return (qh * pick[None, None, :, :, None]).reshape(n, t * N_HEADS, groups * HEAD_DIM).astype(dtype)


def _pad_rows(x, pad):
    n, t, w = x.shape
    return jnp.concatenate([x, jnp.zeros((n, pad - t, w), x.dtype)], axis=1)


def _odd_weight(w_in):
    d = w_in.shape[0]
    gc0 = _O_VW + 128
    main = jnp.concatenate([w_in[:, :gc0], w_in[:, gc0 + 3 * N_HEADS:]], axis=1)
    gc = w_in[:, gc0:gc0 + 3 * N_HEADS].reshape(d, N_KV_C, HPG_C, 3).transpose(0, 1, 3, 2).reshape(d, N_KV_C, 3 * HPG_C)
    gc = jnp.concatenate([gc, jnp.zeros((d, N_KV_C, LANES - 3 * HPG_C), w_in.dtype)], axis=2).reshape(d, N_KV_C * LANES)
    return jnp.concatenate([main, gc], axis=1).astype(BF16)


def kernel(x_prompt, x_sample, cache_a_kv, state_b_h, state_b_conv, cache_c_kv, state_c_win, cache_d_kv, page_table, norm_ffn1, w_ffn1_gate, w_ffn1_up, w_ffn1_down, norm_mix, norm_ffn2, w_ffn2_gate, w_ffn2_up, w_ffn2_down, w_in_even, w_out_even, lru_conv_w, lru_conv_b, lru_w_r, lru_b_r, lru_w_i, lru_b_i, lru_lambda, w_in_odd, w_out_odd, cmp_pe_k, cmp_w1_k, cmp_w2_k, cmp_pe_v, cmp_w1_v, cmp_w2_v, norm_final):
    nb, s_len, d = x_prompt.shape
    ns, t = x_sample.shape[:2]
    depth = norm_mix.shape[0]
    n_pages = page_table.shape[1]
    n_past = n_pages * PAGE_SIZE
    n_pool = cache_a_kv.shape[1]
    hm = HALF_MIX
    pad = 16
    assert t <= pad and n_past % SLC_BLOCK == 0

    yp = x_prompt.reshape(nb * s_len, d)
    ys = x_sample.reshape(ns * t, d)
    cos_p, sin_p = _rope_tables(jnp.arange(s_len, dtype=jnp.int32))
    cos_s, sin_s = _rope_tables(jnp.tile(n_past + jnp.arange(t, dtype=jnp.int32), ns))
    pool_a = _pages_t(cache_a_kv)
    pool_c = _pages_t(cache_c_kv)
    pool_d = _pages_t(cache_d_kv)
    pool_c_cmp = cache_c_kv[:, :, :, :2].reshape(-1, PAGE_SIZE, 2 * LANES)

    outs = {k: [] for k in ("a_p", "a_s", "bh_p", "bh_s", "bc_p", "bc_s", "c_p", "c_s", "cw_p", "cw_s", "d_p", "d_s")}
    for layer in range(depth):
        li = layer // 2
        last = layer == depth - 1
        f1 = (norm_ffn1[layer], w_ffn1_gate[layer].astype(BF16), w_ffn1_up[layer].astype(BF16),
              w_ffn1_down[layer].astype(BF16))
        f2 = (norm_ffn2[layer], w_ffn2_gate[layer].astype(BF16), w_ffn2_up[layer].astype(BF16),
              w_ffn2_down[layer].astype(BF16))
        yp = _half_ffn(yp, *f1)
        ys = _half_ffn(ys, *f1)
        if layer % 2 == 0:
            w_in = w_in_even[li].astype(BF16)
            w_out = w_out_even[li].astype(BF16)
            lru = (lru_conv_w[li], lru_conv_b[li], _block_diag(lru_w_r[li]).astype(BF16), lru_b_r[li],
                   _block_diag(lru_w_i[li]).astype(BF16), lru_b_i[li], lru_lambda[li])
            kv, xb, gb, qh, kh, vh = _proj_even(yp, norm_mix[layer], w_in, True)
            o_a = _sb_prompt(qh, kh, vh, nb, s_len)
            g, h_new, buf_new = _rg_lru(xb.reshape(nb, s_len, hm), gb.reshape(nb, s_len, hm),
                                        jnp.zeros((nb, hm), F32), jnp.zeros((nb, CONV_WIDTH - 1, hm), F32), *lru)
            yp = _outproj(yp, o_a, g.reshape(nb * s_len, hm), w_out)
            outs["a_p"].append(kv.reshape(nb, s_len, 2, N_HEADS, HEAD_DIM))
            outs["bh_p"].append(h_new)
            outs["bc_p"].append(buf_new)
            kv, xb, gb, q = _proj_even(ys, norm_mix[layer], w_in, False)
            kv3 = kv.reshape(ns, t, 2 * hm)
            o_a = _sb_sample(_queries_bd(q.reshape(ns, t, hm), 1, SCALE, BF16),
                             _pad_rows(kv3[:, :, :hm], pad).astype(BF16), _pad_rows(kv3[:, :, hm:], pad).astype(BF16),
                             pool_a, li * n_pool, page_table)
            g, h_new, buf_new = _rg_lru(xb.reshape(ns, t, hm), gb.reshape(ns, t, hm), state_b_h[li],
                                        state_b_conv[li], *lru)
            ys = _outproj(ys, o_a.reshape(ns * t, hm).astype(BF16), g.reshape(ns * t, hm), w_out)
            outs["a_s"].append(kv.reshape(ns, t, 2, N_HEADS, HEAD_DIM))
            outs["bh_s"].append(h_new)
            outs["bc_s"].append(buf_new)
        else:
            w_in = _odd_weight(w_in_odd[li])
            w_out = w_out_odd[li].astype(BF16)
            cmp_k = _cmp_weights(cmp_pe_k[li], cmp_w1_k[li])
            cmp_v = _cmp_weights(cmp_pe_v[li], cmp_w1_v[li])
            w2k = _block_diag(jnp.stack([cmp_w2_k[li]] * N_KV_C)).astype(BF16)
            w2v = _block_diag(jnp.stack([cmp_w2_v[li]] * N_KV_C)).astype(BF16)
            (c_rows, w_rows, d_rows, gates, qraw_h, qrot_h, ks_h, vs_h, kw_h, vw_h, qd_h, kd_h, vd_h) = _proj_odd(
                yp, norm_mix[layer], w_in, cos_p, sin_p, True)
            n_chunk = s_len // CMP_STRIDE
            rows_c = _tile(s_len, (2048, 1024, 512, 256, 128))
            ab = _cmp_ab([c_rows], [pl.BlockSpec((rows_c, LANES), lambda b, j: (b * (s_len // rows_c) + j, 0))],
                         [pl.BlockSpec((rows_c, LANES), lambda b, j: (b * (s_len // rows_c) + j, 1))],
                         (nb, s_len // rows_c), rows_c // CMP_STRIDE, nb, n_chunk, rows_c, cmp_k, cmp_v, 0, ())
            _, _, kcc_g, vcc_g = _cmp_finish(ab, w2k, w2v)
            o_c = _nsa_prompt(qraw_h, qrot_h, gates, kcc_g, vcc_g, ks_h, vs_h, kw_h, vw_h, nb, s_len)
            o_d = _moba_prompt(qd_h, _block_mean(d_rows), kd_h, vd_h, nb, s_len)
            yp = _outproj(yp, o_c, o_d, w_out)
            wlen = min(WINDOW, s_len)
            outs["c_p"].append(c_rows.reshape(nb, s_len, 4, N_KV_C, HEAD_DIM))
            outs["cw_p"].append(w_rows.reshape(nb, s_len, 2, N_KV_C, HEAD_DIM)[:, s_len - wlen:])
            outs["d_p"].append(d_rows.reshape(nb, s_len, 2, N_HEADS, HEAD_DIM))
            c_rows, w_rows, d_rows, gates, qraw, qrot, qd = _proj_odd(ys, norm_mix[layer], w_in, cos_s, sin_s, False)
            slots_c = _tile(n_pages, (16, 8))
            ab = _cmp_ab([pool_c_cmp] * slots_c, _page_specs(slots_c, (PAGE_SIZE, LANES), (0, 0), li * n_pool),
                         _page_specs(slots_c, (PAGE_SIZE, LANES), (0, 1), li * n_pool),
                         (ns, n_pages // slots_c), slots_c * PAGE_SIZE // CMP_STRIDE, ns, n_past // CMP_STRIDE,
                         PAGE_SIZE, cmp_k, cmp_v, 1, (page_table,))
            kcc_f, vcc_f, _, _ = _cmp_finish(ab, w2k, w2v)
            nsb = -(-(n_past + t) // SLC_BLOCK)
            nbp = LANES * (-(-nsb // LANES))
            win_l = state_c_win[li]
            wb = win_l.shape[1]
            qraw_bd = _queries_bd(qraw.reshape(ns, t, hm), HPG_C, SCALE, BF16)
            qrot_bd = _queries_bd(qrot.reshape(ns, t, hm), HPG_C, SCALE, BF16)
            c3 = c_rows.reshape(ns, t, 4 * LANES)
            w3 = w_rows.reshape(ns, t, 2 * LANES)
            d3 = d_rows.reshape(ns, t, 2 * hm)
            o_cmp, o_win, sel = _nsa_sample_head(qraw_bd, qrot_bd, kcc_f, vcc_f, win_l.reshape(ns, wb, 2 * LANES),
                                                 _pad_rows(w3, pad).astype(BF16), n_past, nbp)
            sel_rows = jnp.repeat(sel.reshape(ns, t, N_KV_C, nbp), HPG_C, axis=2).reshape(ns, t * N_HEADS, nbp)
            gt = gates.reshape(ns, t, N_KV_C, LANES)[..., :3 * HPG_C].reshape(ns, t, N_KV_C, 3, HPG_C)
            gt = gt.transpose(0, 1, 2, 4, 3).reshape(ns, t * N_HEADS, 3)
            sel_bias = jnp.where(sel_rows > 0.5, 0.0, NEG).astype(BF16)
            o_c = _nsa_sample_slc(qrot_bd, sel_bias, _pad_rows(c3[:, :, 2 * LANES:], pad).astype(BF16),
                                  o_cmp, o_win, gt, pool_c, li * n_pool, page_table)
            ksums = _page_sums(pool_d, li * n_pool, page_table, LANES * (-(-(n_past // MOBA_BLOCK) // LANES)))
            qd3 = qd.reshape(ns, t, hm)
            o_d = _moba_sample(_queries_bd(qd3, 1, SCALE, BF16), _queries_bd(qd3, 1, 1.0, F32), ksums,
                               _pad_rows(d3[:, :, :hm], pad).astype(BF16), _pad_rows(d3[:, :, hm:], pad).astype(BF16),
                               pool_d, li * n_pool, page_table, n_past)
            ys = _outproj(ys, o_c.reshape(ns * t, hm).astype(BF16), o_d.reshape(ns * t, hm).astype(BF16), w_out)
            w5 = w_rows.reshape(ns, t, 2, N_KV_C, HEAD_DIM)
            outs["c_s"].append(c_rows.reshape(ns, t, 4, N_KV_C, HEAD_DIM))
            outs["cw_s"].append(jnp.concatenate([win_l, w5], axis=1)[:, -wb:])
            outs["d_s"].append(d_rows.reshape(ns, t, 2, N_HEADS, HEAD_DIM))
        gf = norm_final if last else None
        yp = _half_ffn(yp, *f2, g_final=gf)
        ys = _half_ffn(ys, *f2, g_final=gf)
    st = lambda k: jnp.stack(outs[k])
    return (yp.reshape(nb, s_len, d), ys.reshape(ns, t, d), st("a_p"), st("a_s"), st("bh_p"), st("bh_s"),
            st("bc_p"), st("bc_s"), st("c_p"), st("c_s"), st("cw_p"), st("cw_s"), st("d_p"), st("d_s"))
```

```python
import functools
import math

import jax
import jax.numpy as jnp
import numpy as np
from jax import lax
from jax.experimental import pallas as pl
from jax.experimental.pallas import tpu as pltpu

F32 = jnp.float32
BF16 = jnp.bfloat16

HEAD_DIM = 64
HALF_MIX = 512
N_HEADS = HALF_MIX // HEAD_DIM
N_KV_C = 2
HPG_C = N_HEADS // N_KV_C
PAGE_SIZE = 128
CONV_WIDTH = 4
LRU_C = 8.0
CMP_LEN = 32
CMP_STRIDE = 16
SLC_BLOCK = 64
N_SLC = 16
WINDOW = 512
MOBA_BLOCK = 256
MOBA_TOPK = 3
ROPE_THETA = 10000.0
RMS_EPS = 1e-6
NEG = -1e30
BIG = 1e30
SCALE = HEAD_DIM ** -0.5

LANES = 128
VMEM_LIMIT_BYTES = 56 * 1024 * 1024
SB_CUTOFF = 120.0
HIGHEST = lax.Precision.HIGHEST


def _cparams(*sem):
    return pltpu.CompilerParams(dimension_semantics=sem, vmem_limit_bytes=VMEM_LIMIT_BYTES)


def _tile(n, prefs):
    for t in prefs:
        if n % t == 0:
            return t
    return n


def _dot(a, b, precision=None):
    return jnp.dot(a, b, preferred_element_type=F32, precision=precision)


def _dot_nt(a, b, precision=None):
    return lax.dot_general(a, b, (((1,), (1,)), ((), ())), preferred_element_type=F32, precision=precision)


def _rms(x, g):
    ms = jnp.mean(x * x, axis=-1, keepdims=True)
    return x * lax.rsqrt(ms + RMS_EPS) * g


def _softplus(z):
    return jnp.maximum(z, 0.0) + jnp.log(1.0 + jnp.exp(-jnp.abs(z)))


def _one_minus_exp(x):
    poly = x
    for k in range(7, 1, -1):
        poly = x * (1.0 + poly * (1.0 / k))
    return jnp.where(x > -0.125, -poly, 1.0 - jnp.exp(x))


def _split_dot(x, t):
    hi = x.astype(BF16)
    lo = (x - hi.astype(F32)).astype(BF16)
    return _dot(hi, t) + _dot(lo, t)


def _masked_softmax(s, mask):
    sm = jnp.where(mask, s, NEG)
    m = jnp.max(sm, axis=-1, keepdims=True)
    e = jnp.where(mask, jnp.exp(sm - m), 0.0)
    d = jnp.sum(e, axis=-1, keepdims=True)
    return e / jnp.maximum(d, 1e-30)


def _online_step(s, mask, v, m, l, acc):
    sm = jnp.where(mask, s, NEG)
    m_new = jnp.maximum(m, jnp.max(sm, axis=-1, keepdims=True))
    alpha = jnp.exp(m - m_new)
    p = jnp.where(mask, jnp.exp(sm - m_new), 0.0)
    l = alpha * l + jnp.sum(p, axis=-1, keepdims=True)
    rows = acc.shape[0]
    acc = alpha.reshape(rows, 1) * acc + _dot(p.reshape(rows, p.shape[-1]).astype(BF16), v)
    return m_new, l, acc


def _flash_multi(qs, ks, v1s, biases, m_refs, acc_refs):
    n = len(qs)
    ss = [_dot_nt(qs[i], ks[i]) for i in range(n)]
    ss = [s if b is None else s + b for s, b in zip(ss, biases)]
    m_old = [ref[...] for ref in m_refs]
    m_new = [jnp.maximum(m_old[i], jnp.max(ss[i], axis=-1, keepdims=True)) for i in range(n)]
    ps = [jnp.exp(ss[i] - m_new[i]).astype(BF16) for i in range(n)]
    pvs = [_dot(ps[i], v1s[i]) for i in range(n)]
    for i in range(n):
        acc_refs[i][...] = jnp.exp(m_old[i] - m_new[i]) * acc_refs[i][...] + pvs[i]
        m_refs[i][...] = m_new[i]


def _joint_update(m_s, l_s, acc_s, scores, vals, keys_on_lanes=False):
    m_old = m_s[...]
    m_new = m_old
    for sc in scores:
        m_new = jnp.maximum(m_new, jnp.max(sc, axis=-1, keepdims=True))
    alpha = jnp.exp(m_old - m_new)
    l = alpha * l_s[...]
    acc = alpha * acc_s[...]
    for sc, v in zip(scores, vals):
        p = jnp.exp(sc - m_new)
        l = l + jnp.sum(p, axis=-1, keepdims=True)
        acc = acc + (_dot_nt(p.astype(BF16), v) if keys_on_lanes else _dot(p.astype(BF16), v))
    m_s[...] = m_new
    l_s[...] = l
    acc_s[...] = acc


def _flash_result(acc):
    return acc[:, 0:HEAD_DIM] / acc[:, HEAD_DIM:HEAD_DIM + 1]


def _top_select(val, idx, n):
    sel = jnp.zeros(val.shape, F32)
    big_i = jnp.int32(2 ** 30)
    for _ in range(n):
        m = jnp.max(val, axis=-1, keepdims=True)
        first = jnp.min(jnp.where(val == m, idx, big_i), axis=-1, keepdims=True)
        hit = idx == first
        sel = jnp.where(hit & (m > 0.5 * NEG), 1.0, sel)
        val = jnp.where(hit, -jnp.inf, val)
    return sel


def _ffn_kernel(*refs, n_f, post_norm):
    if post_norm:
        x_ref, g_ref, wg_ref, wu_ref, wd_ref, gf_ref, o_ref, xn_ref, acc_ref = refs
    else:
        x_ref, g_ref, wg_ref, wu_ref, wd_ref, o_ref, xn_ref, acc_ref = refs
    j = pl.program_id(1)

    @pl.when(j == 0)
    def _():
        xn_ref[...] = _rms(x_ref[...], g_ref[...]).astype(BF16)
        acc_ref[...] = jnp.zeros_like(acc_ref)

    xn = xn_ref[...]
    hg = _dot(xn, wg_ref[...])
    hu = _dot(xn, wu_ref[...])
    a = (hg * jax.nn.sigmoid(hg) * hu).astype(BF16)
    acc_ref[...] += _dot(a, wd_ref[...])

    @pl.when(j == n_f - 1)
    def _():
        y = x_ref[...] + 0.5 * acc_ref[...]
        if post_norm:
            y = _rms(y, gf_ref[...])
        o_ref[...] = y


def _half_ffn(x, g, wg, wu, wd, g_final=None):
    m, d = x.shape
    f = wg.shape[1]
    tm = _tile(m, (1024, 512, 256, 128))
    tf = _tile(f, (256, 128))
    n_f = f // tf
    post = g_final is not None
    in_specs = [pl.BlockSpec((tm, d), lambda i, j: (i, 0)),
                pl.BlockSpec((1, d), lambda i, j: (0, 0)),
                pl.BlockSpec((d, tf), lambda i, j: (0, j)),
                pl.BlockSpec((d, tf), lambda i, j: (0, j)),
                pl.BlockSpec((tf, d), lambda i, j: (j, 0))]
    args = [x, g.reshape(1, d), wg, wu, wd]
    if post:
        in_specs.append(pl.BlockSpec((1, d), lambda i, j: (0, 0)))
        args.append(g_final.reshape(1, d))
    return pl.pallas_call(
        functools.partial(_ffn_kernel, n_f=n_f, post_norm=post),
        grid=(m // tm, n_f),
        in_specs=in_specs,
        out_specs=pl.BlockSpec((tm, d), lambda i, j: (i, 0)),
        out_shape=jax.ShapeDtypeStruct((m, d), F32),
        scratch_shapes=[pltpu.VMEM((tm, d), BF16), pltpu.VMEM((tm, d), F32)],
        compiler_params=_cparams("parallel", "arbitrary"),
        name="half_ffn",
    )(*args)


def _rope(seg, cos_t, sin_t):
    w = seg.shape[1]
    reps = w // LANES
    c = jnp.concatenate([cos_t] * reps, axis=1) if reps > 1 else cos_t
    s = jnp.concatenate([sin_t] * reps, axis=1) if reps > 1 else sin_t
    lane = lax.broadcasted_iota(jnp.int32, seg.shape, 1)
    first = (lane % HEAD_DIM) < (HEAD_DIM // 2)
    rot = jnp.where(first, pltpu.roll(seg, w - HEAD_DIM // 2, 1), pltpu.roll(seg, HEAD_DIM // 2, 1))
    return seg * c + rot * s


def _store_heads(ref, seg, scale=None, ones=False):
    if ones:
        lane = lax.broadcasted_iota(jnp.int32, (seg.shape[0], HEAD_DIM), 1)
        tail = (lane == 0).astype(seg.dtype)
    for h in range(seg.shape[1] // HEAD_DIM):
        piece = seg[:, h * HEAD_DIM:(h + 1) * HEAD_DIM]
        if scale is not None:
            piece = piece * scale
        if ones:
            piece = jnp.concatenate([piece, tail], axis=1)
        ref[h] = piece.astype(ref.dtype)


def _proj_even_kernel(x_ref, g_ref, w_ref, kv_ref, xb_ref, gb_ref, *outs, heads):
    hm = HALF_MIX
    xn = _rms(x_ref[...], g_ref[...]).astype(BF16)
    y = _dot(xn, w_ref[...])
    kv_ref[...] = y[:, hm:3 * hm]
    xb_ref[...] = y[:, 3 * hm:4 * hm]
    gb_ref[...] = y[:, 4 * hm:5 * hm]
    if heads:
        qh_ref, kh_ref, vh_ref = outs
        _store_heads(qh_ref, y[:, 0:hm], SCALE)
        _store_heads(kh_ref, y[:, hm:2 * hm])
        _store_heads(vh_ref, y[:, 2 * hm:3 * hm])
    else:
        (q_ref,) = outs
        q_ref[...] = y[:, 0:hm]


def _proj_even(x, g, w, heads):
    m, d = x.shape
    n = w.shape[1]
    hm = HALF_MIX
    tm = _tile(m, (512, 256, 128))
    row = lambda i: (i, 0)
    out_shape = [jax.ShapeDtypeStruct((m, 2 * hm), F32), jax.ShapeDtypeStruct((m, hm), F32),
                 jax.ShapeDtypeStruct((m, hm), F32)]
    out_specs = [pl.BlockSpec((tm, 2 * hm), row), pl.BlockSpec((tm, hm), row), pl.BlockSpec((tm, hm), row)]
    if heads:
        for _ in range(3):
            out_shape.append(jax.ShapeDtypeStruct((N_HEADS, m, HEAD_DIM), BF16))
            out_specs.append(pl.BlockSpec((N_HEADS, tm, HEAD_DIM), lambda i: (0, i, 0)))
    else:
        out_shape.append(jax.ShapeDtypeStruct((m, hm), F32))
        out_specs.append(pl.BlockSpec((tm, hm), row))
    return pl.pallas_call(
        functools.partial(_proj_even_kernel, heads=heads),
        grid=(m // tm,),
        in_specs=[pl.BlockSpec((tm, d), row), pl.BlockSpec((1, d), lambda i: (0, 0)),
                  pl.BlockSpec((d, n), lambda i: (0, 0))],
        out_specs=out_specs, out_shape=out_shape,
        compiler_params=_cparams("parallel"),
        name="proj_even",
    )(x, g.reshape(1, d), w)


_O_QC, _O_KC, _O_VC, _O_KS, _O_VS, _O_KW, _O_VW, _O_QD, _O_KD, _O_VD, _O_GT, _O_END = (
    0, 512, 640, 768, 896, 1024, 1152, 1280, 1792, 2304, 2816, 3072)


def _proj_odd_kernel(x_ref, g_ref, w_ref, cos_ref, sin_ref, c_ref, w_out_ref, d_ref, gt_ref, *outs, heads):
    xn = _rms(x_ref[...], g_ref[...]).astype(BF16)
    y = _dot(xn, w_ref[...])
    cos_t = cos_ref[...]
    sin_t = sin_ref[...]
    qc = y[:, _O_QC:_O_KC]
    qc_rot = _rope(qc, cos_t, sin_t)
    ks_rot = _rope(y[:, _O_KS:_O_VS], cos_t, sin_t)
    kw_rot = _rope(y[:, _O_KW:_O_VW], cos_t, sin_t)
    qd_rot = _rope(y[:, _O_QD:_O_KD], cos_t, sin_t)
    kd_rot = _rope(y[:, _O_KD:_O_VD], cos_t, sin_t)
    vs = y[:, _O_VS:_O_KW]
    vw = y[:, _O_VW:_O_QD]
    vd = y[:, _O_VD:_O_GT]
    c_ref[:, 0:256] = y[:, _O_KC:_O_KS]
    c_ref[:, 256:384] = ks_rot
    c_ref[:, 384:512] = vs
    w_out_ref[:, 0:128] = kw_rot
    w_out_ref[:, 128:256] = vw
    d_ref[:, 0:512] = kd_rot
    d_ref[:, 512:1024] = vd
    gt_ref[...] = jax.nn.sigmoid(y[:, _O_GT:_O_END])
    if heads:
        qraw_h, qrot_h, ks_h, vs_h, kw_h, vw_h, qd_h, kd_h, vd_h = outs
        _store_heads(qraw_h, qc, SCALE)
        _store_heads(qrot_h, qc_rot, SCALE)
        _store_heads(ks_h, ks_rot)
        _store_heads(vs_h, vs, ones=True)
        _store_heads(kw_h, kw_rot)
        _store_heads(vw_h, vw, ones=True)
        _store_heads(qd_h, qd_rot, SCALE)
        _store_heads(kd_h, kd_rot)
        _store_heads(vd_h, vd, ones=True)
    else:
        qraw_ref, qrot_ref, qd_ref = outs
        qraw_ref[...] = qc
        qrot_ref[...] = qc_rot
        qd_ref[...] = qd_rot


def _proj_odd(x, g, w, cos_t, sin_t, heads):
    m, d = x.shape
    n = w.shape[1]
    tm = _tile(m, (512, 256, 128))
    period = cos_t.shape[0] // tm
    row = lambda i: (i, 0)
    tab = lambda i: (i % period, 0)
    out_shape = [jax.ShapeDtypeStruct((m, 512), F32), jax.ShapeDtypeStruct((m, 256), F32),
                 jax.ShapeDtypeStruct((m, 1024), F32), jax.ShapeDtypeStruct((m, 256), F32)]
    out_specs = [pl.BlockSpec((tm, 512), row), pl.BlockSpec((tm, 256), row), pl.BlockSpec((tm, 1024), row),
                 pl.BlockSpec((tm, 256), row)]
    if heads:
        wide = (False, False, False, True, False, True, False, False, True)
        for nh, wd in zip((N_HEADS, N_HEADS, N_KV_C, N_KV_C, N_KV_C, N_KV_C, N_HEADS, N_HEADS, N_HEADS), wide):
            width = LANES if wd else HEAD_DIM
            out_shape.append(jax.ShapeDtypeStruct((nh, m, width), BF16))
            out_specs.append(pl.BlockSpec((nh, tm, width), lambda i: (0, i, 0)))
    else:
        for _ in range(3):
            out_shape.append(jax.ShapeDtypeStruct((m, HALF_MIX), F32))
            out_specs.append(pl.BlockSpec((tm, HALF_MIX), row))
    return pl.pallas_call(
        functools.partial(_proj_odd_kernel, heads=heads),
        grid=(m // tm,),
        in_specs=[pl.BlockSpec((tm, d), row), pl.BlockSpec((1, d), lambda i: (0, 0)),
                  pl.BlockSpec((d, n), lambda i: (0, 0)),
                  pl.BlockSpec((tm, LANES), tab), pl.BlockSpec((tm, LANES), tab)],
        out_specs=out_specs, out_shape=out_shape,
        compiler_params=_cparams("parallel"),
        name="proj_odd",
    )(x, g.reshape(1, d), w, cos_t, sin_t)


def _outproj_kernel(x_ref, a_ref, b_ref, w_ref, o_ref, *, heads):
    hm = HALF_MIX
    if heads:
        a = jnp.concatenate([a_ref[h] for h in range(N_HEADS)], axis=1)
        b = b_ref[...] if len(b_ref.shape) == 2 else jnp.concatenate([b_ref[h] for h in range(N_HEADS)], axis=1)
    else:
        a = a_ref[...]
        b = b_ref[...]
    o_ref[...] = x_ref[...] + _dot(a, w_ref[0:hm, :]) + _dot(b, w_ref[hm:2 * hm, :])


def _outproj(x, a, b, w):
    m, d = x.shape
    tm = _tile(m, (512, 256, 128))
    row = lambda i: (i, 0)

    def spec(t):
        if t.ndim == 2:
            return pl.BlockSpec((tm, t.shape[1]), row)
        return pl.BlockSpec((t.shape[0], tm, t.shape[2]), lambda i: (0, i, 0))

    return pl.pallas_call(
        functools.partial(_outproj_kernel, heads=a.ndim == 3),
        grid=(m // tm,),
        in_specs=[pl.BlockSpec((tm, d), row), spec(a), spec(b), pl.BlockSpec(w.shape, lambda i: (0, 0))],
        out_specs=pl.BlockSpec((tm, d), row),
        out_shape=jax.ShapeDtypeStruct((m, d), F32),
        compiler_params=_cparams("parallel"),
        name="outproj",
    )(x, a, b, w)


def _sb_tile(q, k, v, t_incl, carry, acc, mask, keys_on_lanes=False):
    z = _dot(q, k) if keys_on_lanes else _dot_nt(q, k)
    lk = -_softplus(z)
    if mask is not None:
        lk = jnp.where(mask, lk, 0.0)
    incl = _split_dot(lk, t_incl)
    w = jnp.exp(z + incl + carry)
    if mask is not None:
        w = jnp.where(mask, w, 0.0)
    wb = w.astype(BF16)
    acc = acc + (_dot_nt(wb, v) if keys_on_lanes else _dot(wb, v))
    return carry + incl[:, 0:1], acc


def _tri_incl(n):
    r = lax.broadcasted_iota(jnp.int32, (n, n), 0)
    c = lax.broadcasted_iota(jnp.int32, (n, n), 1)
    return (r >= c).astype(BF16)


def _sb_multi(qs, ks, vs, t_incl, c_refs, a_refs, mask):
    n = len(qs)
    zs = [_dot_nt(qs[i], ks[i]) for i in range(n)]
    lks = [-_softplus(z) for z in zs]
    if mask is not None:
        lks = [jnp.where(mask, lk, 0.0) for lk in lks]
    incls = [_split_dot(lk, t_incl) for lk in lks]
    carries = [ref[...] for ref in c_refs]
    ws = [jnp.exp(zs[i] + incls[i] + carries[i]) for i in range(n)]
    if mask is not None:
        ws = [jnp.where(mask, w, 0.0) for w in ws]
    pvs = [_dot(ws[i].astype(BF16), vs[i]) for i in range(n)]
    alive = None
    for i in range(n):
        carry = carries[i] + incls[i][:, 0:1]
        c_refs[i][...] = carry
        a_refs[i][...] = a_refs[i][...] + pvs[i]
        top = jnp.max(carry)
        alive = top if alive is None else jnp.maximum(alive, top)
    return alive


def _sb_prompt_kernel(q_ref, k_ref, v_ref, o_ref, *state, tq, hb):
    i = pl.program_id(2)
    c_refs = state[:hb]
    a_refs = state[hb:]
    t_incl = _tri_incl(tq)
    r = lax.broadcasted_iota(jnp.int32, (tq, tq), 0)
    c = lax.broadcasted_iota(jnp.int32, (tq, tq), 1)
    for ref in state:
        ref[...] = jnp.zeros_like(ref)
    qs = [q_ref[h] for h in range(hb)]

    def tile(kt, mask):
        start = pl.multiple_of(kt * tq, tq)
        return _sb_multi(qs, [k_ref[h, pl.ds(start, tq), :] for h in range(hb)],
                         [v_ref[h, pl.ds(start, tq), :] for h in range(hb)], t_incl, c_refs, a_refs, mask)

    def cond(s):
        kt, alive = s
        return jnp.logical_and(kt >= 0, alive > -SB_CUTOFF)

    def body(s):
        kt, _ = s
        return kt - 1, tile(kt, None)

    lax.while_loop(cond, body, (i - 1, tile(i, c < r)))
    for h in range(hb):
        o_ref[h] = a_refs[h][...].astype(o_ref.dtype)


def _sb_prompt(qh, kh, vh, n_seq, s_len):
    tq = _tile(s_len, (256, 128))
    nq = s_len // tq
    hb = 4
    return pl.pallas_call(
        functools.partial(_sb_prompt_kernel, tq=tq, hb=hb),
        grid=(n_seq, N_HEADS // hb, nq),
        in_specs=[pl.BlockSpec((hb, tq, HEAD_DIM), lambda b, h, i: (h, b * nq + i, 0)),
                  pl.BlockSpec((hb, s_len, HEAD_DIM), lambda b, h, i: (h, b, 0)),
                  pl.BlockSpec((hb, s_len, HEAD_DIM), lambda b, h, i: (h, b, 0))],
        out_specs=pl.BlockSpec((hb, tq, HEAD_DIM), lambda b, h, i: (h, b * nq + i, 0)),
        out_shape=jax.ShapeDtypeStruct(qh.shape, BF16),
        scratch_shapes=[pltpu.VMEM((tq, 1), F32)] * hb + [pltpu.VMEM((tq, HEAD_DIM), F32)] * hb,
        compiler_params=_cparams("parallel", "parallel", "arbitrary"),
        name="sb_prompt",
    )(qh, kh, vh)


def _lru_kernel(xb_ref, gb_ref, h0_ref, buf0_ref, cw_ref, cb_ref, wr_ref, br_ref, wi_ref, bi_ref, lam_ref,
                g_ref, hl_ref, bn_ref, xpad, a_s, u_s, hs_s, h_s, *, tc, n_t):
    j = pl.program_id(1)
    keep = CONV_WIDTH - 1

    @pl.when(j == 0)
    def _():
        xpad[8 - keep:8, :] = buf0_ref[0]
        h_s[...] = h0_ref[0]

    @pl.when(j > 0)
    def _():
        xpad[8 - keep:8, :] = xpad[8 + tc - keep:8 + tc, :]

    x = xb_ref[0]
    xpad[8:8 + tc, :] = x
    xc = cb_ref[...] + x * cw_ref[keep:keep + 1, :]
    for d in range(1, CONV_WIDTH):
        xc = xc + xpad[8 - d:8 - d + tc, :] * cw_ref[keep - d:keep - d + 1, :]
    xcb = xc.astype(BF16)
    r = jax.nn.sigmoid(_dot(xcb, wr_ref[...]) + br_ref[...])
    gi = jax.nn.sigmoid(_dot(xcb, wi_ref[...]) + bi_ref[...])
    log_a = -LRU_C * r * _softplus(-lam_ref[...])
    a_s[...] = jnp.exp(log_a)
    u_s[...] = jnp.sqrt(_one_minus_exp(2.0 * log_a)) * (gi * xc)

    def step(t, h):
        h = a_s[pl.ds(t, 1), :] * h + u_s[pl.ds(t, 1), :]
        hs_s[pl.ds(t, 1), :] = h
        return h

    h = lax.fori_loop(0, tc, step, h_s[...], unroll=min(tc, 8))
    h_s[...] = h
    g_ref[0] = (jax.nn.gelu(gb_ref[0]) * hs_s[...]).astype(g_ref.dtype)

    @pl.when(j == n_t - 1)
    def _():
        hl_ref[0] = h
        bn_ref[0] = xpad[8 + tc - keep:8 + tc, :]


def _rg_lru(xb, gb, h0, buf0, conv_w, conv_b, wr_bd, b_r, wi_bd, b_i, lam):
    n, t, w = xb.shape
    keep = CONV_WIDTH - 1
    assert t >= keep
    tc = _tile(t, (512, 256, 128))
    n_t = t // tc
    seq = lambda b, j: (b, j, 0)
    one = lambda b, j: (b, 0, 0)
    const = lambda b, j: (0, 0)
    vec = pl.BlockSpec((1, w), const)
    g, hl, bn = pl.pallas_call(
        functools.partial(_lru_kernel, tc=tc, n_t=n_t),
        grid=(n, n_t),
        in_specs=[pl.BlockSpec((1, tc, w), seq), pl.BlockSpec((1, tc, w), seq),
                  pl.BlockSpec((1, 1, w), one), pl.BlockSpec((1, keep, w), one),
                  pl.BlockSpec((CONV_WIDTH, w), const), vec,
                  pl.BlockSpec((w, w), const), vec, pl.BlockSpec((w, w), const), vec, vec],
        out_specs=[pl.BlockSpec((1, tc, w), seq), pl.BlockSpec((1, 1, w), one), pl.BlockSpec((1, keep, w), one)],
        out_shape=[jax.ShapeDtypeStruct((n, t, w), BF16), jax.ShapeDtypeStruct((n, 1, w), F32),
                   jax.ShapeDtypeStruct((n, keep, w), F32)],
        scratch_shapes=[pltpu.VMEM((tc + 8, w), F32), pltpu.VMEM((tc, w), F32), pltpu.VMEM((tc, w), F32),
                        pltpu.VMEM((tc, w), F32), pltpu.VMEM((1, w), F32)],
        compiler_params=_cparams("parallel", "arbitrary"),
        name="rg_lru",
    )(xb, gb, h0.reshape(n, 1, w), buf0, conv_w, conv_b.reshape(1, w), wr_bd, b_r.reshape(1, w), wi_bd,
      b_i.reshape(1, w), lam.reshape(1, w))
    return g, hl.reshape(n, w), bn


def _cmp_ab_kernel(*refs, n_in, rows, n_prefetch, pages_t):
    refs = refs[n_prefetch:]
    k_refs = refs[:n_in]
    v_refs = refs[n_in:2 * n_in]
    pe_ak, pe_bk, pe_av, pe_bv, w_ak, w_bk, w_av, w_bv, o_ref = refs[2 * n_in:2 * n_in + 9]
    nch = rows // CMP_STRIDE
    if pages_t:
        xk_s, xv_s = refs[2 * n_in + 9:]
        for s in range(n_in):
            xk_s[s * rows:(s + 1) * rows, :] = k_refs[s][...].T
            xv_s[s * rows:(s + 1) * rows, :] = v_refs[s][...].T
        k_refs, v_refs, nch = [xk_s], [xv_s], n_in * nch

    def chunks(in_refs):
        per = [jnp.concatenate([r[pl.ds(p, nch, stride=CMP_STRIDE), :] for p in range(CMP_STRIDE)], axis=1)
               for r in in_refs]
        return per[0] if len(in_refs) == 1 else jnp.concatenate(per, axis=0)

    xk = chunks(k_refs)
    xv = chunks(v_refs)
    o_ref[0, :, 0:128] = _dot((xk + pe_ak[...]).astype(BF16), w_ak[...])
    o_ref[0, :, 128:256] = _dot((xk + pe_bk[...]).astype(BF16), w_bk[...])
    o_ref[0, :, 256:384] = _dot((xv + pe_av[...]).astype(BF16), w_av[...])
    o_ref[0, :, 384:512] = _dot((xv + pe_bv[...]).astype(BF16), w_bv[...])


def _cmp_weights(pe, w1):
    hid = w1.shape[-1]
    eye = jnp.eye(N_KV_C, dtype=F32)
    out = []
    for half in range(CMP_LEN // CMP_STRIDE):
        sl = slice(half * CMP_STRIDE, (half + 1) * CMP_STRIDE)
        pe_row = jnp.broadcast_to(pe[sl][:, None, :], (CMP_STRIDE, N_KV_C, HEAD_DIM)).reshape(1, -1)
        wbd = jnp.einsum('pdh,gk->pgdkh', w1[sl], eye).reshape(CMP_STRIDE * N_KV_C * HEAD_DIM, N_KV_C * hid)
        out.append((pe_row, wbd.astype(BF16)))
    return out


def _cmp_ab(in_arrays, k_specs, v_specs, grid, out_rows, n_seq, n_chunks, rows, cmp_k, cmp_v, num_prefetch,
            prefetch, pages_t=False):
    in_specs = list(k_specs) + list(v_specs)
    (pe_ak, w_ak), (pe_bk, w_bk) = cmp_k
    (pe_av, w_av), (pe_bv, w_bv) = cmp_v
    nd = len(grid)
    const = lambda *a: (0, 0)
    pes = [pe_ak, pe_bk, pe_av, pe_bv]
    ws = [w_ak, w_bk, w_av, w_bv]
    specs = list(in_specs) + [pl.BlockSpec(p.shape, const) for p in pes] + [pl.BlockSpec(w.shape, const) for w in ws]
    gs = pltpu.PrefetchScalarGridSpec(
        num_scalar_prefetch=num_prefetch, grid=grid, in_specs=specs,
        out_specs=pl.BlockSpec((1, out_rows, 512), lambda b, j, *a: (b, j, 0)),
        scratch_shapes=[pltpu.VMEM((len(in_arrays) * rows, LANES), F32)] * 2 if pages_t else [])
    return pl.pallas_call(
        functools.partial(_cmp_ab_kernel, n_in=len(in_arrays), rows=rows, n_prefetch=num_prefetch, pages_t=pages_t),
        grid_spec=gs,
        out_shape=jax.ShapeDtypeStruct((n_seq, n_chunks, 512), F32),
        compiler_params=_cparams(*(["parallel"] + ["arbitrary"] * (nd - 1))),
        name="cmp_ab",
    )(*prefetch, *in_arrays, *in_arrays, *pes, *ws)


def _cmp_finish_kernel(ab_ref, w2k_ref, w2v_ref, kf_ref, vf_ref, kg_ref, vg_ref, *, nc):
    n = nc - 1
    hk = jax.nn.gelu(ab_ref[0, 0:n, 0:128] + ab_ref[0, 1:nc, 128:256]).astype(BF16)
    hv = jax.nn.gelu(ab_ref[0, 0:n, 256:384] + ab_ref[0, 1:nc, 384:512]).astype(BF16)
    kc = _dot(hk, w2k_ref[...]).astype(BF16)
    vc = _dot(hv, w2v_ref[...]).astype(BF16)
    zero = jnp.zeros((1, LANES), BF16)
    for ref_f, ref_g, val in ((kf_ref, kg_ref, kc), (vf_ref, vg_ref, vc)):
        full = jnp.concatenate([val, zero], axis=0)
        ref_f[0] = full
        for g in range(N_KV_C):
            ref_g[0, g] = full[:, g * HEAD_DIM:(g + 1) * HEAD_DIM]


def _cmp_finish(ab, w2k_bd, w2v_bd):
    n, nc, _ = ab.shape
    flat = jax.ShapeDtypeStruct((n, nc, LANES), BF16)
    grp = jax.ShapeDtypeStruct((n, N_KV_C, nc, HEAD_DIM), BF16)
    fspec = pl.BlockSpec((1, nc, LANES), lambda b: (b, 0, 0))
    gspec = pl.BlockSpec((1, N_KV_C, nc, HEAD_DIM), lambda b: (b, 0, 0, 0))
    return pl.pallas_call(
        functools.partial(_cmp_finish_kernel, nc=nc),
        grid=(n,),
        in_specs=[pl.BlockSpec((1, nc, 512), lambda b: (b, 0, 0)),
                  pl.BlockSpec((LANES, LANES), lambda b: (0, 0)), pl.BlockSpec((LANES, LANES), lambda b: (0, 0))],
        out_specs=[fspec, fspec, gspec, gspec], out_shape=[flat, flat, grp, grp],
        compiler_params=_cparams("parallel"),
        name="cmp_finish",
    )(ab, w2k_bd, w2v_bd)


def _overlap_t(nb, nc):
    b = lax.broadcasted_iota(jnp.int32, (nb, nc), 0)
    c = lax.broadcasted_iota(jnp.int32, (nb, nc), 1)
    return ((c * CMP_STRIDE < (b + 1) * SLC_BLOCK) & (c * CMP_STRIDE + CMP_LEN > b * SLC_BLOCK)).astype(F32)


def _nsa_prompt_kernel(qraw_ref, qrot_ref, gt_ref, kcc_ref, vcc_ref, ks_ref, vs_ref, kw_ref, vw_ref, o_ref,
                       *state, tq, nc, nsb, ck):
    i = pl.program_id(2)
    q0 = i * tq
    rows = HPG_C * tq
    nbp = LANES
    qpos3 = q0 + lax.broadcasted_iota(jnp.int32, (1, tq, 1), 1)

    qraw = qraw_ref[...].reshape(rows, HEAD_DIM)
    sc = _dot_nt(qraw, kcc_ref[0, 0]).reshape(HPG_C, tq, nc)
    cidx = lax.broadcasted_iota(jnp.int32, (1, 1, nc), 2)
    mask_c = (cidx * CMP_STRIDE + CMP_LEN - 1 <= qpos3) & (cidx < nc - 1)
    pc = _masked_softmax(sc, mask_c)
    o_c = _dot(pc.reshape(rows, nc).astype(BF16), vcc_ref[0, 0])

    psum = jnp.sum(pc, axis=0)
    imp_t = _dot_nt(_overlap_t(nsb, nc), psum, precision=HIGHEST)
    blk = lax.broadcasted_iota(jnp.int32, (nsb, tq), 0)
    own = (q0 + lax.broadcasted_iota(jnp.int32, (nsb, tq), 1)) // SLC_BLOCK
    imp_t = jnp.where(blk == own, BIG, jnp.where(blk < own, imp_t, NEG))
    cnt = jnp.zeros((nsb, tq), jnp.int32)
    for b2 in range(nsb):
        rowv = imp_t[b2:b2 + 1, :]
        ahead = (rowv > imp_t) | ((rowv == imp_t) & (b2 < blk))
        cnt = cnt + ahead.astype(jnp.int32)
    sel_t = ((cnt < N_SLC) & (blk <= own)).astype(F32)
    if nsb < nbp:
        sel_t = jnp.concatenate([sel_t, jnp.zeros((nbp - nsb, tq), F32)], axis=0)
    sel_bias = jnp.where(sel_t.T > 0.5, 0.0, NEG).astype(BF16)
    m_refs = state[:2 * HPG_C]
    acc_refs = state[2 * HPG_C:]
    qs = [qrot_ref[j] for j in range(HPG_C)]
    for ref in m_refs:
        ref[...] = jnp.full(ref.shape, NEG, F32)
    for ref in acc_refs:
        ref[...] = jnp.zeros_like(ref)

    eb = lax.broadcasted_iota(jnp.int32, (nbp, ck), 0)
    ej = lax.broadcasted_iota(jnp.int32, (nbp, ck), 1)

    def slc_bias(start):
        expand = ((start + ej) // SLC_BLOCK == eb).astype(BF16)
        return _dot(sel_bias, expand)

    c_own = q0 // ck
    start = pl.multiple_of(c_own * ck, ck)
    row = lax.broadcasted_iota(jnp.int32, (tq, ck), 0)
    lane = lax.broadcasted_iota(jnp.int32, (tq, ck), 1)
    bias_s = jnp.where(start + lane <= q0 + row, slc_bias(start), NEG)
    wk = WINDOW + tq
    start_w = pl.multiple_of(jnp.maximum(q0 - WINDOW, 0), tq)
    dist = (q0 + lax.broadcasted_iota(jnp.int32, (tq, wk), 0)) - (start_w + lax.broadcasted_iota(jnp.int32, (tq, wk), 1))
    bias_w = jnp.where((dist >= 0) & (dist <= WINDOW), 0.0, NEG)
    _flash_multi(qs + qs,
                 [ks_ref[0, pl.ds(start, ck), :]] * HPG_C + [kw_ref[0, pl.ds(start_w, wk), :]] * HPG_C,
                 [vs_ref[0, pl.ds(start, ck), :]] * HPG_C + [vw_ref[0, pl.ds(start_w, wk), :]] * HPG_C,
                 [bias_s] * HPG_C + [bias_w] * HPG_C, m_refs, acc_refs)

    def slc_step(cc, carry):
        st = pl.multiple_of(cc * ck, ck)
        _flash_multi(qs, [ks_ref[0, pl.ds(st, ck), :]] * HPG_C, [vs_ref[0, pl.ds(st, ck), :]] * HPG_C,
                     [slc_bias(st)] * HPG_C, m_refs[:HPG_C], acc_refs[:HPG_C])
        return carry

    lax.fori_loop(0, c_own, slc_step, 0)

    gt = gt_ref[...]
    for j in range(HPG_C):
        sl = slice(j * tq, (j + 1) * tq)
        g_c = gt[:, j:j + 1]
        g_s = gt[:, HPG_C + j:HPG_C + j + 1]
        g_w = gt[:, 2 * HPG_C + j:2 * HPG_C + j + 1]
        o_ref[j] = (o_c[sl] * g_c + _flash_result(acc_refs[j][...]) * g_s
                    + _flash_result(acc_refs[HPG_C + j][...]) * g_w).astype(o_ref.dtype)


def _nsa_prompt(qraw_h, qrot_h, gates, kcc_g, vcc_g, ks_h, vs_h, kw_h, vw_h, n_seq, s_len):
    tq = 128
    nq = s_len // tq
    nc = kcc_g.shape[2]
    nsb = -(-s_len // SLC_BLOCK)
    ck = _tile(s_len, (512, 256, 128))
    assert s_len % tq == 0 and nsb <= LANES and WINDOW % tq == 0 and s_len >= WINDOW + tq and ck % tq == 0
    qspec = pl.BlockSpec((HPG_C, tq, HEAD_DIM), lambda b, g, i: (g, b * nq + i, 0))
    cspec = pl.BlockSpec((1, 1, nc, HEAD_DIM), lambda b, g, i: (b, g, 0, 0))
    kspec = pl.BlockSpec((1, s_len, HEAD_DIM), lambda b, g, i: (g, b, 0))
    vspec = pl.BlockSpec((1, s_len, LANES), lambda b, g, i: (g, b, 0))
    return pl.pallas_call(
        functools.partial(_nsa_prompt_kernel, tq=tq, nc=nc, nsb=nsb, ck=ck),
        grid=(n_seq, N_KV_C, nq),
        in_specs=[qspec, qspec, pl.BlockSpec((tq, LANES), lambda b, g, i: (b * nq + i, g)),
                  cspec, cspec, kspec, vspec, kspec, vspec],
        out_specs=qspec,
        out_shape=jax.ShapeDtypeStruct(qraw_h.shape, BF16),
        scratch_shapes=[pltpu.VMEM((tq, 1), F32)] * (2 * HPG_C) + [pltpu.VMEM((tq, LANES), F32)] * (2 * HPG_C),
        compiler_params=_cparams("parallel", "parallel", "arbitrary"),
        name="nsa_prompt",
    )(qraw_h, qrot_h, gates, kcc_g, vcc_g, ks_h, vs_h, kw_h, vw_h)


def _block_mean_kernel(x_ref, o_ref, *, nb):
    x = x_ref[...]
    m = jnp.mean(x.reshape(nb, MOBA_BLOCK, x.shape[1]), axis=1)
    for h in range(N_HEADS):
        o_ref[h] = m[:, h * HEAD_DIM:(h + 1) * HEAD_DIM]


def _block_mean(d_rows):
    m = d_rows.shape[0]
    nb_total = m // MOBA_BLOCK
    nb = _tile(nb_total, (8,))
    return pl.pallas_call(
        functools.partial(_block_mean_kernel, nb=nb),
        grid=(nb_total // nb,),
        in_specs=[pl.BlockSpec((nb * MOBA_BLOCK, HALF_MIX), lambda i: (i, 0))],
        out_specs=pl.BlockSpec((N_HEADS, nb, HEAD_DIM), lambda i: (0, i, 0)),
        out_shape=jax.ShapeDtypeStruct((N_HEADS, nb_total, HEAD_DIM), F32),
        compiler_params=_cparams("parallel"),
        name="moba_block_mean",
    )(d_rows)


def _moba_prompt_kernel(q_ref, km_ref, k_ref, v_ref, o_ref, *state, tq, nb, sub, hb):
    i = pl.program_id(2)
    nh = tq // sub
    nchain = hb * nh
    m_refs = state[:nchain]
    acc_refs = state[nchain:]
    nbp = LANES

    blk_t = lax.broadcasted_iota(jnp.int32, (nb, tq), 0)
    sel_bias = []
    for hh in range(hb):
        gate_t = _dot_nt(km_ref[hh], q_ref[hh].astype(F32) * (1.0 / SCALE), precision=HIGHEST)
        gate_t = jnp.where(blk_t < i, gate_t, NEG)
        cnt = jnp.zeros((nb, tq), jnp.int32)
        for b2 in range(nb):
            rowv = gate_t[b2:b2 + 1, :]
            cnt = cnt + ((rowv > gate_t) | ((rowv == gate_t) & (b2 < blk_t))).astype(jnp.int32)
        sel_t = ((cnt < MOBA_TOPK) & (blk_t < i)).astype(F32)
        sel_t = jnp.concatenate([sel_t, jnp.zeros((nbp - nb, tq), F32)], axis=0)
        sel_bias.append(jnp.where(sel_t.T > 0.5, 0.0, NEG).astype(BF16))

    qs = [q_ref[ci // nh, (ci % nh) * sub:(ci % nh + 1) * sub, :] for ci in range(nchain)]

    def update(start, width, biases):
        ks = [k_ref[hh, pl.ds(start, width), :] for hh in range(hb)]
        v1s = [v_ref[hh, pl.ds(start, width), :] for hh in range(hb)]
        _flash_multi(qs, [ks[ci // nh] for ci in range(nchain)], [v1s[ci // nh] for ci in range(nchain)], biases,
                     m_refs, acc_refs)

    def block_bias(expand):
        return [_dot(sel_bias[ci // nh][(ci % nh) * sub:(ci % nh + 1) * sub, :], expand) for ci in range(nchain)]

    for ref in m_refs:
        ref[...] = jnp.full(ref.shape, NEG, F32)
    for ref in acc_refs:
        ref[...] = jnp.zeros_like(ref)

    r = lax.broadcasted_iota(jnp.int32, (sub, tq), 0)
    c = lax.broadcasted_iota(jnp.int32, (sub, tq), 1)
    update(pl.multiple_of(i * tq, tq), tq, [jnp.where(c <= r + (ci % nh) * sub, 0.0, NEG) for ci in range(nchain)])

    pk = 2 * tq
    eb2 = lax.broadcasted_iota(jnp.int32, (nbp, pk), 0)
    ej2 = lax.broadcasted_iota(jnp.int32, (nbp, pk), 1)

    def pair_step(cc, carry):
        update(pl.multiple_of(cc * pk, pk), pk, block_bias((eb2 == 2 * cc + ej2 // tq).astype(BF16)))
        return carry

    lax.fori_loop(0, i // 2, pair_step, 0)

    @pl.when(i % 2 == 1)
    def _():
        eb1 = lax.broadcasted_iota(jnp.int32, (nbp, tq), 0)
        update(pl.multiple_of((i - 1) * tq, tq), tq, block_bias((eb1 == i - 1).astype(BF16)))

    for ci in range(nchain):
        o_ref[ci // nh, (ci % nh) * sub:(ci % nh + 1) * sub, :] = _flash_result(acc_refs[ci][...]).astype(o_ref.dtype)


def _moba_prompt(qd_h, kmean_h, kd_h, vd_h, n_seq, s_len):
    tq = MOBA_BLOCK
    assert s_len % tq == 0
    nq = s_len // tq
    assert nq <= LANES
    sub = 128
    hb = 4
    nchain = hb * (tq // sub)
    kspec = pl.BlockSpec((hb, s_len, HEAD_DIM), lambda b, h, i: (h, b, 0))
    vspec = pl.BlockSpec((hb, s_len, LANES), lambda b, h, i: (h, b, 0))
    qspec = pl.BlockSpec((hb, tq, HEAD_DIM), lambda b, h, i: (h, b * nq + i, 0))
    return pl.pallas_call(
        functools.partial(_moba_prompt_kernel, tq=tq, nb=nq, sub=sub, hb=hb),
        grid=(n_seq, N_HEADS // hb, nq),
        in_specs=[qspec, pl.BlockSpec((hb, nq, HEAD_DIM), lambda b, h, i: (h, b, 0)), kspec, vspec],
        out_specs=qspec,
        out_shape=jax.ShapeDtypeStruct(qd_h.shape, BF16),
        scratch_shapes=[pltpu.VMEM((sub, 1), F32)] * nchain + [pltpu.VMEM((sub, LANES), F32)] * nchain,
        compiler_params=_cparams("parallel", "parallel", "arbitrary"),
        name="moba_prompt",
    )(qd_h, kmean_h, kd_h, vd_h)


def _page_specs(n_slots, block, tail, base):
    def make(s):
        return lambda b, j, pt: (base + pt[b, j * n_slots + s],) + tuple(tail)

    return [pl.BlockSpec((None,) + tuple(block), make(s)) for s in range(n_slots)]


def _pages_t(cache):
    l, p, r, a, b, d = cache.shape
    return jnp.transpose(cache, (0, 1, 3, 4, 5, 2)).reshape(l * p, a, b * d, r)


def _head_diag(rows, width, per):
    r = lax.broadcasted_iota(jnp.int32, (rows, width), 0)
    c = lax.broadcasted_iota(jnp.int32, (rows, width), 1)
    return ((r % N_HEADS) * HEAD_DIM // per) == (c // HEAD_DIM * HEAD_DIM // per)


def _fold_heads(acc, t):
    kept = jnp.where(_head_diag(acc.shape[0], acc.shape[1], HEAD_DIM), acc, 0.0)
    return jnp.sum(kept.reshape(t, N_HEADS, acc.shape[1]), axis=1)


def _sb_sample_kernel(pt_ref, q_ref, kn_ref, vn_ref, pool_ref, o_ref, buf, sem, *, n_pages, t, pad, base):
    b = pl.program_id(0)
    rows = t * N_HEADS
    q = q_ref[0]

    def page_copy(p, slot):
        return pltpu.make_async_copy(pool_ref.at[base + pt_ref[b, p]], buf.at[slot], sem.at[slot])

    page_copy(n_pages - 1, 0).start()

    r = lax.broadcasted_iota(jnp.int32, (rows, pad), 0) // N_HEADS
    c = lax.broadcasted_iota(jnp.int32, (rows, pad), 1)
    carry, acc = _sb_tile(q, kn_ref[0], vn_ref[0], _tri_incl(pad), jnp.zeros((rows, 1), F32),
                          jnp.zeros((rows, HALF_MIX), F32), (c < r) & (c < t))
    t_incl = _tri_incl(PAGE_SIZE)

    def cond(s):
        p, carry, _ = s
        return jnp.logical_and(p >= 0, jnp.max(carry) > -SB_CUTOFF)

    def body(s):
        p, carry, acc = s
        slot = (n_pages - 1 - p) % 2
        page_copy(p, slot).wait()

        @pl.when(p > 0)
        def _():
            page_copy(p - 1, 1 - slot).start()

        carry, acc = _sb_tile(q, buf[slot, 0].astype(BF16), buf[slot, 1].astype(BF16), t_incl, carry, acc, None,
                              keys_on_lanes=True)
        return p - 1, carry, acc

    p_end, _, acc = lax.while_loop(cond, body, (jnp.int32(n_pages - 1), carry, acc))

    @pl.when(p_end >= 0)
    def _():
        page_copy(p_end, (n_pages - 1 - p_end) % 2).wait()

    o_ref[0] = _fold_heads(acc, t)


def _sb_sample(qbd, k_new, v_new, pool_t, base, page_table):
    n, rows, _ = qbd.shape
    t = rows // N_HEADS
    pad = k_new.shape[1]
    n_pages = page_table.shape[1]
    seq = lambda b, pt: (b, 0, 0)
    gs = pltpu.PrefetchScalarGridSpec(
        num_scalar_prefetch=1, grid=(n,),
        in_specs=[pl.BlockSpec((1, rows, HALF_MIX), seq), pl.BlockSpec((1, pad, HALF_MIX), seq),
                  pl.BlockSpec((1, pad, HALF_MIX), seq), pl.BlockSpec(memory_space=pl.ANY)],
        out_specs=pl.BlockSpec((1, t, HALF_MIX), seq),
        scratch_shapes=[pltpu.VMEM((2,) + pool_t.shape[1:], pool_t.dtype), pltpu.SemaphoreType.DMA((2,))])
    return pl.pallas_call(
        functools.partial(_sb_sample_kernel, n_pages=n_pages, t=t, pad=pad, base=base),
        grid_spec=gs,
        out_shape=jax.ShapeDtypeStruct((n, t, HALF_MIX), F32),
        compiler_params=_cparams("arbitrary"),
        name="sb_sample",
    )(page_table, qbd, k_new, v_new, pool_t)


def _nsa_sample_head_kernel(qraw_ref, qrot_ref, kcc_ref, vcc_ref, win_ref, wn_ref, oc_ref, ow_ref, sel_ref,
                            *, t, nc, n_past, nbp, pad):
    rows = t * N_HEADS
    wb = win_ref.shape[1]
    tok = lax.broadcasted_iota(jnp.int32, (rows, 1), 0) // N_HEADS
    qpos = n_past + tok

    def fold(x):
        g0 = (lax.broadcasted_iota(jnp.int32, (rows, HEAD_DIM), 0) % N_HEADS) < HPG_C
        return jnp.where(g0, x[:, 0:HEAD_DIM], x[:, HEAD_DIM:2 * HEAD_DIM])

    sc = _dot_nt(qraw_ref[0], kcc_ref[0])
    cidx = lax.broadcasted_iota(jnp.int32, (1, nc), 1)
    pc = _masked_softmax(sc, (cidx * CMP_STRIDE + CMP_LEN - 1 <= qpos) & (cidx < nc - 1))
    oc_ref[0] = fold(_dot(pc.astype(BF16), vcc_ref[0]))

    ng = t * N_KV_C
    gr = lax.broadcasted_iota(jnp.int32, (ng, rows), 0)
    rr = lax.broadcasted_iota(jnp.int32, (ng, rows), 1)
    same = (gr // N_KV_C == rr // N_HEADS) & (gr % N_KV_C == (rr % N_HEADS) // HPG_C)
    psum = _dot(same.astype(F32), pc, precision=HIGHEST)
    imp = _dot_nt(psum, _overlap_t(nbp, nc), precision=HIGHEST)
    blk = lax.broadcasted_iota(jnp.int32, (ng, nbp), 1)
    own = (n_past + lax.broadcasted_iota(jnp.int32, (ng, nbp), 0) // N_KV_C) // SLC_BLOCK
    imp = jnp.where(blk == own, BIG, jnp.where(blk < own, imp, NEG))
    sel_ref[0] = _top_select(imp, blk, N_SLC)

    qrot = qrot_ref[0]
    win = win_ref[0]
    wn = wn_ref[0]
    s1 = _dot_nt(qrot, win[:, 0:LANES].astype(BF16))
    wpos1 = n_past - wb + lax.broadcasted_iota(jnp.int32, (1, wb), 1)
    d1 = qpos - wpos1
    mk1 = (d1 >= 0) & (d1 <= WINDOW) & (wpos1 >= 0)
    s2 = _dot_nt(qrot, wn[:, 0:LANES])
    j2 = lax.broadcasted_iota(jnp.int32, (1, pad), 1)
    mk2 = (j2 <= tok) & (j2 < t)
    m = jnp.maximum(jnp.max(jnp.where(mk1, s1, NEG), axis=-1, keepdims=True),
                    jnp.max(jnp.where(mk2, s2, NEG), axis=-1, keepdims=True))
    p1 = jnp.where(mk1, jnp.exp(jnp.where(mk1, s1, NEG) - m), 0.0)
    p2 = jnp.where(mk2, jnp.exp(jnp.where(mk2, s2, NEG) - m), 0.0)
    den = jnp.sum(p1, axis=-1, keepdims=True) + jnp.sum(p2, axis=-1, keepdims=True)
    ow = _dot(p1.astype(BF16), win[:, LANES:2 * LANES].astype(BF16)) + _dot(p2.astype(BF16), wn[:, LANES:2 * LANES])
    ow_ref[0] = fold(ow / den)


def _nsa_sample_head(qraw_bd, qrot_bd, kcc_f, vcc_f, win, w_new, n_past, nbp):
    n, rows, _ = qraw_bd.shape
    t = rows // N_HEADS
    nc = kcc_f.shape[1]
    wb = win.shape[1]
    pad = w_new.shape[1]
    seq = lambda b: (b, 0, 0)
    return pl.pallas_call(
        functools.partial(_nsa_sample_head_kernel, t=t, nc=nc, n_past=n_past, nbp=nbp, pad=pad),
        grid=(n,),
        in_specs=[pl.BlockSpec((1, rows, LANES), seq), pl.BlockSpec((1, rows, LANES), seq),
                  pl.BlockSpec((1, nc, LANES), seq), pl.BlockSpec((1, nc, LANES), seq),
                  pl.BlockSpec((1, wb, 2 * LANES), seq), pl.BlockSpec((1, pad, 2 * LANES), seq)],
        out_specs=[pl.BlockSpec((1, rows, HEAD_DIM), seq), pl.BlockSpec((1, rows, HEAD_DIM), seq),
                   pl.BlockSpec((1, t * N_KV_C, nbp), seq)],
        out_shape=[jax.ShapeDtypeStruct((n, rows, HEAD_DIM), F32), jax.ShapeDtypeStruct((n, rows, HEAD_DIM), F32),
                   jax.ShapeDtypeStruct((n, t * N_KV_C, nbp), F32)],
        compiler_params=_cparams("parallel"),
        name="nsa_sample_head",
    )(qraw_bd, qrot_bd, kcc_f, vcc_f, win, w_new)


def _nsa_sample_slc_kernel(pt_ref, q_ref, sel_ref, sn_ref, oc_ref, ow_ref, gt_ref, *refs,
                           n_slots, n_steps, t, nbp, pad):
    page_refs = refs[:n_slots]
    o_ref, m_s, l_s, acc_s = refs[n_slots:]
    j = pl.program_id(1)
    rows = t * N_HEADS
    q = q_ref[0]
    tok = lax.broadcasted_iota(jnp.int32, (rows, 1), 0) // N_HEADS

    @pl.when(j == 0)
    def _():
        sn = sn_ref[0]
        j2 = lax.broadcasted_iota(jnp.int32, (1, pad), 1)
        m, l, acc = _online_step(_dot_nt(q, sn[:, 0:LANES]), (j2 <= tok) & (j2 < t), sn[:, LANES:2 * LANES],
                                 jnp.full((rows, 1), NEG, F32), jnp.zeros((rows, 1), F32),
                                 jnp.zeros((rows, LANES), F32))
        m_s[...] = m
        l_s[...] = l
        acc_s[...] = acc

    sel_bias = sel_ref[0]
    eb = lax.broadcasted_iota(jnp.int32, (nbp, PAGE_SIZE), 0)
    ej = lax.broadcasted_iota(jnp.int32, (nbp, PAGE_SIZE), 1)
    group = 8
    for s0 in range(0, n_slots, group):
        scores, vals = [], []
        for s in range(s0, min(s0 + group, n_slots)):
            first_blk = (j * n_slots + s) * (PAGE_SIZE // SLC_BLOCK)
            bias = _dot(sel_bias, (first_blk + ej // SLC_BLOCK == eb).astype(BF16))
            scores.append(_dot(q, page_refs[s][0].astype(BF16)) + bias)
            vals.append(page_refs[s][1].astype(BF16))
        _joint_update(m_s, l_s, acc_s, scores, vals, keys_on_lanes=True)

    @pl.when(j == n_steps - 1)
    def _():
        o = acc_s[...] / l_s[...]
        g0 = (lax.broadcasted_iota(jnp.int32, (rows, HEAD_DIM), 0) % N_HEADS) < HPG_C
        o_s = jnp.where(g0, o[:, 0:HEAD_DIM], o[:, HEAD_DIM:2 * HEAD_DIM])
        gt = gt_ref[0]
        o_ref[0] = oc_ref[0] * gt[:, 0:1] + o_s * gt[:, 1:2] + ow_ref[0] * gt[:, 2:3]


def _nsa_sample_slc(qrot_bd, sel_rows, s_new, o_c, o_w, gates, pool, base, page_table):
    n, rows, _ = qrot_bd.shape
    t = rows // N_HEADS
    nbp = sel_rows.shape[2]
    pad = s_new.shape[1]
    n_pages = page_table.shape[1]
    n_slots = _tile(n_pages, (16, 8, 4, 2))
    n_steps = n_pages // n_slots
    seq = lambda b, j, pt: (b, 0, 0)
    gs = pltpu.PrefetchScalarGridSpec(
        num_scalar_prefetch=1, grid=(n, n_steps),
        in_specs=[pl.BlockSpec((1, rows, LANES), seq), pl.BlockSpec((1, rows, nbp), seq),
                  pl.BlockSpec((1, pad, 2 * LANES), seq), pl.BlockSpec((1, rows, HEAD_DIM), seq),
                  pl.BlockSpec((1, rows, HEAD_DIM), seq), pl.BlockSpec((1, rows, 3), seq)]
        + _page_specs(n_slots, (2, LANES, PAGE_SIZE), (1, 0, 0), base),
        out_specs=pl.BlockSpec((1, rows, HEAD_DIM), seq),
        scratch_shapes=[pltpu.VMEM((rows, 1), F32), pltpu.VMEM((rows, 1), F32), pltpu.VMEM((rows, LANES), F32)])
    return pl.pallas_call(
        functools.partial(_nsa_sample_slc_kernel, n_slots=n_slots, n_steps=n_steps, t=t, nbp=nbp, pad=pad),
        grid_spec=gs,
        out_shape=jax.ShapeDtypeStruct((n, rows, HEAD_DIM), F32),
        compiler_params=_cparams("parallel", "arbitrary"),
        name="nsa_sample_slc",
    )(page_table, qrot_bd, sel_rows, s_new, o_c, o_w, gates, *([pool] * n_slots))


def _page_sum_kernel(pt_ref, *refs, n_slots):
    o_ref = refs[n_slots]
    j = pl.program_id(1)
    ppb = MOBA_BLOCK // PAGE_SIZE

    @pl.when(j == 0)
    def _():
        o_ref[...] = jnp.zeros_like(o_ref)

    lane = lax.broadcasted_iota(jnp.int32, o_ref.shape[1:], 1)
    acc = o_ref[0]
    for s in range(n_slots):
        tot = jnp.sum(refs[s][...], axis=1, keepdims=True)
        acc = acc + jnp.where(lane == (j * n_slots + s) // ppb, tot, 0.0)
    o_ref[0] = acc


def _page_sums(pool_t, base, page_table, nbp):
    n, n_pages = page_table.shape
    n_slots = _tile(n_pages, (16, 8, 4, 2))
    gs = pltpu.PrefetchScalarGridSpec(
        num_scalar_prefetch=1, grid=(n, n_pages // n_slots),
        in_specs=_page_specs(n_slots, (None, HALF_MIX, PAGE_SIZE), (0, 0, 0), base),
        out_specs=pl.BlockSpec((1, HALF_MIX, nbp), lambda b, j, pt: (b, 0, 0)))
    return pl.pallas_call(
        functools.partial(_page_sum_kernel, n_slots=n_slots),
        grid_spec=gs,
        out_shape=jax.ShapeDtypeStruct((n, HALF_MIX, nbp), F32),
        compiler_params=_cparams("parallel", "arbitrary"),
        name="moba_page_sums",
    )(page_table, *([pool_t] * n_slots))


def _moba_sample_kernel(pt_ref, q_ref, qf_ref, ks_ref, kn_ref, vn_ref, *refs, n_slots, n_steps, t, nbp, pad,
                        n_past):
    page_refs = refs[:n_slots]
    o_ref, sel_s, m_s, l_s, acc_s = refs[n_slots:]
    j = pl.program_id(1)
    rows = t * N_HEADS
    ppb = MOBA_BLOCK // PAGE_SIZE
    nb_full = n_past // MOBA_BLOCK
    q = q_ref[0]
    tok = lax.broadcasted_iota(jnp.int32, (rows, 1), 0) // N_HEADS

    @pl.when(j == 0)
    def _():
        gate = _dot(qf_ref[0], ks_ref[0] * (1.0 / MOBA_BLOCK), precision=HIGHEST)
        blk = lax.broadcasted_iota(jnp.int32, (rows, nbp), 1)
        own = (n_past + tok) // MOBA_BLOCK
        gate = jnp.where((blk < own) & (blk < nb_full), gate, NEG)
        sel_s[...] = jnp.where(_top_select(gate, blk, MOBA_TOPK) > 0.5, 0.0, NEG).astype(BF16)
        j2 = lax.broadcasted_iota(jnp.int32, (1, pad), 1)
        m, l, acc = _online_step(_dot_nt(q, kn_ref[0]), (j2 <= tok) & (j2 < t), vn_ref[0],
                                 jnp.full((rows, 1), NEG, F32), jnp.zeros((rows, 1), F32),
                                 jnp.zeros((rows, HALF_MIX), F32))
        m_s[...] = m
        l_s[...] = l
        acc_s[...] = acc

    sel_bias = sel_s[...]
    eb = lax.broadcasted_iota(jnp.int32, (nbp, PAGE_SIZE), 0)
    scores, vals = [], []
    for s in range(n_slots):
        blk_of_page = (j * n_slots + s) // ppb
        bias = _dot(sel_bias, (eb == blk_of_page).astype(BF16))
        scores.append(_dot(q, page_refs[s][0].astype(BF16)) + bias)
        vals.append(page_refs[s][1].astype(BF16))
    _joint_update(m_s, l_s, acc_s, scores, vals, keys_on_lanes=True)

    @pl.when(j == n_steps - 1)
    def _():
        o_ref[0] = _fold_heads(acc_s[...] / l_s[...], t)


def _moba_sample(qbd, qbd_f32, ksums, k_new, v_new, pool, base, page_table, n_past):
    n, rows, _ = qbd.shape
    t = rows // N_HEADS
    pad = k_new.shape[1]
    n_pages = page_table.shape[1]
    assert n_past % MOBA_BLOCK == 0 and t <= MOBA_BLOCK
    nbp = ksums.shape[2]
    n_slots = _tile(n_pages, (8, 4, 2))
    n_steps = n_pages // n_slots
    seq = lambda b, j, pt: (b, 0, 0)
    gs = pltpu.PrefetchScalarGridSpec(
        num_scalar_prefetch=1, grid=(n, n_steps),
        in_specs=[pl.BlockSpec((1, rows, HALF_MIX), seq), pl.BlockSpec((1, rows, HALF_MIX), seq),
                  pl.BlockSpec((1, HALF_MIX, nbp), seq),
                  pl.BlockSpec((1, pad, HALF_MIX), seq), pl.BlockSpec((1, pad, HALF_MIX), seq)]
        + _page_specs(n_slots, (2, HALF_MIX, PAGE_SIZE), (0, 0, 0), base),
        out_specs=pl.BlockSpec((1, t, HALF_MIX), seq),
        scratch_shapes=[pltpu.VMEM((rows, nbp), BF16), pltpu.VMEM((rows, 1), F32), pltpu.VMEM((rows, 1), F32),
                        pltpu.VMEM((rows, HALF_MIX), F32)])
    return pl.pallas_call(
        functools.partial(_moba_sample_kernel, n_slots=n_slots, n_steps=n_steps, t=t, nbp=nbp, pad=pad,
                          n_past=n_past),
        grid_spec=gs,
        out_shape=jax.ShapeDtypeStruct((n, t, HALF_MIX), F32),
        compiler_params=_cparams("parallel", "arbitrary"),
        name="moba_sample",
    )(page_table, qbd, qbd_f32, ksums, k_new, v_new, *([pool] * n_slots))


def _block_diag(blocks):
    nb, bi, bj = blocks.shape
    eye = jnp.eye(nb, dtype=blocks.dtype)
    return jnp.einsum('bij,bc->bicj', blocks, eye).reshape(nb * bi, nb * bj)


def _rope_tables(pos):
    half = HEAD_DIM // 2
    inv_freq = ROPE_THETA ** (-jnp.arange(half, dtype=F32) / half)
    ang = pos.astype(F32)[:, None] * inv_freq[None, :]
    cos = jnp.cos(ang)
    sin = jnp.sin(ang)
    return jnp.concatenate([cos, cos, cos, cos], axis=1), jnp.concatenate([-sin, sin, -sin, sin], axis=1)


def _queries_bd(q, per, scale, dtype):
    n, t, _ = q.shape
    groups = N_HEADS // per
    qh = q.reshape(n, t, N_HEADS, 1, HEAD_DIM) * scale
    pick = (jnp.arange(N_HEADS)[:, None] // per == jnp.arange(groups)[None, :]).astype(q.dtype)
    return (qh * pick[None, None, :, :, None]).reshape(n, t * N_HEADS, groups * HEAD_DIM).astype(dtype)


def _pad_rows(x, pad):
    n, t, w = x.shape
    return jnp.concatenate([x, jnp.zeros((n, pad - t, w), x.dtype)], axis=1)


def _odd_weight(w_in):
    d = w_in.shape[0]
    gc0 = _O_VW + 128
    main = jnp.concatenate([w_in[:, :gc0], w_in[:, gc0 + 3 * N_HEADS:]], axis=1)
    gc = w_in[:, gc0:gc0 + 3 * N_HEADS].reshape(d, N_KV_C, HPG_C, 3).transpose(0, 1, 3, 2).reshape(d, N_KV_C, 3 * HPG_C)
    gc = jnp.concatenate([gc, jnp.zeros((d, N_KV_C, LANES - 3 * HPG_C), w_in.dtype)], axis=2).reshape(d, N_KV_C * LANES)
    return jnp.concatenate([main, gc], axis=1).astype(BF16)


def kernel(x_prompt, x_sample, cache_a_kv, state_b_h, state_b_conv, cache_c_kv, state_c_win, cache_d_kv, page_table, norm_ffn1, w_ffn1_gate, w_ffn1_up, w_ffn1_down, norm_mix, norm_ffn2, w_ffn2_gate, w_ffn2_up, w_ffn2_down, w_in_even, w_out_even, lru_conv_w, lru_conv_b, lru_w_r, lru_b_r, lru_w_i, lru_b_i, lru_lambda, w_in_odd, w_out_odd, cmp_pe_k, cmp_w1_k, cmp_w2_k, cmp_pe_v, cmp_w1_v, cmp_w2_v, norm_final):
    nb, s_len, d = x_prompt.shape
    ns, t = x_sample.shape[:2]
    depth = norm_mix.shape[0]
    n_pages = page_table.shape[1]
    n_past = n_pages * PAGE_SIZE
    n_pool = cache_a_kv.shape[1]
    hm = HALF_MIX
    pad = 16
    assert t <= pad and n_past % SLC_BLOCK == 0

    yp = x_prompt.reshape(nb * s_len, d)
    ys = x_sample.reshape(ns * t, d)
    cos_p, sin_p = _rope_tables(jnp.arange(s_len, dtype=jnp.int32))
    cos_s, sin_s = _rope_tables(jnp.tile(n_past + jnp.arange(t, dtype=jnp.int32), ns))
    pool_a = _pages_t(cache_a_kv)
    pool_c = _pages_t(cache_c_kv)
    pool_d = _pages_t(cache_d_kv)

    outs = {k: [] for k in ("a_p", "a_s", "bh_p", "bh_s", "bc_p", "bc_s", "c_p", "c_s", "cw_p", "cw_s", "d_p", "d_s")}
    for layer in range(depth):
        li = layer // 2
        last = layer == depth - 1
        f1 = (norm_ffn1[layer], w_ffn1_gate[layer].astype(BF16), w_ffn1_up[layer].astype(BF16),
              w_ffn1_down[layer].astype(BF16))
        f2 = (norm_ffn2[layer], w_ffn2_gate[layer].astype(BF16), w_ffn2_up[layer].astype(BF16),
              w_ffn2_down[layer].astype(BF16))
        yp = _half_ffn(yp, *f1)
        ys = _half_ffn(ys, *f1)
        if layer % 2 == 0:
            w_in = w_in_even[li].astype(BF16)
            w_out = w_out_even[li].astype(BF16)
            lru = (lru_conv_w[li], lru_conv_b[li], _block_diag(lru_w_r[li]).astype(BF16), lru_b_r[li],
                   _block_diag(lru_w_i[li]).astype(BF16), lru_b_i[li], lru_lambda[li])
            kv, xb, gb, qh, kh, vh = _proj_even(yp, norm_mix[layer], w_in, True)
            o_a = _sb_prompt(qh, kh, vh, nb, s_len)
            g, h_new, buf_new = _rg_lru(xb.reshape(nb, s_len, hm), gb.reshape(nb, s_len, hm),
                                        jnp.zeros((nb, hm), F32), jnp.zeros((nb, CONV_WIDTH - 1, hm), F32), *lru)
            yp = _outproj(yp, o_a, g.reshape(nb * s_len, hm), w_out)
            outs["a_p"].append(kv.reshape(nb, s_len, 2, N_HEADS, HEAD_DIM))
            outs["bh_p"].append(h_new)
            outs["bc_p"].append(buf_new)
            kv, xb, gb, q = _proj_even(ys, norm_mix[layer], w_in, False)
            kv3 = kv.reshape(ns, t, 2 * hm)
            o_a = _sb_sample(_queries_bd(q.reshape(ns, t, hm), 1, SCALE, BF16),
                             _pad_rows(kv3[:, :, :hm], pad).astype(BF16), _pad_rows(kv3[:, :, hm:], pad).astype(BF16),
                             pool_a, li * n_pool, page_table)
            g, h_new, buf_new = _rg_lru(xb.reshape(ns, t, hm), gb.reshape(ns, t, hm), state_b_h[li],
                                        state_b_conv[li], *lru)
            ys = _outproj(ys, o_a.reshape(ns * t, hm).astype(BF16), g.reshape(ns * t, hm), w_out)
            outs["a_s"].append(kv.reshape(ns, t, 2, N_HEADS, HEAD_DIM))
            outs["bh_s"].append(h_new)
            outs["bc_s"].append(buf_new)
        else:
            w_in = _odd_weight(w_in_odd[li])
            w_out = w_out_odd[li].astype(BF16)
            cmp_k = _cmp_weights(cmp_pe_k[li], cmp_w1_k[li])
            cmp_v = _cmp_weights(cmp_pe_v[li], cmp_w1_v[li])
            w2k = _block_diag(jnp.stack([cmp_w2_k[li]] * N_KV_C)).astype(BF16)
            w2v = _block_diag(jnp.stack([cmp_w2_v[li]] * N_KV_C)).astype(BF16)
            (c_rows, w_rows, d_rows, gates, qraw_h, qrot_h, ks_h, vs_h, kw_h, vw_h, qd_h, kd_h, vd_h) = _proj_odd(
                yp, norm_mix[layer], w_in, cos_p, sin_p, True)
            n_chunk = s_len // CMP_STRIDE
            rows_c = _tile(s_len, (2048, 1024, 512, 256, 128))
            ab = _cmp_ab([c_rows], [pl.BlockSpec((rows_c, LANES), lambda b, j: (b * (s_len // rows_c) + j, 0))],
                         [pl.BlockSpec((rows_c, LANES), lambda b, j: (b * (s_len // rows_c) + j, 1))],
                         (nb, s_len // rows_c), rows_c // CMP_STRIDE, nb, n_chunk, rows_c, cmp_k, cmp_v, 0, ())
            _, _, kcc_g, vcc_g = _cmp_finish(ab, w2k, w2v)
            o_c = _nsa_prompt(qraw_h, qrot_h, gates, kcc_g, vcc_g, ks_h, vs_h, kw_h, vw_h, nb, s_len)
            o_d = _moba_prompt(qd_h, _block_mean(d_rows), kd_h, vd_h, nb, s_len)
            yp = _outproj(yp, o_c, o_d, w_out)
            wlen = min(WINDOW, s_len)
            outs["c_p"].append(c_rows.reshape(nb, s_len, 4, N_KV_C, HEAD_DIM))
            outs["cw_p"].append(w_rows.reshape(nb, s_len, 2, N_KV_C, HEAD_DIM)[:, s_len - wlen:])
            outs["d_p"].append(d_rows.reshape(nb, s_len, 2, N_HEADS, HEAD_DIM))
            c_rows, w_rows, d_rows, gates, qraw, qrot, qd = _proj_odd(ys, norm_mix[layer], w_in, cos_s, sin_s, False)
            slots_c = _tile(n_pages, (16, 8))
            ab = _cmp_ab([pool_c] * slots_c, _page_specs(slots_c, (None, LANES, PAGE_SIZE), (0, 0, 0), li * n_pool),
                         _page_specs(slots_c, (None, LANES, PAGE_SIZE), (1, 0, 0), li * n_pool),
                         (ns, n_pages // slots_c), slots_c * PAGE_SIZE // CMP_STRIDE, ns, n_past // CMP_STRIDE,
                         PAGE_SIZE, cmp_k, cmp_v, 1, (page_table,), pages_t=True)
            kcc_f, vcc_f, _, _ = _cmp_finish(ab, w2k, w2v)
            nsb = -(-(n_past + t) // SLC_BLOCK)
            nbp = LANES * (-(-nsb // LANES))
            win_l = state_c_win[li]
            wb = win_l.shape[1]
            qraw_bd = _queries_bd(qraw.reshape(ns, t, hm), HPG_C, SCALE, BF16)
            qrot_bd = _queries_bd(qrot.reshape(ns, t, hm), HPG_C, SCALE, BF16)
            c3 = c_rows.reshape(ns, t, 4 * LANES)
            w3 = w_rows.reshape(ns, t, 2 * LANES)
            d3 = d_rows.reshape(ns, t, 2 * hm)
            o_cmp, o_win, sel = _nsa_sample_head(qraw_bd, qrot_bd, kcc_f, vcc_f, win_l.reshape(ns, wb, 2 * LANES),
                                                 _pad_rows(w3, pad).astype(BF16), n_past, nbp)
            sel_rows = jnp.repeat(sel.reshape(ns, t, N_KV_C, nbp), HPG_C, axis=2).reshape(ns, t * N_HEADS, nbp)
            gt = gates.reshape(ns, t, N_KV_C, LANES)[..., :3 * HPG_C].reshape(ns, t, N_KV_C, 3, HPG_C)
            gt = gt.transpose(0, 1, 2, 4, 3).reshape(ns, t * N_HEADS, 3)
            sel_bias = jnp.where(sel_rows > 0.5, 0.0, NEG).astype(BF16)
            o_c = _nsa_sample_slc(qrot_bd, sel_bias, _pad_rows(c3[:, :, 2 * LANES:], pad).astype(BF16),
                                  o_cmp, o_win, gt, pool_c, li * n_pool, page_table)
            ksums = _page_sums(pool_d, li * n_pool, page_table, LANES * (-(-(n_past // MOBA_BLOCK) // LANES)))
            qd3 = qd.reshape(ns, t, hm)
            o_d = _moba_sample(_queries_bd(qd3, 1, SCALE, BF16), _queries_bd(qd3, 1, 1.0, F32), ksums,
                               _pad_rows(d3[:, :, :hm], pad).astype(BF16), _pad_rows(d3[:, :, hm:], pad).astype(BF16),
                               pool_d, li * n_pool, page_table, n_past)
            ys = _outproj(ys, o_c.reshape(ns * t, hm).astype(BF16), o_d.reshape(ns * t, hm).astype(BF16), w_out)
            w5 = w_rows.reshape(ns, t, 2, N_KV_C, HEAD_DIM)
            outs["c_s"].append(c_rows.reshape(ns, t, 4, N_KV_C, HEAD_DIM))
            outs["cw_s"].append(jnp.concatenate([win_l, w5], axis=1)[:, -wb:])
            outs["d_s"].append(d_rows.reshape(ns, t, 2, N_HEADS, HEAD_DIM))
        gf = norm_final if last else None
        yp = _half_ffn(yp, *f2, g_final=gf)
        ys = _half_ffn(ys, *f2, g_final=gf)
    st = lambda k: jnp.stack(outs[k])
    return (yp.reshape(nb, s_len, d), ys.reshape(ns, t, d), st("a_p"), st("a_s"), st("bh_p"), st("bh_s"),
            st("bc_p"), st("bc_s"), st("c_p"), st("c_s"), st("cw_p"), st("cw_s"), st("d_p"), st("d_s"))
```

```python
import functools
import math

import jax
import jax.numpy as jnp
import numpy as np
from jax import lax
from jax.experimental import pallas as pl
from jax.experimental.pallas import tpu as pltpu

F32 = jnp.float32
BF16 = jnp.bfloat16

HEAD_DIM = 64
HALF_MIX = 512
N_HEADS = HALF_MIX // HEAD_DIM
N_KV_C = 2
HPG_C = N_HEADS // N_KV_C
PAGE_SIZE = 128
CONV_WIDTH = 4
LRU_C = 8.0
CMP_LEN = 32
CMP_STRIDE = 16
SLC_BLOCK = 64
N_SLC = 16
WINDOW = 512
MOBA_BLOCK = 256
MOBA_TOPK = 3
ROPE_THETA = 10000.0
RMS_EPS = 1e-6
NEG = -1e30
BIG = 1e30
SCALE = HEAD_DIM ** -0.5

LANES = 128
VMEM_LIMIT_BYTES = 56 * 1024 * 1024
SB_CUTOFF = 120.0
HIGHEST = lax.Precision.HIGHEST


def _cparams(*sem):
    return pltpu.CompilerParams(dimension_semantics=sem, vmem_limit_bytes=VMEM_LIMIT_BYTES)


def _tile(n, prefs):
    for t in prefs:
        if n % t == 0:
            return t
    return n


def _dot(a, b, precision=None):
    return jnp.dot(a, b, preferred_element_type=F32, precision=precision)


def _dot_nt(a, b, precision=None):
    return lax.dot_general(a, b, (((1,), (1,)), ((), ())), preferred_element_type=F32, precision=precision)


def _rms(x, g):
    ms = jnp.mean(x * x, axis=-1, keepdims=True)
    return x * lax.rsqrt(ms + RMS_EPS) * g


def _softplus(z):
    return jnp.maximum(z, 0.0) + jnp.log(1.0 + jnp.exp(-jnp.abs(z)))


def _one_minus_exp(x):
    poly = x
    for k in range(7, 1, -1):
        poly = x * (1.0 + poly * (1.0 / k))
    return jnp.where(x > -0.125, -poly, 1.0 - jnp.exp(x))


def _split_dot(x, t):
    hi = x.astype(BF16)
    lo = (x - hi.astype(F32)).astype(BF16)
    return _dot(hi, t) + _dot(lo, t)


def _masked_softmax(s, mask):
    sm = jnp.where(mask, s, NEG)
    m = jnp.max(sm, axis=-1, keepdims=True)
    e = jnp.where(mask, jnp.exp(sm - m), 0.0)
    d = jnp.sum(e, axis=-1, keepdims=True)
    return e / jnp.maximum(d, 1e-30)


def _online_step(s, mask, v, m, l, acc):
    sm = jnp.where(mask, s, NEG)
    m_new = jnp.maximum(m, jnp.max(sm, axis=-1, keepdims=True))
    alpha = jnp.exp(m - m_new)
    p = jnp.where(mask, jnp.exp(sm - m_new), 0.0)
    l = alpha * l + jnp.sum(p, axis=-1, keepdims=True)
    rows = acc.shape[0]
    acc = alpha.reshape(rows, 1) * acc + _dot(p.reshape(rows, p.shape[-1]).astype(BF16), v)
    return m_new, l, acc


def _flash_multi(qs, ks, v1s, biases, m_refs, acc_refs):
    n = len(qs)
    ss = [_dot_nt(qs[i], ks[i]) for i in range(n)]
    ss = [s if b is None else s + b for s, b in zip(ss, biases)]
    m_old = [ref[...] for ref in m_refs]
    m_new = [jnp.maximum(m_old[i], jnp.max(ss[i], axis=-1, keepdims=True)) for i in range(n)]
    ps = [jnp.exp(ss[i] - m_new[i]).astype(BF16) for i in range(n)]
    pvs = [_dot(ps[i], v1s[i]) for i in range(n)]
    for i in range(n):
        acc_refs[i][...] = jnp.exp(m_old[i] - m_new[i]) * acc_refs[i][...] + pvs[i]
        m_refs[i][...] = m_new[i]


def _joint_update(m_s, l_s, acc_s, scores, vals, keys_on_lanes=False):
    m_old = m_s[...]
    m_new = m_old
    for sc in scores:
        m_new = jnp.maximum(m_new, jnp.max(sc, axis=-1, keepdims=True))
    alpha = jnp.exp(m_old - m_new)
    l = alpha * l_s[...]
    acc = alpha * acc_s[...]
    for sc, v in zip(scores, vals):
        p = jnp.exp(sc - m_new)
        l = l + jnp.sum(p, axis=-1, keepdims=True)
        acc = acc + (_dot_nt(p.astype(BF16), v) if keys_on_lanes else _dot(p.astype(BF16), v))
    m_s[...] = m_new
    l_s[...] = l
    acc_s[...] = acc


def _flash_result(acc):
    return acc[:, 0:HEAD_DIM] / acc[:, HEAD_DIM:HEAD_DIM + 1]


def _top_select(val, idx, n):
    sel = jnp.zeros(val.shape, F32)
    big_i = jnp.int32(2 ** 30)
    for _ in range(n):
        m = jnp.max(val, axis=-1, keepdims=True)
        first = jnp.min(jnp.where(val == m, idx, big_i), axis=-1, keepdims=True)
        hit = idx == first
        sel = jnp.where(hit & (m > 0.5 * NEG), 1.0, sel)
        val = jnp.where(hit, -jnp.inf, val)
    return sel


def _ffn_kernel(*refs, n_f, post_norm):
    if post_norm:
        x_ref, g_ref, wg_ref, wu_ref, wd_ref, gf_ref, o_ref, xn_ref, acc_ref = refs
    else:
        x_ref, g_ref, wg_ref, wu_ref, wd_ref, o_ref, xn_ref, acc_ref = refs
    j = pl.program_id(1)

    @pl.when(j == 0)
    def _():
        xn_ref[...] = _rms(x_ref[...], g_ref[...]).astype(BF16)
        acc_ref[...] = jnp.zeros_like(acc_ref)

    xn = xn_ref[...]
    hg = _dot(xn, wg_ref[...])
    hu = _dot(xn, wu_ref[...])
    a = (hg * jax.nn.sigmoid(hg) * hu).astype(BF16)
    acc_ref[...] += _dot(a, wd_ref[...])

    @pl.when(j == n_f - 1)
    def _():
        y = x_ref[...] + 0.5 * acc_ref[...]
        if post_norm:
            y = _rms(y, gf_ref[...])
        o_ref[...] = y


def _half_ffn(x, g, wg, wu, wd, g_final=None):
    m, d = x.shape
    f = wg.shape[1]
    tm = _tile(m, (1024, 512, 256, 128))
    tf = _tile(f, (256, 128))
    n_f = f // tf
    post = g_final is not None
    in_specs = [pl.BlockSpec((tm, d), lambda i, j: (i, 0)),
                pl.BlockSpec((1, d), lambda i, j: (0, 0)),
                pl.BlockSpec((d, tf), lambda i, j: (0, j)),
                pl.BlockSpec((d, tf), lambda i, j: (0, j)),
                pl.BlockSpec((tf, d), lambda i, j: (j, 0))]
    args = [x, g.reshape(1, d), wg, wu, wd]
    if post:
        in_specs.append(pl.BlockSpec((1, d), lambda i, j: (0, 0)))
        args.append(g_final.reshape(1, d))
    return pl.pallas_call(
        functools.partial(_ffn_kernel, n_f=n_f, post_norm=post),
        grid=(m // tm, n_f),
        in_specs=in_specs,
        out_specs=pl.BlockSpec((tm, d), lambda i, j: (i, 0)),
        out_shape=jax.ShapeDtypeStruct((m, d), F32),
        scratch_shapes=[pltpu.VMEM((tm, d), BF16), pltpu.VMEM((tm, d), F32)],
        compiler_params=_cparams("parallel", "arbitrary"),
        name="half_ffn",
    )(*args)


def _rope(seg, cos_t, sin_t):
    w = seg.shape[1]
    reps = w // LANES
    c = jnp.concatenate([cos_t] * reps, axis=1) if reps > 1 else cos_t
    s = jnp.concatenate([sin_t] * reps, axis=1) if reps > 1 else sin_t
    lane = lax.broadcasted_iota(jnp.int32, seg.shape, 1)
    first = (lane % HEAD_DIM) < (HEAD_DIM // 2)
    rot = jnp.where(first, pltpu.roll(seg, w - HEAD_DIM // 2, 1), pltpu.roll(seg, HEAD_DIM // 2, 1))
    return seg * c + rot * s


def _store_heads(ref, seg, scale=None, tail=None):
    for h in range(seg.shape[1] // HEAD_DIM):
        piece = seg[:, h * HEAD_DIM:(h + 1) * HEAD_DIM]
        if scale is not None:
            piece = piece * scale
        if tail is not None:
            piece = jnp.concatenate([piece, tail], axis=1)
        ref[h] = piece.astype(ref.dtype)


def _proj_even_kernel(x_ref, g_ref, w_ref, kv_ref, xb_ref, gb_ref, *outs, heads):
    hm = HALF_MIX
    xn = _rms(x_ref[...], g_ref[...]).astype(BF16)
    y = _dot(xn, w_ref[...])
    kv_ref[...] = y[:, hm:3 * hm]
    xb_ref[...] = y[:, 3 * hm:4 * hm]
    gb_ref[...] = y[:, 4 * hm:5 * hm]
    if heads:
        qh_ref, kh_ref, vh_ref = outs
        _store_heads(qh_ref, y[:, 0:hm], SCALE)
        _store_heads(kh_ref, y[:, hm:2 * hm])
        _store_heads(vh_ref, y[:, 2 * hm:3 * hm])
    else:
        (q_ref,) = outs
        q_ref[...] = y[:, 0:hm]


def _proj_even(x, g, w, heads):
    m, d = x.shape
    n = w.shape[1]
    hm = HALF_MIX
    tm = _tile(m, (512, 256, 128))
    row = lambda i: (i, 0)
    out_shape = [jax.ShapeDtypeStruct((m, 2 * hm), F32), jax.ShapeDtypeStruct((m, hm), F32),
                 jax.ShapeDtypeStruct((m, hm), F32)]
    out_specs = [pl.BlockSpec((tm, 2 * hm), row), pl.BlockSpec((tm, hm), row), pl.BlockSpec((tm, hm), row)]
    if heads:
        for _ in range(3):
            out_shape.append(jax.ShapeDtypeStruct((N_HEADS, m, HEAD_DIM), BF16))
            out_specs.append(pl.BlockSpec((N_HEADS, tm, HEAD_DIM), lambda i: (0, i, 0)))
    else:
        out_shape.append(jax.ShapeDtypeStruct((m, hm), F32))
        out_specs.append(pl.BlockSpec((tm, hm), row))
    return pl.pallas_call(
        functools.partial(_proj_even_kernel, heads=heads),
        grid=(m // tm,),
        in_specs=[pl.BlockSpec((tm, d), row), pl.BlockSpec((1, d), lambda i: (0, 0)),
                  pl.BlockSpec((d, n), lambda i: (0, 0))],
        out_specs=out_specs, out_shape=out_shape,
        compiler_params=_cparams("parallel"),
        name="proj_even",
    )(x, g.reshape(1, d), w)


_O_QC, _O_KC, _O_VC, _O_KS, _O_VS, _O_KW, _O_VW, _O_QD, _O_KD, _O_VD, _O_GT, _O_END = (
    0, 512, 640, 768, 896, 1024, 1152, 1280, 1792, 2304, 2816, 3072)


def _proj_odd_kernel(x_ref, g_ref, w_ref, cos_ref, sin_ref, c_ref, w_out_ref, d_ref, gt_ref, *outs, heads, tm,
                     period):
    xn = _rms(x_ref[...], g_ref[...]).astype(BF16)
    y = _dot(xn, w_ref[...])
    cos_t = cos_ref[...]
    sin_t = sin_ref[...]
    qc = y[:, _O_QC:_O_KC]
    qc_rot = _rope(qc, cos_t, sin_t)
    ks_rot = _rope(y[:, _O_KS:_O_VS], cos_t, sin_t)
    kw_rot = _rope(y[:, _O_KW:_O_VW], cos_t, sin_t)
    qd_rot = _rope(y[:, _O_QD:_O_KD], cos_t, sin_t)
    kd_rot = _rope(y[:, _O_KD:_O_VD], cos_t, sin_t)
    vs = y[:, _O_VS:_O_KW]
    vw = y[:, _O_VW:_O_QD]
    vd = y[:, _O_VD:_O_GT]
    c_ref[:, 0:256] = y[:, _O_KC:_O_KS]
    c_ref[:, 256:384] = ks_rot
    c_ref[:, 384:512] = vs
    w_out_ref[:, 0:128] = kw_rot
    w_out_ref[:, 128:256] = vw
    d_ref[:, 0:512] = kd_rot
    d_ref[:, 512:1024] = vd
    gt_ref[...] = jax.nn.sigmoid(y[:, _O_GT:_O_END])
    if heads:
        qraw_h, qrot_h, ks_h, vs_h, kw_h, vw_h, qd_h, kd_h, vd_h = outs
        pos = (pl.program_id(0) % period) * tm + lax.broadcasted_iota(jnp.int32, (tm, HEAD_DIM), 0)
        lane = lax.broadcasted_iota(jnp.int32, (tm, HEAD_DIM), 1)
        ones = (lane == 0).astype(F32)
        _store_heads(qraw_h, qc, SCALE)
        _store_heads(qrot_h, qc_rot, SCALE)
        _store_heads(ks_h, ks_rot, tail=(lane == pos // SLC_BLOCK).astype(F32))
        _store_heads(vs_h, vs, tail=ones)
        _store_heads(kw_h, kw_rot)
        _store_heads(vw_h, vw, tail=ones)
        _store_heads(qd_h, qd_rot, SCALE)
        _store_heads(kd_h, kd_rot, tail=(lane == pos // MOBA_BLOCK).astype(F32))
        _store_heads(vd_h, vd, tail=ones)
    else:
        qraw_ref, qrot_ref, qd_ref = outs
        qraw_ref[...] = qc
        qrot_ref[...] = qc_rot
        qd_ref[...] = qd_rot


def _proj_odd(x, g, w, cos_t, sin_t, heads):
    m, d = x.shape
    n = w.shape[1]
    tm = _tile(m, (512, 256, 128))
    period = cos_t.shape[0] // tm
    row = lambda i: (i, 0)
    tab = lambda i: (i % period, 0)
    out_shape = [jax.ShapeDtypeStruct((m, 512), F32), jax.ShapeDtypeStruct((m, 256), F32),
                 jax.ShapeDtypeStruct((m, 1024), F32), jax.ShapeDtypeStruct((m, 256), F32)]
    out_specs = [pl.BlockSpec((tm, 512), row), pl.BlockSpec((tm, 256), row), pl.BlockSpec((tm, 1024), row),
                 pl.BlockSpec((tm, 256), row)]
    if heads:
        wide = (False, False, True, True, False, True, False, True, True)
        for nh, wd in zip((N_HEADS, N_HEADS, N_KV_C, N_KV_C, N_KV_C, N_KV_C, N_HEADS, N_HEADS, N_HEADS), wide):
            width = LANES if wd else HEAD_DIM
            out_shape.append(jax.ShapeDtypeStruct((nh, m, width), BF16))
            out_specs.append(pl.BlockSpec((nh, tm, width), lambda i: (0, i, 0)))
    else:
        for _ in range(3):
            out_shape.append(jax.ShapeDtypeStruct((m, HALF_MIX), F32))
            out_specs.append(pl.BlockSpec((tm, HALF_MIX), row))
    return pl.pallas_call(
        functools.partial(_proj_odd_kernel, heads=heads, tm=tm, period=period),
        grid=(m // tm,),
        in_specs=[pl.BlockSpec((tm, d), row), pl.BlockSpec((1, d), lambda i: (0, 0)),
                  pl.BlockSpec((d, n), lambda i: (0, 0)),
                  pl.BlockSpec((tm, LANES), tab), pl.BlockSpec((tm, LANES), tab)],
        out_specs=out_specs, out_shape=out_shape,
        compiler_params=_cparams("parallel"),
        name="proj_odd",
    )(x, g.reshape(1, d), w, cos_t, sin_t)


def _outproj_kernel(x_ref, a_ref, b_ref, w_ref, o_ref, *, heads):
    hm = HALF_MIX
    if heads:
        a = jnp.concatenate([a_ref[h] for h in range(N_HEADS)], axis=1)
        b = b_ref[...] if len(b_ref.shape) == 2 else jnp.concatenate([b_ref[h] for h in range(N_HEADS)], axis=1)
    else:
        a = a_ref[...]
        b = b_ref[...]
    o_ref[...] = x_ref[...] + _dot(a, w_ref[0:hm, :]) + _dot(b, w_ref[hm:2 * hm, :])


def _outproj(x, a, b, w):
    m, d = x.shape
    tm = _tile(m, (512, 256, 128))
    row = lambda i: (i, 0)

    def spec(t):
        if t.ndim == 2:
            return pl.BlockSpec((tm, t.shape[1]), row)
        return pl.BlockSpec((t.shape[0], tm, t.shape[2]), lambda i: (0, i, 0))

    return pl.pallas_call(
        functools.partial(_outproj_kernel, heads=a.ndim == 3),
        grid=(m // tm,),
        in_specs=[pl.BlockSpec((tm, d), row), spec(a), spec(b), pl.BlockSpec(w.shape, lambda i: (0, 0))],
        out_specs=pl.BlockSpec((tm, d), row),
        out_shape=jax.ShapeDtypeStruct((m, d), F32),
        compiler_params=_cparams("parallel"),
        name="outproj",
    )(x, a, b, w)


def _sb_tile(q, k, v, t_incl, carry, acc, mask, keys_on_lanes=False):
    z = _dot(q, k) if keys_on_lanes else _dot_nt(q, k)
    lk = -_softplus(z)
    if mask is not None:
        lk = jnp.where(mask, lk, 0.0)
    incl = _split_dot(lk, t_incl)
    w = jnp.exp(z + incl + carry)
    if mask is not None:
        w = jnp.where(mask, w, 0.0)
    wb = w.astype(BF16)
    acc = acc + (_dot_nt(wb, v) if keys_on_lanes else _dot(wb, v))
    return carry + incl[:, 0:1], acc


def _tri_incl(n):
    r = lax.broadcasted_iota(jnp.int32, (n, n), 0)
    c = lax.broadcasted_iota(jnp.int32, (n, n), 1)
    return (r >= c).astype(BF16)


def _sb_multi(qs, ks, vs, t_incl, c_refs, a_refs, mask):
    n = len(qs)
    zs = [_dot_nt(qs[i], ks[i]) for i in range(n)]
    lks = [-_softplus(z) for z in zs]
    if mask is not None:
        lks = [jnp.where(mask, lk, 0.0) for lk in lks]
    incls = [_split_dot(lk, t_incl) for lk in lks]
    carries = [ref[...] for ref in c_refs]
    ws = [jnp.exp(zs[i] + incls[i] + carries[i]) for i in range(n)]
    if mask is not None:
        ws = [jnp.where(mask, w, 0.0) for w in ws]
    pvs = [_dot(ws[i].astype(BF16), vs[i]) for i in range(n)]
    alive = None
    for i in range(n):
        carry = carries[i] + incls[i][:, 0:1]
        c_refs[i][...] = carry
        a_refs[i][...] = a_refs[i][...] + pvs[i]
        top = jnp.max(carry)
        alive = top if alive is None else jnp.maximum(alive, top)
    return alive


def _sb_prompt_kernel(q_ref, k_ref, v_ref, o_ref, *state, tq, hb):
    i = pl.program_id(2)
    c_refs = state[:hb]
    a_refs = state[hb:]
    t_incl = _tri_incl(tq)
    r = lax.broadcasted_iota(jnp.int32, (tq, tq), 0)
    c = lax.broadcasted_iota(jnp.int32, (tq, tq), 1)
    for ref in state:
        ref[...] = jnp.zeros_like(ref)
    qs = [q_ref[h] for h in range(hb)]

    def tile(kt, mask):
        start = pl.multiple_of(kt * tq, tq)
        return _sb_multi(qs, [k_ref[h, pl.ds(start, tq), :] for h in range(hb)],
                         [v_ref[h, pl.ds(start, tq), :] for h in range(hb)], t_incl, c_refs, a_refs, mask)

    def cond(s):
        kt, alive = s
        return jnp.logical_and(kt >= 0, alive > -SB_CUTOFF)

    def body(s):
        kt, _ = s
        return kt - 1, tile(kt, None)

    lax.while_loop(cond, body, (i - 1, tile(i, c < r)))
    for h in range(hb):
        o_ref[h] = a_refs[h][...].astype(o_ref.dtype)


def _sb_prompt(qh, kh, vh, n_seq, s_len):
    tq = _tile(s_len, (256, 128))
    nq = s_len // tq
    hb = 4
    return pl.pallas_call(
        functools.partial(_sb_prompt_kernel, tq=tq, hb=hb),
        grid=(n_seq, N_HEADS // hb, nq),
        in_specs=[pl.BlockSpec((hb, tq, HEAD_DIM), lambda b, h, i: (h, b * nq + i, 0)),
                  pl.BlockSpec((hb, s_len, HEAD_DIM), lambda b, h, i: (h, b, 0)),
                  pl.BlockSpec((hb, s_len, HEAD_DIM), lambda b, h, i: (h, b, 0))],
        out_specs=pl.BlockSpec((hb, tq, HEAD_DIM), lambda b, h, i: (h, b * nq + i, 0)),
        out_shape=jax.ShapeDtypeStruct(qh.shape, BF16),
        scratch_shapes=[pltpu.VMEM((tq, 1), F32)] * hb + [pltpu.VMEM((tq, HEAD_DIM), F32)] * hb,
        compiler_params=_cparams("parallel", "parallel", "arbitrary"),
        name="sb_prompt",
    )(qh, kh, vh)


def _lru_kernel(xb_ref, gb_ref, h0_ref, buf0_ref, cw_ref, cb_ref, wr_ref, br_ref, wi_ref, bi_ref, lam_ref,
                g_ref, hl_ref, bn_ref, xpad, a_s, u_s, hs_s, h_s, *, tc, n_t):
    j = pl.program_id(1)
    keep = CONV_WIDTH - 1

    @pl.when(j == 0)
    def _():
        xpad[8 - keep:8, :] = buf0_ref[0]
        h_s[...] = h0_ref[0]

    @pl.when(j > 0)
    def _():
        xpad[8 - keep:8, :] = xpad[8 + tc - keep:8 + tc, :]

    x = xb_ref[0]
    xpad[8:8 + tc, :] = x
    xc = cb_ref[...] + x * cw_ref[keep:keep + 1, :]
    for d in range(1, CONV_WIDTH):
        xc = xc + xpad[8 - d:8 - d + tc, :] * cw_ref[keep - d:keep - d + 1, :]
    xcb = xc.astype(BF16)
    r = jax.nn.sigmoid(_dot(xcb, wr_ref[...]) + br_ref[...])
    gi = jax.nn.sigmoid(_dot(xcb, wi_ref[...]) + bi_ref[...])
    log_a = -LRU_C * r * _softplus(-lam_ref[...])
    a_s[...] = jnp.exp(log_a)
    u_s[...] = jnp.sqrt(_one_minus_exp(2.0 * log_a)) * (gi * xc)

    def step(t, h):
        h = a_s[pl.ds(t, 1), :] * h + u_s[pl.ds(t, 1), :]
        hs_s[pl.ds(t, 1), :] = h
        return h

    h = lax.fori_loop(0, tc, step, h_s[...], unroll=min(tc, 8))
    h_s[...] = h
    g_ref[0] = (jax.nn.gelu(gb_ref[0]) * hs_s[...]).astype(g_ref.dtype)

    @pl.when(j == n_t - 1)
    def _():
        hl_ref[0] = h
        bn_ref[0] = xpad[8 + tc - keep:8 + tc, :]


def _rg_lru(xb, gb, h0, buf0, conv_w, conv_b, wr_bd, b_r, wi_bd, b_i, lam):
    n, t, w = xb.shape
    keep = CONV_WIDTH - 1
    assert t >= keep
    tc = _tile(t, (512, 256, 128))
    n_t = t // tc
    seq = lambda b, j: (b, j, 0)
    one = lambda b, j: (b, 0, 0)
    const = lambda b, j: (0, 0)
    vec = pl.BlockSpec((1, w), const)
    g, hl, bn = pl.pallas_call(
        functools.partial(_lru_kernel, tc=tc, n_t=n_t),
        grid=(n, n_t),
        in_specs=[pl.BlockSpec((1, tc, w), seq), pl.BlockSpec((1, tc, w), seq),
                  pl.BlockSpec((1, 1, w), one), pl.BlockSpec((1, keep, w), one),
                  pl.BlockSpec((CONV_WIDTH, w), const), vec,
                  pl.BlockSpec((w, w), const), vec, pl.BlockSpec((w, w), const), vec, vec],
        out_specs=[pl.BlockSpec((1, tc, w), seq), pl.BlockSpec((1, 1, w), one), pl.BlockSpec((1, keep, w), one)],
        out_shape=[jax.ShapeDtypeStruct((n, t, w), BF16), jax.ShapeDtypeStruct((n, 1, w), F32),
                   jax.ShapeDtypeStruct((n, keep, w), F32)],
        scratch_shapes=[pltpu.VMEM((tc + 8, w), F32), pltpu.VMEM((tc, w), F32), pltpu.VMEM((tc, w), F32),
                        pltpu.VMEM((tc, w), F32), pltpu.VMEM((1, w), F32)],
        compiler_params=_cparams("parallel", "arbitrary"),
        name="rg_lru",
    )(xb, gb, h0.reshape(n, 1, w), buf0, conv_w, conv_b.reshape(1, w), wr_bd, b_r.reshape(1, w), wi_bd,
      b_i.reshape(1, w), lam.reshape(1, w))
    return g, hl.reshape(n, w), bn


def _cmp_ab_kernel(*refs, n_in, rows, n_prefetch, pages_t):
    refs = refs[n_prefetch:]
    k_refs = refs[:n_in]
    v_refs = refs[n_in:2 * n_in]
    pe_ak, pe_bk, pe_av, pe_bv, w_ak, w_bk, w_av, w_bv, o_ref = refs[2 * n_in:2 * n_in + 9]
    nch = rows // CMP_STRIDE
    if pages_t:
        xk_s, xv_s = refs[2 * n_in + 9:]
        for s in range(n_in):
            xk_s[s * rows:(s + 1) * rows, :] = k_refs[s][...].T
            xv_s[s * rows:(s + 1) * rows, :] = v_refs[s][...].T
        k_refs, v_refs, nch = [xk_s], [xv_s], n_in * nch

    def chunks(in_refs):
        per = [jnp.concatenate([r[pl.ds(p, nch, stride=CMP_STRIDE), :] for p in range(CMP_STRIDE)], axis=1)
               for r in in_refs]
        return per[0] if len(in_refs) == 1 else jnp.concatenate(per, axis=0)

    xk = chunks(k_refs)
    xv = chunks(v_refs)
    o_ref[0, :, 0:128] = _dot((xk + pe_ak[...]).astype(BF16), w_ak[...])
    o_ref[0, :, 128:256] = _dot((xk + pe_bk[...]).astype(BF16), w_bk[...])
    o_ref[0, :, 256:384] = _dot((xv + pe_av[...]).astype(BF16), w_av[...])
    o_ref[0, :, 384:512] = _dot((xv + pe_bv[...]).astype(BF16), w_bv[...])


def _cmp_weights(pe, w1):
    hid = w1.shape[-1]
    eye = jnp.eye(N_KV_C, dtype=F32)
    out = []
    for half in range(CMP_LEN // CMP_STRIDE):
        sl = slice(half * CMP_STRIDE, (half + 1) * CMP_STRIDE)
        pe_row = jnp.broadcast_to(pe[sl][:, None, :], (CMP_STRIDE, N_KV_C, HEAD_DIM)).reshape(1, -1)
        wbd = jnp.einsum('pdh,gk->pgdkh', w1[sl], eye).reshape(CMP_STRIDE * N_KV_C * HEAD_DIM, N_KV_C * hid)
        out.append((pe_row, wbd.astype(BF16)))
    return out


def _cmp_ab(in_arrays, k_specs, v_specs, grid, out_rows, n_seq, n_chunks, rows, cmp_k, cmp_v, num_prefetch,
            prefetch, pages_t=False):
    in_specs = list(k_specs) + list(v_specs)
    (pe_ak, w_ak), (pe_bk, w_bk) = cmp_k
    (pe_av, w_av), (pe_bv, w_bv) = cmp_v
    nd = len(grid)
    const = lambda *a: (0, 0)
    pes = [pe_ak, pe_bk, pe_av, pe_bv]
    ws = [w_ak, w_bk, w_av, w_bv]
    specs = list(in_specs) + [pl.BlockSpec(p.shape, const) for p in pes] + [pl.BlockSpec(w.shape, const) for w in ws]
    gs = pltpu.PrefetchScalarGridSpec(
        num_scalar_prefetch=num_prefetch, grid=grid, in_specs=specs,
        out_specs=pl.BlockSpec((1, out_rows, 512), lambda b, j, *a: (b, j, 0)),
        scratch_shapes=[pltpu.VMEM((len(in_arrays) * rows, LANES), F32)] * 2 if pages_t else [])
    return pl.pallas_call(
        functools.partial(_cmp_ab_kernel, n_in=len(in_arrays), rows=rows, n_prefetch=num_prefetch, pages_t=pages_t),
        grid_spec=gs,
        out_shape=jax.ShapeDtypeStruct((n_seq, n_chunks, 512), F32),
        compiler_params=_cparams(*(["parallel"] + ["arbitrary"] * (nd - 1))),
        name="cmp_ab",
    )(*prefetch, *in_arrays, *in_arrays, *pes, *ws)


def _cmp_finish_kernel(ab_ref, w2k_ref, w2v_ref, kf_ref, vf_ref, kg_ref, vg_ref, *, nc):
    n = nc - 1
    hk = jax.nn.gelu(ab_ref[0, 0:n, 0:128] + ab_ref[0, 1:nc, 128:256]).astype(BF16)
    hv = jax.nn.gelu(ab_ref[0, 0:n, 256:384] + ab_ref[0, 1:nc, 384:512]).astype(BF16)
    kc = _dot(hk, w2k_ref[...]).astype(BF16)
    vc = _dot(hv, w2v_ref[...]).astype(BF16)
    zero = jnp.zeros((1, LANES), BF16)
    for ref_f, ref_g, val in ((kf_ref, kg_ref, kc), (vf_ref, vg_ref, vc)):
        full = jnp.concatenate([val, zero], axis=0)
        ref_f[0] = full
        for g in range(N_KV_C):
            ref_g[0, g] = full[:, g * HEAD_DIM:(g + 1) * HEAD_DIM]


def _cmp_finish(ab, w2k_bd, w2v_bd):
    n, nc, _ = ab.shape
    flat = jax.ShapeDtypeStruct((n, nc, LANES), BF16)
    grp = jax.ShapeDtypeStruct((n, N_KV_C, nc, HEAD_DIM), BF16)
    fspec = pl.BlockSpec((1, nc, LANES), lambda b: (b, 0, 0))
    gspec = pl.BlockSpec((1, N_KV_C, nc, HEAD_DIM), lambda b: (b, 0, 0, 0))
    return pl.pallas_call(
        functools.partial(_cmp_finish_kernel, nc=nc),
        grid=(n,),
        in_specs=[pl.BlockSpec((1, nc, 512), lambda b: (b, 0, 0)),
                  pl.BlockSpec((LANES, LANES), lambda b: (0, 0)), pl.BlockSpec((LANES, LANES), lambda b: (0, 0))],
        out_specs=[fspec, fspec, gspec, gspec], out_shape=[flat, flat, grp, grp],
        compiler_params=_cparams("parallel"),
        name="cmp_finish",
    )(ab, w2k_bd, w2v_bd)


def _overlap_t(nb, nc):
    b = lax.broadcasted_iota(jnp.int32, (nb, nc), 0)
    c = lax.broadcasted_iota(jnp.int32, (nb, nc), 1)
    return ((c * CMP_STRIDE < (b + 1) * SLC_BLOCK) & (c * CMP_STRIDE + CMP_LEN > b * SLC_BLOCK)).astype(F32)


def _nsa_prompt_kernel(qraw_ref, qrot_ref, gt_ref, kcc_ref, vcc_ref, ks_ref, vs_ref, kw_ref, vw_ref, o_ref,
                       *state, tq, nc, nsb, ck):
    i = pl.program_id(1)
    q0 = i * tq
    rows = HPG_C * tq
    nbp = LANES
    qpos3 = q0 + lax.broadcasted_iota(jnp.int32, (1, tq, 1), 1)

    def compressed_and_selection(g):
        qraw = qraw_ref[g * HPG_C:(g + 1) * HPG_C].reshape(rows, HEAD_DIM)
        sc = _dot_nt(qraw, kcc_ref[0, g]).reshape(HPG_C, tq, nc)
        cidx = lax.broadcasted_iota(jnp.int32, (1, 1, nc), 2)
        mask_c = (cidx * CMP_STRIDE + CMP_LEN - 1 <= qpos3) & (cidx < nc - 1)
        pc = _masked_softmax(sc, mask_c)
        o_c = _dot(pc.reshape(rows, nc).astype(BF16), vcc_ref[0, g])

        psum = jnp.sum(pc, axis=0)
        imp_t = _dot_nt(_overlap_t(nsb, nc), psum, precision=HIGHEST)
        blk = lax.broadcasted_iota(jnp.int32, (nsb, tq), 0)
        own = (q0 + lax.broadcasted_iota(jnp.int32, (nsb, tq), 1)) // SLC_BLOCK
        imp_t = jnp.where(blk == own, BIG, jnp.where(blk < own, imp_t, NEG))
        cnt = jnp.zeros((nsb, tq), jnp.int32)
        for b2 in range(nsb):
            rowv = imp_t[b2:b2 + 1, :]
            ahead = (rowv > imp_t) | ((rowv == imp_t) & (b2 < blk))
            cnt = cnt + ahead.astype(jnp.int32)
        sel_t = ((cnt < N_SLC) & (blk <= own)).astype(F32)
        if nsb < nbp:
            sel_t = jnp.concatenate([sel_t, jnp.zeros((nbp - nsb, tq), F32)], axis=0)
        return o_c, jnp.where(sel_t.T > 0.5, 0.0, NEG).astype(BF16)

    per_group = [compressed_and_selection(g) for g in range(N_KV_C)]
    nh = N_HEADS
    m_refs = state[:2 * nh]
    acc_refs = state[2 * nh:]
    qs = [qrot_ref[h] for h in range(nh)]
    qs_sel = [jnp.concatenate([qs[h], per_group[h // HPG_C][1][:, 0:HEAD_DIM]], axis=1) for h in range(nh)]
    grp = [h // HPG_C for h in range(nh)]
    for ref in m_refs:
        ref[...] = jnp.full(ref.shape, NEG, F32)
    for ref in acc_refs:
        ref[...] = jnp.zeros_like(ref)

    c_own = q0 // ck
    start = pl.multiple_of(c_own * ck, ck)
    row = lax.broadcasted_iota(jnp.int32, (tq, ck), 0)
    lane = lax.broadcasted_iota(jnp.int32, (tq, ck), 1)
    bias_s = jnp.where(start + lane <= q0 + row, 0.0, NEG)
    wk = WINDOW + tq
    start_w = pl.multiple_of(jnp.maximum(q0 - WINDOW, 0), tq)
    dist = (q0 + lax.broadcasted_iota(jnp.int32, (tq, wk), 0)) - (start_w + lax.broadcasted_iota(jnp.int32, (tq, wk), 1))
    bias_w = jnp.where((dist >= 0) & (dist <= WINDOW), 0.0, NEG)
    ks_own = [ks_ref[g, pl.ds(start, ck), :] for g in range(N_KV_C)]
    vs_own = [vs_ref[g, pl.ds(start, ck), :] for g in range(N_KV_C)]
    kw_all = [kw_ref[g, pl.ds(start_w, wk), :] for g in range(N_KV_C)]
    vw_all = [vw_ref[g, pl.ds(start_w, wk), :] for g in range(N_KV_C)]
    _flash_multi(qs_sel + qs,
                 [ks_own[g] for g in grp] + [kw_all[g] for g in grp],
                 [vs_own[g] for g in grp] + [vw_all[g] for g in grp],
                 [bias_s] * nh + [bias_w] * nh, m_refs, acc_refs)

    def slc_step(cc, carry):
        st = pl.multiple_of(cc * ck, ck)
        ks_c = [ks_ref[g, pl.ds(st, ck), :] for g in range(N_KV_C)]
        vs_c = [vs_ref[g, pl.ds(st, ck), :] for g in range(N_KV_C)]
        _flash_multi(qs_sel, [ks_c[g] for g in grp], [vs_c[g] for g in grp], [None] * nh, m_refs[:nh], acc_refs[:nh])
        return carry

    lax.fori_loop(0, c_own, slc_step, 0)

    for h in range(nh):
        g, j = h // HPG_C, h % HPG_C
        gt = gt_ref[:, g * LANES:(g + 1) * LANES]
        g_c = gt[:, j:j + 1]
        g_s = gt[:, HPG_C + j:HPG_C + j + 1]
        g_w = gt[:, 2 * HPG_C + j:2 * HPG_C + j + 1]
        o_c = per_group[g][0][j * tq:(j + 1) * tq]
        o_ref[h] = (o_c * g_c + _flash_result(acc_refs[h][...]) * g_s
                    + _flash_result(acc_refs[nh + h][...]) * g_w).astype(o_ref.dtype)


def _nsa_prompt(qraw_h, qrot_h, gates, kcc_g, vcc_g, ks_h, vs_h, kw_h, vw_h, n_seq, s_len):
    tq = 128
    nq = s_len // tq
    nc = kcc_g.shape[2]
    nsb = -(-s_len // SLC_BLOCK)
    ck = _tile(s_len, (512, 256, 128))
    assert s_len % tq == 0 and nsb <= HEAD_DIM and WINDOW % tq == 0 and s_len >= WINDOW + tq and ck % tq == 0
    qspec = pl.BlockSpec((N_HEADS, tq, HEAD_DIM), lambda b, i: (0, b * nq + i, 0))
    cspec = pl.BlockSpec((1, N_KV_C, nc, HEAD_DIM), lambda b, i: (b, 0, 0, 0))
    kspec = pl.BlockSpec((N_KV_C, s_len, HEAD_DIM), lambda b, i: (0, b, 0))
    vspec = pl.BlockSpec((N_KV_C, s_len, LANES), lambda b, i: (0, b, 0))
    return pl.pallas_call(
        functools.partial(_nsa_prompt_kernel, tq=tq, nc=nc, nsb=nsb, ck=ck),
        grid=(n_seq, nq),
        in_specs=[qspec, qspec, pl.BlockSpec((tq, N_KV_C * LANES), lambda b, i: (b * nq + i, 0)),
                  cspec, cspec, vspec, vspec, kspec, vspec],
        out_specs=qspec,
        out_shape=jax.ShapeDtypeStruct(qraw_h.shape, BF16),
        scratch_shapes=[pltpu.VMEM((tq, 1), F32)] * (2 * N_HEADS) + [pltpu.VMEM((tq, LANES), F32)] * (2 * N_HEADS),
        compiler_params=_cparams("parallel", "arbitrary"),
        name="nsa_prompt",
    )(qraw_h, qrot_h, gates, kcc_g, vcc_g, ks_h, vs_h, kw_h, vw_h)


def _block_mean_kernel(x_ref, o_ref, *, nb):
    x = x_ref[...]
    m = jnp.mean(x.reshape(nb, MOBA_BLOCK, x.shape[1]), axis=1)
    for h in range(N_HEADS):
        o_ref[h] = m[:, h * HEAD_DIM:(h + 1) * HEAD_DIM]


def _block_mean(d_rows):
    m = d_rows.shape[0]
    nb_total = m // MOBA_BLOCK
    nb = _tile(nb_total, (8,))
    return pl.pallas_call(
        functools.partial(_block_mean_kernel, nb=nb),
        grid=(nb_total // nb,),
        in_specs=[pl.BlockSpec((nb * MOBA_BLOCK, HALF_MIX), lambda i: (i, 0))],
        out_specs=pl.BlockSpec((N_HEADS, nb, HEAD_DIM), lambda i: (0, i, 0)),
        out_shape=jax.ShapeDtypeStruct((N_HEADS, nb_total, HEAD_DIM), F32),
        compiler_params=_cparams("parallel"),
        name="moba_block_mean",
    )(d_rows)


def _moba_prompt_kernel(q_ref, km_ref, k_ref, v_ref, o_ref, *state, tq, nb, sub, hb):
    i = pl.program_id(2)
    nh = tq // sub
    nchain = hb * nh
    m_refs = state[:nchain]
    acc_refs = state[nchain:]
    nbp = LANES

    blk_t = lax.broadcasted_iota(jnp.int32, (nb, tq), 0)
    sel_bias = []
    for hh in range(hb):
        gate_t = _dot_nt(km_ref[hh], q_ref[hh].astype(F32) * (1.0 / SCALE), precision=HIGHEST)
        gate_t = jnp.where(blk_t < i, gate_t, NEG)
        cnt = jnp.zeros((nb, tq), jnp.int32)
        for b2 in range(nb):
            rowv = gate_t[b2:b2 + 1, :]
            cnt = cnt + ((rowv > gate_t) | ((rowv == gate_t) & (b2 < blk_t))).astype(jnp.int32)
        sel_t = ((cnt < MOBA_TOPK) & (blk_t < i)).astype(F32)
        sel_t = jnp.concatenate([sel_t, jnp.zeros((nbp - nb, tq), F32)], axis=0)
        sel_bias.append(jnp.where(sel_t.T > 0.5, 0.0, NEG).astype(BF16))

    rows_of = lambda ci: slice((ci % nh) * sub, (ci % nh + 1) * sub)
    q_plain = [q_ref[ci // nh, rows_of(ci), :] for ci in range(nchain)]
    qs_sel = [jnp.concatenate([q_plain[ci], sel_bias[ci // nh][rows_of(ci), 0:HEAD_DIM]], axis=1)
              for ci in range(nchain)]
    qs_own = [jnp.concatenate([q, jnp.zeros((sub, HEAD_DIM), BF16)], axis=1) for q in q_plain]

    def update(qs, start, width, biases):
        ks = [k_ref[hh, pl.ds(start, width), :] for hh in range(hb)]
        v1s = [v_ref[hh, pl.ds(start, width), :] for hh in range(hb)]
        _flash_multi(qs, [ks[ci // nh] for ci in range(nchain)], [v1s[ci // nh] for ci in range(nchain)], biases,
                     m_refs, acc_refs)

    for ref in m_refs:
        ref[...] = jnp.full(ref.shape, NEG, F32)
    for ref in acc_refs:
        ref[...] = jnp.zeros_like(ref)

    r = lax.broadcasted_iota(jnp.int32, (sub, tq), 0)
    c = lax.broadcasted_iota(jnp.int32, (sub, tq), 1)
    update(qs_own, pl.multiple_of(i * tq, tq), tq,
           [jnp.where(c <= r + (ci % nh) * sub, 0.0, NEG) for ci in range(nchain)])

    pk = 2 * tq

    def pair_step(cc, carry):
        update(qs_sel, pl.multiple_of(cc * pk, pk), pk, [None] * nchain)
        return carry

    lax.fori_loop(0, i // 2, pair_step, 0)

    @pl.when(i % 2 == 1)
    def _():
        update(qs_sel, pl.multiple_of((i - 1) * tq, tq), tq, [None] * nchain)

    for ci in range(nchain):
        o_ref[ci // nh, (ci % nh) * sub:(ci % nh + 1) * sub, :] = _flash_result(acc_refs[ci][...]).astype(o_ref.dtype)


def _moba_prompt(qd_h, kmean_h, kd_h, vd_h, n_seq, s_len):
    tq = MOBA_BLOCK
    assert s_len % tq == 0
    nq = s_len // tq
    assert nq <= HEAD_DIM
    sub = 128
    hb = 4
    nchain = hb * (tq // sub)
    vspec = pl.BlockSpec((hb, s_len, LANES), lambda b, h, i: (h, b, 0))
    kspec = vspec
    qspec = pl.BlockSpec((hb, tq, HEAD_DIM), lambda b, h, i: (h, b * nq + i, 0))
    return pl.pallas_call(
        functools.partial(_moba_prompt_kernel, tq=tq, nb=nq, sub=sub, hb=hb),
        grid=(n_seq, N_HEADS // hb, nq),
        in_specs=[qspec, pl.BlockSpec((hb, nq, HEAD_DIM), lambda b, h, i: (h, b, 0)), kspec, vspec],
        out_specs=qspec,
        out_shape=jax.ShapeDtypeStruct(qd_h.shape, BF16),
        scratch_shapes=[pltpu.VMEM((sub, 1), F32)] * nchain + [pltpu.VMEM((sub, LANES), F32)] * nchain,
        compiler_params=_cparams("parallel", "parallel", "arbitrary"),
        name="moba_prompt",
    )(qd_h, kmean_h, kd_h, vd_h)


def _page_specs(n_slots, block, tail, base):
    def make(s):
        return lambda b, j, pt: (base + pt[b, j * n_slots + s],) + tuple(tail)

    return [pl.BlockSpec((None,) + tuple(block), make(s)) for s in range(n_slots)]


def _pages_t(cache):
    l, p, r, a, b, d = cache.shape
    return jnp.transpose(cache, (0, 1, 3, 4, 5, 2)).reshape(l * p, a, b * d, r)


def _head_diag(rows, width, per):
    r = lax.broadcasted_iota(jnp.int32, (rows, width), 0)
    c = lax.broadcasted_iota(jnp.int32, (rows, width), 1)
    return ((r % N_HEADS) * HEAD_DIM // per) == (c // HEAD_DIM * HEAD_DIM // per)


def _fold_heads(acc, t):
    kept = jnp.where(_head_diag(acc.shape[0], acc.shape[1], HEAD_DIM), acc, 0.0)
    return jnp.sum(kept.reshape(t, N_HEADS, acc.shape[1]), axis=1)


def _sb_sample_kernel(pt_ref, q_ref, kn_ref, vn_ref, pool_ref, o_ref, buf, sem, *, n_pages, t, pad, base):
    b = pl.program_id(0)
    rows = t * N_HEADS
    q = q_ref[0]

    def page_copy(p, slot):
        return pltpu.make_async_copy(pool_ref.at[base + pt_ref[b, p]], buf.at[slot], sem.at[slot])

    page_copy(n_pages - 1, 0).start()

    r = lax.broadcasted_iota(jnp.int32, (rows, pad), 0) // N_HEADS
    c = lax.broadcasted_iota(jnp.int32, (rows, pad), 1)
    carry, acc = _sb_tile(q, kn_ref[0], vn_ref[0], _tri_incl(pad), jnp.zeros((rows, 1), F32),
                          jnp.zeros((rows, HALF_MIX), F32), (c < r) & (c < t))
    t_incl = _tri_incl(PAGE_SIZE)

    def cond(s):
        p, carry, _ = s
        return jnp.logical_and(p >= 0, jnp.max(carry) > -SB_CUTOFF)

    def body(s):
        p, carry, acc = s
        slot = (n_pages - 1 - p) % 2
        page_copy(p, slot).wait()

        @pl.when(p > 0)
        def _():
            page_copy(p - 1, 1 - slot).start()

        carry, acc = _sb_tile(q, buf[slot, 0].astype(BF16), buf[slot, 1].astype(BF16), t_incl, carry, acc, None,
                              keys_on_lanes=True)
        return p - 1, carry, acc

    p_end, _, acc = lax.while_loop(cond, body, (jnp.int32(n_pages - 1), carry, acc))

    @pl.when(p_end >= 0)
    def _():
        page_copy(p_end, (n_pages - 1 - p_end) % 2).wait()

    o_ref[0] = _fold_heads(acc, t)


def _sb_sample(qbd, k_new, v_new, pool_t, base, page_table):
    n, rows, _ = qbd.shape
    t = rows // N_HEADS
    pad = k_new.shape[1]
    n_pages = page_table.shape[1]
    seq = lambda b, pt: (b, 0, 0)
    gs = pltpu.PrefetchScalarGridSpec(
        num_scalar_prefetch=1, grid=(n,),
        in_specs=[pl.BlockSpec((1, rows, HALF_MIX), seq), pl.BlockSpec((1, pad, HALF_MIX), seq),
                  pl.BlockSpec((1, pad, HALF_MIX), seq), pl.BlockSpec(memory_space=pl.ANY)],
        out_specs=pl.BlockSpec((1, t, HALF_MIX), seq),
        scratch_shapes=[pltpu.VMEM((2,) + pool_t.shape[1:], pool_t.dtype), pltpu.SemaphoreType.DMA((2,))])
    return pl.pallas_call(
        functools.partial(_sb_sample_kernel, n_pages=n_pages, t=t, pad=pad, base=base),
        grid_spec=gs,
        out_shape=jax.ShapeDtypeStruct((n, t, HALF_MIX), F32),
        compiler_params=_cparams("arbitrary"),
        name="sb_sample",
    )(page_table, qbd, k_new, v_new, pool_t)


def _nsa_sample_head_kernel(qraw_ref, qrot_ref, kcc_ref, vcc_ref, win_ref, wn_ref, oc_ref, ow_ref, sel_ref,
                            *, t, nc, n_past, nbp, pad):
    rows = t * N_HEADS
    wb = win_ref.shape[1]
    tok = lax.broadcasted_iota(jnp.int32, (rows, 1), 0) // N_HEADS
    qpos = n_past + tok

    def fold(x):
        g0 = (lax.broadcasted_iota(jnp.int32, (rows, HEAD_DIM), 0) % N_HEADS) < HPG_C
        return jnp.where(g0, x[:, 0:HEAD_DIM], x[:, HEAD_DIM:2 * HEAD_DIM])

    sc = _dot_nt(qraw_ref[0], kcc_ref[0])
    cidx = lax.broadcasted_iota(jnp.int32, (1, nc), 1)
    pc = _masked_softmax(sc, (cidx * CMP_STRIDE + CMP_LEN - 1 <= qpos) & (cidx < nc - 1))
    oc_ref[0] = fold(_dot(pc.astype(BF16), vcc_ref[0]))

    ng = t * N_KV_C
    gr = lax.broadcasted_iota(jnp.int32, (ng, rows), 0)
    rr = lax.broadcasted_iota(jnp.int32, (ng, rows), 1)
    same = (gr // N_KV_C == rr // N_HEADS) & (gr % N_KV_C == (rr % N_HEADS) // HPG_C)
    psum = _dot(same.astype(F32), pc, precision=HIGHEST)
    imp = _dot_nt(psum, _overlap_t(nbp, nc), precision=HIGHEST)
    blk = lax.broadcasted_iota(jnp.int32, (ng, nbp), 1)
    own = (n_past + lax.broadcasted_iota(jnp.int32, (ng, nbp), 0) // N_KV_C) // SLC_BLOCK
    imp = jnp.where(blk == own, BIG, jnp.where(blk < own, imp, NEG))
    sel_ref[0] = _top_select(imp, blk, N_SLC)

    qrot = qrot_ref[0]
    win = win_ref[0]
    wn = wn_ref[0]
    s1 = _dot_nt(qrot, win[:, 0:LANES].astype(BF16))
    wpos1 = n_past - wb + lax.broadcasted_iota(jnp.int32, (1, wb), 1)
    d1 = qpos - wpos1
    mk1 = (d1 >= 0) & (d1 <= WINDOW) & (wpos1 >= 0)
    s2 = _dot_nt(qrot, wn[:, 0:LANES])
    j2 = lax.broadcasted_iota(jnp.int32, (1, pad), 1)
    mk2 = (j2 <= tok) & (j2 < t)
    m = jnp.maximum(jnp.max(jnp.where(mk1, s1, NEG), axis=-1, keepdims=True),
                    jnp.max(jnp.where(mk2, s2, NEG), axis=-1, keepdims=True))
    p1 = jnp.where(mk1, jnp.exp(jnp.where(mk1, s1, NEG) - m), 0.0)
    p2 = jnp.where(mk2, jnp.exp(jnp.where(mk2, s2, NEG) - m), 0.0)
    den = jnp.sum(p1, axis=-1, keepdims=True) + jnp.sum(p2, axis=-1, keepdims=True)
    ow = _dot(p1.astype(BF16), win[:, LANES:2 * LANES].astype(BF16)) + _dot(p2.astype(BF16), wn[:, LANES:2 * LANES])
    ow_ref[0] = fold(ow / den)


def _nsa_sample_head(qraw_bd, qrot_bd, kcc_f, vcc_f, win, w_new, n_past, nbp):
    n, rows, _ = qraw_bd.shape
    t = rows // N_HEADS
    nc = kcc_f.shape[1]
    wb = win.shape[1]
    pad = w_new.shape[1]
    seq = lambda b: (b, 0, 0)
    return pl.pallas_call(
        functools.partial(_nsa_sample_head_kernel, t=t, nc=nc, n_past=n_past, nbp=nbp, pad=pad),
        grid=(n,),
        in_specs=[pl.BlockSpec((1, rows, LANES), seq), pl.BlockSpec((1, rows, LANES), seq),
                  pl.BlockSpec((1, nc, LANES), seq), pl.BlockSpec((1, nc, LANES), seq),
                  pl.BlockSpec((1, wb, 2 * LANES), seq), pl.BlockSpec((1, pad, 2 * LANES), seq)],
        out_specs=[pl.BlockSpec((1, rows, HEAD_DIM), seq), pl.BlockSpec((1, rows, HEAD_DIM), seq),
                   pl.BlockSpec((1, t * N_KV_C, nbp), seq)],
        out_shape=[jax.ShapeDtypeStruct((n, rows, HEAD_DIM), F32), jax.ShapeDtypeStruct((n, rows, HEAD_DIM), F32),
                   jax.ShapeDtypeStruct((n, t * N_KV_C, nbp), F32)],
        compiler_params=_cparams("parallel"),
        name="nsa_sample_head",
    )(qraw_bd, qrot_bd, kcc_f, vcc_f, win, w_new)


def _nsa_sample_slc_kernel(pt_ref, q_ref, sel_ref, sn_ref, oc_ref, ow_ref, gt_ref, *refs,
                           n_slots, n_steps, t, nbp, pad):
    page_refs = refs[:n_slots]
    o_ref, m_s, l_s, acc_s = refs[n_slots:]
    j = pl.program_id(1)
    rows = t * N_HEADS
    q = q_ref[0]
    tok = lax.broadcasted_iota(jnp.int32, (rows, 1), 0) // N_HEADS

    @pl.when(j == 0)
    def _():
        sn = sn_ref[0]
        j2 = lax.broadcasted_iota(jnp.int32, (1, pad), 1)
        m, l, acc = _online_step(_dot_nt(q, sn[:, 0:LANES]), (j2 <= tok) & (j2 < t), sn[:, LANES:2 * LANES],
                                 jnp.full((rows, 1), NEG, F32), jnp.zeros((rows, 1), F32),
                                 jnp.zeros((rows, LANES), F32))
        m_s[...] = m
        l_s[...] = l
        acc_s[...] = acc

    sel_bias = sel_ref[0]
    eb = lax.broadcasted_iota(jnp.int32, (nbp, PAGE_SIZE), 0)
    ej = lax.broadcasted_iota(jnp.int32, (nbp, PAGE_SIZE), 1)
    group = 8
    for s0 in range(0, n_slots, group):
        scores, vals = [], []
        for s in range(s0, min(s0 + group, n_slots)):
            first_blk = (j * n_slots + s) * (PAGE_SIZE // SLC_BLOCK)
            bias = _dot(sel_bias, (first_blk + ej // SLC_BLOCK == eb).astype(BF16))
            scores.append(_dot(q, page_refs[s][0].astype(BF16)) + bias)
            vals.append(page_refs[s][1].astype(BF16))
        _joint_update(m_s, l_s, acc_s, scores, vals, keys_on_lanes=True)

    @pl.when(j == n_steps - 1)
    def _():
        o = acc_s[...] / l_s[...]
        g0 = (lax.broadcasted_iota(jnp.int32, (rows, HEAD_DIM), 0) % N_HEADS) < HPG_C
        o_s = jnp.where(g0, o[:, 0:HEAD_DIM], o[:, HEAD_DIM:2 * HEAD_DIM])
        gt = gt_ref[0]
        o_ref[0] = oc_ref[0] * gt[:, 0:1] + o_s * gt[:, 1:2] + ow_ref[0] * gt[:, 2:3]


def _nsa_sample_slc(qrot_bd, sel_rows, s_new, o_c, o_w, gates, pool, base, page_table):
    n, rows, _ = qrot_bd.shape
    t = rows // N_HEADS
    nbp = sel_rows.shape[2]
    pad = s_new.shape[1]
    n_pages = page_table.shape[1]
    n_slots = _tile(n_pages, (16, 8, 4, 2))
    n_steps = n_pages // n_slots
    seq = lambda b, j, pt: (b, 0, 0)
    gs = pltpu.PrefetchScalarGridSpec(
        num_scalar_prefetch=1, grid=(n, n_steps),
        in_specs=[pl.BlockSpec((1, rows, LANES), seq), pl.BlockSpec((1, rows, nbp), seq),
                  pl.BlockSpec((1, pad, 2 * LANES), seq), pl.BlockSpec((1, rows, HEAD_DIM), seq),
                  pl.BlockSpec((1, rows, HEAD_DIM), seq), pl.BlockSpec((1, rows, 3), seq)]
        + _page_specs(n_slots, (2, LANES, PAGE_SIZE), (1, 0, 0), base),
        out_specs=pl.BlockSpec((1, rows, HEAD_DIM), seq),
        scratch_shapes=[pltpu.VMEM((rows, 1), F32), pltpu.VMEM((rows, 1), F32), pltpu.VMEM((rows, LANES), F32)])
    return pl.pallas_call(
        functools.partial(_nsa_sample_slc_kernel, n_slots=n_slots, n_steps=n_steps, t=t, nbp=nbp, pad=pad),
        grid_spec=gs,
        out_shape=jax.ShapeDtypeStruct((n, rows, HEAD_DIM), F32),
        compiler_params=_cparams("parallel", "arbitrary"),
        name="nsa_sample_slc",
    )(page_table, qrot_bd, sel_rows, s_new, o_c, o_w, gates, *([pool] * n_slots))


def _page_sum_kernel(pt_ref, *refs, n_slots):
    o_ref = refs[n_slots]
    j = pl.program_id(1)
    ppb = MOBA_BLOCK // PAGE_SIZE

    @pl.when(j == 0)
    def _():
        o_ref[...] = jnp.zeros_like(o_ref)

    lane = lax.broadcasted_iota(jnp.int32, o_ref.shape[1:], 1)
    acc = o_ref[0]
    for s in range(n_slots):
        tot = jnp.sum(refs[s][...], axis=1, keepdims=True)
        acc = acc + jnp.where(lane == (j * n_slots + s) // ppb, tot, 0.0)
    o_ref[0] = acc


def _page_sums(pool_t, base, page_table, nbp):
    n, n_pages = page_table.shape
    n_slots = _tile(n_pages, (16, 8, 4, 2))
    gs = pltpu.PrefetchScalarGridSpec(
        num_scalar_prefetch=1, grid=(n, n_pages // n_slots),
        in_specs=_page_specs(n_slots, (None, HALF_MIX, PAGE_SIZE), (0, 0, 0), base),
        out_specs=pl.BlockSpec((1, HALF_MIX, nbp), lambda b, j, pt: (b, 0, 0)))
    return pl.pallas_call(
        functools.partial(_page_sum_kernel, n_slots=n_slots),
        grid_spec=gs,
        out_shape=jax.ShapeDtypeStruct((n, HALF_MIX, nbp), F32),
        compiler_params=_cparams("parallel", "arbitrary"),
        name="moba_page_sums",
    )(page_table, *([pool_t] * n_slots))


def _moba_sample_kernel(pt_ref, q_ref, qf_ref, ks_ref, kn_ref, vn_ref, *refs, n_slots, n_steps, t, nbp, pad,
                        n_past):
    page_refs = refs[:n_slots]
    o_ref, sel_s, m_s, l_s, acc_s = refs[n_slots:]
    j = pl.program_id(1)
    rows = t * N_HEADS
    ppb = MOBA_BLOCK // PAGE_SIZE
    nb_full = n_past // MOBA_BLOCK
    q = q_ref[0]
    tok = lax.broadcasted_iota(jnp.int32, (rows, 1), 0) // N_HEADS

    @pl.when(j == 0)
    def _():
        gate = _dot(qf_ref[0], ks_ref[0] * (1.0 / MOBA_BLOCK), precision=HIGHEST)
        blk = lax.broadcasted_iota(jnp.int32, (rows, nbp), 1)
        own = (n_past + tok) // MOBA_BLOCK
        gate = jnp.where((blk < own) & (blk < nb_full), gate, NEG)
        sel_s[...] = jnp.where(_top_select(gate, blk, MOBA_TOPK) > 0.5, 0.0, NEG).astype(BF16)
        j2 = lax.broadcasted_iota(jnp.int32, (1, pad), 1)
        m, l, acc = _online_step(_dot_nt(q, kn_ref[0]), (j2 <= tok) & (j2 < t), vn_ref[0],
                                 jnp.full((rows, 1), NEG, F32), jnp.zeros((rows, 1), F32),
                                 jnp.zeros((rows, HALF_MIX), F32))
        m_s[...] = m
        l_s[...] = l
        acc_s[...] = acc

    sel_bias = sel_s[...]
    eb = lax.broadcasted_iota(jnp.int32, (nbp, PAGE_SIZE), 0)
    scores, vals = [], []
    for s in range(n_slots):
        blk_of_page = (j * n_slots + s) // ppb
        bias = _dot(sel_bias, (eb == blk_of_page).astype(BF16))
        scores.append(_dot(q, page_refs[s][0].astype(BF16)) + bias)
        vals.append(page_refs[s][1].astype(BF16))
    _joint_update(m_s, l_s, acc_s, scores, vals, keys_on_lanes=True)

    @pl.when(j == n_steps - 1)
    def _():
        o_ref[0] = _fold_heads(acc_s[...] / l_s[...], t)


def _moba_sample(qbd, qbd_f32, ksums, k_new, v_new, pool, base, page_table, n_past):
    n, rows, _ = qbd.shape
    t = rows // N_HEADS
    pad = k_new.shape[1]
    n_pages = page_table.shape[1]
    assert n_past % MOBA_BLOCK == 0 and t <= MOBA_BLOCK
    nbp = ksums.shape[2]
    n_slots = _tile(n_pages, (8, 4, 2))
    n_steps = n_pages // n_slots
    seq = lambda b, j, pt: (b, 0, 0)
    gs = pltpu.PrefetchScalarGridSpec(
        num_scalar_prefetch=1, grid=(n, n_steps),
        in_specs=[pl.BlockSpec((1, rows, HALF_MIX), seq), pl.BlockSpec((1, rows, HALF_MIX), seq),
                  pl.BlockSpec((1, HALF_MIX, nbp), seq),
                  pl.BlockSpec((1, pad, HALF_MIX), seq), pl.BlockSpec((1, pad, HALF_MIX), seq)]
        + _page_specs(n_slots, (2, HALF_MIX, PAGE_SIZE), (0, 0, 0), base),
        out_specs=pl.BlockSpec((1, t, HALF_MIX), seq),
        scratch_shapes=[pltpu.VMEM((rows, nbp), BF16), pltpu.VMEM((rows, 1), F32), pltpu.VMEM((rows, 1), F32),
                        pltpu.VMEM((rows, HALF_MIX), F32)])
    return pl.pallas_call(
        functools.partial(_moba_sample_kernel, n_slots=n_slots, n_steps=n_steps, t=t, nbp=nbp, pad=pad,
                          n_past=n_past),
        grid_spec=gs,
        out_shape=jax.ShapeDtypeStruct((n, t, HALF_MIX), F32),
        compiler_params=_cparams("parallel", "arbitrary"),
        name="moba_sample",
    )(page_table, qbd, qbd_f32, ksums, k_new, v_new, *([pool] * n_slots))


def _block_diag(blocks):
    nb, bi, bj = blocks.shape
    eye = jnp.eye(nb, dtype=blocks.dtype)
    return jnp.einsum('bij,bc->bicj', blocks, eye).reshape(nb * bi, nb * bj)


def _rope_tables(pos):
    half = HEAD_DIM // 2
    inv_freq = ROPE_THETA ** (-jnp.arange(half, dtype=F32) / half)
    ang = pos.astype(F32)[:, None] * inv_freq[None, :]
    cos = jnp.cos(ang)
    sin = jnp.sin(ang)
    return jnp.concatenate([cos, cos, cos, cos], axis=1), jnp.concatenate([-sin, sin, -sin, sin], axis=1)


def _queries_bd(q, per, scale, dtype):
    n, t, _ = q.shape
    groups = N_HEADS // per
    qh = q.reshape(n, t, N_HEADS, 1, HEAD_DIM) * scale
    pick = (jnp.arange(N_HEADS)[:, None] // per == jnp.arange(groups)[None, :]).astype(q.dtype)
    return (qh * pick[None, None, :, :, None]).reshape(n, t * N_HEADS, groups * HEAD_DIM).astype(dtype)


def _pad_rows(x, pad):
    n, t, w = x.shape
    return jnp.concatenate([x, jnp.zeros((n, pad - t, w), x.dtype)], axis=1)


def _odd_weight(w_in):
    d = w_in.shape[0]
    gc0 = _O_VW + 128
    main = jnp.concatenate([w_in[:, :gc0], w_in[:, gc0 + 3 * N_HEADS:]], axis=1)
    gc = w_in[:, gc0:gc0 + 3 * N_HEADS].reshape(d, N_KV_C, HPG_C, 3).transpose(0, 1, 3, 2).reshape(d, N_KV_C, 3 * HPG_C)
    gc = jnp.concatenate([gc, jnp.zeros((d, N_KV_C, LANES - 3 * HPG_C), w_in.dtype)], axis=2).reshape(d, N_KV_C * LANES)
    return jnp.concatenate([main, gc], axis=1).astype(BF16)


def kernel(x_prompt, x_sample, cache_a_kv, state_b_h, state_b_conv, cache_c_kv, state_c_win, cache_d_kv, page_table, norm_ffn1, w_ffn1_gate, w_ffn1_up, w_ffn1_down, norm_mix, norm_ffn2, w_ffn2_gate, w_ffn2_up, w_ffn2_down, w_in_even, w_out_even, lru_conv_w, lru_conv_b, lru_w_r, lru_b_r, lru_w_i, lru_b_i, lru_lambda, w_in_odd, w_out_odd, cmp_pe_k, cmp_w1_k, cmp_w2_k, cmp_pe_v, cmp_w1_v, cmp_w2_v, norm_final):
    nb, s_len, d = x_prompt.shape
    ns, t = x_sample.shape[:2]
    depth = norm_mix.shape[0]
    n_pages = page_table.shape[1]
    n_past = n_pages * PAGE_SIZE
    n_pool = cache_a_kv.shape[1]
    hm = HALF_MIX
    pad = 16
    assert t <= pad and n_past % SLC_BLOCK == 0

    yp = x_prompt.reshape(nb * s_len, d)
    ys = x_sample.reshape(ns * t, d)
    cos_p, sin_p = _rope_tables(jnp.arange(s_len, dtype=jnp.int32))
    cos_s, sin_s = _rope_tables(jnp.tile(n_past + jnp.arange(t, dtype=jnp.int32), ns))
    pool_a = _pages_t(cache_a_kv)
    pool_c = _pages_t(cache_c_kv)
    pool_d = _pages_t(cache_d_kv)

    outs = {k: [] for k in ("a_p", "a_s", "bh_p", "bh_s", "bc_p", "bc_s", "c_p", "c_s", "cw_p", "cw_s", "d_p", "d_s")}
    for layer in range(depth):
        li = layer // 2
        last = layer == depth - 1
        f1 = (norm_ffn1[layer], w_ffn1_gate[layer].astype(BF16), w_ffn1_up[layer].astype(BF16),
              w_ffn1_down[layer].astype(BF16))
        f2 = (norm_ffn2[layer], w_ffn2_gate[layer].astype(BF16), w_ffn2_up[layer].astype(BF16),
              w_ffn2_down[layer].astype(BF16))
        yp = _half_ffn(yp, *f1)
        ys = _half_ffn(ys, *f1)
        if layer % 2 == 0:
            w_in = w_in_even[li].astype(BF16)
            w_out = w_out_even[li].astype(BF16)
            lru = (lru_conv_w[li], lru_conv_b[li], _block_diag(lru_w_r[li]).astype(BF16), lru_b_r[li],
                   _block_diag(lru_w_i[li]).astype(BF16), lru_b_i[li], lru_lambda[li])
            kv, xb, gb, qh, kh, vh = _proj_even(yp, norm_mix[layer], w_in, True)
            o_a = _sb_prompt(qh, kh, vh, nb, s_len)
            g, h_new, buf_new = _rg_lru(xb.reshape(nb, s_len, hm), gb.reshape(nb, s_len, hm),
                                        jnp.zeros((nb, hm), F32), jnp.zeros((nb, CONV_WIDTH - 1, hm), F32), *lru)
            yp = _outproj(yp, o_a, g.reshape(nb * s_len, hm), w_out)
            outs["a_p"].append(kv.reshape(nb, s_len, 2, N_HEADS, HEAD_DIM))
            outs["bh_p"].append(h_new)
            outs["bc_p"].append(buf_new)
            kv, xb, gb, q = _proj_even(ys, norm_mix[layer], w_in, False)
            kv3 = kv.reshape(ns, t, 2 * hm)
            o_a = _sb_sample(_queries_bd(q.reshape(ns, t, hm), 1, SCALE, BF16),
                             _pad_rows(kv3[:, :, :hm], pad).astype(BF16), _pad_rows(kv3[:, :, hm:], pad).astype(BF16),
                             pool_a, li * n_pool, page_table)
            g, h_new, buf_new = _rg_lru(xb.reshape(ns, t, hm), gb.reshape(ns, t, hm), state_b_h[li],
                                        state_b_conv[li], *lru)
            ys = _outproj(ys, o_a.reshape(ns * t, hm).astype(BF16), g.reshape(ns * t, hm), w_out)
            outs["a_s"].append(kv.reshape(ns, t, 2, N_HEADS, HEAD_DIM))
            outs["bh_s"].append(h_new)
            outs["bc_s"].append(buf_new)
        else:
            w_in = _odd_weight(w_in_odd[li])
            w_out = w_out_odd[li].astype(BF16)
            cmp_k = _cmp_weights(cmp_pe_k[li], cmp_w1_k[li])
            cmp_v = _cmp_weights(cmp_pe_v[li], cmp_w1_v[li])
            w2k = _block_diag(jnp.stack([cmp_w2_k[li]] * N_KV_C)).astype(BF16)
            w2v = _block_diag(jnp.stack([cmp_w2_v[li]] * N_KV_C)).astype(BF16)
            (c_rows, w_rows, d_rows, gates, qraw_h, qrot_h, ks_h, vs_h, kw_h, vw_h, qd_h, kd_h, vd_h) = _proj_odd(
                yp, norm_mix[layer], w_in, cos_p, sin_p, True)
            n_chunk = s_len // CMP_STRIDE
            rows_c = _tile(s_len, (2048, 1024, 512, 256, 128))
            ab = _cmp_ab([c_rows], [pl.BlockSpec((rows_c, LANES), lambda b, j: (b * (s_len // rows_c) + j, 0))],
                         [pl.BlockSpec((rows_c, LANES), lambda b, j: (b * (s_len // rows_c) + j, 1))],
                         (nb, s_len // rows_c), rows_c // CMP_STRIDE, nb, n_chunk, rows_c, cmp_k, cmp_v, 0, ())
            _, _, kcc_g, vcc_g = _cmp_finish(ab, w2k, w2v)
            o_c = _nsa_prompt(qraw_h, qrot_h, gates, kcc_g, vcc_g, ks_h, vs_h, kw_h, vw_h, nb, s_len)
            o_d = _moba_prompt(qd_h, _block_mean(d_rows), kd_h, vd_h, nb, s_len)
            yp = _outproj(yp, o_c, o_d, w_out)
            wlen = min(WINDOW, s_len)
            outs["c_p"].append(c_rows.reshape(nb, s_len, 4, N_KV_C, HEAD_DIM))
            outs["cw_p"].append(w_rows.reshape(nb, s_len, 2, N_KV_C, HEAD_DIM)[:, s_len - wlen:])
            outs["d_p"].append(d_rows.reshape(nb, s_len, 2, N_HEADS, HEAD_DIM))
            c_rows, w_rows, d_rows, gates, qraw, qrot, qd = _proj_odd(ys, norm_mix[layer], w_in, cos_s, sin_s, False)
            slots_c = _tile(n_pages, (16, 8))
            ab = _cmp_ab([pool_c] * slots_c, _page_specs(slots_c, (None, LANES, PAGE_SIZE), (0, 0, 0), li * n_pool),
                         _page_specs(slots_c, (None, LANES, PAGE_SIZE), (1, 0, 0), li * n_pool),
                         (ns, n_pages // slots_c), slots_c * PAGE_SIZE // CMP_STRIDE, ns, n_past // CMP_STRIDE,
                         PAGE_SIZE, cmp_k, cmp_v, 1, (page_table,), pages_t=True)
            kcc_f, vcc_f, _, _ = _cmp_finish(ab, w2k, w2v)
            nsb = -(-(n_past + t) // SLC_BLOCK)
            nbp = LANES * (-(-nsb // LANES))
            win_l = state_c_win[li]
            wb = win_l.shape[1]
            qraw_bd = _queries_bd(qraw.reshape(ns, t, hm), HPG_C, SCALE, BF16)
            qrot_bd = _queries_bd(qrot.reshape(ns, t, hm), HPG_C, SCALE, BF16)
            c3 = c_rows.reshape(ns, t, 4 * LANES)
            w3 = w_rows.reshape(ns, t, 2 * LANES)
            d3 = d_rows.reshape(ns, t, 2 * hm)
            o_cmp, o_win, sel = _nsa_sample_head(qraw_bd, qrot_bd, kcc_f, vcc_f, win_l.reshape(ns, wb, 2 * LANES),
                                                 _pad_rows(w3, pad).astype(BF16), n_past, nbp)
            sel_rows = jnp.repeat(sel.reshape(ns, t, N_KV_C, nbp), HPG_C, axis=2).reshape(ns, t * N_HEADS, nbp)
            gt = gates.reshape(ns, t, N_KV_C, LANES)[..., :3 * HPG_C].reshape(ns, t, N_KV_C, 3, HPG_C)
            gt = gt.transpose(0, 1, 2, 4, 3).reshape(ns, t * N_HEADS, 3)
            sel_bias = jnp.where(sel_rows > 0.5, 0.0, NEG).astype(BF16)
            o_c = _nsa_sample_slc(qrot_bd, sel_bias, _pad_rows(c3[:, :, 2 * LANES:], pad).astype(BF16),
                                  o_cmp, o_win, gt, pool_c, li * n_pool, page_table)
            ksums = _page_sums(pool_d, li * n_pool, page_table, LANES * (-(-(n_past // MOBA_BLOCK) // LANES)))
            qd3 = qd.reshape(ns, t, hm)
            o_d = _moba_sample(_queries_bd(qd3, 1, SCALE, BF16), _queries_bd(qd3, 1, 1.0, F32), ksums,
                               _pad_rows(d3[:, :, :hm], pad).astype(BF16), _pad_rows(d3[:, :, hm:], pad).astype(BF16),
                               pool_d, li * n_pool, page_table, n_past)
            ys = _outproj(ys, o_c.reshape(ns * t, hm).astype(BF16), o_d.reshape(ns * t, hm).astype(BF16), w_out)
            w5 = w_rows.reshape(ns, t, 2, N_KV_C, HEAD_DIM)
            outs["c_s"].append(c_rows.reshape(ns, t, 4, N_KV_C, HEAD_DIM))
            outs["cw_s"].append(jnp.concatenate([win_l, w5], axis=1)[:, -wb:])
            outs["d_s"].append(d_rows.reshape(ns, t, 2, N_HEADS, HEAD_DIM))
        gf = norm_final if last else None
        yp = _half_ffn(yp, *f2, g_final=gf)
        ys = _half_ffn(ys, *f2, g_final=gf)
    st = lambda k: jnp.stack(outs[k])
    return (yp.reshape(nb, s_len, d), ys.reshape(ns, t, d), st("a_p"), st("a_s"), st("bh_p"), st("bh_s"),
            st("bc_p"), st("bc_s"), st("c_p"), st("c_s"), st("cw_p"), st("cw_s"), st("d_p"), st("d_s"))
```

```python
import functools
import math

import jax
import jax.numpy as jnp
import numpy as np
from jax import lax
from jax.experimental import pallas as pl
from jax.experimental.pallas import tpu as pltpu

F32 = jnp.float32
BF16 = jnp.bfloat16

HEAD_DIM = 64
HALF_MIX = 512
N_HEADS = HALF_MIX // HEAD_DIM
N_KV_C = 2
HPG_C = N_HEADS // N_KV_C
PAGE_SIZE = 128
CONV_WIDTH = 4
LRU_C = 8.0
CMP_LEN = 32
CMP_STRIDE = 16
SLC_BLOCK = 64
N_SLC = 16
WINDOW = 512
MOBA_BLOCK = 256
MOBA_TOPK = 3
ROPE_THETA = 10000.0
RMS_EPS = 1e-6
NEG = -1e30
BIG = 1e30
SCALE = HEAD_DIM ** -0.5

LANES = 128
VMEM_LIMIT_BYTES = 56 * 1024 * 1024
SB_CUTOFF = 120.0
HIGHEST = lax.Precision.HIGHEST


def _cparams(*sem):
    return pltpu.CompilerParams(dimension_semantics=sem, vmem_limit_bytes=VMEM_LIMIT_BYTES)


def _tile(n, prefs):
    for t in prefs:
        if n % t == 0:
            return t
    return n


def _dot(a, b, precision=None):
    return jnp.dot(a, b, preferred_element_type=F32, precision=precision)


def _dot_nt(a, b, precision=None):
    return lax.dot_general(a, b, (((1,), (1,)), ((), ())), preferred_element_type=F32, precision=precision)


def _rms(x, g):
    ms = jnp.mean(x * x, axis=-1, keepdims=True)
    return x * lax.rsqrt(ms + RMS_EPS) * g


def _softplus(z):
    return jnp.maximum(z, 0.0) + jnp.log(1.0 + jnp.exp(-jnp.abs(z)))


def _one_minus_exp(x):
    poly = x
    for k in range(7, 1, -1):
        poly = x * (1.0 + poly * (1.0 / k))
    return jnp.where(x > -0.125, -poly, 1.0 - jnp.exp(x))


def _split_dot(x, t):
    hi = x.astype(BF16)
    lo = (x - hi.astype(F32)).astype(BF16)
    return _dot(hi, t) + _dot(lo, t)


def _masked_softmax(s, mask):
    sm = jnp.where(mask, s, NEG)
    m = jnp.max(sm, axis=-1, keepdims=True)
    e = jnp.where(mask, jnp.exp(sm - m), 0.0)
    d = jnp.sum(e, axis=-1, keepdims=True)
    return e / jnp.maximum(d, 1e-30)


def _online_step(s, mask, v, m, l, acc):
    sm = jnp.where(mask, s, NEG)
    m_new = jnp.maximum(m, jnp.max(sm, axis=-1, keepdims=True))
    alpha = jnp.exp(m - m_new)
    p = jnp.where(mask, jnp.exp(sm - m_new), 0.0)
    l = alpha * l + jnp.sum(p, axis=-1, keepdims=True)
    rows = acc.shape[0]
    acc = alpha.reshape(rows, 1) * acc + _dot(p.reshape(rows, p.shape[-1]).astype(BF16), v)
    return m_new, l, acc


def _flash_multi(qs, ks, v1s, biases, m_refs, acc_refs):
    n = len(qs)
    ss = [_dot_nt(qs[i], ks[i]) for i in range(n)]
    ss = [s if b is None else s + b for s, b in zip(ss, biases)]
    m_old = [ref[...] for ref in m_refs]
    m_new = [jnp.maximum(m_old[i], jnp.max(ss[i], axis=-1, keepdims=True)) for i in range(n)]
    ps = [jnp.exp(ss[i] - m_new[i]).astype(BF16) for i in range(n)]
    pvs = [_dot(ps[i], v1s[i]) for i in range(n)]
    for i in range(n):
        acc_refs[i][...] = jnp.exp(m_old[i] - m_new[i]) * acc_refs[i][...] + pvs[i]
        m_refs[i][...] = m_new[i]


def _joint_update(m_s, l_s, acc_s, scores, vals, keys_on_lanes=False):
    m_old = m_s[...]
    m_new = m_old
    for sc in scores:
        m_new = jnp.maximum(m_new, jnp.max(sc, axis=-1, keepdims=True))
    alpha = jnp.exp(m_old - m_new)
    l = alpha * l_s[...]
    acc = alpha * acc_s[...]
    for sc, v in zip(scores, vals):
        p = jnp.exp(sc - m_new)
        l = l + jnp.sum(p, axis=-1, keepdims=True)
        acc = acc + (_dot_nt(p.astype(BF16), v) if keys_on_lanes else _dot(p.astype(BF16), v))
    m_s[...] = m_new
    l_s[...] = l
    acc_s[...] = acc


def _flash_result(acc):
    return acc[:, 0:HEAD_DIM] / acc[:, HEAD_DIM:HEAD_DIM + 1]


def _top_select(val, idx, n):
    sel = jnp.zeros(val.shape, F32)
    big_i = jnp.int32(2 ** 30)
    for _ in range(n):
        m = jnp.max(val, axis=-1, keepdims=True)
        first = jnp.min(jnp.where(val == m, idx, big_i), axis=-1, keepdims=True)
        hit = idx == first
        sel = jnp.where(hit & (m > 0.5 * NEG), 1.0, sel)
        val = jnp.where(hit, -jnp.inf, val)
    return sel


def _ffn_kernel(*refs, n_f, post_norm):
    if post_norm:
        x_ref, g_ref, wg_ref, wu_ref, wd_ref, gf_ref, o_ref, xn_ref, acc_ref = refs
    else:
        x_ref, g_ref, wg_ref, wu_ref, wd_ref, o_ref, xn_ref, acc_ref = refs
    j = pl.program_id(1)

    @pl.when(j == 0)
    def _():
        xn_ref[...] = _rms(x_ref[...], g_ref[...]).astype(BF16)
        acc_ref[...] = jnp.zeros_like(acc_ref)

    xn = xn_ref[...]
    hg = _dot(xn, wg_ref[...])
    hu = _dot(xn, wu_ref[...])
    a = (hg * jax.nn.sigmoid(hg) * hu).astype(BF16)
    acc_ref[...] += _dot(a, wd_ref[...])

    @pl.when(j == n_f - 1)
    def _():
        y = x_ref[...] + 0.5 * acc_ref[...]
        if post_norm:
            y = _rms(y, gf_ref[...])
        o_ref[...] = y


def _half_ffn(x, g, wg, wu, wd, g_final=None):
    m, d = x.shape
    f = wg.shape[1]
    tm = _tile(m, (1024, 512, 256, 128))
    tf = _tile(f, (256, 128))
    n_f = f // tf
    post = g_final is not None
    in_specs = [pl.BlockSpec((tm, d), lambda i, j: (i, 0)),
                pl.BlockSpec((1, d), lambda i, j: (0, 0)),
                pl.BlockSpec((d, tf), lambda i, j: (0, j)),
                pl.BlockSpec((d, tf), lambda i, j: (0, j)),
                pl.BlockSpec((tf, d), lambda i, j: (j, 0))]
    args = [x, g.reshape(1, d), wg, wu, wd]
    if post:
        in_specs.append(pl.BlockSpec((1, d), lambda i, j: (0, 0)))
        args.append(g_final.reshape(1, d))
    return pl.pallas_call(
        functools.partial(_ffn_kernel, n_f=n_f, post_norm=post),
        grid=(m // tm, n_f),
        in_specs=in_specs,
        out_specs=pl.BlockSpec((tm, d), lambda i, j: (i, 0)),
        out_shape=jax.ShapeDtypeStruct((m, d), F32),
        scratch_shapes=[pltpu.VMEM((tm, d), BF16), pltpu.VMEM((tm, d), F32)],
        compiler_params=_cparams("parallel", "arbitrary"),
        name="half_ffn",
    )(*args)


def _rope(seg, cos_t, sin_t):
    w = seg.shape[1]
    reps = w // LANES
    c = jnp.concatenate([cos_t] * reps, axis=1) if reps > 1 else cos_t
    s = jnp.concatenate([sin_t] * reps, axis=1) if reps > 1 else sin_t
    lane = lax.broadcasted_iota(jnp.int32, seg.shape, 1)
    first = (lane % HEAD_DIM) < (HEAD_DIM // 2)
    rot = jnp.where(first, pltpu.roll(seg, w - HEAD_DIM // 2, 1), pltpu.roll(seg, HEAD_DIM // 2, 1))
    return seg * c + rot * s


def _store_heads(ref, seg, scale=None, tail=None):
    for h in range(seg.shape[1] // HEAD_DIM):
        piece = seg[:, h * HEAD_DIM:(h + 1) * HEAD_DIM]
        if scale is not None:
            piece = piece * scale
        if tail is not None:
            piece = jnp.concatenate([piece, tail], axis=1)
        ref[h] = piece.astype(ref.dtype)


def _proj_even_kernel(x_ref, g_ref, w_ref, kv_ref, xb_ref, gb_ref, *outs, heads):
    hm = HALF_MIX
    xn = _rms(x_ref[...], g_ref[...]).astype(BF16)
    y = _dot(xn, w_ref[...])
    kv_ref[...] = y[:, hm:3 * hm]
    xb_ref[...] = y[:, 3 * hm:4 * hm]
    gb_ref[...] = y[:, 4 * hm:5 * hm]
    if heads:
        qh_ref, kh_ref, vh_ref = outs
        _store_heads(qh_ref, y[:, 0:hm], SCALE)
        _store_heads(kh_ref, y[:, hm:2 * hm])
        _store_heads(vh_ref, y[:, 2 * hm:3 * hm])
    else:
        (q_ref,) = outs
        q_ref[...] = y[:, 0:hm]


def _proj_even(x, g, w, heads):
    m, d = x.shape
    n = w.shape[1]
    hm = HALF_MIX
    tm = _tile(m, (512, 256, 128))
    row = lambda i: (i, 0)
    out_shape = [jax.ShapeDtypeStruct((m, 2 * hm), F32), jax.ShapeDtypeStruct((m, hm), F32),
                 jax.ShapeDtypeStruct((m, hm), F32)]
    out_specs = [pl.BlockSpec((tm, 2 * hm), row), pl.BlockSpec((tm, hm), row), pl.BlockSpec((tm, hm), row)]
    if heads:
        for _ in range(3):
            out_shape.append(jax.ShapeDtypeStruct((N_HEADS, m, HEAD_DIM), BF16))
            out_specs.append(pl.BlockSpec((N_HEADS, tm, HEAD_DIM), lambda i: (0, i, 0)))
    else:
        out_shape.append(jax.ShapeDtypeStruct((m, hm), F32))
        out_specs.append(pl.BlockSpec((tm, hm), row))
    return pl.pallas_call(
        functools.partial(_proj_even_kernel, heads=heads),
        grid=(m // tm,),
        in_specs=[pl.BlockSpec((tm, d), row), pl.BlockSpec((1, d), lambda i: (0, 0)),
                  pl.BlockSpec((d, n), lambda i: (0, 0))],
        out_specs=out_specs, out_shape=out_shape,
        compiler_params=_cparams("parallel"),
        name="proj_even",
    )(x, g.reshape(1, d), w)


_O_QC, _O_KC, _O_VC, _O_KS, _O_VS, _O_KW, _O_VW, _O_QD, _O_KD, _O_VD, _O_GT, _O_END = (
    0, 512, 640, 768, 896, 1024, 1152, 1280, 1792, 2304, 2816, 3072)


def _proj_odd_kernel(x_ref, g_ref, w_ref, cos_ref, sin_ref, c_ref, w_out_ref, d_ref, gt_ref, *outs, heads, tm,
                     period):
    xn = _rms(x_ref[...], g_ref[...]).astype(BF16)
    y = _dot(xn, w_ref[...])
    cos_t = cos_ref[...]
    sin_t = sin_ref[...]
    qc = y[:, _O_QC:_O_KC]
    qc_rot = _rope(qc, cos_t, sin_t)
    ks_rot = _rope(y[:, _O_KS:_O_VS], cos_t, sin_t)
    kw_rot = _rope(y[:, _O_KW:_O_VW], cos_t, sin_t)
    qd_rot = _rope(y[:, _O_QD:_O_KD], cos_t, sin_t)
    kd_rot = _rope(y[:, _O_KD:_O_VD], cos_t, sin_t)
    vs = y[:, _O_VS:_O_KW]
    vw = y[:, _O_VW:_O_QD]
    vd = y[:, _O_VD:_O_GT]
    c_ref[:, 0:256] = y[:, _O_KC:_O_KS]
    c_ref[:, 256:384] = ks_rot
    c_ref[:, 384:512] = vs
    w_out_ref[:, 0:128] = kw_rot
    w_out_ref[:, 128:256] = vw
    d_ref[:, 0:512] = kd_rot
    d_ref[:, 512:1024] = vd
    gt_ref[...] = jax.nn.sigmoid(y[:, _O_GT:_O_END])
    if heads:
        qraw_h, qrot_h, ks_h, vs_h, kw_h, vw_h, qd_h, kd_h, vd_h = outs
        pos = (pl.program_id(0) % period) * tm + lax.broadcasted_iota(jnp.int32, (tm, HEAD_DIM), 0)
        lane = lax.broadcasted_iota(jnp.int32, (tm, HEAD_DIM), 1)
        ones = (lane == 0).astype(F32)
        _store_heads(qraw_h, qc, SCALE)
        _store_heads(qrot_h, qc_rot, SCALE)
        _store_heads(ks_h, ks_rot, tail=(lane == pos // SLC_BLOCK).astype(F32))
        _store_heads(vs_h, vs, tail=ones)
        _store_heads(kw_h, kw_rot)
        _store_heads(vw_h, vw, tail=ones)
        _store_heads(qd_h, qd_rot, SCALE)
        _store_heads(kd_h, kd_rot, tail=(lane == pos // MOBA_BLOCK).astype(F32))
        _store_heads(vd_h, vd, tail=ones)
    else:
        qraw_ref, qrot_ref, qd_ref = outs
        qraw_ref[...] = qc
        qrot_ref[...] = qc_rot
        qd_ref[...] = qd_rot


def _proj_odd(x, g, w, cos_t, sin_t, heads):
    m, d = x.shape
    n = w.shape[1]
    tm = _tile(m, (512, 256, 128))
    period = cos_t.shape[0] // tm
    row = lambda i: (i, 0)
    tab = lambda i: (i % period, 0)
    out_shape = [jax.ShapeDtypeStruct((m, 512), F32), jax.ShapeDtypeStruct((m, 256), F32),
                 jax.ShapeDtypeStruct((m, 1024), F32), jax.ShapeDtypeStruct((m, 256), F32)]
    out_specs = [pl.BlockSpec((tm, 512), row), pl.BlockSpec((tm, 256), row), pl.BlockSpec((tm, 1024), row),
                 pl.BlockSpec((tm, 256), row)]
    if heads:
        wide = (False, False, True, True, False, True, False, True, True)
        for nh, wd in zip((N_HEADS, N_HEADS, N_KV_C, N_KV_C, N_KV_C, N_KV_C, N_HEADS, N_HEADS, N_HEADS), wide):
            width = LANES if wd else HEAD_DIM
            out_shape.append(jax.ShapeDtypeStruct((nh, m, width), BF16))
            out_specs.append(pl.BlockSpec((nh, tm, width), lambda i: (0, i, 0)))
    else:
        for _ in range(3):
            out_shape.append(jax.ShapeDtypeStruct((m, HALF_MIX), F32))
            out_specs.append(pl.BlockSpec((tm, HALF_MIX), row))
    return pl.pallas_call(
        functools.partial(_proj_odd_kernel, heads=heads, tm=tm, period=period),
        grid=(m // tm,),
        in_specs=[pl.BlockSpec((tm, d), row), pl.BlockSpec((1, d), lambda i: (0, 0)),
                  pl.BlockSpec((d, n), lambda i: (0, 0)),
                  pl.BlockSpec((tm, LANES), tab), pl.BlockSpec((tm, LANES), tab)],
        out_specs=out_specs, out_shape=out_shape,
        compiler_params=_cparams("parallel"),
        name="proj_odd",
    )(x, g.reshape(1, d), w, cos_t, sin_t)


def _outproj_kernel(x_ref, a_ref, b_ref, w_ref, o_ref, *, heads):
    hm = HALF_MIX
    if heads:
        a = jnp.concatenate([a_ref[h] for h in range(N_HEADS)], axis=1)
        b = b_ref[...] if len(b_ref.shape) == 2 else jnp.concatenate([b_ref[h] for h in range(N_HEADS)], axis=1)
    else:
        a = a_ref[...]
        b = b_ref[...]
    o_ref[...] = x_ref[...] + _dot(a, w_ref[0:hm, :]) + _dot(b, w_ref[hm:2 * hm, :])


def _outproj(x, a, b, w):
    m, d = x.shape
    tm = _tile(m, (512, 256, 128))
    row = lambda i: (i, 0)

    def spec(t):
        if t.ndim == 2:
            return pl.BlockSpec((tm, t.shape[1]), row)
        return pl.BlockSpec((t.shape[0], tm, t.shape[2]), lambda i: (0, i, 0))

    return pl.pallas_call(
        functools.partial(_outproj_kernel, heads=a.ndim == 3),
        grid=(m // tm,),
        in_specs=[pl.BlockSpec((tm, d), row), spec(a), spec(b), pl.BlockSpec(w.shape, lambda i: (0, 0))],
        out_specs=pl.BlockSpec((tm, d), row),
        out_shape=jax.ShapeDtypeStruct((m, d), F32),
        compiler_params=_cparams("parallel"),
        name="outproj",
    )(x, a, b, w)


def _sb_tile(q, k, v, t_incl, carry, acc, mask, keys_on_lanes=False):
    z = _dot(q, k) if keys_on_lanes else _dot_nt(q, k)
    lk = -_softplus(z)
    if mask is not None:
        lk = jnp.where(mask, lk, 0.0)
    incl = _split_dot(lk, t_incl)
    w = jnp.exp(z + incl + carry)
    if mask is not None:
        w = jnp.where(mask, w, 0.0)
    wb = w.astype(BF16)
    acc = acc + (_dot_nt(wb, v) if keys_on_lanes else _dot(wb, v))
    return carry + incl[:, 0:1], acc


def _tri_incl(n):
    r = lax.broadcasted_iota(jnp.int32, (n, n), 0)
    c = lax.broadcasted_iota(jnp.int32, (n, n), 1)
    return (r >= c).astype(BF16)


def _sb_multi(qs, ks, vs, t_incl, c_refs, a_refs, mask):
    n = len(qs)
    zs = [_dot_nt(qs[i], ks[i]) for i in range(n)]
    lks = [-_softplus(z) for z in zs]
    if mask is not None:
        lks = [jnp.where(mask, lk, 0.0) for lk in lks]
    incls = [_split_dot(lk, t_incl) for lk in lks]
    carries = [ref[...] for ref in c_refs]
    ws = [jnp.exp(zs[i] + incls[i] + carries[i]) for i in range(n)]
    if mask is not None:
        ws = [jnp.where(mask, w, 0.0) for w in ws]
    pvs = [_dot(ws[i].astype(BF16), vs[i]) for i in range(n)]
    alive = None
    for i in range(n):
        carry = carries[i] + incls[i][:, 0:1]
        c_refs[i][...] = carry
        a_refs[i][...] = a_refs[i][...] + pvs[i]
        top = jnp.max(carry)
        alive = top if alive is None else jnp.maximum(alive, top)
    return alive


def _sb_prompt_kernel(q_ref, k_ref, v_ref, o_ref, *state, tq, hb):
    i = pl.program_id(2)
    c_refs = state[:hb]
    a_refs = state[hb:]
    t_incl = _tri_incl(tq)
    r = lax.broadcasted_iota(jnp.int32, (tq, tq), 0)
    c = lax.broadcasted_iota(jnp.int32, (tq, tq), 1)
    for ref in state:
        ref[...] = jnp.zeros_like(ref)
    qs = [q_ref[h] for h in range(hb)]

    def tile(kt, mask):
        start = pl.multiple_of(kt * tq, tq)
        return _sb_multi(qs, [k_ref[h, pl.ds(start, tq), :] for h in range(hb)],
                         [v_ref[h, pl.ds(start, tq), :] for h in range(hb)], t_incl, c_refs, a_refs, mask)

    def cond(s):
        kt, alive = s
        return jnp.logical_and(kt >= 0, alive > -SB_CUTOFF)

    def body(s):
        kt, _ = s
        return kt - 1, tile(kt, None)

    lax.while_loop(cond, body, (i - 1, tile(i, c < r)))
    for h in range(hb):
        o_ref[h] = a_refs[h][...].astype(o_ref.dtype)


def _sb_prompt(qh, kh, vh, n_seq, s_len):
    tq = _tile(s_len, (256, 128))
    nq = s_len // tq
    hb = 4
    return pl.pallas_call(
        functools.partial(_sb_prompt_kernel, tq=tq, hb=hb),
        grid=(n_seq, N_HEADS // hb, nq),
        in_specs=[pl.BlockSpec((hb, tq, HEAD_DIM), lambda b, h, i: (h, b * nq + i, 0)),
                  pl.BlockSpec((hb, s_len, HEAD_DIM), lambda b, h, i: (h, b, 0)),
                  pl.BlockSpec((hb, s_len, HEAD_DIM), lambda b, h, i: (h, b, 0))],
        out_specs=pl.BlockSpec((hb, tq, HEAD_DIM), lambda b, h, i: (h, b * nq + i, 0)),
        out_shape=jax.ShapeDtypeStruct(qh.shape, BF16),
        scratch_shapes=[pltpu.VMEM((tq, 1), F32)] * hb + [pltpu.VMEM((tq, HEAD_DIM), F32)] * hb,
        compiler_params=_cparams("parallel", "parallel", "arbitrary"),
        name="sb_prompt",
    )(qh, kh, vh)


def _lru_kernel(xb_ref, gb_ref, h0_ref, buf0_ref, cw_ref, cb_ref, wr_ref, br_ref, wi_ref, bi_ref, lam_ref,
                g_ref, hl_ref, bn_ref, xpad, a_s, u_s, hs_s, h_s, *, tc, n_t):
    j = pl.program_id(1)
    keep = CONV_WIDTH - 1

    @pl.when(j == 0)
    def _():
        xpad[8 - keep:8, :] = buf0_ref[0]
        h_s[...] = h0_ref[0]

    @pl.when(j > 0)
    def _():
        xpad[8 - keep:8, :] = xpad[8 + tc - keep:8 + tc, :]

    x = xb_ref[0]
    xpad[8:8 + tc, :] = x
    xc = cb_ref[...] + x * cw_ref[keep:keep + 1, :]
    for d in range(1, CONV_WIDTH):
        xc = xc + xpad[8 - d:8 - d + tc, :] * cw_ref[keep - d:keep - d + 1, :]
    xcb = xc.astype(BF16)
    r = jax.nn.sigmoid(_dot(xcb, wr_ref[...]) + br_ref[...])
    gi = jax.nn.sigmoid(_dot(xcb, wi_ref[...]) + bi_ref[...])
    log_a = -LRU_C * r * _softplus(-lam_ref[...])
    a_s[...] = jnp.exp(log_a)
    u_s[...] = jnp.sqrt(_one_minus_exp(2.0 * log_a)) * (gi * xc)

    def step(t, h):
        h = a_s[pl.ds(t, 1), :] * h + u_s[pl.ds(t, 1), :]
        hs_s[pl.ds(t, 1), :] = h
        return h

    h = lax.fori_loop(0, tc, step, h_s[...], unroll=min(tc, 8))
    h_s[...] = h
    g_ref[0] = (jax.nn.gelu(gb_ref[0]) * hs_s[...]).astype(g_ref.dtype)

    @pl.when(j == n_t - 1)
    def _():
        hl_ref[0] = h
        bn_ref[0] = xpad[8 + tc - keep:8 + tc, :]


def _rg_lru(xb, gb, h0, buf0, conv_w, conv_b, wr_bd, b_r, wi_bd, b_i, lam):
    n, t, w = xb.shape
    keep = CONV_WIDTH - 1
    assert t >= keep
    tc = _tile(t, (512, 256, 128))
    n_t = t // tc
    seq = lambda b, j: (b, j, 0)
    one = lambda b, j: (b, 0, 0)
    const = lambda b, j: (0, 0)
    vec = pl.BlockSpec((1, w), const)
    g, hl, bn = pl.pallas_call(
        functools.partial(_lru_kernel, tc=tc, n_t=n_t),
        grid=(n, n_t),
        in_specs=[pl.BlockSpec((1, tc, w), seq), pl.BlockSpec((1, tc, w), seq),
                  pl.BlockSpec((1, 1, w), one), pl.BlockSpec((1, keep, w), one),
                  pl.BlockSpec((CONV_WIDTH, w), const), vec,
                  pl.BlockSpec((w, w), const), vec, pl.BlockSpec((w, w), const), vec, vec],
        out_specs=[pl.BlockSpec((1, tc, w), seq), pl.BlockSpec((1, 1, w), one), pl.BlockSpec((1, keep, w), one)],
        out_shape=[jax.ShapeDtypeStruct((n, t, w), BF16), jax.ShapeDtypeStruct((n, 1, w), F32),
                   jax.ShapeDtypeStruct((n, keep, w), F32)],
        scratch_shapes=[pltpu.VMEM((tc + 8, w), F32), pltpu.VMEM((tc, w), F32), pltpu.VMEM((tc, w), F32),
                        pltpu.VMEM((tc, w), F32), pltpu.VMEM((1, w), F32)],
        compiler_params=_cparams("parallel", "arbitrary"),
        name="rg_lru",
    )(xb, gb, h0.reshape(n, 1, w), buf0, conv_w, conv_b.reshape(1, w), wr_bd, b_r.reshape(1, w), wi_bd,
      b_i.reshape(1, w), lam.reshape(1, w))
    return g, hl.reshape(n, w), bn


def _cmp_ab_kernel(*refs, n_in, rows, n_prefetch, pages_t):
    refs = refs[n_prefetch:]
    k_refs = refs[:n_in]
    v_refs = refs[n_in:2 * n_in]
    pe_ak, pe_bk, pe_av, pe_bv, w_ak, w_bk, w_av, w_bv, o_ref = refs[2 * n_in:2 * n_in + 9]
    nch = rows // CMP_STRIDE
    if pages_t:
        xk_s, xv_s = refs[2 * n_in + 9:]
        for s in range(n_in):
            xk_s[s * rows:(s + 1) * rows, :] = k_refs[s][...].T
            xv_s[s * rows:(s + 1) * rows, :] = v_refs[s][...].T
        k_refs, v_refs, nch = [xk_s], [xv_s], n_in * nch

    def chunks(in_refs):
        per = [jnp.concatenate([r[pl.ds(p, nch, stride=CMP_STRIDE), :] for p in range(CMP_STRIDE)], axis=1)
               for r in in_refs]
        return per[0] if len(in_refs) == 1 else jnp.concatenate(per, axis=0)

    xk = chunks(k_refs)
    xv = chunks(v_refs)
    o_ref[0, :, 0:128] = _dot((xk + pe_ak[...]).astype(BF16), w_ak[...])
    o_ref[0, :, 128:256] = _dot((xk + pe_bk[...]).astype(BF16), w_bk[...])
    o_ref[0, :, 256:384] = _dot((xv + pe_av[...]).astype(BF16), w_av[...])
    o_ref[0, :, 384:512] = _dot((xv + pe_bv[...]).astype(BF16), w_bv[...])


def _cmp_weights(pe, w1):
    hid = w1.shape[-1]
    eye = jnp.eye(N_KV_C, dtype=F32)
    out = []
    for half in range(CMP_LEN // CMP_STRIDE):
        sl = slice(half * CMP_STRIDE, (half + 1) * CMP_STRIDE)
        pe_row = jnp.broadcast_to(pe[sl][:, None, :], (CMP_STRIDE, N_KV_C, HEAD_DIM)).reshape(1, -1)
        wbd = jnp.einsum('pdh,gk->pgdkh', w1[sl], eye).reshape(CMP_STRIDE * N_KV_C * HEAD_DIM, N_KV_C * hid)
        out.append((pe_row, wbd.astype(BF16)))
    return out


def _cmp_ab(in_arrays, k_specs, v_specs, grid, out_rows, n_seq, n_chunks, rows, cmp_k, cmp_v, num_prefetch,
            prefetch, pages_t=False):
    in_specs = list(k_specs) + list(v_specs)
    (pe_ak, w_ak), (pe_bk, w_bk) = cmp_k
    (pe_av, w_av), (pe_bv, w_bv) = cmp_v
    nd = len(grid)
    const = lambda *a: (0, 0)
    pes = [pe_ak, pe_bk, pe_av, pe_bv]
    ws = [w_ak, w_bk, w_av, w_bv]
    specs = list(in_specs) + [pl.BlockSpec(p.shape, const) for p in pes] + [pl.BlockSpec(w.shape, const) for w in ws]
    gs = pltpu.PrefetchScalarGridSpec(
        num_scalar_prefetch=num_prefetch, grid=grid, in_specs=specs,
        out_specs=pl.BlockSpec((1, out_rows, 512), lambda b, j, *a: (b, j, 0)),
        scratch_shapes=[pltpu.VMEM((len(in_arrays) * rows, LANES), F32)] * 2 if pages_t else [])
    return pl.pallas_call(
        functools.partial(_cmp_ab_kernel, n_in=len(in_arrays), rows=rows, n_prefetch=num_prefetch, pages_t=pages_t),
        grid_spec=gs,
        out_shape=jax.ShapeDtypeStruct((n_seq, n_chunks, 512), F32),
        compiler_params=_cparams(*(["parallel"] + ["arbitrary"] * (nd - 1))),
        name="cmp_ab",
    )(*prefetch, *in_arrays, *in_arrays, *pes, *ws)


def _cmp_finish_kernel(ab_ref, w2k_ref, w2v_ref, kf_ref, vf_ref, kg_ref, vg_ref, *, nc):
    n = nc - 1
    hk = jax.nn.gelu(ab_ref[0, 0:n, 0:128] + ab_ref[0, 1:nc, 128:256]).astype(BF16)
    hv = jax.nn.gelu(ab_ref[0, 0:n, 256:384] + ab_ref[0, 1:nc, 384:512]).astype(BF16)
    kc = _dot(hk, w2k_ref[...]).astype(BF16)
    vc = _dot(hv, w2v_ref[...]).astype(BF16)
    zero = jnp.zeros((1, LANES), BF16)
    for ref_f, ref_g, val in ((kf_ref, kg_ref, kc), (vf_ref, vg_ref, vc)):
        full = jnp.concatenate([val, zero], axis=0)
        ref_f[0] = full
        for g in range(N_KV_C):
            ref_g[0, g] = full[:, g * HEAD_DIM:(g + 1) * HEAD_DIM]


def _cmp_finish(ab, w2k_bd, w2v_bd):
    n, nc, _ = ab.shape
    flat = jax.ShapeDtypeStruct((n, nc, LANES), BF16)
    grp = jax.ShapeDtypeStruct((n, N_KV_C, nc, HEAD_DIM), BF16)
    fspec = pl.BlockSpec((1, nc, LANES), lambda b: (b, 0, 0))
    gspec = pl.BlockSpec((1, N_KV_C, nc, HEAD_DIM), lambda b: (b, 0, 0, 0))
    return pl.pallas_call(
        functools.partial(_cmp_finish_kernel, nc=nc),
        grid=(n,),
        in_specs=[pl.BlockSpec((1, nc, 512), lambda b: (b, 0, 0)),
                  pl.BlockSpec((LANES, LANES), lambda b: (0, 0)), pl.BlockSpec((LANES, LANES), lambda b: (0, 0))],
        out_specs=[fspec, fspec, gspec, gspec], out_shape=[flat, flat, grp, grp],
        compiler_params=_cparams("parallel"),
        name="cmp_finish",
    )(ab, w2k_bd, w2v_bd)


def _overlap_t(nb, nc):
    b = lax.broadcasted_iota(jnp.int32, (nb, nc), 0)
    c = lax.broadcasted_iota(jnp.int32, (nb, nc), 1)
    return ((c * CMP_STRIDE < (b + 1) * SLC_BLOCK) & (c * CMP_STRIDE + CMP_LEN > b * SLC_BLOCK)).astype(F32)


def _nsa_prompt_kernel(qraw_ref, qrot_ref, gt_ref, kcc_ref, vcc_ref, ks_ref, vs_ref, kw_ref, vw_ref, o_ref,
                       *state, tq, nc, nsb, ck):
    i = pl.program_id(1)
    q0 = i * tq
    rows = HPG_C * tq
    nbp = LANES
    qpos3 = q0 + lax.broadcasted_iota(jnp.int32, (1, tq, 1), 1)

    def compressed_and_selection(g):
        qraw = qraw_ref[g * HPG_C:(g + 1) * HPG_C].reshape(rows, HEAD_DIM)
        sc = _dot_nt(qraw, kcc_ref[0, g]).reshape(HPG_C, tq, nc)
        cidx = lax.broadcasted_iota(jnp.int32, (1, 1, nc), 2)
        mask_c = (cidx * CMP_STRIDE + CMP_LEN - 1 <= qpos3) & (cidx < nc - 1)
        pc = _masked_softmax(sc, mask_c)
        o_c = _dot(pc.reshape(rows, nc).astype(BF16), vcc_ref[0, g])

        psum = jnp.sum(pc, axis=0)
        imp_t = _dot_nt(_overlap_t(nsb, nc), psum, precision=HIGHEST)
        blk = lax.broadcasted_iota(jnp.int32, (nsb, tq), 0)
        own = (q0 + lax.broadcasted_iota(jnp.int32, (nsb, tq), 1)) // SLC_BLOCK
        imp_t = jnp.where(blk == own, BIG, jnp.where(blk < own, imp_t, NEG))
        cnt = jnp.zeros((nsb, tq), jnp.int32)
        for b2 in range(nsb):
            rowv = imp_t[b2:b2 + 1, :]
            ahead = (rowv > imp_t) | ((rowv == imp_t) & (b2 < blk))
            cnt = cnt + ahead.astype(jnp.int32)
        sel_t = ((cnt < N_SLC) & (blk <= own)).astype(F32)
        if nsb < nbp:
            sel_t = jnp.concatenate([sel_t, jnp.zeros((nbp - nsb, tq), F32)], axis=0)
        return o_c, jnp.where(sel_t.T > 0.5, 0.0, NEG).astype(BF16)

    per_group = [compressed_and_selection(g) for g in range(N_KV_C)]
    nh = N_HEADS
    m_refs = state[:2 * nh]
    acc_refs = state[2 * nh:]
    qs = [qrot_ref[h] for h in range(nh)]
    qs_sel = [jnp.concatenate([qs[h], per_group[h // HPG_C][1][:, 0:HEAD_DIM]], axis=1) for h in range(nh)]
    grp = [h // HPG_C for h in range(nh)]
    for ref in m_refs:
        ref[...] = jnp.full(ref.shape, NEG, F32)
    for ref in acc_refs:
        ref[...] = jnp.zeros_like(ref)

    c_own = q0 // ck
    start = pl.multiple_of(c_own * ck, ck)
    row = lax.broadcasted_iota(jnp.int32, (tq, ck), 0)
    lane = lax.broadcasted_iota(jnp.int32, (tq, ck), 1)
    bias_s = jnp.where(start + lane <= q0 + row, 0.0, NEG)
    wk = WINDOW + tq
    start_w = pl.multiple_of(jnp.maximum(q0 - WINDOW, 0), tq)
    dist = (q0 + lax.broadcasted_iota(jnp.int32, (tq, wk), 0)) - (start_w + lax.broadcasted_iota(jnp.int32, (tq, wk), 1))
    bias_w = jnp.where((dist >= 0) & (dist <= WINDOW), 0.0, NEG)
    ks_own = [ks_ref[g, pl.ds(start, ck), :] for g in range(N_KV_C)]
    vs_own = [vs_ref[g, pl.ds(start, ck), :] for g in range(N_KV_C)]
    kw_all = [kw_ref[g, pl.ds(start_w, wk), :] for g in range(N_KV_C)]
    vw_all = [vw_ref[g, pl.ds(start_w, wk), :] for g in range(N_KV_C)]
    _flash_multi(qs_sel + qs,
                 [ks_own[g] for g in grp] + [kw_all[g] for g in grp],
                 [vs_own[g] for g in grp] + [vw_all[g] for g in grp],
                 [bias_s] * nh + [bias_w] * nh, m_refs, acc_refs)

    def slc_step(cc, carry):
        st = pl.multiple_of(cc * ck, ck)
        ks_c = [ks_ref[g, pl.ds(st, ck), :] for g in range(N_KV_C)]
        vs_c = [vs_ref[g, pl.ds(st, ck), :] for g in range(N_KV_C)]
        _flash_multi(qs_sel, [ks_c[g] for g in grp], [vs_c[g] for g in grp], [None] * nh, m_refs[:nh], acc_refs[:nh])
        return carry

    lax.fori_loop(0, c_own, slc_step, 0)

    for h in range(nh):
        g, j = h // HPG_C, h % HPG_C
        gt = gt_ref[:, g * LANES:(g + 1) * LANES]
        g_c = gt[:, j:j + 1]
        g_s = gt[:, HPG_C + j:HPG_C + j + 1]
        g_w = gt[:, 2 * HPG_C + j:2 * HPG_C + j + 1]
        o_c = per_group[g][0][j * tq:(j + 1) * tq]
        o_ref[h] = (o_c * g_c + _flash_result(acc_refs[h][...]) * g_s
                    + _flash_result(acc_refs[nh + h][...]) * g_w).astype(o_ref.dtype)


def _nsa_prompt(qraw_h, qrot_h, gates, kcc_g, vcc_g, ks_h, vs_h, kw_h, vw_h, n_seq, s_len):
    tq = 128
    nq = s_len // tq
    nc = kcc_g.shape[2]
    nsb = -(-s_len // SLC_BLOCK)
    ck = _tile(s_len, (512, 256, 128))
    assert s_len % tq == 0 and nsb <= HEAD_DIM and WINDOW % tq == 0 and s_len >= WINDOW + tq and ck % tq == 0
    qspec = pl.BlockSpec((N_HEADS, tq, HEAD_DIM), lambda b, i: (0, b * nq + i, 0))
    cspec = pl.BlockSpec((1, N_KV_C, nc, HEAD_DIM), lambda b, i: (b, 0, 0, 0))
    kspec = pl.BlockSpec((N_KV_C, s_len, HEAD_DIM), lambda b, i: (0, b, 0))
    vspec = pl.BlockSpec((N_KV_C, s_len, LANES), lambda b, i: (0, b, 0))
    return pl.pallas_call(
        functools.partial(_nsa_prompt_kernel, tq=tq, nc=nc, nsb=nsb, ck=ck),
        grid=(n_seq, nq),
        in_specs=[qspec, qspec, pl.BlockSpec((tq, N_KV_C * LANES), lambda b, i: (b * nq + i, 0)),
                  cspec, cspec, vspec, vspec, kspec, vspec],
        out_specs=qspec,
        out_shape=jax.ShapeDtypeStruct(qraw_h.shape, BF16),
        scratch_shapes=[pltpu.VMEM((tq, 1), F32)] * (2 * N_HEADS) + [pltpu.VMEM((tq, LANES), F32)] * (2 * N_HEADS),
        compiler_params=_cparams("parallel", "arbitrary"),
        name="nsa_prompt",
    )(qraw_h, qrot_h, gates, kcc_g, vcc_g, ks_h, vs_h, kw_h, vw_h)


def _block_mean_kernel(x_ref, o_ref, *, nb):
    x = x_ref[...]
    m = jnp.mean(x.reshape(nb, MOBA_BLOCK, x.shape[1]), axis=1)
    for h in range(N_HEADS):
        o_ref[h] = m[:, h * HEAD_DIM:(h + 1) * HEAD_DIM]


def _block_mean(d_rows):
    m = d_rows.shape[0]
    nb_total = m // MOBA_BLOCK
    nb = _tile(nb_total, (8,))
    return pl.pallas_call(
        functools.partial(_block_mean_kernel, nb=nb),
        grid=(nb_total // nb,),
        in_specs=[pl.BlockSpec((nb * MOBA_BLOCK, HALF_MIX), lambda i: (i, 0))],
        out_specs=pl.BlockSpec((N_HEADS, nb, HEAD_DIM), lambda i: (0, i, 0)),
        out_shape=jax.ShapeDtypeStruct((N_HEADS, nb_total, HEAD_DIM), F32),
        compiler_params=_cparams("parallel"),
        name="moba_block_mean",
    )(d_rows)


def _moba_prompt_kernel(q_ref, km_ref, k_ref, v_ref, o_ref, *state, tq, nb, sub, hb):
    i = pl.program_id(2)
    nh = tq // sub
    nchain = hb * nh
    m_refs = state[:nchain]
    acc_refs = state[nchain:]
    nbp = LANES

    blk_t = lax.broadcasted_iota(jnp.int32, (nb, tq), 0)
    sel_bias = []
    for hh in range(hb):
        gate_t = _dot_nt(km_ref[hh], q_ref[hh].astype(F32) * (1.0 / SCALE), precision=HIGHEST)
        gate_t = jnp.where(blk_t < i, gate_t, NEG)
        cnt = jnp.zeros((nb, tq), jnp.int32)
        for b2 in range(nb):
            rowv = gate_t[b2:b2 + 1, :]
            cnt = cnt + ((rowv > gate_t) | ((rowv == gate_t) & (b2 < blk_t))).astype(jnp.int32)
        sel_t = ((cnt < MOBA_TOPK) & (blk_t < i)).astype(F32)
        sel_t = jnp.concatenate([sel_t, jnp.zeros((nbp - nb, tq), F32)], axis=0)
        sel_bias.append(jnp.where(sel_t.T > 0.5, 0.0, NEG).astype(BF16))

    rows_of = lambda ci: slice((ci % nh) * sub, (ci % nh + 1) * sub)
    q_plain = [q_ref[ci // nh, rows_of(ci), :] for ci in range(nchain)]
    qs_sel = [jnp.concatenate([q_plain[ci], sel_bias[ci // nh][rows_of(ci), 0:HEAD_DIM]], axis=1)
              for ci in range(nchain)]
    qs_own = [jnp.concatenate([q, jnp.zeros((sub, HEAD_DIM), BF16)], axis=1) for q in q_plain]

    def update(qs, start, width, biases):
        ks = [k_ref[hh, pl.ds(start, width), :] for hh in range(hb)]
        v1s = [v_ref[hh, pl.ds(start, width), :] for hh in range(hb)]
        _flash_multi(qs, [ks[ci // nh] for ci in range(nchain)], [v1s[ci // nh] for ci in range(nchain)], biases,
                     m_refs, acc_refs)

    for ref in m_refs:
        ref[...] = jnp.full(ref.shape, NEG, F32)
    for ref in acc_refs:
        ref[...] = jnp.zeros_like(ref)

    r = lax.broadcasted_iota(jnp.int32, (sub, tq), 0)
    c = lax.broadcasted_iota(jnp.int32, (sub, tq), 1)
    update(qs_own, pl.multiple_of(i * tq, tq), tq,
           [jnp.where(c <= r + (ci % nh) * sub, 0.0, NEG) for ci in range(nchain)])

    pk = 2 * tq

    def pair_step(cc, carry):
        update(qs_sel, pl.multiple_of(cc * pk, pk), pk, [None] * nchain)
        return carry

    lax.fori_loop(0, i // 2, pair_step, 0)

    @pl.when(i % 2 == 1)
    def _():
        update(qs_sel, pl.multiple_of((i - 1) * tq, tq), tq, [None] * nchain)

    for ci in range(nchain):
        o_ref[ci // nh, (ci % nh) * sub:(ci % nh + 1) * sub, :] = _flash_result(acc_refs[ci][...]).astype(o_ref.dtype)


def _moba_prompt(qd_h, kmean_h, kd_h, vd_h, n_seq, s_len):
    tq = MOBA_BLOCK
    assert s_len % tq == 0
    nq = s_len // tq
    assert nq <= HEAD_DIM
    sub = 128
    hb = 4
    nchain = hb * (tq // sub)
    vspec = pl.BlockSpec((hb, s_len, LANES), lambda b, h, i: (h, b, 0))
    kspec = vspec
    qspec = pl.BlockSpec((hb, tq, HEAD_DIM), lambda b, h, i: (h, b * nq + i, 0))
    return pl.pallas_call(
        functools.partial(_moba_prompt_kernel, tq=tq, nb=nq, sub=sub, hb=hb),
        grid=(n_seq, N_HEADS // hb, nq),
        in_specs=[qspec, pl.BlockSpec((hb, nq, HEAD_DIM), lambda b, h, i: (h, b, 0)), kspec, vspec],
        out_specs=qspec,
        out_shape=jax.ShapeDtypeStruct(qd_h.shape, BF16),
        scratch_shapes=[pltpu.VMEM((sub, 1), F32)] * nchain + [pltpu.VMEM((sub, LANES), F32)] * nchain,
        compiler_params=_cparams("parallel", "parallel", "arbitrary"),
        name="moba_prompt",
    )(qd_h, kmean_h, kd_h, vd_h)


def _page_specs(n_slots, block, tail, base):
    def make(s):
        return lambda b, j, pt: (base + pt[b, j * n_slots + s],) + tuple(tail)

    return [pl.BlockSpec((None,) + tuple(block), make(s)) for s in range(n_slots)]


def _pages_t(cache):
    l, p, r, a, b, d = cache.shape
    return jnp.transpose(cache, (0, 1, 3, 4, 5, 2)).reshape(l * p, a, b * d, r)


def _head_diag(rows, width, per):
    r = lax.broadcasted_iota(jnp.int32, (rows, width), 0)
    c = lax.broadcasted_iota(jnp.int32, (rows, width), 1)
    return ((r % N_HEADS) * HEAD_DIM // per) == (c // HEAD_DIM * HEAD_DIM // per)


def _fold_heads(acc, t):
    kept = jnp.where(_head_diag(acc.shape[0], acc.shape[1], HEAD_DIM), acc, 0.0)
    return jnp.sum(kept.reshape(t, N_HEADS, acc.shape[1]), axis=1)


def _sb_sample_kernel(pt_ref, q_ref, kn_ref, vn_ref, pool_ref, o_ref, buf, sem, *, n_pages, t, pad, base):
    b = pl.program_id(0)
    rows = t * N_HEADS
    q = q_ref[0]

    def page_copy(p, slot):
        return pltpu.make_async_copy(pool_ref.at[base + pt_ref[b, p]], buf.at[slot], sem.at[slot])

    page_copy(n_pages - 1, 0).start()

    r = lax.broadcasted_iota(jnp.int32, (rows, pad), 0) // N_HEADS
    c = lax.broadcasted_iota(jnp.int32, (rows, pad), 1)
    carry, acc = _sb_tile(q, kn_ref[0], vn_ref[0], _tri_incl(pad), jnp.zeros((rows, 1), F32),
                          jnp.zeros((rows, HALF_MIX), F32), (c < r) & (c < t))
    t_incl = _tri_incl(PAGE_SIZE)

    def cond(s):
        p, carry, _ = s
        return jnp.logical_and(p >= 0, jnp.max(carry) > -SB_CUTOFF)

    def body(s):
        p, carry, acc = s
        slot = (n_pages - 1 - p) % 2
        page_copy(p, slot).wait()

        @pl.when(p > 0)
        def _():
            page_copy(p - 1, 1 - slot).start()

        carry, acc = _sb_tile(q, buf[slot, 0].astype(BF16), buf[slot, 1].astype(BF16), t_incl, carry, acc, None,
                              keys_on_lanes=True)
        return p - 1, carry, acc

    p_end, _, acc = lax.while_loop(cond, body, (jnp.int32(n_pages - 1), carry, acc))

    @pl.when(p_end >= 0)
    def _():
        page_copy(p_end, (n_pages - 1 - p_end) % 2).wait()

    o_ref[0] = _fold_heads(acc, t)


def _sb_sample(qbd, k_new, v_new, pool_t, base, page_table):
    n, rows, _ = qbd.shape
    t = rows // N_HEADS
    pad = k_new.shape[1]
    n_pages = page_table.shape[1]
    seq = lambda b, pt: (b, 0, 0)
    gs = pltpu.PrefetchScalarGridSpec(
        num_scalar_prefetch=1, grid=(n,),
        in_specs=[pl.BlockSpec((1, rows, HALF_MIX), seq), pl.BlockSpec((1, pad, HALF_MIX), seq),
                  pl.BlockSpec((1, pad, HALF_MIX), seq), pl.BlockSpec(memory_space=pl.ANY)],
        out_specs=pl.BlockSpec((1, t, HALF_MIX), seq),
        scratch_shapes=[pltpu.VMEM((2,) + pool_t.shape[1:], pool_t.dtype), pltpu.SemaphoreType.DMA((2,))])
    return pl.pallas_call(
        functools.partial(_sb_sample_kernel, n_pages=n_pages, t=t, pad=pad, base=base),
        grid_spec=gs,
        out_shape=jax.ShapeDtypeStruct((n, t, HALF_MIX), F32),
        compiler_params=_cparams("arbitrary"),
        name="sb_sample",
    )(page_table, qbd, k_new, v_new, pool_t)


def _nsa_sample_head_kernel(qraw_ref, qrot_ref, kcc_ref, vcc_ref, win_ref, wn_ref, oc_ref, ow_ref, sel_ref,
                            *, t, nc, n_past, nbp, pad):
    rows = t * N_HEADS
    wb = win_ref.shape[1]
    tok = lax.broadcasted_iota(jnp.int32, (rows, 1), 0) // N_HEADS
    qpos = n_past + tok

    def fold(x):
        g0 = (lax.broadcasted_iota(jnp.int32, (rows, HEAD_DIM), 0) % N_HEADS) < HPG_C
        return jnp.where(g0, x[:, 0:HEAD_DIM], x[:, HEAD_DIM:2 * HEAD_DIM])

    sc = _dot_nt(qraw_ref[0], kcc_ref[0])
    cidx = lax.broadcasted_iota(jnp.int32, (1, nc), 1)
    pc = _masked_softmax(sc, (cidx * CMP_STRIDE + CMP_LEN - 1 <= qpos) & (cidx < nc - 1))
    oc_ref[0] = fold(_dot(pc.astype(BF16), vcc_ref[0]))

    ng = t * N_KV_C
    gr = lax.broadcasted_iota(jnp.int32, (ng, rows), 0)
    rr = lax.broadcasted_iota(jnp.int32, (ng, rows), 1)
    same = (gr // N_KV_C == rr // N_HEADS) & (gr % N_KV_C == (rr % N_HEADS) // HPG_C)
    psum = _dot(same.astype(F32), pc, precision=HIGHEST)
    imp = _dot_nt(psum, _overlap_t(nbp, nc), precision=HIGHEST)
    blk = lax.broadcasted_iota(jnp.int32, (ng, nbp), 1)
    own = (n_past + lax.broadcasted_iota(jnp.int32, (ng, nbp), 0) // N_KV_C) // SLC_BLOCK
    imp = jnp.where(blk == own, BIG, jnp.where(blk < own, imp, NEG))
    sel_ref[0] = _top_select(imp, blk, N_SLC)

    qrot = qrot_ref[0]
    win = win_ref[0]
    wn = wn_ref[0]
    s1 = _dot_nt(qrot, win[:, 0:LANES].astype(BF16))
    wpos1 = n_past - wb + lax.broadcasted_iota(jnp.int32, (1, wb), 1)
    d1 = qpos - wpos1
    mk1 = (d1 >= 0) & (d1 <= WINDOW) & (wpos1 >= 0)
    s2 = _dot_nt(qrot, wn[:, 0:LANES])
    j2 = lax.broadcasted_iota(jnp.int32, (1, pad), 1)
    mk2 = (j2 <= tok) & (j2 < t)
    m = jnp.maximum(jnp.max(jnp.where(mk1, s1, NEG), axis=-1, keepdims=True),
                    jnp.max(jnp.where(mk2, s2, NEG), axis=-1, keepdims=True))
    p1 = jnp.where(mk1, jnp.exp(jnp.where(mk1, s1, NEG) - m), 0.0)
    p2 = jnp.where(mk2, jnp.exp(jnp.where(mk2, s2, NEG) - m), 0.0)
    den = jnp.sum(p1, axis=-1, keepdims=True) + jnp.sum(p2, axis=-1, keepdims=True)
    ow = _dot(p1.astype(BF16), win[:, LANES:2 * LANES].astype(BF16)) + _dot(p2.astype(BF16), wn[:, LANES:2 * LANES])
    ow_ref[0] = fold(ow / den)


def _nsa_sample_head(qraw_bd, qrot_bd, kcc_f, vcc_f, win, w_new, n_past, nbp):
    n, rows, _ = qraw_bd.shape
    t = rows // N_HEADS
    nc = kcc_f.shape[1]
    wb = win.shape[1]
    pad = w_new.shape[1]
    seq = lambda b: (b, 0, 0)
    return pl.pallas_call(
        functools.partial(_nsa_sample_head_kernel, t=t, nc=nc, n_past=n_past, nbp=nbp, pad=pad),
        grid=(n,),
        in_specs=[pl.BlockSpec((1, rows, LANES), seq), pl.BlockSpec((1, rows, LANES), seq),
                  pl.BlockSpec((1, nc, LANES), seq), pl.BlockSpec((1, nc, LANES), seq),
                  pl.BlockSpec((1, wb, 2 * LANES), seq), pl.BlockSpec((1, pad, 2 * LANES), seq)],
        out_specs=[pl.BlockSpec((1, rows, HEAD_DIM), seq), pl.BlockSpec((1, rows, HEAD_DIM), seq),
                   pl.BlockSpec((1, t * N_KV_C, nbp), seq)],
        out_shape=[jax.ShapeDtypeStruct((n, rows, HEAD_DIM), F32), jax.ShapeDtypeStruct((n, rows, HEAD_DIM), F32),
                   jax.ShapeDtypeStruct((n, t * N_KV_C, nbp), F32)],
        compiler_params=_cparams("parallel"),
        name="nsa_sample_head",
    )(qraw_bd, qrot_bd, kcc_f, vcc_f, win, w_new)


def _nsa_sample_slc_kernel(pt_ref, q_ref, sel_ref, sn_ref, oc_ref, ow_ref, gt_ref, *refs,
                           n_slots, n_steps, t, nbp, pad):
    page_refs = refs[:n_slots]
    o_ref, m_s, l_s, acc_s = refs[n_slots:]
    j = pl.program_id(1)
    rows = t * N_HEADS
    q = q_ref[0]
    tok = lax.broadcasted_iota(jnp.int32, (rows, 1), 0) // N_HEADS

    @pl.when(j == 0)
    def _():
        sn = sn_ref[0]
        j2 = lax.broadcasted_iota(jnp.int32, (1, pad), 1)
        m, l, acc = _online_step(_dot_nt(q, sn[:, 0:LANES]), (j2 <= tok) & (j2 < t), sn[:, LANES:2 * LANES],
                                 jnp.full((rows, 1), NEG, F32), jnp.zeros((rows, 1), F32),
                                 jnp.zeros((rows, LANES), F32))
        m_s[...] = m
        l_s[...] = l
        acc_s[...] = acc

    sel_bias = sel_ref[0]
    eb = lax.broadcasted_iota(jnp.int32, (nbp, PAGE_SIZE), 0)
    ej = lax.broadcasted_iota(jnp.int32, (nbp, PAGE_SIZE), 1)
    group = 16
    for s0 in range(0, n_slots, group):
        scores, vals = [], []
        for s in range(s0, min(s0 + group, n_slots)):
            first_blk = (j * n_slots + s) * (PAGE_SIZE // SLC_BLOCK)
            bias = _dot(sel_bias, (first_blk + ej // SLC_BLOCK == eb).astype(BF16))
            scores.append(_dot(q, page_refs[s][0].astype(BF16)) + bias)
            vals.append(page_refs[s][1].astype(BF16))
        _joint_update(m_s, l_s, acc_s, scores, vals, keys_on_lanes=True)

    @pl.when(j == n_steps - 1)
    def _():
        o = acc_s[...] / l_s[...]
        g0 = (lax.broadcasted_iota(jnp.int32, (rows, HEAD_DIM), 0) % N_HEADS) < HPG_C
        o_s = jnp.where(g0, o[:, 0:HEAD_DIM], o[:, HEAD_DIM:2 * HEAD_DIM])
        gt = gt_ref[0]
        o_ref[0] = oc_ref[0] * gt[:, 0:1] + o_s * gt[:, 1:2] + ow_ref[0] * gt[:, 2:3]


def _nsa_sample_slc(qrot_bd, sel_rows, s_new, o_c, o_w, gates, pool, base, page_table):
    n, rows, _ = qrot_bd.shape
    t = rows // N_HEADS
    nbp = sel_rows.shape[2]
    pad = s_new.shape[1]
    n_pages = page_table.shape[1]
    n_slots = _tile(n_pages, (16, 8, 4, 2))
    n_steps = n_pages // n_slots
    seq = lambda b, j, pt: (b, 0, 0)
    gs = pltpu.PrefetchScalarGridSpec(
        num_scalar_prefetch=1, grid=(n, n_steps),
        in_specs=[pl.BlockSpec((1, rows, LANES), seq), pl.BlockSpec((1, rows, nbp), seq),
                  pl.BlockSpec((1, pad, 2 * LANES), seq), pl.BlockSpec((1, rows, HEAD_DIM), seq),
                  pl.BlockSpec((1, rows, HEAD_DIM), seq), pl.BlockSpec((1, rows, 3), seq)]
        + _page_specs(n_slots, (2, LANES, PAGE_SIZE), (1, 0, 0), base),
        out_specs=pl.BlockSpec((1, rows, HEAD_DIM), seq),
        scratch_shapes=[pltpu.VMEM((rows, 1), F32), pltpu.VMEM((rows, 1), F32), pltpu.VMEM((rows, LANES), F32)])
    return pl.pallas_call(
        functools.partial(_nsa_sample_slc_kernel, n_slots=n_slots, n_steps=n_steps, t=t, nbp=nbp, pad=pad),
        grid_spec=gs,
        out_shape=jax.ShapeDtypeStruct((n, rows, HEAD_DIM), F32),
        compiler_params=_cparams("parallel", "arbitrary"),
        name="nsa_sample_slc",
    )(page_table, qrot_bd, sel_rows, s_new, o_c, o_w, gates, *([pool] * n_slots))


def _page_sum_kernel(pt_ref, *refs, n_slots):
    o_ref = refs[n_slots]
    j = pl.program_id(1)
    ppb = MOBA_BLOCK // PAGE_SIZE

    @pl.when(j == 0)
    def _():
        o_ref[...] = jnp.zeros_like(o_ref)

    lane = lax.broadcasted_iota(jnp.int32, o_ref.shape[1:], 1)
    acc = o_ref[0]
    for s in range(n_slots):
        tot = jnp.sum(refs[s][...], axis=1, keepdims=True)
        acc = acc + jnp.where(lane == (j * n_slots + s) // ppb, tot, 0.0)
    o_ref[0] = acc


def _page_sums(pool_t, base, page_table, nbp):
    n, n_pages = page_table.shape
    n_slots = _tile(n_pages, (16, 8, 4, 2))
    gs = pltpu.PrefetchScalarGridSpec(
        num_scalar_prefetch=1, grid=(n, n_pages // n_slots),
        in_specs=_page_specs(n_slots, (None, HALF_MIX, PAGE_SIZE), (0, 0, 0), base),
        out_specs=pl.BlockSpec((1, HALF_MIX, nbp), lambda b, j, pt: (b, 0, 0)))
    return pl.pallas_call(
        functools.partial(_page_sum_kernel, n_slots=n_slots),
        grid_spec=gs,
        out_shape=jax.ShapeDtypeStruct((n, HALF_MIX, nbp), F32),
        compiler_params=_cparams("parallel", "arbitrary"),
        name="moba_page_sums",
    )(page_table, *([pool_t] * n_slots))


def _moba_sample_kernel(pt_ref, q_ref, qf_ref, ks_ref, kn_ref, vn_ref, *refs, n_slots, n_steps, t, nbp, pad,
                        n_past):
    page_refs = refs[:n_slots]
    o_ref, sel_s, m_s, l_s, acc_s = refs[n_slots:]
    j = pl.program_id(1)
    rows = t * N_HEADS
    ppb = MOBA_BLOCK // PAGE_SIZE
    nb_full = n_past // MOBA_BLOCK
    q = q_ref[0]
    tok = lax.broadcasted_iota(jnp.int32, (rows, 1), 0) // N_HEADS

    @pl.when(j == 0)
    def _():
        gate = _dot(qf_ref[0], ks_ref[0] * (1.0 / MOBA_BLOCK), precision=HIGHEST)
        blk = lax.broadcasted_iota(jnp.int32, (rows, nbp), 1)
        own = (n_past + tok) // MOBA_BLOCK
        gate = jnp.where((blk < own) & (blk < nb_full), gate, NEG)
        sel_s[...] = jnp.where(_top_select(gate, blk, MOBA_TOPK) > 0.5, 0.0, NEG).astype(BF16)
        j2 = lax.broadcasted_iota(jnp.int32, (1, pad), 1)
        m, l, acc = _online_step(_dot_nt(q, kn_ref[0]), (j2 <= tok) & (j2 < t), vn_ref[0],
                                 jnp.full((rows, 1), NEG, F32), jnp.zeros((rows, 1), F32),
                                 jnp.zeros((rows, HALF_MIX), F32))
        m_s[...] = m
        l_s[...] = l
        acc_s[...] = acc

    sel_bias = sel_s[...]
    eb = lax.broadcasted_iota(jnp.int32, (nbp, PAGE_SIZE), 0)
    scores, vals = [], []
    for s in range(n_slots):
        blk_of_page = (j * n_slots + s) // ppb
        bias = _dot(sel_bias, (eb == blk_of_page).astype(BF16))
        scores.append(_dot(q, page_refs[s][0].astype(BF16)) + bias)
        vals.append(page_refs[s][1].astype(BF16))
    _joint_update(m_s, l_s, acc_s, scores, vals, keys_on_lanes=True)

    @pl.when(j == n_steps - 1)
    def _():
        o_ref[0] = _fold_heads(acc_s[...] / l_s[...], t)


def _moba_sample(qbd, qbd_f32, ksums, k_new, v_new, pool, base, page_table, n_past):
    n, rows, _ = qbd.shape
    t = rows // N_HEADS
    pad = k_new.shape[1]
    n_pages = page_table.shape[1]
    assert n_past % MOBA_BLOCK == 0 and t <= MOBA_BLOCK
    nbp = ksums.shape[2]
    n_slots = _tile(n_pages, (16, 8, 4, 2))
    n_steps = n_pages // n_slots
    seq = lambda b, j, pt: (b, 0, 0)
    gs = pltpu.PrefetchScalarGridSpec(
        num_scalar_prefetch=1, grid=(n, n_steps),
        in_specs=[pl.BlockSpec((1, rows, HALF_MIX), seq), pl.BlockSpec((1, rows, HALF_MIX), seq),
                  pl.BlockSpec((1, HALF_MIX, nbp), seq),
                  pl.BlockSpec((1, pad, HALF_MIX), seq), pl.BlockSpec((1, pad, HALF_MIX), seq)]
        + _page_specs(n_slots, (2, HALF_MIX, PAGE_SIZE), (0, 0, 0), base),
        out_specs=pl.BlockSpec((1, t, HALF_MIX), seq),
        scratch_shapes=[pltpu.VMEM((rows, nbp), BF16), pltpu.VMEM((rows, 1), F32), pltpu.VMEM((rows, 1), F32),
                        pltpu.VMEM((rows, HALF_MIX), F32)])
    return pl.pallas_call(
        functools.partial(_moba_sample_kernel, n_slots=n_slots, n_steps=n_steps, t=t, nbp=nbp, pad=pad,
                          n_past=n_past),
        grid_spec=gs,
        out_shape=jax.ShapeDtypeStruct((n, t, HALF_MIX), F32),
        compiler_params=_cparams("parallel", "arbitrary"),
        name="moba_sample",
    )(page_table, qbd, qbd_f32, ksums, k_new, v_new, *([pool] * n_slots))


def _block_diag(blocks):
    nb, bi, bj = blocks.shape
    eye = jnp.eye(nb, dtype=blocks.dtype)
    return jnp.einsum('bij,bc->bicj', blocks, eye).reshape(nb * bi, nb * bj)


def _rope_tables(pos):
    half = HEAD_DIM // 2
    inv_freq = ROPE_THETA ** (-jnp.arange(half, dtype=F32) / half)
    ang = pos.astype(F32)[:, None] * inv_freq[None, :]
    cos = jnp.cos(ang)
    sin = jnp.sin(ang)
    return jnp.concatenate([cos, cos, cos, cos], axis=1), jnp.concatenate([-sin, sin, -sin, sin], axis=1)


def _queries_bd(q, per, scale, dtype):
    n, t, _ = q.shape
    groups = N_HEADS // per
    qh = q.reshape(n, t, N_HEADS, 1, HEAD_DIM) * scale
    pick = (jnp.arange(N_HEADS)[:, None] // per == jnp.arange(groups)[None, :]).astype(q.dtype)
    return (qh * pick[None, None, :, :, None]).reshape(n, t * N_HEADS, groups * HEAD_DIM).astype(dtype)


def _pad_rows(x, pad):
    n, t, w = x.shape
    return jnp.concatenate([x, jnp.zeros((n, pad - t, w), x.dtype)], axis=1)


def _odd_weight(w_in):
    d = w_in.shape[0]
    gc0 = _O_VW + 128
    main = jnp.concatenate([w_in[:, :gc0], w_in[:, gc0 + 3 * N_HEADS:]], axis=1)
    gc = w_in[:, gc0:gc0 + 3 * N_HEADS].reshape(d, N_KV_C, HPG_C, 3).transpose(0, 1, 3, 2).reshape(d, N_KV_C, 3 * HPG_C)
    gc = jnp.concatenate([gc, jnp.zeros((d, N_KV_C, LANES - 3 * HPG_C), w_in.dtype)], axis=2).reshape(d, N_KV_C * LANES)
    return jnp.concatenate([main, gc], axis=1).astype(BF16)


def kernel(x_prompt, x_sample, cache_a_kv, state_b_h, state_b_conv, cache_c_kv, state_c_win, cache_d_kv, page_table, norm_ffn1, w_ffn1_gate, w_ffn1_up, w_ffn1_down, norm_mix, norm_ffn2, w_ffn2_gate, w_ffn2_up, w_ffn2_down, w_in_even, w_out_even, lru_conv_w, lru_conv_b, lru_w_r, lru_b_r, lru_w_i, lru_b_i, lru_lambda, w_in_odd, w_out_odd, cmp_pe_k, cmp_w1_k, cmp_w2_k, cmp_pe_v, cmp_w1_v, cmp_w2_v, norm_final):
    nb, s_len, d = x_prompt.shape
    ns, t = x_sample.shape[:2]
    depth = norm_mix.shape[0]
    n_pages = page_table.shape[1]
    n_past = n_pages * PAGE_SIZE
    n_pool = cache_a_kv.shape[1]
    hm = HALF_MIX
    pad = 16
    assert t <= pad and n_past % SLC_BLOCK == 0

    yp = x_prompt.reshape(nb * s_len, d)
    ys = x_sample.reshape(ns * t, d)
    cos_p, sin_p = _rope_tables(jnp.arange(s_len, dtype=jnp.int32))
    cos_s, sin_s = _rope_tables(jnp.tile(n_past + jnp.arange(t, dtype=jnp.int32), ns))
    pool_a = _pages_t(cache_a_kv)
    pool_c = _pages_t(cache_c_kv)
    pool_d = _pages_t(cache_d_kv)

    outs = {k: [] for k in ("a_p", "a_s", "bh_p", "bh_s", "bc_p", "bc_s", "c_p", "c_s", "cw_p", "cw_s", "d_p", "d_s")}
    for layer in range(depth):
        li = layer // 2
        last = layer == depth - 1
        f1 = (norm_ffn1[layer], w_ffn1_gate[layer].astype(BF16), w_ffn1_up[layer].astype(BF16),
              w_ffn1_down[layer].astype(BF16))
        f2 = (norm_ffn2[layer], w_ffn2_gate[layer].astype(BF16), w_ffn2_up[layer].astype(BF16),
              w_ffn2_down[layer].astype(BF16))
        yp = _half_ffn(yp, *f1)
        ys = _half_ffn(ys, *f1)
        if layer % 2 == 0:
            w_in = w_in_even[li].astype(BF16)
            w_out = w_out_even[li].astype(BF16)
            lru = (lru_conv_w[li], lru_conv_b[li], _block_diag(lru_w_r[li]).astype(BF16), lru_b_r[li],
                   _block_diag(lru_w_i[li]).astype(BF16), lru_b_i[li], lru_lambda[li])
            kv, xb, gb, qh, kh, vh = _proj_even(yp, norm_mix[layer], w_in, True)
            o_a = _sb_prompt(qh, kh, vh, nb, s_len)
            g, h_new, buf_new = _rg_lru(xb.reshape(nb, s_len, hm), gb.reshape(nb, s_len, hm),
                                        jnp.zeros((nb, hm), F32), jnp.zeros((nb, CONV_WIDTH - 1, hm), F32), *lru)
            yp = _outproj(yp, o_a, g.reshape(nb * s_len, hm), w_out)
            outs["a_p"].append(kv.reshape(nb, s_len, 2, N_HEADS, HEAD_DIM))
            outs["bh_p"].append(h_new)
            outs["bc_p"].append(buf_new)
            kv, xb, gb, q = _proj_even(ys, norm_mix[layer], w_in, False)
            kv3 = kv.reshape(ns, t, 2 * hm)
            o_a = _sb_sample(_queries_bd(q.reshape(ns, t, hm), 1, SCALE, BF16),
                             _pad_rows(kv3[:, :, :hm], pad).astype(BF16), _pad_rows(kv3[:, :, hm:], pad).astype(BF16),
                             pool_a, li * n_pool, page_table)
            g, h_new, buf_new = _rg_lru(xb.reshape(ns, t, hm), gb.reshape(ns, t, hm), state_b_h[li],
                                        state_b_conv[li], *lru)
            ys = _outproj(ys, o_a.reshape(ns * t, hm).astype(BF16), g.reshape(ns * t, hm), w_out)
            outs["a_s"].append(kv.reshape(ns, t, 2, N_HEADS, HEAD_DIM))
            outs["bh_s"].append(h_new)
            outs["bc_s"].append(buf_new)
        else:
            w_in = _odd_weight(w_in_odd[li])
            w_out = w_out_odd[li].astype(BF16)
            cmp_k = _cmp_weights(cmp_pe_k[li], cmp_w1_k[li])
            cmp_v = _cmp_weights(cmp_pe_v[li], cmp_w1_v[li])
            w2k = _block_diag(jnp.stack([cmp_w2_k[li]] * N_KV_C)).astype(BF16)
            w2v = _block_diag(jnp.stack([cmp_w2_v[li]] * N_KV_C)).astype(BF16)
            (c_rows, w_rows, d_rows, gates, qraw_h, qrot_h, ks_h, vs_h, kw_h, vw_h, qd_h, kd_h, vd_h) = _proj_odd(
                yp, norm_mix[layer], w_in, cos_p, sin_p, True)
            n_chunk = s_len // CMP_STRIDE
            rows_c = _tile(s_len, (2048, 1024, 512, 256, 128))
            ab = _cmp_ab([c_rows], [pl.BlockSpec((rows_c, LANES), lambda b, j: (b * (s_len // rows_c) + j, 0))],
                         [pl.BlockSpec((rows_c, LANES), lambda b, j: (b * (s_len // rows_c) + j, 1))],
                         (nb, s_len // rows_c), rows_c // CMP_STRIDE, nb, n_chunk, rows_c, cmp_k, cmp_v, 0, ())
            _, _, kcc_g, vcc_g = _cmp_finish(ab, w2k, w2v)
            o_c = _nsa_prompt(qraw_h, qrot_h, gates, kcc_g, vcc_g, ks_h, vs_h, kw_h, vw_h, nb, s_len)
            o_d = _moba_prompt(qd_h, _block_mean(d_rows), kd_h, vd_h, nb, s_len)
            yp = _outproj(yp, o_c, o_d, w_out)
            wlen = min(WINDOW, s_len)
            outs["c_p"].append(c_rows.reshape(nb, s_len, 4, N_KV_C, HEAD_DIM))
            outs["cw_p"].append(w_rows.reshape(nb, s_len, 2, N_KV_C, HEAD_DIM)[:, s_len - wlen:])
            outs["d_p"].append(d_rows.reshape(nb, s_len, 2, N_HEADS, HEAD_DIM))
            c_rows, w_rows, d_rows, gates, qraw, qrot, qd = _proj_odd(ys, norm_mix[layer], w_in, cos_s, sin_s, False)
            slots_c = _tile(n_pages, (16, 8))
            ab = _cmp_ab([pool_c] * slots_c, _page_specs(slots_c, (None, LANES, PAGE_SIZE), (0, 0, 0), li * n_pool),
                         _page_specs(slots_c, (None, LANES, PAGE_SIZE), (1, 0, 0), li * n_pool),
                         (ns, n_pages // slots_c), slots_c * PAGE_SIZE // CMP_STRIDE, ns, n_past // CMP_STRIDE,
                         PAGE_SIZE, cmp_k, cmp_v, 1, (page_table,), pages_t=True)
            kcc_f, vcc_f, _, _ = _cmp_finish(ab, w2k, w2v)
            nsb = -(-(n_past + t) // SLC_BLOCK)
            nbp = LANES * (-(-nsb // LANES))
            win_l = state_c_win[li]
            wb = win_l.shape[1]
            qraw_bd = _queries_bd(qraw.reshape(ns, t, hm), HPG_C, SCALE, BF16)
            qrot_bd = _queries_bd(qrot.reshape(ns, t, hm), HPG_C, SCALE, BF16)
            c3 = c_rows.reshape(ns, t, 4 * LANES)
            w3 = w_rows.reshape(ns, t, 2 * LANES)
            d3 = d_rows.reshape(ns, t, 2 * hm)
            o_cmp, o_win, sel = _nsa_sample_head(qraw_bd, qrot_bd, kcc_f, vcc_f, win_l.reshape(ns, wb, 2 * LANES),
                                                 _pad_rows(w3, pad).astype(BF16), n_past, nbp)
            sel_rows = jnp.repeat(sel.reshape(ns, t, N_KV_C, nbp), HPG_C, axis=2).reshape(ns, t * N_HEADS, nbp)
            gt = gates.reshape(ns, t, N_KV_C, LANES)[..., :3 * HPG_C].reshape(ns, t, N_KV_C, 3, HPG_C)
            gt = gt.transpose(0, 1, 2, 4, 3).reshape(ns, t * N_HEADS, 3)
            sel_bias = jnp.where(sel_rows > 0.5, 0.0, NEG).astype(BF16)
            o_c = _nsa_sample_slc(qrot_bd, sel_bias, _pad_rows(c3[:, :, 2 * LANES:], pad).astype(BF16),
                                  o_cmp, o_win, gt, pool_c, li * n_pool, page_table)
            ksums = _page_sums(pool_d, li * n_pool, page_table, LANES * (-(-(n_past // MOBA_BLOCK) // LANES)))
            qd3 = qd.reshape(ns, t, hm)
            o_d = _moba_sample(_queries_bd(qd3, 1, SCALE, BF16), _queries_bd(qd3, 1, 1.0, F32), ksums,
                               _pad_rows(d3[:, :, :hm], pad).astype(BF16), _pad_rows(d3[:, :, hm:], pad).astype(BF16),
                               pool_d, li * n_pool, page_table, n_past)
            ys = _outproj(ys, o_c.reshape(ns * t, hm).astype(BF16), o_d.reshape(ns * t, hm).astype(BF16), w_out)
            w5 = w_rows.reshape(ns, t, 2, N_KV_C, HEAD_DIM)
            outs["c_s"].append(c_rows.reshape(ns, t, 4, N_KV_C, HEAD_DIM))
            outs["cw_s"].append(jnp.concatenate([win_l, w5], axis=1)[:, -wb:])
            outs["d_s"].append(d_rows.reshape(ns, t, 2, N_HEADS, HEAD_DIM))
        gf = norm_final if last else None
        yp = _half_ffn(yp, *f2, g_final=gf)
        ys = _half_ffn(ys, *f2, g_final=gf)
    st = lambda k: jnp.stack(outs[k])
    return (yp.reshape(nb, s_len, d), ys.reshape(ns, t, d), st("a_p"), st("a_s"), st("bh_p"), st("bh_s"),
            st("bc_p"), st("bc_s"), st("c_p"), st("c_s"), st("cw_p"), st("cw_s"), st("d_p"), st("d_s"))
```

```python
import functools
import math

import jax
import jax.numpy as jnp
import numpy as np
from jax import lax
from jax.experimental import pallas as pl
from jax.experimental.pallas import tpu as pltpu

F32 = jnp.float32
BF16 = jnp.bfloat16

HEAD_DIM = 64
HALF_MIX = 512
N_HEADS = HALF_MIX // HEAD_DIM
N_KV_C = 2
HPG_C = N_HEADS // N_KV_C
PAGE_SIZE = 128
CONV_WIDTH = 4
LRU_C = 8.0
CMP_LEN = 32
CMP_STRIDE = 16
SLC_BLOCK = 64
N_SLC = 16
WINDOW = 512
MOBA_BLOCK = 256
MOBA_TOPK = 3
ROPE_THETA = 10000.0
RMS_EPS = 1e-6
NEG = -1e30
BIG = 1e30
SCALE = HEAD_DIM ** -0.5

LANES = 128
VMEM_LIMIT_BYTES = 56 * 1024 * 1024
SB_CUTOFF = 120.0
HIGHEST = lax.Precision.HIGHEST


def _cparams(*sem):
    return pltpu.CompilerParams(dimension_semantics=sem, vmem_limit_bytes=VMEM_LIMIT_BYTES)


def _tile(n, prefs):
    for t in prefs:
        if n % t == 0:
            return t
    return n


def _dot(a, b, precision=None):
    return jnp.dot(a, b, preferred_element_type=F32, precision=precision)


def _dot_nt(a, b, precision=None):
    return lax.dot_general(a, b, (((1,), (1,)), ((), ())), preferred_element_type=F32, precision=precision)


def _rms(x, g):
    ms = jnp.mean(x * x, axis=-1, keepdims=True)
    return x * lax.rsqrt(ms + RMS_EPS) * g


def _softplus(z):
    return jnp.maximum(z, 0.0) + jnp.log(1.0 + jnp.exp(-jnp.abs(z)))


def _one_minus_exp(x):
    poly = x
    for k in range(7, 1, -1):
        poly = x * (1.0 + poly * (1.0 / k))
    return jnp.where(x > -0.125, -poly, 1.0 - jnp.exp(x))


def _split_dot(x, t):
    hi = x.astype(BF16)
    lo = (x - hi.astype(F32)).astype(BF16)
    return _dot(hi, t) + _dot(lo, t)


def _masked_softmax(s, mask):
    sm = jnp.where(mask, s, NEG)
    m = jnp.max(sm, axis=-1, keepdims=True)
    e = jnp.where(mask, jnp.exp(sm - m), 0.0)
    d = jnp.sum(e, axis=-1, keepdims=True)
    return e / jnp.maximum(d, 1e-30)


def _online_step(s, mask, v, m, l, acc):
    sm = jnp.where(mask, s, NEG)
    m_new = jnp.maximum(m, jnp.max(sm, axis=-1, keepdims=True))
    alpha = jnp.exp(m - m_new)
    p = jnp.where(mask, jnp.exp(sm - m_new), 0.0)
    l = alpha * l + jnp.sum(p, axis=-1, keepdims=True)
    rows = acc.shape[0]
    acc = alpha.reshape(rows, 1) * acc + _dot(p.reshape(rows, p.shape[-1]).astype(BF16), v)
    return m_new, l, acc


def _flash_multi(qs, ks, v1s, biases, m_refs, acc_refs, kt=False):
    n = len(qs)
    ss = [(_dot(qs[i], ks[i]) if kt else _dot_nt(qs[i], ks[i])) for i in range(n)]
    ss = [s if b is None else s + b for s, b in zip(ss, biases)]
    m_old = [ref[...] for ref in m_refs]
    m_new = [jnp.maximum(m_old[i], jnp.max(ss[i], axis=-1, keepdims=True)) for i in range(n)]
    ps = [jnp.exp(ss[i] - m_new[i]).astype(BF16) for i in range(n)]
    pvs = [_dot(ps[i], v1s[i]) for i in range(n)]
    for i in range(n):
        acc_refs[i][...] = jnp.exp(m_old[i] - m_new[i]) * acc_refs[i][...] + pvs[i]
        m_refs[i][...] = m_new[i]


def _joint_update(m_s, l_s, acc_s, scores, vals, keys_on_lanes=False):
    m_old = m_s[...]
    m_new = m_old
    for sc in scores:
        m_new = jnp.maximum(m_new, jnp.max(sc, axis=-1, keepdims=True))
    alpha = jnp.exp(m_old - m_new)
    l = alpha * l_s[...]
    acc = alpha * acc_s[...]
    for sc, v in zip(scores, vals):
        p = jnp.exp(sc - m_new)
        l = l + jnp.sum(p, axis=-1, keepdims=True)
        acc = acc + (_dot_nt(p.astype(BF16), v) if keys_on_lanes else _dot(p.astype(BF16), v))
    m_s[...] = m_new
    l_s[...] = l
    acc_s[...] = acc


def _flash_result(acc):
    return acc[:, 0:HEAD_DIM] / acc[:, HEAD_DIM:HEAD_DIM + 1]


def _top_select(val, idx, n):
    sel = jnp.zeros(val.shape, F32)
    big_i = jnp.int32(2 ** 30)
    for _ in range(n):
        m = jnp.max(val, axis=-1, keepdims=True)
        first = jnp.min(jnp.where(val == m, idx, big_i), axis=-1, keepdims=True)
        hit = idx == first
        sel = jnp.where(hit & (m > 0.5 * NEG), 1.0, sel)
        val = jnp.where(hit, -jnp.inf, val)
    return sel


def _ffn_kernel(*refs, n_f, post_norm):
    if post_norm:
        x_ref, g_ref, wg_ref, wu_ref, wd_ref, gf_ref, o_ref, xn_ref, acc_ref = refs
    else:
        x_ref, g_ref, wg_ref, wu_ref, wd_ref, o_ref, xn_ref, acc_ref = refs
    j = pl.program_id(1)

    @pl.when(j == 0)
    def _():
        xn_ref[...] = _rms(x_ref[...], g_ref[...]).astype(BF16)
        acc_ref[...] = jnp.zeros_like(acc_ref)

    xn = xn_ref[...]
    hg = _dot(xn, wg_ref[...])
    hu = _dot(xn, wu_ref[...])
    a = (hg * jax.nn.sigmoid(hg) * hu).astype(BF16)
    acc_ref[...] += _dot(a, wd_ref[...])

    @pl.when(j == n_f - 1)
    def _():
        y = x_ref[...] + 0.5 * acc_ref[...]
        if post_norm:
            y = _rms(y, gf_ref[...])
        o_ref[...] = y


def _half_ffn(x, g, wg, wu, wd, g_final=None):
    m, d = x.shape
    f = wg.shape[1]
    tm = _tile(m, (1024, 512, 256, 128))
    tf = _tile(f, (256, 128))
    n_f = f // tf
    post = g_final is not None
    in_specs = [pl.BlockSpec((tm, d), lambda i, j: (i, 0)),
                pl.BlockSpec((1, d), lambda i, j: (0, 0)),
                pl.BlockSpec((d, tf), lambda i, j: (0, j)),
                pl.BlockSpec((d, tf), lambda i, j: (0, j)),
                pl.BlockSpec((tf, d), lambda i, j: (j, 0))]
    args = [x, g.reshape(1, d), wg, wu, wd]
    if post:
        in_specs.append(pl.BlockSpec((1, d), lambda i, j: (0, 0)))
        args.append(g_final.reshape(1, d))
    return pl.pallas_call(
        functools.partial(_ffn_kernel, n_f=n_f, post_norm=post),
        grid=(m // tm, n_f),
        in_specs=in_specs,
        out_specs=pl.BlockSpec((tm, d), lambda i, j: (i, 0)),
        out_shape=jax.ShapeDtypeStruct((m, d), F32),
        scratch_shapes=[pltpu.VMEM((tm, d), BF16), pltpu.VMEM((tm, d), F32)],
        compiler_params=_cparams("parallel", "arbitrary"),
        name="half_ffn",
    )(*args)


def _rope(seg, cos_t, sin_t):
    w = seg.shape[1]
    reps = w // LANES
    c = jnp.concatenate([cos_t] * reps, axis=1) if reps > 1 else cos_t
    s = jnp.concatenate([sin_t] * reps, axis=1) if reps > 1 else sin_t
    lane = lax.broadcasted_iota(jnp.int32, seg.shape, 1)
    first = (lane % HEAD_DIM) < (HEAD_DIM // 2)
    rot = jnp.where(first, pltpu.roll(seg, w - HEAD_DIM // 2, 1), pltpu.roll(seg, HEAD_DIM // 2, 1))
    return seg * c + rot * s


def _store_heads(ref, seg, scale=None, tail=None):
    for h in range(seg.shape[1] // HEAD_DIM):
        piece = seg[:, h * HEAD_DIM:(h + 1) * HEAD_DIM]
        if scale is not None:
            piece = piece * scale
        if tail is not None:
            piece = jnp.concatenate([piece, tail], axis=1)
        ref[h] = piece.astype(ref.dtype)


def _proj_even_kernel(x_ref, g_ref, w_ref, kv_ref, xb_ref, gb_ref, *outs, heads):
    hm = HALF_MIX
    xn = _rms(x_ref[...], g_ref[...]).astype(BF16)
    y = _dot(xn, w_ref[...])
    kv_ref[...] = y[:, hm:3 * hm]
    xb_ref[...] = y[:, 3 * hm:4 * hm]
    gb_ref[...] = y[:, 4 * hm:5 * hm]
    if heads:
        qh_ref, kh_ref, vh_ref = outs
        _store_heads(qh_ref, y[:, 0:hm], SCALE)
        _store_heads(kh_ref, y[:, hm:2 * hm])
        _store_heads(vh_ref, y[:, 2 * hm:3 * hm])
    else:
        (q_ref,) = outs
        q_ref[...] = y[:, 0:hm]


def _proj_even(x, g, w, heads):
    m, d = x.shape
    n = w.shape[1]
    hm = HALF_MIX
    tm = _tile(m, (512, 256, 128))
    row = lambda i: (i, 0)
    out_shape = [jax.ShapeDtypeStruct((m, 2 * hm), F32), jax.ShapeDtypeStruct((m, hm), F32),
                 jax.ShapeDtypeStruct((m, hm), F32)]
    out_specs = [pl.BlockSpec((tm, 2 * hm), row), pl.BlockSpec((tm, hm), row), pl.BlockSpec((tm, hm), row)]
    if heads:
        for _ in range(3):
            out_shape.append(jax.ShapeDtypeStruct((N_HEADS, m, HEAD_DIM), BF16))
            out_specs.append(pl.BlockSpec((N_HEADS, tm, HEAD_DIM), lambda i: (0, i, 0)))
    else:
        out_shape.append(jax.ShapeDtypeStruct((m, hm), F32))
        out_specs.append(pl.BlockSpec((tm, hm), row))
    return pl.pallas_call(
        functools.partial(_proj_even_kernel, heads=heads),
        grid=(m // tm,),
        in_specs=[pl.BlockSpec((tm, d), row), pl.BlockSpec((1, d), lambda i: (0, 0)),
                  pl.BlockSpec((d, n), lambda i: (0, 0))],
        out_specs=out_specs, out_shape=out_shape,
        compiler_params=_cparams("parallel"),
        name="proj_even",
    )(x, g.reshape(1, d), w)


_O_QC, _O_KC, _O_VC, _O_KS, _O_VS, _O_KW, _O_VW, _O_QD, _O_KD, _O_VD, _O_GT, _O_END = (
    0, 512, 640, 768, 896, 1024, 1152, 1280, 1792, 2304, 2816, 3072)


def _proj_odd_kernel(x_ref, g_ref, w_ref, cos_ref, sin_ref, c_ref, w_out_ref, d_ref, gt_ref, *outs, heads, tm,
                     period):
    xn = _rms(x_ref[...], g_ref[...]).astype(BF16)
    y = _dot(xn, w_ref[...])
    cos_t = cos_ref[...]
    sin_t = sin_ref[...]
    qc = y[:, _O_QC:_O_KC]
    qc_rot = _rope(qc, cos_t, sin_t)
    ks_rot = _rope(y[:, _O_KS:_O_VS], cos_t, sin_t)
    kw_rot = _rope(y[:, _O_KW:_O_VW], cos_t, sin_t)
    qd_rot = _rope(y[:, _O_QD:_O_KD], cos_t, sin_t)
    kd_rot = _rope(y[:, _O_KD:_O_VD], cos_t, sin_t)
    vs = y[:, _O_VS:_O_KW]
    vw = y[:, _O_VW:_O_QD]
    vd = y[:, _O_VD:_O_GT]
    c_ref[:, 0:256] = y[:, _O_KC:_O_KS]
    c_ref[:, 256:384] = ks_rot
    c_ref[:, 384:512] = vs
    w_out_ref[:, 0:128] = kw_rot
    w_out_ref[:, 128:256] = vw
    d_ref[:, 0:512] = kd_rot
    d_ref[:, 512:1024] = vd
    gt_ref[...] = jax.nn.sigmoid(y[:, _O_GT:_O_END])
    if heads:
        qraw_h, qrot_h, ks_h, vs_h, kw_h, vw_h, qd_h, kd_h, vd_h = outs
        pos = (pl.program_id(0) % period) * tm + lax.broadcasted_iota(jnp.int32, (tm, HEAD_DIM), 0)
        lane = lax.broadcasted_iota(jnp.int32, (tm, HEAD_DIM), 1)
        ones = (lane == 0).astype(F32)
        _store_heads(qraw_h, qc, SCALE)
        _store_heads(qrot_h, qc_rot, SCALE)
        _store_heads(ks_h, ks_rot, tail=(lane == pos // SLC_BLOCK).astype(F32))
        _store_heads(vs_h, vs, tail=ones)
        _store_heads(kw_h, kw_rot)
        _store_heads(vw_h, vw, tail=ones)
        _store_heads(qd_h, qd_rot, SCALE)
        _store_heads(kd_h, kd_rot, tail=(lane == pos // MOBA_BLOCK).astype(F32))
        _store_heads(vd_h, vd, tail=ones)
    else:
        qraw_ref, qrot_ref, qd_ref = outs
        qraw_ref[...] = qc
        qrot_ref[...] = qc_rot
        qd_ref[...] = qd_rot


def _proj_odd(x, g, w, cos_t, sin_t, heads):
    m, d = x.shape
    n = w.shape[1]
    tm = _tile(m, (512, 256, 128))
    period = cos_t.shape[0] // tm
    row = lambda i: (i, 0)
    tab = lambda i: (i % period, 0)
    out_shape = [jax.ShapeDtypeStruct((m, 512), F32), jax.ShapeDtypeStruct((m, 256), F32),
                 jax.ShapeDtypeStruct((m, 1024), F32), jax.ShapeDtypeStruct((m, 256), F32)]
    out_specs = [pl.BlockSpec((tm, 512), row), pl.BlockSpec((tm, 256), row), pl.BlockSpec((tm, 1024), row),
                 pl.BlockSpec((tm, 256), row)]
    if heads:
        wide = (False, False, True, True, False, True, False, True, True)
        for nh, wd in zip((N_HEADS, N_HEADS, N_KV_C, N_KV_C, N_KV_C, N_KV_C, N_HEADS, N_HEADS, N_HEADS), wide):
            width = LANES if wd else HEAD_DIM
            out_shape.append(jax.ShapeDtypeStruct((nh, m, width), BF16))
            out_specs.append(pl.BlockSpec((nh, tm, width), lambda i: (0, i, 0)))
    else:
        for _ in range(3):
            out_shape.append(jax.ShapeDtypeStruct((m, HALF_MIX), F32))
            out_specs.append(pl.BlockSpec((tm, HALF_MIX), row))
    return pl.pallas_call(
        functools.partial(_proj_odd_kernel, heads=heads, tm=tm, period=period),
        grid=(m // tm,),
        in_specs=[pl.BlockSpec((tm, d), row), pl.BlockSpec((1, d), lambda i: (0, 0)),
                  pl.BlockSpec((d, n), lambda i: (0, 0)),
                  pl.BlockSpec((tm, LANES), tab), pl.BlockSpec((tm, LANES), tab)],
        out_specs=out_specs, out_shape=out_shape,
        compiler_params=_cparams("parallel"),
        name="proj_odd",
    )(x, g.reshape(1, d), w, cos_t, sin_t)


def _outproj_kernel(x_ref, a_ref, b_ref, w_ref, o_ref, *, heads):
    hm = HALF_MIX
    if heads:
        a = jnp.concatenate([a_ref[h] for h in range(N_HEADS)], axis=1)
        b = b_ref[...] if len(b_ref.shape) == 2 else jnp.concatenate([b_ref[h] for h in range(N_HEADS)], axis=1)
    else:
        a = a_ref[...]
        b = b_ref[...]
    o_ref[...] = x_ref[...] + _dot(a, w_ref[0:hm, :]) + _dot(b, w_ref[hm:2 * hm, :])


def _outproj(x, a, b, w):
    m, d = x.shape
    tm = _tile(m, (512, 256, 128))
    row = lambda i: (i, 0)

    def spec(t):
        if t.ndim == 2:
            return pl.BlockSpec((tm, t.shape[1]), row)
        return pl.BlockSpec((t.shape[0], tm, t.shape[2]), lambda i: (0, i, 0))

    return pl.pallas_call(
        functools.partial(_outproj_kernel, heads=a.ndim == 3),
        grid=(m // tm,),
        in_specs=[pl.BlockSpec((tm, d), row), spec(a), spec(b), pl.BlockSpec(w.shape, lambda i: (0, 0))],
        out_specs=pl.BlockSpec((tm, d), row),
        out_shape=jax.ShapeDtypeStruct((m, d), F32),
        compiler_params=_cparams("parallel"),
        name="outproj",
    )(x, a, b, w)


def _sb_tile(q, k, v, t_incl, carry, acc, mask, keys_on_lanes=False):
    z = _dot(q, k) if keys_on_lanes else _dot_nt(q, k)
    lk = -_softplus(z)
    if mask is not None:
        lk = jnp.where(mask, lk, 0.0)
    incl = _split_dot(lk, t_incl)
    w = jnp.exp(z + incl + carry)
    if mask is not None:
        w = jnp.where(mask, w, 0.0)
    wb = w.astype(BF16)
    acc = acc + (_dot_nt(wb, v) if keys_on_lanes else _dot(wb, v))
    return carry + incl[:, 0:1], acc


def _tri_incl(n):
    r = lax.broadcasted_iota(jnp.int32, (n, n), 0)
    c = lax.broadcasted_iota(jnp.int32, (n, n), 1)
    return (r >= c).astype(BF16)


def _sb_multi(qs, ks, vs, t_incl, c_refs, a_refs, mask):
    n = len(qs)
    zs = [_dot_nt(qs[i], ks[i]) for i in range(n)]
    lks = [-_softplus(z) for z in zs]
    if mask is not None:
        lks = [jnp.where(mask, lk, 0.0) for lk in lks]
    incls = [_split_dot(lk, t_incl) for lk in lks]
    carries = [ref[...] for ref in c_refs]
    ws = [jnp.exp(zs[i] + incls[i] + carries[i]) for i in range(n)]
    if mask is not None:
        ws = [jnp.where(mask, w, 0.0) for w in ws]
    pvs = [_dot(ws[i].astype(BF16), vs[i]) for i in range(n)]
    alive = None
    for i in range(n):
        carry = carries[i] + incls[i][:, 0:1]
        c_refs[i][...] = carry
        a_refs[i][...] = a_refs[i][...] + pvs[i]
        top = jnp.max(carry)
        alive = top if alive is None else jnp.maximum(alive, top)
    return alive


def _sb_prompt_kernel(q_ref, k_ref, v_ref, o_ref, *state, tq, hb):
    i = pl.program_id(2)
    c_refs = state[:hb]
    a_refs = state[hb:]
    t_incl = _tri_incl(tq)
    r = lax.broadcasted_iota(jnp.int32, (tq, tq), 0)
    c = lax.broadcasted_iota(jnp.int32, (tq, tq), 1)
    for ref in state:
        ref[...] = jnp.zeros_like(ref)
    qs = [q_ref[h] for h in range(hb)]

    def tile(kt, mask):
        start = pl.multiple_of(kt * tq, tq)
        return _sb_multi(qs, [k_ref[h, pl.ds(start, tq), :] for h in range(hb)],
                         [v_ref[h, pl.ds(start, tq), :] for h in range(hb)], t_incl, c_refs, a_refs, mask)

    def cond(s):
        kt, alive = s
        return jnp.logical_and(kt >= 0, alive > -SB_CUTOFF)

    def body(s):
        kt, _ = s
        return kt - 1, tile(kt, None)

    lax.while_loop(cond, body, (i - 1, tile(i, c < r)))
    for h in range(hb):
        o_ref[h] = a_refs[h][...].astype(o_ref.dtype)


def _sb_prompt(qh, kh, vh, n_seq, s_len):
    tq = _tile(s_len, (256, 128))
    nq = s_len // tq
    hb = 4
    return pl.pallas_call(
        functools.partial(_sb_prompt_kernel, tq=tq, hb=hb),
        grid=(n_seq, N_HEADS // hb, nq),
        in_specs=[pl.BlockSpec((hb, tq, HEAD_DIM), lambda b, h, i: (h, b * nq + i, 0)),
                  pl.BlockSpec((hb, s_len, HEAD_DIM), lambda b, h, i: (h, b, 0)),
                  pl.BlockSpec((hb, s_len, HEAD_DIM), lambda b, h, i: (h, b, 0))],
        out_specs=pl.BlockSpec((hb, tq, HEAD_DIM), lambda b, h, i: (h, b * nq + i, 0)),
        out_shape=jax.ShapeDtypeStruct(qh.shape, BF16),
        scratch_shapes=[pltpu.VMEM((tq, 1), F32)] * hb + [pltpu.VMEM((tq, HEAD_DIM), F32)] * hb,
        compiler_params=_cparams("parallel", "parallel", "arbitrary"),
        name="sb_prompt",
    )(qh, kh, vh)


def _lru_kernel(xb_ref, gb_ref, h0_ref, buf0_ref, cw_ref, cb_ref, wr_ref, br_ref, wi_ref, bi_ref, lam_ref,
                g_ref, hl_ref, bn_ref, xpad, a_s, u_s, hs_s, h_s, *, tc, n_t):
    j = pl.program_id(1)
    keep = CONV_WIDTH - 1

    @pl.when(j == 0)
    def _():
        xpad[8 - keep:8, :] = buf0_ref[0]
        h_s[...] = h0_ref[0]

    @pl.when(j > 0)
    def _():
        xpad[8 - keep:8, :] = xpad[8 + tc - keep:8 + tc, :]

    x = xb_ref[0]
    xpad[8:8 + tc, :] = x
    xc = cb_ref[...] + x * cw_ref[keep:keep + 1, :]
    for d in range(1, CONV_WIDTH):
        xc = xc + xpad[8 - d:8 - d + tc, :] * cw_ref[keep - d:keep - d + 1, :]
    xcb = xc.astype(BF16)
    r = jax.nn.sigmoid(_dot(xcb, wr_ref[...]) + br_ref[...])
    gi = jax.nn.sigmoid(_dot(xcb, wi_ref[...]) + bi_ref[...])
    log_a = -LRU_C * r * _softplus(-lam_ref[...])
    a_s[...] = jnp.exp(log_a)
    u_s[...] = jnp.sqrt(_one_minus_exp(2.0 * log_a)) * (gi * xc)

    def step(t, h):
        h = a_s[pl.ds(t, 1), :] * h + u_s[pl.ds(t, 1), :]
        hs_s[pl.ds(t, 1), :] = h
        return h

    h = lax.fori_loop(0, tc, step, h_s[...], unroll=min(tc, 8))
    h_s[...] = h
    g_ref[0] = (jax.nn.gelu(gb_ref[0]) * hs_s[...]).astype(g_ref.dtype)

    @pl.when(j == n_t - 1)
    def _():
        hl_ref[0] = h
        bn_ref[0] = xpad[8 + tc - keep:8 + tc, :]


def _rg_lru(xb, gb, h0, buf0, conv_w, conv_b, wr_bd, b_r, wi_bd, b_i, lam):
    n, t, w = xb.shape
    keep = CONV_WIDTH - 1
    assert t >= keep
    tc = _tile(t, (512, 256, 128))
    n_t = t // tc
    seq = lambda b, j: (b, j, 0)
    one = lambda b, j: (b, 0, 0)
    const = lambda b, j: (0, 0)
    vec = pl.BlockSpec((1, w), const)
    g, hl, bn = pl.pallas_call(
        functools.partial(_lru_kernel, tc=tc, n_t=n_t),
        grid=(n, n_t),
        in_specs=[pl.BlockSpec((1, tc, w), seq), pl.BlockSpec((1, tc, w), seq),
                  pl.BlockSpec((1, 1, w), one), pl.BlockSpec((1, keep, w), one),
                  pl.BlockSpec((CONV_WIDTH, w), const), vec,
                  pl.BlockSpec((w, w), const), vec, pl.BlockSpec((w, w), const), vec, vec],
        out_specs=[pl.BlockSpec((1, tc, w), seq), pl.BlockSpec((1, 1, w), one), pl.BlockSpec((1, keep, w), one)],
        out_shape=[jax.ShapeDtypeStruct((n, t, w), BF16), jax.ShapeDtypeStruct((n, 1, w), F32),
                   jax.ShapeDtypeStruct((n, keep, w), F32)],
        scratch_shapes=[pltpu.VMEM((tc + 8, w), F32), pltpu.VMEM((tc, w), F32), pltpu.VMEM((tc, w), F32),
                        pltpu.VMEM((tc, w), F32), pltpu.VMEM((1, w), F32)],
        compiler_params=_cparams("parallel", "arbitrary"),
        name="rg_lru",
    )(xb, gb, h0.reshape(n, 1, w), buf0, conv_w, conv_b.reshape(1, w), wr_bd, b_r.reshape(1, w), wi_bd,
      b_i.reshape(1, w), lam.reshape(1, w))
    return g, hl.reshape(n, w), bn


def _cmp_ab_kernel(*refs, n_in, rows, n_prefetch, pages_t):
    refs = refs[n_prefetch:]
    k_refs = refs[:n_in]
    v_refs = refs[n_in:2 * n_in]
    pe_ak, pe_bk, pe_av, pe_bv, w_ak, w_bk, w_av, w_bv, o_ref = refs[2 * n_in:2 * n_in + 9]
    nch = rows // CMP_STRIDE
    if pages_t:
        xk_s, xv_s = refs[2 * n_in + 9:]
        for s in range(n_in):
            xk_s[s * rows:(s + 1) * rows, :] = k_refs[s][...].T
            xv_s[s * rows:(s + 1) * rows, :] = v_refs[s][...].T
        k_refs, v_refs, nch = [xk_s], [xv_s], n_in * nch

    def chunks(in_refs):
        per = [jnp.concatenate([r[pl.ds(p, nch, stride=CMP_STRIDE), :] for p in range(CMP_STRIDE)], axis=1)
               for r in in_refs]
        return per[0] if len(in_refs) == 1 else jnp.concatenate(per, axis=0)

    xk = chunks(k_refs)
    xv = chunks(v_refs)
    o_ref[0, :, 0:128] = _dot((xk + pe_ak[...]).astype(BF16), w_ak[...])
    o_ref[0, :, 128:256] = _dot((xk + pe_bk[...]).astype(BF16), w_bk[...])
    o_ref[0, :, 256:384] = _dot((xv + pe_av[...]).astype(BF16), w_av[...])
    o_ref[0, :, 384:512] = _dot((xv + pe_bv[...]).astype(BF16), w_bv[...])


def _cmp_weights(pe, w1):
    hid = w1.shape[-1]
    eye = jnp.eye(N_KV_C, dtype=F32)
    out = []
    for half in range(CMP_LEN // CMP_STRIDE):
        sl = slice(half * CMP_STRIDE, (half + 1) * CMP_STRIDE)
        pe_row = jnp.broadcast_to(pe[sl][:, None, :], (CMP_STRIDE, N_KV_C, HEAD_DIM)).reshape(1, -1)
        wbd = jnp.einsum('pdh,gk->pgdkh', w1[sl], eye).reshape(CMP_STRIDE * N_KV_C * HEAD_DIM, N_KV_C * hid)
        out.append((pe_row, wbd.astype(BF16)))
    return out


def _cmp_ab(in_arrays, k_specs, v_specs, grid, out_rows, n_seq, n_chunks, rows, cmp_k, cmp_v, num_prefetch,
            prefetch, pages_t=False):
    in_specs = list(k_specs) + list(v_specs)
    (pe_ak, w_ak), (pe_bk, w_bk) = cmp_k
    (pe_av, w_av), (pe_bv, w_bv) = cmp_v
    nd = len(grid)
    const = lambda *a: (0, 0)
    pes = [pe_ak, pe_bk, pe_av, pe_bv]
    ws = [w_ak, w_bk, w_av, w_bv]
    specs = list(in_specs) + [pl.BlockSpec(p.shape, const) for p in pes] + [pl.BlockSpec(w.shape, const) for w in ws]
    gs = pltpu.PrefetchScalarGridSpec(
        num_scalar_prefetch=num_prefetch, grid=grid, in_specs=specs,
        out_specs=pl.BlockSpec((1, out_rows, 512), lambda b, j, *a: (b, j, 0)),
        scratch_shapes=[pltpu.VMEM((len(in_arrays) * rows, LANES), F32)] * 2 if pages_t else [])
    return pl.pallas_call(
        functools.partial(_cmp_ab_kernel, n_in=len(in_arrays), rows=rows, n_prefetch=num_prefetch, pages_t=pages_t),
        grid_spec=gs,
        out_shape=jax.ShapeDtypeStruct((n_seq, n_chunks, 512), F32),
        compiler_params=_cparams(*(["parallel"] + ["arbitrary"] * (nd - 1))),
        name="cmp_ab",
    )(*prefetch, *in_arrays, *in_arrays, *pes, *ws)


def _cmp_finish_kernel(ab_ref, w2k_ref, w2v_ref, kf_ref, vf_ref, kg_ref, vg_ref, *, nc):
    n = nc - 1
    hk = jax.nn.gelu(ab_ref[0, 0:n, 0:128] + ab_ref[0, 1:nc, 128:256]).astype(BF16)
    hv = jax.nn.gelu(ab_ref[0, 0:n, 256:384] + ab_ref[0, 1:nc, 384:512]).astype(BF16)
    kc = _dot(hk, w2k_ref[...]).astype(BF16)
    vc = _dot(hv, w2v_ref[...]).astype(BF16)
    zero = jnp.zeros((1, LANES), BF16)
    for ref_f, ref_g, val in ((kf_ref, kg_ref, kc), (vf_ref, vg_ref, vc)):
        full = jnp.concatenate([val, zero], axis=0)
        ref_f[0] = full
        for g in range(N_KV_C):
            ref_g[0, g] = full[:, g * HEAD_DIM:(g + 1) * HEAD_DIM]


def _cmp_finish(ab, w2k_bd, w2v_bd):
    n, nc, _ = ab.shape
    flat = jax.ShapeDtypeStruct((n, nc, LANES), BF16)
    grp = jax.ShapeDtypeStruct((n, N_KV_C, nc, HEAD_DIM), BF16)
    fspec = pl.BlockSpec((1, nc, LANES), lambda b: (b, 0, 0))
    gspec = pl.BlockSpec((1, N_KV_C, nc, HEAD_DIM), lambda b: (b, 0, 0, 0))
    return pl.pallas_call(
        functools.partial(_cmp_finish_kernel, nc=nc),
        grid=(n,),
        in_specs=[pl.BlockSpec((1, nc, 512), lambda b: (b, 0, 0)),
                  pl.BlockSpec((LANES, LANES), lambda b: (0, 0)), pl.BlockSpec((LANES, LANES), lambda b: (0, 0))],
        out_specs=[fspec, fspec, gspec, gspec], out_shape=[flat, flat, grp, grp],
        compiler_params=_cparams("parallel"),
        name="cmp_finish",
    )(ab, w2k_bd, w2v_bd)


def _overlap_t(nb, nc):
    b = lax.broadcasted_iota(jnp.int32, (nb, nc), 0)
    c = lax.broadcasted_iota(jnp.int32, (nb, nc), 1)
    return ((c * CMP_STRIDE < (b + 1) * SLC_BLOCK) & (c * CMP_STRIDE + CMP_LEN > b * SLC_BLOCK)).astype(F32)


def _nsa_prompt_kernel(qraw_ref, qrot_ref, gt_ref, kcc_ref, vcc_ref, ks_ref, vs_ref, kw_ref, vw_ref, o_ref,
                       *state, tq, nc, nsb, ck):
    i = pl.program_id(1)
    q0 = i * tq
    rows = HPG_C * tq
    nbp = LANES
    qpos3 = q0 + lax.broadcasted_iota(jnp.int32, (1, tq, 1), 1)

    def compressed_and_selection(g):
        qraw = qraw_ref[g * HPG_C:(g + 1) * HPG_C].reshape(rows, HEAD_DIM)
        sc = _dot_nt(qraw, kcc_ref[0, g]).reshape(HPG_C, tq, nc)
        cidx = lax.broadcasted_iota(jnp.int32, (1, 1, nc), 2)
        mask_c = (cidx * CMP_STRIDE + CMP_LEN - 1 <= qpos3) & (cidx < nc - 1)
        pc = _masked_softmax(sc, mask_c)
        o_c = _dot(pc.reshape(rows, nc).astype(BF16), vcc_ref[0, g])

        psum = jnp.sum(pc, axis=0)
        imp_t = _dot_nt(_overlap_t(nsb, nc), psum, precision=HIGHEST)
        blk = lax.broadcasted_iota(jnp.int32, (nsb, tq), 0)
        own = (q0 + lax.broadcasted_iota(jnp.int32, (nsb, tq), 1)) // SLC_BLOCK
        imp_t = jnp.where(blk == own, BIG, jnp.where(blk < own, imp_t, NEG))
        cnt = jnp.zeros((nsb, tq), jnp.int32)
        for b2 in range(nsb):
            rowv = imp_t[b2:b2 + 1, :]
            ahead = (rowv > imp_t) | ((rowv == imp_t) & (b2 < blk))
            cnt = cnt + ahead.astype(jnp.int32)
        sel_t = ((cnt < N_SLC) & (blk <= own)).astype(F32)
        if nsb < nbp:
            sel_t = jnp.concatenate([sel_t, jnp.zeros((nbp - nsb, tq), F32)], axis=0)
        return o_c, jnp.where(sel_t.T > 0.5, 0.0, NEG).astype(BF16)

    per_group = [compressed_and_selection(g) for g in range(N_KV_C)]
    nh = N_HEADS
    m_refs = state[:2 * nh]
    acc_refs = state[2 * nh:]
    qs = [qrot_ref[h] for h in range(nh)]
    qs_sel = [jnp.concatenate([qs[h], per_group[h // HPG_C][1][:, 0:HEAD_DIM]], axis=1) for h in range(nh)]
    grp = [h // HPG_C for h in range(nh)]
    for ref in m_refs:
        ref[...] = jnp.full(ref.shape, NEG, F32)
    for ref in acc_refs:
        ref[...] = jnp.zeros_like(ref)

    c_own = q0 // ck
    start = pl.multiple_of(c_own * ck, ck)
    row = lax.broadcasted_iota(jnp.int32, (tq, ck), 0)
    lane = lax.broadcasted_iota(jnp.int32, (tq, ck), 1)
    bias_s = jnp.where(start + lane <= q0 + row, 0.0, NEG)
    wk = WINDOW + tq
    start_w = pl.multiple_of(jnp.maximum(q0 - WINDOW, 0), tq)
    dist = (q0 + lax.broadcasted_iota(jnp.int32, (tq, wk), 0)) - (start_w + lax.broadcasted_iota(jnp.int32, (tq, wk), 1))
    bias_w = jnp.where((dist >= 0) & (dist <= WINDOW), 0.0, NEG)
    ks_own = [ks_ref[g, pl.ds(start, ck), :] for g in range(N_KV_C)]
    vs_own = [vs_ref[g, pl.ds(start, ck), :] for g in range(N_KV_C)]
    kw_all = [kw_ref[g, pl.ds(start_w, wk), :] for g in range(N_KV_C)]
    vw_all = [vw_ref[g, pl.ds(start_w, wk), :] for g in range(N_KV_C)]
    _flash_multi(qs_sel + qs,
                 [ks_own[g] for g in grp] + [kw_all[g] for g in grp],
                 [vs_own[g] for g in grp] + [vw_all[g] for g in grp],
                 [bias_s] * nh + [bias_w] * nh, m_refs, acc_refs)

    def slc_step(cc, carry):
        st = pl.multiple_of(cc * ck, ck)
        ks_c = [ks_ref[g, pl.ds(st, ck), :] for g in range(N_KV_C)]
        vs_c = [vs_ref[g, pl.ds(st, ck), :] for g in range(N_KV_C)]
        _flash_multi(qs_sel, [ks_c[g] for g in grp], [vs_c[g] for g in grp], [None] * nh, m_refs[:nh], acc_refs[:nh])
        return carry

    lax.fori_loop(0, c_own, slc_step, 0)

    for h in range(nh):
        g, j = h // HPG_C, h % HPG_C
        gt = gt_ref[:, g * LANES:(g + 1) * LANES]
        g_c = gt[:, j:j + 1]
        g_s = gt[:, HPG_C + j:HPG_C + j + 1]
        g_w = gt[:, 2 * HPG_C + j:2 * HPG_C + j + 1]
        o_c = per_group[g][0][j * tq:(j + 1) * tq]
        o_ref[h] = (o_c * g_c + _flash_result(acc_refs[h][...]) * g_s
                    + _flash_result(acc_refs[nh + h][...]) * g_w).astype(o_ref.dtype)


def _nsa_prompt(qraw_h, qrot_h, gates, kcc_g, vcc_g, ks_h, vs_h, kw_h, vw_h, n_seq, s_len):
    tq = 128
    nq = s_len // tq
    nc = kcc_g.shape[2]
    nsb = -(-s_len // SLC_BLOCK)
    ck = _tile(s_len, (512, 256, 128))
    assert s_len % tq == 0 and nsb <= HEAD_DIM and WINDOW % tq == 0 and s_len >= WINDOW + tq and ck % tq == 0
    qspec = pl.BlockSpec((N_HEADS, tq, HEAD_DIM), lambda b, i: (0, b * nq + i, 0))
    cspec = pl.BlockSpec((1, N_KV_C, nc, HEAD_DIM), lambda b, i: (b, 0, 0, 0))
    kspec = pl.BlockSpec((N_KV_C, s_len, HEAD_DIM), lambda b, i: (0, b, 0))
    vspec = pl.BlockSpec((N_KV_C, s_len, LANES), lambda b, i: (0, b, 0))
    return pl.pallas_call(
        functools.partial(_nsa_prompt_kernel, tq=tq, nc=nc, nsb=nsb, ck=ck),
        grid=(n_seq, nq),
        in_specs=[qspec, qspec, pl.BlockSpec((tq, N_KV_C * LANES), lambda b, i: (b * nq + i, 0)),
                  cspec, cspec, vspec, vspec, kspec, vspec],
        out_specs=qspec,
        out_shape=jax.ShapeDtypeStruct(qraw_h.shape, BF16),
        scratch_shapes=[pltpu.VMEM((tq, 1), F32)] * (2 * N_HEADS) + [pltpu.VMEM((tq, LANES), F32)] * (2 * N_HEADS),
        compiler_params=_cparams("parallel", "arbitrary"),
        name="nsa_prompt",
    )(qraw_h, qrot_h, gates, kcc_g, vcc_g, ks_h, vs_h, kw_h, vw_h)


def _block_mean_kernel(x_ref, o_ref, *, nb):
    x = x_ref[...]
    m = jnp.mean(x.reshape(nb, MOBA_BLOCK, x.shape[1]), axis=1)
    for h in range(N_HEADS):
        o_ref[h] = m[:, h * HEAD_DIM:(h + 1) * HEAD_DIM]


def _block_mean(d_rows):
    m = d_rows.shape[0]
    nb_total = m // MOBA_BLOCK
    nb = _tile(nb_total, (8,))
    return pl.pallas_call(
        functools.partial(_block_mean_kernel, nb=nb),
        grid=(nb_total // nb,),
        in_specs=[pl.BlockSpec((nb * MOBA_BLOCK, HALF_MIX), lambda i: (i, 0))],
        out_specs=pl.BlockSpec((N_HEADS, nb, HEAD_DIM), lambda i: (0, i, 0)),
        out_shape=jax.ShapeDtypeStruct((N_HEADS, nb_total, HEAD_DIM), F32),
        compiler_params=_cparams("parallel"),
        name="moba_block_mean",
    )(d_rows)


def _moba_prompt_kernel(q_ref, km_ref, k_ref, v_ref, o_ref, *state, tq, nb, sub, hb):
    i = pl.program_id(2)
    nh = tq // sub
    nchain = hb * nh
    m_refs = state[:nchain]
    acc_refs = state[nchain:]
    nbp = LANES

    blk_t = lax.broadcasted_iota(jnp.int32, (nb, tq), 0)
    sel_bias = []
    for hh in range(hb):
        gate_t = _dot_nt(km_ref[hh], q_ref[hh].astype(F32) * (1.0 / SCALE), precision=HIGHEST)
        gate_t = jnp.where(blk_t < i, gate_t, NEG)
        cnt = jnp.zeros((nb, tq), jnp.int32)
        for b2 in range(nb):
            rowv = gate_t[b2:b2 + 1, :]
            cnt = cnt + ((rowv > gate_t) | ((rowv == gate_t) & (b2 < blk_t))).astype(jnp.int32)
        sel_t = ((cnt < MOBA_TOPK) & (blk_t < i)).astype(F32)
        sel_t = jnp.concatenate([sel_t, jnp.zeros((nbp - nb, tq), F32)], axis=0)
        sel_bias.append(jnp.where(sel_t.T > 0.5, 0.0, NEG).astype(BF16))

    rows_of = lambda ci: slice((ci % nh) * sub, (ci % nh + 1) * sub)
    q_plain = [q_ref[ci // nh, rows_of(ci), :] for ci in range(nchain)]
    qs_sel = [jnp.concatenate([q_plain[ci], sel_bias[ci // nh][rows_of(ci), 0:HEAD_DIM]], axis=1)
              for ci in range(nchain)]
    qs_own = [jnp.concatenate([q, jnp.zeros((sub, HEAD_DIM), BF16)], axis=1) for q in q_plain]

    def update(qs, start, width, biases):
        ks = [k_ref[hh, :, pl.ds(start, width)] for hh in range(hb)]
        v1s = [v_ref[hh, pl.ds(start, width), :] for hh in range(hb)]
        _flash_multi(qs, [ks[ci // nh] for ci in range(nchain)], [v1s[ci // nh] for ci in range(nchain)], biases,
                     m_refs, acc_refs, kt=True)

    for ref in m_refs:
        ref[...] = jnp.full(ref.shape, NEG, F32)
    for ref in acc_refs:
        ref[...] = jnp.zeros_like(ref)

    r = lax.broadcasted_iota(jnp.int32, (sub, tq), 0)
    c = lax.broadcasted_iota(jnp.int32, (sub, tq), 1)
    update(qs_own, pl.multiple_of(i * tq, tq), tq,
           [jnp.where(c <= r + (ci % nh) * sub, 0.0, NEG) for ci in range(nchain)])

    pk = 2 * tq

    def pair_step(cc, carry):
        update(qs_sel, pl.multiple_of(cc * pk, pk), pk, [None] * nchain)
        return carry

    lax.fori_loop(0, i // 2, pair_step, 0)

    @pl.when(i % 2 == 1)
    def _():
        update(qs_sel, pl.multiple_of((i - 1) * tq, tq), tq, [None] * nchain)

    for ci in range(nchain):
        o_ref[ci // nh, (ci % nh) * sub:(ci % nh + 1) * sub, :] = _flash_result(acc_refs[ci][...]).astype(o_ref.dtype)


def _moba_prompt(qd_h, kmean_h, kd_h, vd_h, n_seq, s_len):
    tq = MOBA_BLOCK
    assert s_len % tq == 0
    nq = s_len // tq
    assert nq <= HEAD_DIM
    sub = 128
    hb = 4
    nchain = hb * (tq // sub)
    vspec = pl.BlockSpec((hb, s_len, LANES), lambda b, h, i: (h, b, 0))
    kspec = pl.BlockSpec((hb, LANES, s_len), lambda b, h, i: (h, 0, b))
    qspec = pl.BlockSpec((hb, tq, HEAD_DIM), lambda b, h, i: (h, b * nq + i, 0))
    return pl.pallas_call(
        functools.partial(_moba_prompt_kernel, tq=tq, nb=nq, sub=sub, hb=hb),
        grid=(n_seq, N_HEADS // hb, nq),
        in_specs=[qspec, pl.BlockSpec((hb, nq, HEAD_DIM), lambda b, h, i: (h, b, 0)), kspec, vspec],
        out_specs=qspec,
        out_shape=jax.ShapeDtypeStruct(qd_h.shape, BF16),
        scratch_shapes=[pltpu.VMEM((sub, 1), F32)] * nchain + [pltpu.VMEM((sub, LANES), F32)] * nchain,
        compiler_params=_cparams("parallel", "parallel", "arbitrary"),
        name="moba_prompt",
    )(qd_h, kmean_h, jnp.swapaxes(kd_h, 1, 2), vd_h)


def _page_specs(n_slots, block, tail, base):
    def make(s):
        return lambda b, j, pt: (base + pt[b, j * n_slots + s],) + tuple(tail)

    return [pl.BlockSpec((None,) + tuple(block), make(s)) for s in range(n_slots)]


def _pages_t(cache):
    l, p, r, a, b, d = cache.shape
    return jnp.transpose(cache, (0, 1, 3, 4, 5, 2)).reshape(l * p, a, b * d, r)


def _head_diag(rows, width, per):
    r = lax.broadcasted_iota(jnp.int32, (rows, width), 0)
    c = lax.broadcasted_iota(jnp.int32, (rows, width), 1)
    return ((r % N_HEADS) * HEAD_DIM // per) == (c // HEAD_DIM * HEAD_DIM // per)


def _fold_heads(acc, t):
    kept = jnp.where(_head_diag(acc.shape[0], acc.shape[1], HEAD_DIM), acc, 0.0)
    return jnp.sum(kept.reshape(t, N_HEADS, acc.shape[1]), axis=1)


def _sb_sample_kernel(pt_ref, q_ref, kn_ref, vn_ref, pool_ref, o_ref, buf, sem, *, n_pages, t, pad, base):
    b = pl.program_id(0)
    rows = t * N_HEADS
    q = q_ref[0]

    def page_copy(p, slot):
        return pltpu.make_async_copy(pool_ref.at[base + pt_ref[b, p]], buf.at[slot], sem.at[slot])

    page_copy(n_pages - 1, 0).start()

    r = lax.broadcasted_iota(jnp.int32, (rows, pad), 0) // N_HEADS
    c = lax.broadcasted_iota(jnp.int32, (rows, pad), 1)
    carry, acc = _sb_tile(q, kn_ref[0], vn_ref[0], _tri_incl(pad), jnp.zeros((rows, 1), F32),
                          jnp.zeros((rows, HALF_MIX), F32), (c < r) & (c < t))
    t_incl = _tri_incl(PAGE_SIZE)

    def cond(s):
        p, carry, _ = s
        return jnp.logical_and(p >= 0, jnp.max(carry) > -SB_CUTOFF)

    def body(s):
        p, carry, acc = s
        slot = (n_pages - 1 - p) % 2
        page_copy(p, slot).wait()

        @pl.when(p > 0)
        def _():
            page_copy(p - 1, 1 - slot).start()

        carry, acc = _sb_tile(q, buf[slot, 0].astype(BF16), buf[slot, 1].astype(BF16), t_incl, carry, acc, None,
                              keys_on_lanes=True)
        return p - 1, carry, acc

    p_end, _, acc = lax.while_loop(cond, body, (jnp.int32(n_pages - 1), carry, acc))

    @pl.when(p_end >= 0)
    def _():
        page_copy(p_end, (n_pages - 1 - p_end) % 2).wait()

    o_ref[0] = _fold_heads(acc, t)


def _sb_sample(qbd, k_new, v_new, pool_t, base, page_table):
    n, rows, _ = qbd.shape
    t = rows // N_HEADS
    pad = k_new.shape[1]
    n_pages = page_table.shape[1]
    seq = lambda b, pt: (b, 0, 0)
    gs = pltpu.PrefetchScalarGridSpec(
        num_scalar_prefetch=1, grid=(n,),
        in_specs=[pl.BlockSpec((1, rows, HALF_MIX), seq), pl.BlockSpec((1, pad, HALF_MIX), seq),
                  pl.BlockSpec((1, pad, HALF_MIX), seq), pl.BlockSpec(memory_space=pl.ANY)],
        out_specs=pl.BlockSpec((1, t, HALF_MIX), seq),
        scratch_shapes=[pltpu.VMEM((2,) + pool_t.shape[1:], pool_t.dtype), pltpu.SemaphoreType.DMA((2,))])
    return pl.pallas_call(
        functools.partial(_sb_sample_kernel, n_pages=n_pages, t=t, pad=pad, base=base),
        grid_spec=gs,
        out_shape=jax.ShapeDtypeStruct((n, t, HALF_MIX), F32),
        compiler_params=_cparams("arbitrary"),
        name="sb_sample",
    )(page_table, qbd, k_new, v_new, pool_t)


def _nsa_sample_head_kernel(qraw_ref, qrot_ref, kcc_ref, vcc_ref, win_ref, wn_ref, oc_ref, ow_ref, sel_ref,
                            *, t, nc, n_past, nbp, pad):
    rows = t * N_HEADS
    wb = win_ref.shape[1]
    tok = lax.broadcasted_iota(jnp.int32, (rows, 1), 0) // N_HEADS
    qpos = n_past + tok

    def fold(x):
        g0 = (lax.broadcasted_iota(jnp.int32, (rows, HEAD_DIM), 0) % N_HEADS) < HPG_C
        return jnp.where(g0, x[:, 0:HEAD_DIM], x[:, HEAD_DIM:2 * HEAD_DIM])

    sc = _dot_nt(qraw_ref[0], kcc_ref[0])
    cidx = lax.broadcasted_iota(jnp.int32, (1, nc), 1)
    pc = _masked_softmax(sc, (cidx * CMP_STRIDE + CMP_LEN - 1 <= qpos) & (cidx < nc - 1))
    oc_ref[0] = fold(_dot(pc.astype(BF16), vcc_ref[0]))

    ng = t * N_KV_C
    gr = lax.broadcasted_iota(jnp.int32, (ng, rows), 0)
    rr = lax.broadcasted_iota(jnp.int32, (ng, rows), 1)
    same = (gr // N_KV_C == rr // N_HEADS) & (gr % N_KV_C == (rr % N_HEADS) // HPG_C)
    psum = _dot(same.astype(F32), pc, precision=HIGHEST)
    imp = _dot_nt(psum, _overlap_t(nbp, nc), precision=HIGHEST)
    blk = lax.broadcasted_iota(jnp.int32, (ng, nbp), 1)
    own = (n_past + lax.broadcasted_iota(jnp.int32, (ng, nbp), 0) // N_KV_C) // SLC_BLOCK
    imp = jnp.where(blk == own, BIG, jnp.where(blk < own, imp, NEG))
    sel_ref[0] = _top_select(imp, blk, N_SLC)

    qrot = qrot_ref[0]
    win = win_ref[0]
    wn = wn_ref[0]
    s1 = _dot_nt(qrot, win[:, 0:LANES].astype(BF16))
    wpos1 = n_past - wb + lax.broadcasted_iota(jnp.int32, (1, wb), 1)
    d1 = qpos - wpos1
    mk1 = (d1 >= 0) & (d1 <= WINDOW) & (wpos1 >= 0)
    s2 = _dot_nt(qrot, wn[:, 0:LANES])
    j2 = lax.broadcasted_iota(jnp.int32, (1, pad), 1)
    mk2 = (j2 <= tok) & (j2 < t)
    m = jnp.maximum(jnp.max(jnp.where(mk1, s1, NEG), axis=-1, keepdims=True),
                    jnp.max(jnp.where(mk2, s2, NEG), axis=-1, keepdims=True))
    p1 = jnp.where(mk1, jnp.exp(jnp.where(mk1, s1, NEG) - m), 0.0)
    p2 = jnp.where(mk2, jnp.exp(jnp.where(mk2, s2, NEG) - m), 0.0)
    den = jnp.sum(p1, axis=-1, keepdims=True) + jnp.sum(p2, axis=-1, keepdims=True)
    ow = _dot(p1.astype(BF16), win[:, LANES:2 * LANES].astype(BF16)) + _dot(p2.astype(BF16), wn[:, LANES:2 * LANES])
    ow_ref[0] = fold(ow / den)


def _nsa_sample_head(qraw_bd, qrot_bd, kcc_f, vcc_f, win, w_new, n_past, nbp):
    n, rows, _ = qraw_bd.shape
    t = rows // N_HEADS
    nc = kcc_f.shape[1]
    wb = win.shape[1]
    pad = w_new.shape[1]
    seq = lambda b: (b, 0, 0)
    return pl.pallas_call(
        functools.partial(_nsa_sample_head_kernel, t=t, nc=nc, n_past=n_past, nbp=nbp, pad=pad),
        grid=(n,),
        in_specs=[pl.BlockSpec((1, rows, LANES), seq), pl.BlockSpec((1, rows, LANES), seq),
                  pl.BlockSpec((1, nc, LANES), seq), pl.BlockSpec((1, nc, LANES), seq),
                  pl.BlockSpec((1, wb, 2 * LANES), seq), pl.BlockSpec((1, pad, 2 * LANES), seq)],
        out_specs=[pl.BlockSpec((1, rows, HEAD_DIM), seq), pl.BlockSpec((1, rows, HEAD_DIM), seq),
                   pl.BlockSpec((1, t * N_KV_C, nbp), seq)],
        out_shape=[jax.ShapeDtypeStruct((n, rows, HEAD_DIM), F32), jax.ShapeDtypeStruct((n, rows, HEAD_DIM), F32),
                   jax.ShapeDtypeStruct((n, t * N_KV_C, nbp), F32)],
        compiler_params=_cparams("parallel"),
        name="nsa_sample_head",
    )(qraw_bd, qrot_bd, kcc_f, vcc_f, win, w_new)


def _nsa_sample_slc_kernel(pt_ref, q_ref, sel_ref, sn_ref, oc_ref, ow_ref, gt_ref, *refs,
                           n_slots, n_steps, t, nbp, pad):
    page_refs = refs[:n_slots]
    o_ref, m_s, l_s, acc_s = refs[n_slots:]
    j = pl.program_id(1)
    rows = t * N_HEADS
    q = q_ref[0]
    tok = lax.broadcasted_iota(jnp.int32, (rows, 1), 0) // N_HEADS

    @pl.when(j == 0)
    def _():
        sn = sn_ref[0]
        j2 = lax.broadcasted_iota(jnp.int32, (1, pad), 1)
        m, l, acc = _online_step(_dot_nt(q, sn[:, 0:LANES]), (j2 <= tok) & (j2 < t), sn[:, LANES:2 * LANES],
                                 jnp.full((rows, 1), NEG, F32), jnp.zeros((rows, 1), F32),
                                 jnp.zeros((rows, LANES), F32))
        m_s[...] = m
        l_s[...] = l
        acc_s[...] = acc

    sel_bias = sel_ref[0]
    eb = lax.broadcasted_iota(jnp.int32, (nbp, PAGE_SIZE), 0)
    ej = lax.broadcasted_iota(jnp.int32, (nbp, PAGE_SIZE), 1)
    group = 16
    for s0 in range(0, n_slots, group):
        scores, vals = [], []
        for s in range(s0, min(s0 + group, n_slots)):
            first_blk = (j * n_slots + s) * (PAGE_SIZE // SLC_BLOCK)
            bias = _dot(sel_bias, (first_blk + ej // SLC_BLOCK == eb).astype(BF16))
            scores.append(_dot(q, page_refs[s][0].astype(BF16)) + bias)
            vals.append(page_refs[s][1].astype(BF16))
        _joint_update(m_s, l_s, acc_s, scores, vals, keys_on_lanes=True)

    @pl.when(j == n_steps - 1)
    def _():
        o = acc_s[...] / l_s[...]
        g0 = (lax.broadcasted_iota(jnp.int32, (rows, HEAD_DIM), 0) % N_HEADS) < HPG_C
        o_s = jnp.where(g0, o[:, 0:HEAD_DIM], o[:, HEAD_DIM:2 * HEAD_DIM])
        gt = gt_ref[0]
        o_ref[0] = oc_ref[0] * gt[:, 0:1] + o_s * gt[:, 1:2] + ow_ref[0] * gt[:, 2:3]


def _nsa_sample_slc(qrot_bd, sel_rows, s_new, o_c, o_w, gates, pool, base, page_table):
    n, rows, _ = qrot_bd.shape
    t = rows // N_HEADS
    nbp = sel_rows.shape[2]
    pad = s_new.shape[1]
    n_pages = page_table.shape[1]
    n_slots = _tile(n_pages, (16, 8, 4, 2))
    n_steps = n_pages // n_slots
    seq = lambda b, j, pt: (b, 0, 0)
    gs = pltpu.PrefetchScalarGridSpec(
        num_scalar_prefetch=1, grid=(n, n_steps),
        in_specs=[pl.BlockSpec((1, rows, LANES), seq), pl.BlockSpec((1, rows, nbp), seq),
                  pl.BlockSpec((1, pad, 2 * LANES), seq), pl.BlockSpec((1, rows, HEAD_DIM), seq),
                  pl.BlockSpec((1, rows, HEAD_DIM), seq), pl.BlockSpec((1, rows, 3), seq)]
        + _page_specs(n_slots, (2, LANES, PAGE_SIZE), (1, 0, 0), base),
        out_specs=pl.BlockSpec((1, rows, HEAD_DIM), seq),
        scratch_shapes=[pltpu.VMEM((rows, 1), F32), pltpu.VMEM((rows, 1), F32), pltpu.VMEM((rows, LANES), F32)])
    return pl.pallas_call(
        functools.partial(_nsa_sample_slc_kernel, n_slots=n_slots, n_steps=n_steps, t=t, nbp=nbp, pad=pad),
        grid_spec=gs,
        out_shape=jax.ShapeDtypeStruct((n, rows, HEAD_DIM), F32),
        compiler_params=_cparams("parallel", "arbitrary"),
        name="nsa_sample_slc",
    )(page_table, qrot_bd, sel_rows, s_new, o_c, o_w, gates, *([pool] * n_slots))


def _page_sum_kernel(pt_ref, *refs, n_slots):
    o_ref = refs[n_slots]
    j = pl.program_id(1)
    ppb = MOBA_BLOCK // PAGE_SIZE

    @pl.when(j == 0)
    def _():
        o_ref[...] = jnp.zeros_like(o_ref)

    lane = lax.broadcasted_iota(jnp.int32, o_ref.shape[1:], 1)
    acc = o_ref[0]
    for s in range(n_slots):
        tot = jnp.sum(refs[s][...], axis=1, keepdims=True)
        acc = acc + jnp.where(lane == (j * n_slots + s) // ppb, tot, 0.0)
    o_ref[0] = acc


def _page_sums(pool_t, base, page_table, nbp):
    n, n_pages = page_table.shape
    n_slots = _tile(n_pages, (16, 8, 4, 2))
    gs = pltpu.PrefetchScalarGridSpec(
        num_scalar_prefetch=1, grid=(n, n_pages // n_slots),
        in_specs=_page_specs(n_slots, (None, HALF_MIX, PAGE_SIZE), (0, 0, 0), base),
        out_specs=pl.BlockSpec((1, HALF_MIX, nbp), lambda b, j, pt: (b, 0, 0)))
    return pl.pallas_call(
        functools.partial(_page_sum_kernel, n_slots=n_slots),
        grid_spec=gs,
        out_shape=jax.ShapeDtypeStruct((n, HALF_MIX, nbp), F32),
        compiler_params=_cparams("parallel", "arbitrary"),
        name="moba_page_sums",
    )(page_table, *([pool_t] * n_slots))


def _moba_sample_kernel(pt_ref, q_ref, qf_ref, ks_ref, kn_ref, vn_ref, *refs, n_slots, n_steps, t, nbp, pad,
                        n_past):
    page_refs = refs[:n_slots]
    o_ref, sel_s, m_s, l_s, acc_s = refs[n_slots:]
    j = pl.program_id(1)
    rows = t * N_HEADS
    ppb = MOBA_BLOCK // PAGE_SIZE
    nb_full = n_past // MOBA_BLOCK
    q = q_ref[0]
    tok = lax.broadcasted_iota(jnp.int32, (rows, 1), 0) // N_HEADS

    @pl.when(j == 0)
    def _():
        gate = _dot(qf_ref[0], ks_ref[0] * (1.0 / MOBA_BLOCK), precision=HIGHEST)
        blk = lax.broadcasted_iota(jnp.int32, (rows, nbp), 1)
        own = (n_past + tok) // MOBA_BLOCK
        gate = jnp.where((blk < own) & (blk < nb_full), gate, NEG)
        sel_s[...] = jnp.where(_top_select(gate, blk, MOBA_TOPK) > 0.5, 0.0, NEG).astype(BF16)
        j2 = lax.broadcasted_iota(jnp.int32, (1, pad), 1)
        m, l, acc = _online_step(_dot_nt(q, kn_ref[0]), (j2 <= tok) & (j2 < t), vn_ref[0],
                                 jnp.full((rows, 1), NEG, F32), jnp.zeros((rows, 1), F32),
                                 jnp.zeros((rows, HALF_MIX), F32))
        m_s[...] = m
        l_s[...] = l
        acc_s[...] = acc

    sel_bias = sel_s[...]
    eb = lax.broadcasted_iota(jnp.int32, (nbp, PAGE_SIZE), 0)
    scores, vals = [], []
    for s in range(n_slots):
        blk_of_page = (j * n_slots + s) // ppb
        bias = _dot(sel_bias, (eb == blk_of_page).astype(BF16))
        scores.append(_dot(q, page_refs[s][0].astype(BF16)) + bias)
        vals.append(page_refs[s][1].astype(BF16))
    _joint_update(m_s, l_s, acc_s, scores, vals, keys_on_lanes=True)

    @pl.when(j == n_steps - 1)
    def _():
        o_ref[0] = _fold_heads(acc_s[...] / l_s[...], t)


def _moba_sample(qbd, qbd_f32, ksums, k_new, v_new, pool, base, page_table, n_past):
    n, rows, _ = qbd.shape
    t = rows // N_HEADS
    pad = k_new.shape[1]
    n_pages = page_table.shape[1]
    assert n_past % MOBA_BLOCK == 0 and t <= MOBA_BLOCK
    nbp = ksums.shape[2]
    n_slots = _tile(n_pages, (16, 8, 4, 2))
    n_steps = n_pages // n_slots
    seq = lambda b, j, pt: (b, 0, 0)
    gs = pltpu.PrefetchScalarGridSpec(
        num_scalar_prefetch=1, grid=(n, n_steps),
        in_specs=[pl.BlockSpec((1, rows, HALF_MIX), seq), pl.BlockSpec((1, rows, HALF_MIX), seq),
                  pl.BlockSpec((1, HALF_MIX, nbp), seq),
                  pl.BlockSpec((1, pad, HALF_MIX), seq), pl.BlockSpec((1, pad, HALF_MIX), seq)]
        + _page_specs(n_slots, (2, HALF_MIX, PAGE_SIZE), (0, 0, 0), base),
        out_specs=pl.BlockSpec((1, t, HALF_MIX), seq),
        scratch_shapes=[pltpu.VMEM((rows, nbp), BF16), pltpu.VMEM((rows, 1), F32), pltpu.VMEM((rows, 1), F32),
                        pltpu.VMEM((rows, HALF_MIX), F32)])
    return pl.pallas_call(
        functools.partial(_moba_sample_kernel, n_slots=n_slots, n_steps=n_steps, t=t, nbp=nbp, pad=pad,
                          n_past=n_past),
        grid_spec=gs,
        out_shape=jax.ShapeDtypeStruct((n, t, HALF_MIX), F32),
        compiler_params=_cparams("parallel", "arbitrary"),
        name="moba_sample",
    )(page_table, qbd, qbd_f32, ksums, k_new, v_new, *([pool] * n_slots))


def _block_diag(blocks):
    nb, bi, bj = blocks.shape
    eye = jnp.eye(nb, dtype=blocks.dtype)
    return jnp.einsum('bij,bc->bicj', blocks, eye).reshape(nb * bi, nb * bj)


def _rope_tables(pos):
    half = HEAD_DIM // 2
    inv_freq = ROPE_THETA ** (-jnp.arange(half, dtype=F32) / half)
    ang = pos.astype(F32)[:, None] * inv_freq[None, :]
    cos = jnp.cos(ang)
    sin = jnp.sin(ang)
    return jnp.concatenate([cos, cos, cos, cos], axis=1), jnp.concatenate([-sin, sin, -sin, sin], axis=1)


def _queries_bd(q, per, scale, dtype):
    n, t, _ = q.shape
    groups = N_HEADS // per
    qh = q.reshape(n, t, N_HEADS, 1, HEAD_DIM) * scale
    pick = (jnp.arange(N_HEADS)[:, None] // per == jnp.arange(groups)[None, :]).astype(q.dtype)
    return (qh * pick[None, None, :, :, None]).reshape(n, t * N_HEADS, groups * HEAD_DIM).astype(dtype)


def _pad_rows(x, pad):
    n, t, w = x.shape
    return jnp.concatenate([x, jnp.zeros((n, pad - t, w), x.dtype)], axis=1)


def _odd_weight(w_in):
    d = w_in.shape[0]
    gc0 = _O_VW + 128
    main = jnp.concatenate([w_in[:, :gc0], w_in[:, gc0 + 3 * N_HEADS:]], axis=1)
    gc = w_in[:, gc0:gc0 + 3 * N_HEADS].reshape(d, N_KV_C, HPG_C, 3).transpose(0, 1, 3, 2).reshape(d, N_KV_C, 3 * HPG_C)
    gc = jnp.concatenate([gc, jnp.zeros((d, N_KV_C, LANES - 3 * HPG_C), w_in.dtype)], axis=2).reshape(d, N_KV_C * LANES)
    return jnp.concatenate([main, gc], axis=1).astype(BF16)


def kernel(x_prompt, x_sample, cache_a_kv, state_b_h, state_b_conv, cache_c_kv, state_c_win, cache_d_kv, page_table, norm_ffn1, w_ffn1_gate, w_ffn1_up, w_ffn1_down, norm_mix, norm_ffn2, w_ffn2_gate, w_ffn2_up, w_ffn2_down, w_in_even, w_out_even, lru_conv_w, lru_conv_b, lru_w_r, lru_b_r, lru_w_i, lru_b_i, lru_lambda, w_in_odd, w_out_odd, cmp_pe_k, cmp_w1_k, cmp_w2_k, cmp_pe_v, cmp_w1_v, cmp_w2_v, norm_final):
    nb, s_len, d = x_prompt.shape
    ns, t = x_sample.shape[:2]
    depth = norm_mix.shape[0]
    n_pages = page_table.shape[1]
    n_past = n_pages * PAGE_SIZE
    n_pool = cache_a_kv.shape[1]
    hm = HALF_MIX
    pad = 16
    assert t <= pad and n_past % SLC_BLOCK == 0

    yp = x_prompt.reshape(nb * s_len, d)
    ys = x_sample.reshape(ns * t, d)
    cos_p, sin_p = _rope_tables(jnp.arange(s_len, dtype=jnp.int32))
    cos_s, sin_s = _rope_tables(jnp.tile(n_past + jnp.arange(t, dtype=jnp.int32), ns))
    pool_a = _pages_t(cache_a_kv)
    pool_c = _pages_t(cache_c_kv)
    pool_d = _pages_t(cache_d_kv)

    outs = {k: [] for k in ("a_p", "a_s", "bh_p", "bh_s", "bc_p", "bc_s", "c_p", "c_s", "cw_p", "cw_s", "d_p", "d_s")}
    for layer in range(depth):
        li = layer // 2
        last = layer == depth - 1
        f1 = (norm_ffn1[layer], w_ffn1_gate[layer].astype(BF16), w_ffn1_up[layer].astype(BF16),
              w_ffn1_down[layer].astype(BF16))
        f2 = (norm_ffn2[layer], w_ffn2_gate[layer].astype(BF16), w_ffn2_up[layer].astype(BF16),
              w_ffn2_down[layer].astype(BF16))
        yp = _half_ffn(yp, *f1)
        ys = _half_ffn(ys, *f1)
        if layer % 2 == 0:
            w_in = w_in_even[li].astype(BF16)
            w_out = w_out_even[li].astype(BF16)
            lru = (lru_conv_w[li], lru_conv_b[li], _block_diag(lru_w_r[li]).astype(BF16), lru_b_r[li],
                   _block_diag(lru_w_i[li]).astype(BF16), lru_b_i[li], lru_lambda[li])
            kv, xb, gb, qh, kh, vh = _proj_even(yp, norm_mix[layer], w_in, True)
            o_a = _sb_prompt(qh, kh, vh, nb, s_len)
            g, h_new, buf_new = _rg_lru(xb.reshape(nb, s_len, hm), gb.reshape(nb, s_len, hm),
                                        jnp.zeros((nb, hm), F32), jnp.zeros((nb, CONV_WIDTH - 1, hm), F32), *lru)
            yp = _outproj(yp, o_a, g.reshape(nb * s_len, hm), w_out)
            outs["a_p"].append(kv.reshape(nb, s_len, 2, N_HEADS, HEAD_DIM))
            outs["bh_p"].append(h_new)
            outs["bc_p"].append(buf_new)
            kv, xb, gb, q = _proj_even(ys, norm_mix[layer], w_in, False)
            kv3 = kv.reshape(ns, t, 2 * hm)
            o_a = _sb_sample(_queries_bd(q.reshape(ns, t, hm), 1, SCALE, BF16),
                             _pad_rows(kv3[:, :, :hm], pad).astype(BF16), _pad_rows(kv3[:, :, hm:], pad).astype(BF16),
                             pool_a, li * n_pool, page_table)
            g, h_new, buf_new = _rg_lru(xb.reshape(ns, t, hm), gb.reshape(ns, t, hm), state_b_h[li],
                                        state_b_conv[li], *lru)
            ys = _outproj(ys, o_a.reshape(ns * t, hm).astype(BF16), g.reshape(ns * t, hm), w_out)
            outs["a_s"].append(kv.reshape(ns, t, 2, N_HEADS, HEAD_DIM))
            outs["bh_s"].append(h_new)
            outs["bc_s"].append(buf_new)
        else:
            w_in = _odd_weight(w_in_odd[li])
            w_out = w_out_odd[li].astype(BF16)
            cmp_k = _cmp_weights(cmp_pe_k[li], cmp_w1_k[li])
            cmp_v = _cmp_weights(cmp_pe_v[li], cmp_w1_v[li])
            w2k = _block_diag(jnp.stack([cmp_w2_k[li]] * N_KV_C)).astype(BF16)
            w2v = _block_diag(jnp.stack([cmp_w2_v[li]] * N_KV_C)).astype(BF16)
            (c_rows, w_rows, d_rows, gates, qraw_h, qrot_h, ks_h, vs_h, kw_h, vw_h, qd_h, kd_h, vd_h) = _proj_odd(
                yp, norm_mix[layer], w_in, cos_p, sin_p, True)
            n_chunk = s_len // CMP_STRIDE
            rows_c = _tile(s_len, (2048, 1024, 512, 256, 128))
            ab = _cmp_ab([c_rows], [pl.BlockSpec((rows_c, LANES), lambda b, j: (b * (s_len // rows_c) + j, 0))],
                         [pl.BlockSpec((rows_c, LANES), lambda b, j: (b * (s_len // rows_c) + j, 1))],
                         (nb, s_len // rows_c), rows_c // CMP_STRIDE, nb, n_chunk, rows_c, cmp_k, cmp_v, 0, ())
            _, _, kcc_g, vcc_g = _cmp_finish(ab, w2k, w2v)
            o_c = _nsa_prompt(qraw_h, qrot_h, gates, kcc_g, vcc_g, ks_h, vs_h, kw_h, vw_h, nb, s_len)
            o_d = _moba_prompt(qd_h, _block_mean(d_rows), kd_h, vd_h, nb, s_len)
            yp = _outproj(yp, o_c, o_d, w_out)
            wlen = min(WINDOW, s_len)
            outs["c_p"].append(c_rows.reshape(nb, s_len, 4, N_KV_C, HEAD_DIM))
            outs["cw_p"].append(w_rows.reshape(nb, s_len, 2, N_KV_C, HEAD_DIM)[:, s_len - wlen:])
            outs["d_p"].append(d_rows.reshape(nb, s_len, 2, N_HEADS, HEAD_DIM))
            c_rows, w_rows, d_rows, gates, qraw, qrot, qd = _proj_odd(ys, norm_mix[layer], w_in, cos_s, sin_s, False)
            slots_c = _tile(n_pages, (16, 8))
            ab = _cmp_ab([pool_c] * slots_c, _page_specs(slots_c, (None, LANES, PAGE_SIZE), (0, 0, 0), li * n_pool),
                         _page_specs(slots_c, (None, LANES, PAGE_SIZE), (1, 0, 0), li * n_pool),
                         (ns, n_pages // slots_c), slots_c * PAGE_SIZE // CMP_STRIDE, ns, n_past // CMP_STRIDE,
                         PAGE_SIZE, cmp_k, cmp_v, 1, (page_table,), pages_t=True)
            kcc_f, vcc_f, _, _ = _cmp_finish(ab, w2k, w2v)
            nsb = -(-(n_past + t) // SLC_BLOCK)
            nbp = LANES * (-(-nsb // LANES))
            win_l = state_c_win[li]
            wb = win_l.shape[1]
            qraw_bd = _queries_bd(qraw.reshape(ns, t, hm), HPG_C, SCALE, BF16)
            qrot_bd = _queries_bd(qrot.reshape(ns, t, hm), HPG_C, SCALE, BF16)
            c3 = c_rows.reshape(ns, t, 4 * LANES)
            w3 = w_rows.reshape(ns, t, 2 * LANES)
            d3 = d_rows.reshape(ns, t, 2 * hm)
            o_cmp, o_win, sel = _nsa_sample_head(qraw_bd, qrot_bd, kcc_f, vcc_f, win_l.reshape(ns, wb, 2 * LANES),
                                                 _pad_rows(w3, pad).astype(BF16), n_past, nbp)
            sel_rows = jnp.repeat(sel.reshape(ns, t, N_KV_C, nbp), HPG_C, axis=2).reshape(ns, t * N_HEADS, nbp)
            gt = gates.reshape(ns, t, N_KV_C, LANES)[..., :3 * HPG_C].reshape(ns, t, N_KV_C, 3, HPG_C)
            gt = gt.transpose(0, 1, 2, 4, 3).reshape(ns, t * N_HEADS, 3)
            sel_bias = jnp.where(sel_rows > 0.5, 0.0, NEG).astype(BF16)
            o_c = _nsa_sample_slc(qrot_bd, sel_bias, _pad_rows(c3[:, :, 2 * LANES:], pad).astype(BF16),
                                  o_cmp, o_win, gt, pool_c, li * n_pool, page_table)
            ksums = _page_sums(pool_d, li * n_pool, page_table, LANES * (-(-(n_past // MOBA_BLOCK) // LANES)))
            qd3 = qd.reshape(ns, t, hm)
            o_d = _moba_sample(_queries_bd(qd3, 1, SCALE, BF16), _queries_bd(qd3, 1, 1.0, F32), ksums,
                               _pad_rows(d3[:, :, :hm], pad).astype(BF16), _pad_rows(d3[:, :, hm:], pad).astype(BF16),
                               pool_d, li * n_pool, page_table, n_past)
            ys = _outproj(ys, o_c.reshape(ns * t, hm).astype(BF16), o_d.reshape(ns * t, hm).astype(BF16), w_out)
            w5 = w_rows.reshape(ns, t, 2, N_KV_C, HEAD_DIM)
            outs["c_s"].append(c_rows.reshape(ns, t, 4, N_KV_C, HEAD_DIM))
            outs["cw_s"].append(jnp.concatenate([win_l, w5], axis=1)[:, -wb:])
            outs["d_s"].append(d_rows.reshape(ns, t, 2, N_HEADS, HEAD_DIM))
        gf = norm_final if last else None
        yp = _half_ffn(yp, *f2, g_final=gf)
        ys = _half_ffn(ys, *f2, g_final=gf)
    st = lambda k: jnp.stack(outs[k])
    return (yp.reshape(nb, s_len, d), ys.reshape(ns, t, d), st("a_p"), st("a_s"), st("bh_p"), st("bh_s"),
            st("bc_p"), st("bc_s"), st("c_p"), st("c_s"), st("cw_p"), st("cw_s"), st("d_p"), st("d_s"))
```
